```python
import math
import jax
import jax.numpy as jnp
from jax import lax
import numpy as np


D_MODEL = 2048
BATCH = 4
SEQ = 2048
DEPTH = 4

N_MIXERS = 4
LAYERS_PER_MIXER = tuple(len(range(m, DEPTH, N_MIXERS)) for m in range(N_MIXERS))

MLA_HEADS = 16
MLA_Q_LORA = 512
MLA_KV_LORA = 512
MLA_NOPE = 128
MLA_ROPE = 64
MLA_V = 128
ROPE_THETA = 10000.0
ATTN_Q_BLOCK = 128

GDN_K_HEADS = 16
GDN_V_HEADS = 32
GDN_DK = 128
GDN_DV = 128
GDN_CONV = 4
GDN_CHUNK = 64

GLA_HEADS = 4
GLA_KEY_DIM = D_MODEL // 2
GLA_VAL_DIM = D_MODEL
GLA_GATE_RANK = 16
GLA_GATE_NORMALIZER = 16.0
GLA_CHUNK = 64

MOBA_HEADS = 16
MOBA_HEAD_DIM = D_MODEL // MOBA_HEADS
MOBA_BLOCK = 256
MOBA_TOPK = 3
MOBA_Q_BLOCK = 8

REL_BUCKETS = 32
REL_MAX_DIST = 128

N_EXPERTS = 32
TOP_K = 4
EXPERT_FF = 768
SWIGLU_LIMIT = 7.0
SWIGLU_ALPHA = 1.702

DEEPNORM_ALPHA = (2 * DEPTH) ** 0.25
DEEPNORM_BETA = (8 * DEPTH) ** -0.25
LN_EPS = 1e-5
RMS_EPS = 1e-6

kernel_name = "hybrid_mla_gdn_gla_moba_moe_trunk"


def layer_norm(x, g, b):
    xf = x.astype(jnp.float32)
    mu = jnp.mean(xf, -1, keepdims=True)
    var = jnp.mean(jnp.square(xf - mu), -1, keepdims=True)
    return ((xf - mu) * lax.rsqrt(var + LN_EPS)).astype(x.dtype) * g + b


def rms_norm(x, g):
    xf = x.astype(jnp.float32)
    y = xf * lax.rsqrt(jnp.mean(xf * xf, -1, keepdims=True) + RMS_EPS)
    return y.astype(x.dtype) * g


def l2_normalize(x):
    xf = x.astype(jnp.float32)
    return xf * lax.rsqrt(jnp.sum(xf * xf, -1, keepdims=True) + 1e-6)


def rope_tables(pos, dim):
    half = dim // 2
    inv_freq = ROPE_THETA ** (-jnp.arange(half, dtype=jnp.float32) / half)
    ang = pos.astype(jnp.float32)[:, None] * inv_freq[None, :]
    return jnp.cos(ang), jnp.sin(ang)


def apply_rope(x, cos, sin):
    half = x.shape[-1] // 2
    x1, x2 = x[..., :half], x[..., half:]
    return jnp.concatenate([x1 * cos - x2 * sin, x2 * cos + x1 * sin], -1).astype(x.dtype)


def t5_bucket(dist):
    n = jnp.maximum(dist, 0)
    max_exact = REL_BUCKETS // 2
    large = max_exact + (jnp.log(jnp.maximum(n, 1).astype(jnp.float32) / max_exact)
                         / math.log(REL_MAX_DIST / max_exact) * (REL_BUCKETS - max_exact)).astype(jnp.int32)
    large = jnp.minimum(large, REL_BUCKETS - 1)
    return jnp.where(n < max_exact, n, large)


def causal_depthwise_conv(x, w):
    k_width, ch = w.shape
    return lax.conv_general_dilated(x, w[:, None, :], window_strides=(1,), padding=[(k_width - 1, 0)],
                                    dimension_numbers=('NWC', 'WIO', 'NWC'), feature_group_count=ch)


def mla_mixer(u, pos, w_in, q_norm, kv_norm, w_qb, w_kvb, w_o):
    B, S, _ = u.shape
    lat = u @ w_in
    q_lat, kv_lat, k_rope = jnp.split(lat, [MLA_Q_LORA, MLA_Q_LORA + MLA_KV_LORA], -1)
    q = (rms_norm(q_lat, q_norm) @ w_qb).reshape(B, S, MLA_HEADS, MLA_NOPE + MLA_ROPE)
    kv = (rms_norm(kv_lat, kv_norm) @ w_kvb).reshape(B, S, MLA_HEADS, MLA_NOPE + MLA_V)
    q_nope, q_rope = q[..., :MLA_NOPE], q[..., MLA_NOPE:]
    k_nope, v = kv[..., :MLA_NOPE], kv[..., MLA_NOPE:]
    cos, sin = rope_tables(pos, MLA_ROPE)
    q_rope = apply_rope(q_rope, cos[:, None, :], sin[:, None, :])
    k_rope = apply_rope(k_rope, cos, sin)
    scale = (MLA_NOPE + MLA_ROPE) ** -0.5
    key_pos = jnp.arange(S)

    def block(i):
        s0 = i * ATTN_Q_BLOCK
        qn = lax.dynamic_slice_in_dim(q_nope, s0, ATTN_Q_BLOCK, 1)
        qr = lax.dynamic_slice_in_dim(q_rope, s0, ATTN_Q_BLOCK, 1)
        logits = (jnp.einsum('bqhd,bkhd->bhqk', qn, k_nope)
                  + jnp.einsum('bqhd,bkd->bhqk', qr, k_rope)).astype(jnp.float32) * scale
        q_pos = s0 + jnp.arange(ATTN_Q_BLOCK)
        logits = jnp.where(key_pos[None, :] <= q_pos[:, None], logits, -jnp.inf)
        p = jax.nn.softmax(logits, -1).astype(v.dtype)
        return jnp.einsum('bhqk,bkhd->bqhd', p, v)

    o = lax.map(block, jnp.arange(S // ATTN_Q_BLOCK))
    o = o.transpose(1, 0, 2, 3, 4).reshape(B, S, MLA_HEADS * MLA_V)
    return o @ w_o


def gated_delta_rule_chunked(q, k, v, g, beta):
    B, S, H, DK = q.shape
    DV = v.shape[-1]
    C = GDN_CHUNK
    N = S // C
    f32 = jnp.float32
    q = l2_normalize(q) * (DK ** -0.5)
    k = l2_normalize(k)
    v = v.astype(f32)

    def chunks(t):
        return t.reshape(B, N, C, H, *t.shape[3:]).swapaxes(2, 3)

    q, k, v, g, beta = chunks(q), chunks(k), chunks(v), chunks(g), chunks(beta)
    g = jnp.cumsum(g, -1)
    tri = jnp.tril(jnp.ones((C, C), bool))
    strict = jnp.tril(jnp.ones((C, C), bool), -1)
    decay = jnp.where(tri, jnp.exp(jnp.where(tri, g[..., :, None] - g[..., None, :], 0.0)), 0.0)
    k_beta = k * beta[..., None]
    v_beta = v * beta[..., None]
    a_low = jnp.where(strict, jnp.einsum('bnhid,bnhjd->bnhij', k_beta, k) * decay, 0.0)
    eye = jnp.eye(C, dtype=f32)
    rhs = jnp.concatenate([v_beta, k_beta * jnp.exp(g)[..., None]], -1)
    sol = lax.linalg.triangular_solve(a_low + eye, rhs, left_side=True, lower=True, unit_diagonal=True)
    u_val, w_dec = sol[..., :DV], sol[..., DV:]
    attn_intra = jnp.where(tri, jnp.einsum('bnhid,bnhjd->bnhij', q, k) * decay, 0.0)
    g_last = g[..., -1]
    q_dec = q * jnp.exp(g)[..., None]
    k_tail = k * jnp.exp(g_last[..., None] - g)[..., None]

    def step(state, xs):
        u_c, w_c, a_c, qd_c, kt_c, gl_c = xs
        v_new = u_c - jnp.einsum('bhcd,bhde->bhce', w_c, state)
        o_c = jnp.einsum('bhcd,bhde->bhce', qd_c, state) + jnp.einsum('bhij,bhje->bhie', a_c, v_new)
        state = state * jnp.exp(gl_c)[..., None, None] + jnp.einsum('bhcd,bhce->bhde', kt_c, v_new)
        return state, o_c

    xs = tuple(jnp.moveaxis(t, 1, 0) for t in (u_val, w_dec, attn_intra, q_dec, k_tail, g_last))
    _, o = lax.scan(step, jnp.zeros((B, H, DK, DV), f32), xs)
    return o.transpose(1, 0, 3, 2, 4).reshape(B, S, H, DV)


def gdn_mixer(u, w_in, conv_w, a_log, dt_bias, norm_g, w_o):
    B, S, _ = u.shape
    qk_dim = GDN_K_HEADS * GDN_DK
    v_dim = GDN_V_HEADS * GDN_DV
    proj = u @ w_in
    qkv, z, b, a = jnp.split(proj, [2 * qk_dim + v_dim, 2 * qk_dim + 2 * v_dim,
                                    2 * qk_dim + 2 * v_dim + GDN_V_HEADS], -1)
    qkv = jax.nn.silu(causal_depthwise_conv(qkv, conv_w))
    q, k, v = jnp.split(qkv, [qk_dim, 2 * qk_dim], -1)
    rep = GDN_V_HEADS // GDN_K_HEADS
    q = jnp.repeat(q.reshape(B, S, GDN_K_HEADS, GDN_DK), rep, axis=2)
    k = jnp.repeat(k.reshape(B, S, GDN_K_HEADS, GDN_DK), rep, axis=2)
    v = v.reshape(B, S, GDN_V_HEADS, GDN_DV)
    beta = jax.nn.sigmoid(b.astype(jnp.float32))
    g = -jnp.exp(a_log.astype(jnp.float32)) * jax.nn.softplus(a.astype(jnp.float32) + dt_bias)
    o = gated_delta_rule_chunked(q, k, v, g, beta).astype(u.dtype)
    o = rms_norm(o, norm_g) * jax.nn.silu(z.reshape(B, S, GDN_V_HEADS, GDN_DV))
    return o.reshape(B, S, v_dim) @ w_o


def gla_chunked(q, k, v, log_alpha):
    B, S, H, DK = q.shape
    DV = v.shape[-1]
    C = GLA_CHUNK
    N = S // C
    f32 = jnp.float32

    def chunks(t):
        return t.astype(f32).reshape(B, N, C, H, t.shape[-1]).swapaxes(2, 3)

    q = chunks(q) * (DK ** -0.5)
    k, v, la = chunks(k), chunks(v), chunks(log_alpha)
    b = jnp.cumsum(la, axis=-2)
    b_last = b[..., -1, :]
    q_dec = q * jnp.exp(b)
    k_inv = k * jnp.exp(-b)
    k_tail = k * jnp.exp(b_last[..., None, :] - b)
    causal = jnp.tril(jnp.ones((C, C), bool))
    attn = jnp.where(causal, jnp.einsum('bnhid,bnhjd->bnhij', q_dec, k_inv), 0.0)
    o_intra = jnp.einsum('bnhij,bnhje->bnhie', attn, v)

    def step(state, xs):
        qd, kt, vc, bl = xs
        o_c = jnp.einsum('bhcd,bhde->bhce', qd, state)
        state = state * jnp.exp(bl)[..., None] + jnp.einsum('bhcd,bhce->bhde', kt, vc)
        return state, o_c

    xs = tuple(jnp.moveaxis(t, 1, 0) for t in (q_dec, k_tail, v, b_last))
    _, o_inter = lax.scan(step, jnp.zeros((B, H, DK, DV), f32), xs)
    o = jnp.moveaxis(o_inter, 0, 1) + o_intra
    return o.swapaxes(2, 3).reshape(B, S, H, DV)


def gla_mixer(u, w_in, w_gk, b_gk, norm_g, w_o):
    B, S, _ = u.shape
    H = GLA_HEADS
    dk = GLA_KEY_DIM // H
    dv = GLA_VAL_DIM // H
    proj = u @ w_in
    q, k, v, out_gate, gk_low = jnp.split(
        proj, [GLA_KEY_DIM, 2 * GLA_KEY_DIM, 2 * GLA_KEY_DIM + GLA_VAL_DIM, 2 * GLA_KEY_DIM + 2 * GLA_VAL_DIM], -1)
    log_alpha = jax.nn.log_sigmoid((gk_low @ w_gk + b_gk).astype(jnp.float32)) / GLA_GATE_NORMALIZER
    o = gla_chunked(q.reshape(B, S, H, dk), k.reshape(B, S, H, dk), v.reshape(B, S, H, dv),
                    log_alpha.reshape(B, S, H, dk)).astype(u.dtype)
    o = rms_norm(o, norm_g) * jax.nn.silu(out_gate.reshape(B, S, H, dv))
    return o.reshape(B, S, GLA_VAL_DIM) @ w_o


def moba_mixer(u, pos, w_in, w_o, rel_bias):
    B, S, _ = u.shape
    H, Dh, L = MOBA_HEADS, MOBA_HEAD_DIM, MOBA_BLOCK
    qkv = (u @ w_in).reshape(B, S, 3, H, Dh)
    q = qkv[:, :, 0].transpose(0, 2, 1, 3)
    k = qkv[:, :, 1].transpose(0, 2, 1, 3)
    v = qkv[:, :, 2].transpose(0, 2, 1, 3)
    n_blk = -(-S // L)
    pad = n_blk * L - S
    k_blk = jnp.pad(k, ((0, 0), (0, 0), (0, pad), (0, 0))).reshape(B, H, n_blk, L, Dh)
    v_blk = jnp.pad(v, ((0, 0), (0, 0), (0, pad), (0, 0))).reshape(B, H, n_blk, L, Dh)
    k_mean = jnp.mean(k_blk, axis=-2)
    gate = jnp.einsum('bhsd,bhnd->bhsn', q, k_mean).astype(jnp.float32)
    q_blk = pos // L
    past = jnp.arange(n_blk)[None, :] < q_blk[:, None]
    gate = jnp.where(past, gate, -jnp.inf)
    n_sel = max(min(MOBA_TOPK, n_blk - 1), 1)
    top_val, top_idx = lax.top_k(gate, n_sel)
    own = jnp.broadcast_to(q_blk[None, None, :, None], (B, H, S, 1)).astype(top_idx.dtype)
    idx = jnp.concatenate([top_idx, own], -1)
    valid = jnp.concatenate([jnp.isfinite(top_val), jnp.ones((B, H, S, 1), bool)], -1)
    n_g = n_sel + 1
    nq = S // MOBA_Q_BLOCK

    def split_q(t):
        return jnp.moveaxis(t.reshape(B, H, nq, MOBA_Q_BLOCK, *t.shape[3:]), 2, 0)

    gather = jax.vmap(jax.vmap(lambda blk, ix: blk[ix]))
    scale = Dh ** -0.5
    key_off = jnp.arange(L)
    head_ix = jnp.arange(H)[None, :, None, None, None]
    bias_table = rel_bias.T

    def block(xs):
        qc, ic, vc, pc = xs
        kg = gather(k_blk, ic)
        vg = gather(v_blk, ic)
        logits = jnp.einsum('bhqd,bhqgld->bhqgl', qc, kg).astype(jnp.float32) * scale
        dist = pc[None, None, :, None, None] - (ic[..., None] * L + key_off)
        bias = bias_table[head_ix, t5_bucket(dist)]
        mask = vc[..., None] & (dist >= 0)
        logits = jnp.where(mask, logits + bias, -jnp.inf).reshape(B, H, MOBA_Q_BLOCK, n_g * L)
        p = jax.nn.softmax(logits, -1).reshape(B, H, MOBA_Q_BLOCK, n_g, L).astype(vg.dtype)
        return jnp.einsum('bhqgl,bhqgld->bhqd', p, vg)

    o = lax.map(block, (split_q(q), split_q(idx), split_q(valid), pos.reshape(nq, MOBA_Q_BLOCK)))
    o = o.transpose(1, 0, 3, 2, 4).reshape(B, S, H * Dh)
    return o @ w_o


def moe_ffn(u, router_w, router_b, w_gu, b_gu, w_down, b_down):
    B, S, D = u.shape
    t = u.reshape(B * S, D)
    logits = (t @ router_w + router_b).astype(jnp.float32)
    top_val, top_idx = lax.top_k(logits, TOP_K)
    weights = jax.nn.softmax(top_val, -1)
    gates = jnp.sum(jax.nn.one_hot(top_idx, N_EXPERTS, dtype=jnp.float32) * weights[..., None], axis=1)
    out = jnp.zeros((B * S, D), jnp.float32)
    for e in range(N_EXPERTS):
        gu = t @ w_gu[e] + b_gu[e]
        gl = jnp.minimum(gu[:, :EXPERT_FF], SWIGLU_LIMIT)
        up = jnp.clip(gu[:, EXPERT_FF:], -SWIGLU_LIMIT, SWIGLU_LIMIT)
        h = (up + 1.0) * gl * jax.nn.sigmoid(gl * SWIGLU_ALPHA)
        out = out + gates[:, e:e + 1] * (h @ w_down[e] + b_down[e])
    return out.astype(u.dtype).reshape(B, S, D)


def setup_inputs(seed: int = 0) -> dict:
    key = jax.random.key(seed)
    keys = iter(list(jax.random.split(key, 64)))

    def nrm(shape, scale):
        return jax.random.normal(next(keys), shape, jnp.float32) * scale

    def gain(shape):
        return 1.0 + nrm(shape, 0.02)

    D = D_MODEL
    nA, nB, nC, nD = LAYERS_PER_MIXER
    beta = DEEPNORM_BETA
    mla_in = MLA_Q_LORA + MLA_KV_LORA + MLA_ROPE
    gdn_qk = GDN_K_HEADS * GDN_DK
    gdn_v = GDN_V_HEADS * GDN_DV
    gdn_in = 2 * gdn_qk + 2 * gdn_v + 2 * GDN_V_HEADS
    gdn_conv_ch = 2 * gdn_qk + gdn_v
    gla_in = 2 * GLA_KEY_DIM + 2 * GLA_VAL_DIM + GLA_GATE_RANK
    moba_w = MOBA_HEADS * MOBA_HEAD_DIM
    dt = jnp.exp(jax.random.uniform(next(keys), (nB, GDN_V_HEADS), jnp.float32, math.log(1e-3), math.log(1e-1)))
    a_log = jnp.log(jax.random.uniform(next(keys), (nB, GDN_V_HEADS), jnp.float32, 1.0, 16.0))
    return {
        "x": nrm((BATCH, SEQ, D), 1.0),
        "c": nrm((BATCH, D), 1.0),
        "rel_bias": nrm((REL_BUCKETS, MOBA_HEADS), 0.1),
        "mla_w_in": nrm((nA, D, mla_in), D ** -0.5),
        "mla_q_norm": gain((nA, MLA_Q_LORA)),
        "mla_kv_norm": gain((nA, MLA_KV_LORA)),
        "mla_w_qb": nrm((nA, MLA_Q_LORA, MLA_HEADS * (MLA_NOPE + MLA_ROPE)), MLA_Q_LORA ** -0.5),
        "mla_w_kvb": nrm((nA, MLA_KV_LORA, MLA_HEADS * (MLA_NOPE + MLA_V)), MLA_KV_LORA ** -0.5),
        "mla_w_o": nrm((nA, MLA_HEADS * MLA_V, D), beta * (MLA_HEADS * MLA_V) ** -0.5),
        "gdn_w_in": nrm((nB, D, gdn_in), D ** -0.5),
        "gdn_conv_w": nrm((nB, GDN_CONV, gdn_conv_ch), GDN_CONV ** -0.5),
        "gdn_a_log": a_log,
        "gdn_dt_bias": dt + jnp.log(-jnp.expm1(-dt)),
        "gdn_norm": gain((nB, GDN_DV)),
        "gdn_w_o": nrm((nB, gdn_v, D), beta * gdn_v ** -0.5),
        "gla_w_in": nrm((nC, D, gla_in), D ** -0.5),
        "gla_w_gk": nrm((nC, GLA_GATE_RANK, GLA_KEY_DIM), GLA_GATE_RANK ** -0.5),
        "gla_b_gk": nrm((nC, GLA_KEY_DIM), 0.1),
        "gla_norm": gain((nC, GLA_VAL_DIM // GLA_HEADS)),
        "gla_w_o": nrm((nC, GLA_VAL_DIM, D), beta * GLA_VAL_DIM ** -0.5),
        "moba_w_in": nrm((nD, D, 3 * moba_w), D ** -0.5),
        "moba_w_o": nrm((nD, moba_w, D), beta * moba_w ** -0.5),
        "ada_w": nrm((DEPTH, D, 6 * D), 0.2 * D ** -0.5),
        "ada_b": nrm((DEPTH, 6 * D), 0.02),
        "ln_g": gain((DEPTH, 2, D)),
        "ln_b": nrm((DEPTH, 2, D), 0.02),
        "router_w": nrm((DEPTH, D, N_EXPERTS), D ** -0.5),
        "router_b": nrm((DEPTH, N_EXPERTS), 0.01),
        "moe_w_gu": nrm((DEPTH, N_EXPERTS, D, 2 * EXPERT_FF), D ** -0.5),
        "moe_b_gu": nrm((DEPTH, N_EXPERTS, 2 * EXPERT_FF), 0.02),
        "moe_w_down": nrm((DEPTH, N_EXPERTS, EXPERT_FF, D), beta * EXPERT_FF ** -0.5),
        "moe_b_down": nrm((DEPTH, N_EXPERTS, D), 0.02),
    }


def reference(x, c, rel_bias, mla_w_in, mla_q_norm, mla_kv_norm, mla_w_qb, mla_w_kvb, mla_w_o,
              gdn_w_in, gdn_conv_w, gdn_a_log, gdn_dt_bias, gdn_norm, gdn_w_o,
              gla_w_in, gla_w_gk, gla_b_gk, gla_norm, gla_w_o,
              moba_w_in, moba_w_o, ada_w, ada_b, ln_g, ln_b,
              router_w, router_b, moe_w_gu, moe_b_gu, moe_w_down, moe_b_down):
    B, S, D = x.shape
    pos = jnp.arange(S, dtype=jnp.int32)
    c_act = jax.nn.silu(c)
    for i in range(DEPTH):
        m, j = i % N_MIXERS, i // N_MIXERS
        mod = (c_act @ ada_w[i] + ada_b[i])[:, None, :]
        sh_a, sc_a, g_a, sh_f, sc_f, g_f = jnp.split(mod, 6, -1)
        u = x * (1.0 + sc_a) + sh_a
        if m == 0:
            h = mla_mixer(u, pos, mla_w_in[j], mla_q_norm[j], mla_kv_norm[j], mla_w_qb[j], mla_w_kvb[j], mla_w_o[j])
        elif m == 1:
            h = gdn_mixer(u, gdn_w_in[j], gdn_conv_w[j], gdn_a_log[j], gdn_dt_bias[j], gdn_norm[j], gdn_w_o[j])
        elif m == 2:
            h = gla_mixer(u, gla_w_in[j], gla_w_gk[j], gla_b_gk[j], gla_norm[j], gla_w_o[j])
        else:
            h = moba_mixer(u, pos, moba_w_in[j], moba_w_o[j], rel_bias)
        x = layer_norm(DEEPNORM_ALPHA * x + (1.0 + g_a) * h, ln_g[i, 0], ln_b[i, 0])
        u = x * (1.0 + sc_f) + sh_f
        f = moe_ffn(u, router_w[i], router_b[i], moe_w_gu[i], moe_b_gu[i], moe_w_down[i], moe_b_down[i])
        x = layer_norm(DEEPNORM_ALPHA * x + (1.0 + g_f) * f, ln_g[i, 1], ln_b[i, 1])
    return x
```

```python
import functools
import math

import jax
import jax.numpy as jnp
from jax import lax
from jax.experimental import pallas as pl
from jax.experimental.pallas import tpu as pltpu

N_MIXERS = 4
MLA_HEADS, MLA_Q_LORA, MLA_KV_LORA, MLA_NOPE, MLA_ROPE, MLA_V = 16, 512, 512, 128, 64, 128
ROPE_THETA = 10000.0
GDN_K_HEADS, GDN_V_HEADS, GDN_DK, GDN_DV, GDN_CONV, GDN_CHUNK = 16, 32, 128, 128, 4, 64
GLA_HEADS, GLA_GATE_RANK, GLA_GATE_NORMALIZER, GLA_CHUNK = 4, 16, 16.0, 64
MOBA_HEADS, MOBA_BLOCK, MOBA_TOPK = 16, 256, 3
REL_BUCKETS, REL_MAX_DIST = 32, 128
N_EXPERTS, TOP_K, EXPERT_FF = 32, 4, 768
SWIGLU_LIMIT, SWIGLU_ALPHA = 7.0, 1.702
LN_EPS, RMS_EPS, L2_EPS = 1e-5, 1e-6, 1e-6

V7X_LANES = 128
V7X_VMEM_LIMIT_BYTES = 56 * 1024 * 1024

F32 = jnp.float32
BF16 = jnp.bfloat16
HIGHEST = lax.Precision.HIGHEST
NEG_INF = float("-inf")


def _params(*sem):
    return pltpu.CompilerParams(dimension_semantics=sem, vmem_limit_bytes=V7X_VMEM_LIMIT_BYTES)


def _dot(a, b, dims=None, precision=None):
    if dims is None:
        dims = (((a.ndim - 1,), (0,)), ((), ()))
    return lax.dot_general(a, b, dims, precision=precision, preferred_element_type=F32)


def _dot_nt(a, b, precision=None):
    return _dot(a, b, (((1,), (1,)), ((), ())), precision)


def _dot_tn(a, b, precision=None):
    return _dot(a, b, (((0,), (0,)), ((), ())), precision)


def _sigmoid(x):
    return 1.0 / (1.0 + jnp.exp(-x))


def _silu(x):
    return x * _sigmoid(x)


def _iota(shape, dim):
    return lax.broadcasted_iota(jnp.int32, shape, dim)


def _mm_kernel(x_ref, w_ref, o_ref, wbf_ref):
    @pl.when(pl.program_id(1) == 0)
    def _():
        wbf_ref[...] = w_ref[...].astype(BF16)

    o_ref[...] = _dot(x_ref[...].astype(BF16), wbf_ref[...]).astype(o_ref.dtype)


def _mm_rms_kernel(x_ref, g_ref, w_ref, o_ref, wbf_ref):
    @pl.when(pl.program_id(1) == 0)
    def _():
        wbf_ref[...] = w_ref[...].astype(BF16)

    x = x_ref[...].astype(F32)
    xn = x * lax.rsqrt(jnp.mean(x * x, -1, keepdims=True) + RMS_EPS) * g_ref[...]
    o_ref[...] = _dot(xn.astype(BF16), wbf_ref[...]).astype(o_ref.dtype)


def _mm_tiles(M, K, N):
    tm = 1024 if M % 1024 == 0 else M
    tn = N
    for cand in (1024, 512, 256, 128):
        if N % cand == 0 and K * cand * 4 <= 8 * 1024 * 1024:
            tn = cand
            break
    return tm, tn


def _mm(x, w, out_dtype, rms_gain=None, x_col=0):
    M = x.shape[0]
    K, N = w.shape
    tm, tn = _mm_tiles(M, K, N)
    x_spec = pl.BlockSpec((tm, K), lambda n, m: (m, x_col))
    w_spec = pl.BlockSpec((K, tn), lambda n, m: (0, n))
    if rms_gain is None:
        kern, in_specs, args = _mm_kernel, [x_spec, w_spec], (x, w)
    else:
        g_spec = pl.BlockSpec((1, K), lambda n, m: (0, 0))
        kern, in_specs, args = _mm_rms_kernel, [x_spec, g_spec, w_spec], (x, rms_gain[None, :], w)
    return pl.pallas_call(
        kern,
        grid=(N // tn, M // tm),
        in_specs=in_specs,
        out_specs=pl.BlockSpec((tm, tn), lambda n, m: (m, n)),
        out_shape=jax.ShapeDtypeStruct((M, N), out_dtype),
        scratch_shapes=[pltpu.VMEM((K, tn), BF16)],
        compiler_params=_params("parallel", "arbitrary"),
        name="proj_matmul",
    )(*args)


def _pad_cols(w, mult=V7X_LANES):
    pad = (-w.shape[-1]) % mult
    return jnp.pad(w, ((0, 0), (0, pad))) if pad else w


def _ada_kernel(c_ref, w_ref, b_ref, o_ref):
    c = _silu(c_ref[...]).astype(BF16)
    o_ref[0] = _dot(c, w_ref[0].astype(BF16)) + b_ref[0]


def _ada_mod(c, ada_w, ada_b):
    depth, D, N = ada_w.shape
    B = c.shape[0]
    rows = 8
    c_pad = jnp.pad(c, ((0, rows - B), (0, 0)))
    tn = 1024
    out = pl.pallas_call(
        _ada_kernel,
        grid=(depth, N // tn),
        in_specs=[pl.BlockSpec((rows, D), lambda i, n: (0, 0)),
                  pl.BlockSpec((1, D, tn), lambda i, n: (i, 0, n)),
                  pl.BlockSpec((1, 1, tn), lambda i, n: (i, 0, n))],
        out_specs=pl.BlockSpec((1, rows, tn), lambda i, n: (i, 0, n)),
        out_shape=jax.ShapeDtypeStruct((depth, rows, N), F32),
        compiler_params=_params("parallel", "parallel"),
        name="ada_mod",
    )(c_pad, ada_w, ada_b.reshape(depth, 1, N))
    return out[:, :B]


ROW_TILE = 256


def _modulate_kernel(x_ref, sc_ref, sh_ref, u_ref):
    u_ref[...] = (x_ref[...] * (1.0 + sc_ref[0]) + sh_ref[0]).astype(u_ref.dtype)


def _row_specs(D, S, tr):
    vec = pl.BlockSpec((1, 1, D), lambda i: ((i * tr) // S, 0, 0))
    row = pl.BlockSpec((tr, D), lambda i: (i, 0))
    par = pl.BlockSpec((1, D), lambda i: (0, 0))
    return vec, row, par


def _modulate(x, sc, sh, S):
    T, D = x.shape
    tr = ROW_TILE
    vec, row, _ = _row_specs(D, S, tr)
    return pl.pallas_call(
        _modulate_kernel, grid=(T // tr,), in_specs=[row, vec, vec], out_specs=row,
        out_shape=jax.ShapeDtypeStruct((T, D), BF16),
        compiler_params=_params("parallel"), name="modulate",
    )(x, sc[:, None, :], sh[:, None, :])


def _deepnorm(alpha, x, h, gate, g, b):
    y = alpha * x + (1.0 + gate) * h
    mu = jnp.mean(y, -1, keepdims=True)
    yc = y - mu
    var = jnp.mean(yc * yc, -1, keepdims=True)
    return yc * lax.rsqrt(var + LN_EPS) * g + b


def _ln_mod_kernel(alpha, x_ref, h_ref, gate_ref, g_ref, b_ref, sc_ref, sh_ref, xo_ref, u_ref):
    xn = _deepnorm(alpha, x_ref[...], h_ref[...].astype(F32), gate_ref[0], g_ref[...], b_ref[...])
    xo_ref[...] = xn
    u_ref[...] = (xn * (1.0 + sc_ref[0]) + sh_ref[0]).astype(u_ref.dtype)


def _ln_mod(x, h, gate, ln_g, ln_b, sc, sh, S, alpha):
    T, D = x.shape
    tr = ROW_TILE
    vec, row, par = _row_specs(D, S, tr)
    return pl.pallas_call(
        functools.partial(_ln_mod_kernel, alpha), grid=(T // tr,),
        in_specs=[row, row, vec, par, par, vec, vec], out_specs=[row, row],
        out_shape=[jax.ShapeDtypeStruct((T, D), F32), jax.ShapeDtypeStruct((T, D), BF16)],
        compiler_params=_params("parallel"), name="deepnorm_ln",
    )(x, h, gate[:, None, :], ln_g[None, :], ln_b[None, :], sc[:, None, :], sh[:, None, :])


def _ln_router_kernel(alpha, x_ref, h_ref, gate_ref, g_ref, b_ref, sc_ref, sh_ref, rw_ref, rb_ref,
                      xo_ref, u_ref, idx_ref, wgt_ref):
    xn = _deepnorm(alpha, x_ref[...], h_ref[...].astype(F32), gate_ref[0], g_ref[...], b_ref[...])
    xo_ref[...] = xn
    u = xn * (1.0 + sc_ref[0]) + sh_ref[0]
    u_ref[...] = u.astype(u_ref.dtype)
    logits = _dot(u, rw_ref[...], precision=HIGHEST) + rb_ref[...]
    lane = _iota(logits.shape, 1)
    vals = jnp.where(lane < N_EXPERTS, logits, NEG_INF)
    top_v, top_i = [], []
    for _ in range(TOP_K):
        m = jnp.max(vals, -1, keepdims=True)
        i = jnp.min(jnp.where(vals == m, lane, V7X_LANES), -1, keepdims=True)
        top_v.append(m)
        top_i.append(i)
        vals = jnp.where(lane == i, NEG_INF, vals)
    exps = [jnp.exp(v - top_v[0]) for v in top_v]
    denom = functools.reduce(lambda a, b: a + b, exps)
    idx = jnp.zeros(logits.shape, jnp.int32)
    wgt = jnp.zeros(logits.shape, F32)
    for k in range(TOP_K):
        idx = jnp.where(lane == k, top_i[k], idx)
        wgt = jnp.where(lane == k, exps[k] / denom, wgt)
    idx_ref[...] = idx
    wgt_ref[...] = wgt


def _ln_router(x, h, gate, ln_g, ln_b, sc, sh, router_w, router_b, S, alpha):
    T, D = x.shape
    tr = ROW_TILE
    vec, row, par = _row_specs(D, S, tr)
    lane_row = pl.BlockSpec((tr, V7X_LANES), lambda i: (i, 0))
    rw = _pad_cols(router_w)
    rb = _pad_cols(router_b[None, :])
    xo, u, idx, wgt = pl.pallas_call(
        functools.partial(_ln_router_kernel, alpha), grid=(T // tr,),
        in_specs=[row, row, vec, par, par, vec, vec,
                  pl.BlockSpec((D, V7X_LANES), lambda i: (0, 0)), pl.BlockSpec((1, V7X_LANES), lambda i: (0, 0))],
        out_specs=[row, row, lane_row, lane_row],
        out_shape=[jax.ShapeDtypeStruct((T, D), F32), jax.ShapeDtypeStruct((T, D), BF16),
                   jax.ShapeDtypeStruct((T, V7X_LANES), jnp.int32), jax.ShapeDtypeStruct((T, V7X_LANES), F32)],
        compiler_params=_params("parallel"), name="deepnorm_ln_router",
    )(x, h, gate[:, None, :], ln_g[None, :], ln_b[None, :], sc[:, None, :], sh[:, None, :], rw, rb)
    return xo, u, idx[:, :TOP_K], wgt[:, :TOP_K]


MOE_ROW_TILE = 512


def _expert_changed(te_ref):
    t = pl.program_id(0)
    return jnp.logical_or(t == 0, te_ref[t] != te_ref[jnp.maximum(t - 1, 0)])


def _moe_gu_kernel(te_ref, tv_ref, x_ref, w_ref, b_ref, h_ref, wbf_ref):
    @pl.when(_expert_changed(te_ref))
    def _():
        wbf_ref[...] = w_ref[0].astype(BF16)

    @pl.when(tv_ref[pl.program_id(0)] > 0)
    def _():
        gu = _dot(x_ref[...], wbf_ref[...]) + b_ref[0]
        gl = jnp.minimum(gu[:, :EXPERT_FF], SWIGLU_LIMIT)
        up = jnp.clip(gu[:, EXPERT_FF:], -SWIGLU_LIMIT, SWIGLU_LIMIT)
        h_ref[...] = ((up + 1.0) * gl * _sigmoid(gl * SWIGLU_ALPHA)).astype(h_ref.dtype)


def _moe_down_kernel(te_ref, tv_ref, h_ref, w_ref, b_ref, rw_ref, y_ref, wbf_ref):
    @pl.when(_expert_changed(te_ref))
    def _():
        wbf_ref[...] = w_ref[0].astype(BF16)

    @pl.when(tv_ref[pl.program_id(0)] > 0)
    def _():
        y_ref[...] = (_dot(h_ref[...], wbf_ref[...]) + b_ref[0]) * rw_ref[...]


def _route_metadata(top_idx, top_w, tm):
    T = top_idx.shape[0]
    P = T * TOP_K
    n_tiles = (P + N_EXPERTS * (tm - 1)) // tm
    e_flat = top_idx.reshape(P)
    onehot = (e_flat[:, None] == jnp.arange(N_EXPERTS)[None, :]).astype(jnp.int32)
    csum = jnp.cumsum(onehot, axis=0)
    counts = csum[-1]
    rank = jnp.sum(csum * onehot, axis=1) - 1
    padded = ((counts + tm - 1) // tm) * tm
    ends_p = jnp.cumsum(padded)
    starts_p = ends_p - padded
    dest = starts_p[e_flat] + rank
    rows = n_tiles * tm
    row_token = jnp.zeros((rows,), jnp.int32).at[dest].set(jnp.arange(P, dtype=jnp.int32) // TOP_K)
    row_w = jnp.zeros((rows,), F32).at[dest].set(top_w.reshape(P))
    tile_start = jnp.arange(n_tiles, dtype=jnp.int32) * tm
    tile_valid = (tile_start < ends_p[-1]).astype(jnp.int32)
    tile_expert = jnp.sum((tile_start[:, None] >= ends_p[None, :]).astype(jnp.int32), axis=1)
    last_expert = jnp.max(jnp.where(counts > 0, jnp.arange(N_EXPERTS), 0))
    tile_expert = jnp.where(tile_valid > 0, tile_expert, last_expert).astype(jnp.int32)
    return row_token, row_w, dest.reshape(T, TOP_K), tile_expert, tile_valid


def _moe_ffn(u, top_idx, top_w, w_gu, b_gu, w_down, b_down):
    T, D = u.shape
    tm = MOE_ROW_TILE
    row_token, row_w, dest, tile_expert, tile_valid = _route_metadata(top_idx, top_w, tm)
    rows = row_token.shape[0]
    n_tiles = rows // tm
    x_sorted = u[row_token]
    ff2 = 2 * EXPERT_FF
    h = pl.pallas_call(
        _moe_gu_kernel,
        grid_spec=pltpu.PrefetchScalarGridSpec(
            num_scalar_prefetch=2, grid=(n_tiles,),
            in_specs=[pl.BlockSpec((tm, D), lambda t, te, tv: (t, 0)),
                      pl.BlockSpec((1, D, ff2), lambda t, te, tv: (te[t], 0, 0)),
                      pl.BlockSpec((1, 1, ff2), lambda t, te, tv: (te[t], 0, 0))],
            out_specs=pl.BlockSpec((tm, EXPERT_FF), lambda t, te, tv: (t, 0)),
            scratch_shapes=[pltpu.VMEM((D, ff2), BF16)]),
        out_shape=jax.ShapeDtypeStruct((rows, EXPERT_FF), BF16),
        compiler_params=_params("arbitrary"), name="moe_gate_up",
    )(tile_expert, tile_valid, x_sorted, w_gu, b_gu[:, None, :])
    y = pl.pallas_call(
        _moe_down_kernel,
        grid_spec=pltpu.PrefetchScalarGridSpec(
            num_scalar_prefetch=2, grid=(n_tiles,),
            in_specs=[pl.BlockSpec((tm, EXPERT_FF), lambda t, te, tv: (t, 0)),
                      pl.BlockSpec((1, EXPERT_FF, D), lambda t, te, tv: (te[t], 0, 0)),
                      pl.BlockSpec((1, 1, D), lambda t, te, tv: (te[t], 0, 0)),
                      pl.BlockSpec((tm, 1), lambda t, te, tv: (t, 0))],
            out_specs=pl.BlockSpec((tm, D), lambda t, te, tv: (t, 0)),
            scratch_shapes=[pltpu.VMEM((EXPERT_FF, D), BF16)]),
        out_shape=jax.ShapeDtypeStruct((rows, D), F32),
        compiler_params=_params("arbitrary"), name="moe_down",
    )(tile_expert, tile_valid, h, w_down, b_down[:, None, :], row_w[:, None])
    return jnp.sum(y[dest], axis=1)


ATTN_TILE = 256


def _softmax_block(s, m, l, acc, v):
    m_new = jnp.maximum(m, jnp.max(s, -1, keepdims=True))
    a = jnp.exp(m - m_new)
    p = jnp.exp(s - m_new)
    return m_new, a * l + jnp.sum(p, -1, keepdims=True), a * acc + _dot(p.astype(BF16), v)


def _mla_attn_kernel(scale, qn_ref, qr_ref, kv_ref, kr_ref, o_ref):
    qi = pl.program_id(2)
    t = ATTN_TILE
    row = _iota((t, t), 0)
    col = _iota((t, t), 1)
    for hh in range(2):
        qn = qn_ref[0, :, hh * MLA_NOPE:(hh + 1) * MLA_NOPE]
        qr = qr_ref[0, :, hh * MLA_ROPE:(hh + 1) * MLA_ROPE]
        c0 = hh * (MLA_NOPE + MLA_V)

        def scores(start):
            kn = kv_ref[0, pl.ds(start, t), c0:c0 + MLA_NOPE]
            kr = kr_ref[0, pl.ds(start, t), :]
            return (_dot_nt(qn, kn) + _dot_nt(qr, kr)) * scale

        def values(start):
            return kv_ref[0, pl.ds(start, t), c0 + MLA_NOPE:c0 + MLA_NOPE + MLA_V]

        d0 = pl.multiple_of(qi * t, t)
        s = jnp.where(col <= row, scores(d0), NEG_INF)
        m = jnp.max(s, -1, keepdims=True)
        p = jnp.exp(s - m)
        carry = (m, jnp.sum(p, -1, keepdims=True), _dot(p.astype(BF16), values(d0)))

        def body(j, carry):
            start = pl.multiple_of(j * t, t)
            return _softmax_block(scores(start), *carry, values(start))

        m, l, acc = lax.fori_loop(0, qi, body, carry)
        o_ref[0, :, hh * MLA_V:(hh + 1) * MLA_V] = (acc / l).astype(o_ref.dtype)


def _rope(x, cos, sin):
    half = x.shape[-1] // 2
    x1, x2 = x[..., :half], x[..., half:]
    return jnp.concatenate([x1 * cos - x2 * sin, x2 * cos + x1 * sin], -1)


def _mla_mixer(u, B, S, w_in, q_norm, kv_norm, w_qb, w_kvb, w_o):
    H = MLA_HEADS
    lat = _mm(u, w_in[:, :MLA_Q_LORA + MLA_KV_LORA], F32)
    k_rope = _mm(u, _pad_cols(w_in[:, MLA_Q_LORA + MLA_KV_LORA:]), F32)[:, :MLA_ROPE]
    wq = w_qb.reshape(MLA_Q_LORA, H, MLA_NOPE + MLA_ROPE)
    wq = jnp.concatenate([wq[:, :, :MLA_NOPE].reshape(MLA_Q_LORA, H * MLA_NOPE),
                          wq[:, :, MLA_NOPE:].reshape(MLA_Q_LORA, H * MLA_ROPE)], -1)
    q = _mm(lat, wq, F32, rms_gain=q_norm, x_col=0)
    kv = _mm(lat, w_kvb, BF16, rms_gain=kv_norm, x_col=1)
    inv_freq = ROPE_THETA ** (-jnp.arange(MLA_ROPE // 2, dtype=F32) / (MLA_ROPE // 2))
    ang = jnp.arange(S, dtype=F32)[:, None] * inv_freq[None, :]
    cos, sin = jnp.cos(ang), jnp.sin(ang)
    q_nope = q[:, :H * MLA_NOPE].astype(BF16).reshape(B, S, H * MLA_NOPE)
    q_rope = _rope(q[:, H * MLA_NOPE:].reshape(B, S, H, MLA_ROPE), cos[:, None, :], sin[:, None, :])
    q_rope = q_rope.astype(BF16).reshape(B, S, H * MLA_ROPE)
    k_rope = _rope(k_rope.reshape(B, S, MLA_ROPE), cos, sin).astype(BF16)
    t = ATTN_TILE
    o = pl.pallas_call(
        functools.partial(_mla_attn_kernel, (MLA_NOPE + MLA_ROPE) ** -0.5),
        grid=(B, H // 2, S // t),
        in_specs=[pl.BlockSpec((1, t, 2 * MLA_NOPE), lambda b, h, i: (b, i, h)),
                  pl.BlockSpec((1, t, 2 * MLA_ROPE), lambda b, h, i: (b, i, h)),
                  pl.BlockSpec((1, S, 2 * (MLA_NOPE + MLA_V)), lambda b, h, i: (b, 0, h)),
                  pl.BlockSpec((1, S, MLA_ROPE), lambda b, h, i: (b, 0, 0))],
        out_specs=pl.BlockSpec((1, t, 2 * MLA_V), lambda b, h, i: (b, i, h)),
        out_shape=jax.ShapeDtypeStruct((B, S, H * MLA_V), BF16),
        compiler_params=_params("parallel", "parallel", "arbitrary"), name="mla_attention",
    )(q_nope, q_rope, kv.reshape(B, S, -1), k_rope)
    return _mm(o.reshape(B * S, H * MLA_V), w_o, F32)


def _t5_bucket(dist):
    n = jnp.maximum(dist, 0)
    max_exact = REL_BUCKETS // 2
    large = max_exact + (jnp.log(jnp.maximum(n, 1).astype(F32) / max_exact)
                         / math.log(REL_MAX_DIST / max_exact) * (REL_BUCKETS - max_exact)).astype(jnp.int32)
    large = jnp.minimum(large, REL_BUCKETS - 1)
    return jnp.where(n < max_exact, n, large)


def _moba_kernel(scale, n_sel, q_ref, k_ref, v_ref, bias_ref, o_ref, kbf_ref, vbf_ref, kmean_ref):
    qi = pl.program_id(2)
    L = MOBA_BLOCK
    S = k_ref.shape[1]
    n_blk = S // L

    @pl.when(qi == 0)
    def _():
        k = k_ref[0].astype(F32)
        kbf_ref[...] = k.astype(BF16)
        vbf_ref[...] = v_ref[0].astype(BF16)
        kmean_ref[...] = jnp.zeros(kmean_ref.shape, F32)
        kmean_ref[0:n_blk, :] = jnp.mean(k.reshape(n_blk, L, k.shape[-1]), axis=1)

    q = q_ref[0].astype(F32)
    qb = q.astype(BF16)
    lane = _iota((L, V7X_LANES), 1)
    gate = jnp.where(lane < qi, _dot_nt(q, kmean_ref[...], precision=HIGHEST), NEG_INF)
    row = _iota((L, L), 0)
    col = _iota((L, L), 1)
    d0 = pl.multiple_of(qi * L, L)
    s = _dot_nt(qb, kbf_ref[pl.ds(d0, L), :]) * scale + bias_ref[0, 0]
    s = jnp.where(col <= row, s, NEG_INF)
    m = jnp.max(s, -1, keepdims=True)
    p = jnp.exp(s - m)
    carry = (m, jnp.sum(p, -1, keepdims=True), _dot(p.astype(BF16), vbf_ref[pl.ds(d0, L), :]))

    def body(j, carry):
        gj = jnp.sum(jnp.where(lane == j, gate, 0.0), -1, keepdims=True)
        beats = jnp.logical_or(gate > gj, jnp.logical_and(gate == gj, lane < j))
        sel = jnp.sum(jnp.where(beats, 1.0, 0.0), -1, keepdims=True) < n_sel
        start = pl.multiple_of(j * L, L)
        bias = bias_ref[0, jnp.minimum(qi - j, 2)]
        s = _dot_nt(qb, kbf_ref[pl.ds(start, L), :]) * scale + bias
        s = jnp.where(sel, s, NEG_INF)
        return _softmax_block(s, *carry, vbf_ref[pl.ds(start, L), :])

    m, l, acc = lax.fori_loop(0, qi, body, carry)
    o_ref[0] = (acc / l).astype(o_ref.dtype)


def _moba_mixer(u, B, S, w_in, w_o, rel_bias):
    H, Dh, L = MOBA_HEADS, u.shape[1] // MOBA_HEADS, MOBA_BLOCK
    assert S % L == 0 and Dh == V7X_LANES and REL_MAX_DIST <= L
    n_blk = S // L
    n_sel = max(min(MOBA_TOPK, n_blk - 1), 1)
    qkv = _mm(u, w_in, BF16).reshape(B, S, 3 * H * Dh)
    qk = jnp.arange(L)[:, None] - jnp.arange(L)[None, :]
    dist = jnp.stack([qk, qk + L, qk + 2 * L])
    bias = jnp.transpose(rel_bias[_t5_bucket(dist)], (3, 0, 1, 2))
    o = pl.pallas_call(
        functools.partial(_moba_kernel, Dh ** -0.5, n_sel),
        grid=(B, H, n_blk),
        in_specs=[pl.BlockSpec((1, L, Dh), lambda b, h, i: (b, i, h)),
                  pl.BlockSpec((1, S, Dh), lambda b, h, i: (b, 0, H + h)),
                  pl.BlockSpec((1, S, Dh), lambda b, h, i: (b, 0, 2 * H + h)),
                  pl.BlockSpec((1, 3, L, L), lambda b, h, i: (h, 0, 0, 0))],
        out_specs=pl.BlockSpec((1, L, Dh), lambda b, h, i: (b, i, h)),
        out_shape=jax.ShapeDtypeStruct((B, S, H * Dh), BF16),
        scratch_shapes=[pltpu.VMEM((S, Dh), BF16), pltpu.VMEM((S, Dh), BF16), pltpu.VMEM((V7X_LANES, Dh), F32)],
        compiler_params=_params("parallel", "parallel", "arbitrary"), name="moba_attention",
    )(qkv, qkv, qkv, bias)
    return _mm(o.reshape(B * S, H * Dh), w_o, F32)


GDN_HEAD_GROUP = 8


def _l2norm(x):
    return x * lax.rsqrt(jnp.sum(x * x, -1, keepdims=True) + L2_EPS)


def _unit_lower_inverse(a_low, block):
    C = a_low.shape[0]
    r = _iota((C, C), 0)
    c = _iota((C, C), 1)
    eye = jnp.where(r == c, 1.0, 0.0)
    same = (r // block) == (c // block)
    a_d = jnp.where(same, a_low, 0.0)
    a_off = a_low - a_d
    inv_d = eye - a_d
    pw = a_d
    k = 2
    while k < block:
        pw = _dot(pw, pw, precision=HIGHEST)
        inv_d = _dot(inv_d, eye + pw, precision=HIGHEST)
        k *= 2
    n = _dot(inv_d, a_off, precision=HIGHEST)
    inv_n = eye - n
    pw = n
    k = 2
    while k < C // block:
        pw = _dot(pw, pw, precision=HIGHEST)
        inv_n = _dot(inv_n, eye + pw, precision=HIGHEST)
        k *= 2
    return _dot(inv_n, inv_d, precision=HIGHEST)


def _gdn_kernel(q_ref, k_ref, v_ref, z_ref, gc_ref, gct_ref, beta_ref, ng_ref, o_ref, state_ref):
    G, C, DK, DV = GDN_HEAD_GROUP, GDN_CHUNK, GDN_DK, GDN_DV
    rep = GDN_V_HEADS // GDN_K_HEADS

    @pl.when(pl.program_id(2) == 0)
    def _():
        state_ref[...] = jnp.zeros(state_ref.shape, F32)

    r = _iota((C, C), 0)
    c = _iota((C, C), 1)
    tri = c <= r
    strict = c < r
    gc = gc_ref[0, 0]
    gct = gct_ref[0, 0, 0]
    beta = beta_ref[0, 0]
    for kh in range(G // rep):
        q = _l2norm(q_ref[0, :, kh * DK:(kh + 1) * DK].astype(F32)) * (DK ** -0.5)
        k = _l2norm(k_ref[0, :, kh * DK:(kh + 1) * DK].astype(F32))
        qk = _dot_nt(q, k)
        kk = _dot_nt(k, k)
        for rr in range(rep):
            h = kh * rep + rr
            g_col = gc[:, h:h + 1]
            g_row = gct[h:h + 1, :]
            b_col = beta[:, h:h + 1]
            decay = jnp.where(tri, jnp.exp(jnp.where(tri, g_col - g_row, 0.0)), 0.0)
            a_low = jnp.where(strict, kk * b_col * decay, 0.0)
            t_inv = _unit_lower_inverse(a_low, 16)
            e_g = jnp.exp(g_col)
            v = v_ref[0, :, h * DV:(h + 1) * DV].astype(F32)
            u_val = _dot(t_inv, v * b_col)
            w_dec = _dot(t_inv, k * (b_col * e_g))
            attn = jnp.where(tri, qk * decay, 0.0)
            state = state_ref[h]
            v_new = u_val - _dot(w_dec, state)
            o = _dot(q * e_g, state) + _dot(attn, v_new)
            g_last = g_col[C - 1:C, :]
            k_tail = k * jnp.exp(g_last - g_col)
            state_ref[h] = state * jnp.exp(g_last) + _dot_tn(k_tail, v_new)
            o = o * lax.rsqrt(jnp.mean(o * o, -1, keepdims=True) + RMS_EPS) * ng_ref[...]
            o = o * _silu(z_ref[0, :, h * DV:(h + 1) * DV].astype(F32))
            o_ref[0, :, h * DV:(h + 1) * DV] = o.astype(o_ref.dtype)


def _gdn_mixer(u, B, S, w_in, conv_w, a_log, dt_bias, norm_g, w_o):
    HK, HV, DK, DV, C, G = GDN_K_HEADS, GDN_V_HEADS, GDN_DK, GDN_DV, GDN_CHUNK, GDN_HEAD_GROUP
    qk_dim, v_dim = HK * DK, HV * DV
    n_main = 2 * qk_dim + 2 * v_dim
    proj = _mm(u, w_in[:, :n_main], BF16).reshape(B, S, n_main)
    ba = _mm(u, _pad_cols(w_in[:, n_main:]), F32).reshape(B, S, -1)
    n_conv = 2 * qk_dim + v_dim
    x = jnp.pad(proj[:, :, :n_conv].astype(F32), ((0, 0), (GDN_CONV - 1, 0), (0, 0)))
    qkv = sum(x[:, i:i + S, :] * conv_w[i][None, None, :] for i in range(GDN_CONV))
    qkv = (qkv * jax.nn.sigmoid(qkv)).astype(BF16)
    beta = jax.nn.sigmoid(ba[:, :, :HV])
    g = -jnp.exp(a_log) * jax.nn.softplus(ba[:, :, HV:2 * HV] + dt_bias)
    N = S // C
    gc = jnp.cumsum(g.reshape(B, N, C, HV), axis=2)
    HG = HV // G
    gc_g = gc.reshape(B, N, C, HG, G).transpose(0, 3, 1, 2, 4).reshape(B, HG, S, G)
    gct_g = gc.reshape(B, N, C, HG, G).transpose(0, 3, 1, 4, 2)
    beta_g = beta.reshape(B, S, HG, G).transpose(0, 2, 1, 3)
    kw = (G // (HV // HK)) * DK
    vw = G * DV
    o = pl.pallas_call(
        _gdn_kernel,
        grid=(B, HG, N),
        in_specs=[pl.BlockSpec((1, C, kw), lambda b, h, n: (b, n, h)),
                  pl.BlockSpec((1, C, kw), lambda b, h, n: (b, n, qk_dim // kw + h)),
                  pl.BlockSpec((1, C, vw), lambda b, h, n: (b, n, 2 * qk_dim // vw + h)),
                  pl.BlockSpec((1, C, vw), lambda b, h, n: (b, n, n_conv // vw + h)),
                  pl.BlockSpec((1, 1, C, G), lambda b, h, n: (b, h, n, 0)),
                  pl.BlockSpec((1, 1, 1, G, C), lambda b, h, n: (b, h, n, 0, 0)),
                  pl.BlockSpec((1, 1, C, G), lambda b, h, n: (b, h, n, 0)),
                  pl.BlockSpec((1, DV), lambda b, h, n: (0, 0))],
        out_specs=pl.BlockSpec((1, C, vw), lambda b, h, n: (b, n, h)),
        out_shape=jax.ShapeDtypeStruct((B, S, v_dim), BF16),
        scratch_shapes=[pltpu.VMEM((G, DK, DV), F32)],
        compiler_params=_params("parallel", "parallel", "arbitrary"), name="gdn_chunked",
    )(qkv, qkv, qkv, proj, gc_g, gct_g, beta_g, norm_g[None, :])
    return _mm(o.reshape(B * S, v_dim), w_o, F32)


def _gla_kernel(scale, q_ref, k_ref, v_ref, og_ref, gk_ref, wgk_ref, bgk_ref, ng_ref, o_ref, state_ref):
    C = GLA_CHUNK

    @pl.when(pl.program_id(2) == 0)
    def _():
        state_ref[...] = jnp.zeros(state_ref.shape, F32)

    x = _dot(gk_ref[0].astype(BF16), wgk_ref[...].astype(BF16)) + bgk_ref[...]
    log_alpha = (jnp.minimum(x, 0.0) - jnp.log(1.0 + jnp.exp(-jnp.abs(x)))) / GLA_GATE_NORMALIZER
    r = _iota((C, C), 0)
    c = _iota((C, C), 1)
    causal = c <= r
    b = _dot(jnp.where(causal, 1.0, 0.0), log_alpha, precision=HIGHEST)
    b_last = b[C - 1:C, :]
    q = q_ref[0].astype(F32) * scale
    k = k_ref[0].astype(F32)
    v = v_ref[0].astype(BF16)
    q_dec = (q * jnp.exp(b)).astype(BF16)
    k_inv = (k * jnp.exp(-b)).astype(BF16)
    k_tail = (k * jnp.exp(b_last - b)).astype(BF16)
    attn = jnp.where(causal, _dot_nt(q_dec, k_inv), 0.0)
    state_t = state_ref[...]
    o = _dot(attn.astype(BF16), v) + _dot_nt(q_dec, state_t.astype(BF16))
    state_ref[...] = state_t * jnp.exp(b_last) + _dot_tn(v, k_tail)
    o = o * lax.rsqrt(jnp.mean(o * o, -1, keepdims=True) + RMS_EPS) * ng_ref[...]
    o_ref[0] = (o * _silu(og_ref[0].astype(F32))).astype(o_ref.dtype)


def _gla_mixer(u, B, S, w_in, w_gk, b_gk, norm_g, w_o):
    D = u.shape[1]
    H, C = GLA_HEADS, GLA_CHUNK
    key_dim, val_dim = D // 2, D
    dk, dv = key_dim // H, val_dim // H
    n_main = 2 * key_dim + 2 * val_dim
    proj = _mm(u, w_in[:, :n_main], BF16).reshape(B, S, n_main)
    gk = _mm(u, _pad_cols(w_in[:, n_main:]), F32).reshape(B, S, -1)
    wgk = jnp.pad(w_gk, ((0, gk.shape[-1] - GLA_GATE_RANK), (0, 0)))
    o = pl.pallas_call(
        functools.partial(_gla_kernel, dk ** -0.5),
        grid=(B, H, S // C),
        in_specs=[pl.BlockSpec((1, C, dk), lambda b, h, n: (b, n, h)),
                  pl.BlockSpec((1, C, dk), lambda b, h, n: (b, n, H + h)),
                  pl.BlockSpec((1, C, dv), lambda b, h, n: (b, n, 2 * key_dim // dv + h)),
                  pl.BlockSpec((1, C, dv), lambda b, h, n: (b, n, (2 * key_dim + val_dim) // dv + h)),
                  pl.BlockSpec((1, C, gk.shape[-1]), lambda b, h, n: (b, n, 0)),
                  pl.BlockSpec((gk.shape[-1], dk), lambda b, h, n: (0, h)),
                  pl.BlockSpec((1, dk), lambda b, h, n: (0, h)),
                  pl.BlockSpec((1, dv), lambda b, h, n: (0, 0))],
        out_specs=pl.BlockSpec((1, C, dv), lambda b, h, n: (b, n, h)),
        out_shape=jax.ShapeDtypeStruct((B, S, val_dim), BF16),
        scratch_shapes=[pltpu.VMEM((dv, dk), F32)],
        compiler_params=_params("parallel", "parallel", "arbitrary"), name="gla_chunked",
    )(proj, proj, proj, proj, gk, wgk, b_gk[None, :], norm_g[None, :])
    return _mm(o.reshape(B * S, val_dim), w_o, F32)


def kernel(x, c, rel_bias, mla_w_in, mla_q_norm, mla_kv_norm, mla_w_qb, mla_w_kvb, mla_w_o, gdn_w_in, gdn_conv_w, gdn_a_log, gdn_dt_bias, gdn_norm, gdn_w_o, gla_w_in, gla_w_gk, gla_b_gk, gla_norm, gla_w_o, moba_w_in, moba_w_o, ada_w, ada_b, ln_g, ln_b, router_w, router_b, moe_w_gu, moe_b_gu, moe_w_down, moe_b_down):
    B, S, D = x.shape
    depth = ada_w.shape[0]
    alpha = (2 * depth) ** 0.25
    mod = _ada_mod(c, ada_w, ada_b)
    sh_a, sc_a, g_a, sh_f, sc_f, g_f = (mod[:, :, k * D:(k + 1) * D] for k in range(6))
    xt = x.reshape(B * S, D)
    u = _modulate(xt, sc_a[0], sh_a[0], S)
    for i in range(depth):
        m, j = i % N_MIXERS, i // N_MIXERS
        if m == 0:
            h = _mla_mixer(u, B, S, mla_w_in[j], mla_q_norm[j], mla_kv_norm[j], mla_w_qb[j], mla_w_kvb[j], mla_w_o[j])
        elif m == 1:
            h = _gdn_mixer(u, B, S, gdn_w_in[j], gdn_conv_w[j], gdn_a_log[j], gdn_dt_bias[j], gdn_norm[j], gdn_w_o[j])
        elif m == 2:
            h = _gla_mixer(u, B, S, gla_w_in[j], gla_w_gk[j], gla_b_gk[j], gla_norm[j], gla_w_o[j])
        else:
            h = _moba_mixer(u, B, S, moba_w_in[j], moba_w_o[j], rel_bias)
        xt, u, top_idx, top_w = _ln_router(xt, h, g_a[i], ln_g[i, 0], ln_b[i, 0], sc_f[i], sh_f[i],
                                           router_w[i], router_b[i], S, alpha)
        f = _moe_ffn(u, top_idx, top_w, moe_w_gu[i], moe_b_gu[i], moe_w_down[i], moe_b_down[i])
        nxt = (i + 1) % depth
        xt, u = _ln_mod(xt, f, g_f[i], ln_g[i, 1], ln_b[i, 1], sc_a[nxt], sh_a[nxt], S, alpha)
    return xt.reshape(B, S, D)
```

```python
import functools
import math

import jax
import jax.numpy as jnp
from jax import lax
from jax.experimental import pallas as pl
from jax.experimental.pallas import tpu as pltpu

D_MODEL = 2048
N_MIXERS = 4
MLA_HEADS, MLA_Q_LORA, MLA_KV_LORA, MLA_NOPE, MLA_ROPE, MLA_V = 16, 512, 512, 128, 64, 128
ROPE_THETA = 10000.0
GDN_K_HEADS, GDN_V_HEADS, GDN_DK, GDN_DV, GDN_CONV, GDN_CHUNK = 16, 32, 128, 128, 4, 64
GLA_HEADS, GLA_GATE_RANK, GLA_GATE_NORMALIZER, GLA_CHUNK = 4, 16, 16.0, 64
MOBA_HEADS, MOBA_BLOCK, MOBA_TOPK = 16, 256, 3
REL_BUCKETS, REL_MAX_DIST = 32, 128
N_EXPERTS, TOP_K, EXPERT_FF = 32, 4, 768
SWIGLU_LIMIT, SWIGLU_ALPHA = 7.0, 1.702
LN_EPS, RMS_EPS, L2_EPS = 1e-5, 1e-6, 1e-6

V7X_LANES = 128
V7X_VMEM_LIMIT_BYTES = 56 * 1024 * 1024

F32 = jnp.float32
BF16 = jnp.bfloat16
HIGHEST = lax.Precision.HIGHEST
NEG_INF = float("-inf")


def _params(*sem):
    return pltpu.CompilerParams(dimension_semantics=sem, vmem_limit_bytes=V7X_VMEM_LIMIT_BYTES)


def _dot(a, b, dims=None, precision=None):
    if dims is None:
        dims = (((a.ndim - 1,), (0,)), ((), ()))
    return lax.dot_general(a, b, dims, precision=precision, preferred_element_type=F32)


def _dot_nt(a, b, precision=None):
    return _dot(a, b, (((1,), (1,)), ((), ())), precision)


def _dot_tn(a, b, precision=None):
    return _dot(a, b, (((0,), (0,)), ((), ())), precision)


def _sigmoid(x):
    return 1.0 / (1.0 + jnp.exp(-x))


def _silu(x):
    return x * _sigmoid(x)


def _iota(shape, dim):
    return lax.broadcasted_iota(jnp.int32, shape, dim)


def _mm_kernel(valid_cols, rms, x_ref, *refs):
    g_ref = refs[0] if rms else None
    w_ref, o_ref, wbf_ref = refs[-3:]

    @pl.when(pl.program_id(1) == 0)
    def _():
        w = w_ref[...]
        if valid_cols is not None:
            w = jnp.where(_iota(w.shape, 1) < valid_cols, w, 0.0)
        wbf_ref[...] = w.astype(BF16)

    x = x_ref[...]
    if rms:
        x = x.astype(F32)
        x = x * lax.rsqrt(jnp.mean(x * x, -1, keepdims=True) + RMS_EPS) * g_ref[...]
    o_ref[...] = _dot(x.astype(BF16), wbf_ref[...]).astype(o_ref.dtype)


def _mm_tiles(M, K, N):
    tm = 1024 if M % 1024 == 0 else M
    tn = N
    for cand in (1024, 512, 256, 128):
        if N % cand == 0 and K * cand * 4 <= 8 * 1024 * 1024:
            tn = cand
            break
    return tm, tn


def _mm(x, w, out_dtype, rms_gain=None, x_col=0, col0=0, n_cols=None):
    M = x.shape[0]
    K = w.shape[0]
    N = w.shape[1] if n_cols is None else n_cols
    tm, tn = _mm_tiles(M, K, N)
    assert col0 % tn == 0
    valid_cols = w.shape[1] - col0 if col0 + N > w.shape[1] else None
    assert valid_cols is None or N == tn
    c0 = col0 // tn
    x_spec = pl.BlockSpec((tm, K), lambda n, m: (m, x_col))
    w_spec = pl.BlockSpec((K, tn), lambda n, m: (0, c0 + n))
    if rms_gain is None:
        in_specs, args = [x_spec, w_spec], (x, w)
    else:
        in_specs, args = [x_spec, pl.BlockSpec((1, K), lambda n, m: (0, 0)), w_spec], (x, rms_gain[None, :], w)
    return pl.pallas_call(
        functools.partial(_mm_kernel, valid_cols, rms_gain is not None),
        grid=(N // tn, M // tm),
        in_specs=in_specs,
        out_specs=pl.BlockSpec((tm, tn), lambda n, m: (m, n)),
        out_shape=jax.ShapeDtypeStruct((M, N), out_dtype),
        scratch_shapes=[pltpu.VMEM((K, tn), BF16)],
        compiler_params=_params("parallel", "arbitrary"),
        name="proj_matmul",
    )(*args)


def _pad_cols(w, mult=V7X_LANES):
    pad = (-w.shape[-1]) % mult
    return jnp.pad(w, ((0, 0), (0, pad))) if pad else w


def _ada_kernel(c_ref, w_ref, b_ref, o_ref):
    c = _silu(c_ref[...]).astype(BF16)
    o_ref[0] = _dot(c, w_ref[0].astype(BF16)) + b_ref[0]


def _ada_mod(c, ada_w, ada_b):
    depth, D, N = ada_w.shape
    B = c.shape[0]
    rows = 8
    c_pad = jnp.pad(c, ((0, rows - B), (0, 0)))
    tn = 1024
    out = pl.pallas_call(
        _ada_kernel,
        grid=(depth, N // tn),
        in_specs=[pl.BlockSpec((rows, D), lambda i, n: (0, 0)),
                  pl.BlockSpec((1, D, tn), lambda i, n: (i, 0, n)),
                  pl.BlockSpec((1, 1, tn), lambda i, n: (i, 0, n))],
        out_specs=pl.BlockSpec((1, rows, tn), lambda i, n: (i, 0, n)),
        out_shape=jax.ShapeDtypeStruct((depth, rows, N), F32),
        compiler_params=_params("parallel", "parallel"),
        name="ada_mod",
    )(c_pad, ada_w, ada_b.reshape(depth, 1, N))
    return out[:, :B]


ROW_TILE = 256


def _modulate_kernel(x_ref, sc_ref, sh_ref, u_ref):
    u_ref[...] = (x_ref[...] * (1.0 + sc_ref[0]) + sh_ref[0]).astype(u_ref.dtype)


def _row_specs(D, S, tr):
    vec = pl.BlockSpec((1, 1, D), lambda i: ((i * tr) // S, 0, 0))
    row = pl.BlockSpec((tr, D), lambda i: (i, 0))
    par = pl.BlockSpec((1, D), lambda i: (0, 0))
    return vec, row, par


def _modulate(x, sc, sh, S):
    T, D = x.shape
    tr = ROW_TILE
    vec, row, _ = _row_specs(D, S, tr)
    return pl.pallas_call(
        _modulate_kernel, grid=(T // tr,), in_specs=[row, vec, vec], out_specs=row,
        out_shape=jax.ShapeDtypeStruct((T, D), BF16),
        compiler_params=_params("parallel"), name="modulate",
    )(x, sc[:, None, :], sh[:, None, :])


def _deepnorm(alpha, x, h, gate, g, b):
    y = alpha * x + (1.0 + gate) * h
    mu = jnp.mean(y, -1, keepdims=True)
    yc = y - mu
    var = jnp.mean(yc * yc, -1, keepdims=True)
    return yc * lax.rsqrt(var + LN_EPS) * g + b


ROW_PARTS = D_MODEL // V7X_LANES


def _store_row_tiled(ref, row0, value):
    n, d = value.shape
    w = d // ROW_PARTS
    for j in range(ROW_PARTS):
        ref[pl.ds(row0 * ROW_PARTS + j, n, stride=ROW_PARTS), :] = value[:, j * w:(j + 1) * w]


def _load_row_tiled(ref, row0, n):
    return jnp.concatenate([ref[pl.ds(row0 * ROW_PARTS + j, n, stride=ROW_PARTS), :] for j in range(ROW_PARTS)], -1)


def _ln_router_kernel(alpha, x_ref, h_ref, gate_ref, g_ref, b_ref, sc_ref, sh_ref, rw_ref, rb_ref,
                      xo_ref, u_ref, idx_ref, wgt_ref):
    xn = _deepnorm(alpha, x_ref[...], h_ref[...].astype(F32), gate_ref[0], g_ref[...], b_ref[...])
    xo_ref[...] = xn
    u = xn * (1.0 + sc_ref[0]) + sh_ref[0]
    _store_row_tiled(u_ref, 0, u)
    logits = _dot(u, rw_ref[...], precision=HIGHEST) + rb_ref[...]
    lane = _iota(logits.shape, 1)
    vals = jnp.where(lane < N_EXPERTS, logits, NEG_INF)
    top_v, top_i = [], []
    for _ in range(TOP_K):
        m = jnp.max(vals, -1, keepdims=True)
        i = jnp.min(jnp.where(vals == m, lane, V7X_LANES), -1, keepdims=True)
        top_v.append(m)
        top_i.append(i)
        vals = jnp.where(lane == i, NEG_INF, vals)
    exps = [jnp.exp(v - top_v[0]) for v in top_v]
    denom = functools.reduce(lambda a, b: a + b, exps)
    idx = jnp.zeros(logits.shape, jnp.int32)
    wgt = jnp.zeros(logits.shape, F32)
    for k in range(TOP_K):
        idx = jnp.where(lane == k, top_i[k], idx)
        wgt = jnp.where(lane == k, exps[k] / denom, wgt)
    idx_ref[...] = idx
    wgt_ref[...] = wgt


def _ln_router(x, h, gate, ln_g, ln_b, sc, sh, router_w, router_b, S, alpha):
    T, D = x.shape
    tr = ROW_TILE
    vec, row, par = _row_specs(D, S, tr)
    lane_row = pl.BlockSpec((tr, V7X_LANES), lambda i: (i, 0))
    tiled_row = pl.BlockSpec((tr * ROW_PARTS, D // ROW_PARTS), lambda i: (i, 0))
    rw = _pad_cols(router_w)
    rb = _pad_cols(router_b[None, :])
    xo, u, idx, wgt = pl.pallas_call(
        functools.partial(_ln_router_kernel, alpha), grid=(T // tr,),
        in_specs=[row, row, vec, par, par, vec, vec,
                  pl.BlockSpec((D, V7X_LANES), lambda i: (0, 0)), pl.BlockSpec((1, V7X_LANES), lambda i: (0, 0))],
        out_specs=[row, tiled_row, lane_row, lane_row],
        out_shape=[jax.ShapeDtypeStruct((T, D), F32), jax.ShapeDtypeStruct((T * ROW_PARTS, D // ROW_PARTS), F32),
                   jax.ShapeDtypeStruct((T, V7X_LANES), jnp.int32), jax.ShapeDtypeStruct((T, V7X_LANES), F32)],
        compiler_params=_params("parallel"), name="deepnorm_ln_router",
    )(x, h, gate[:, None, :], ln_g[None, :], ln_b[None, :], sc[:, None, :], sh[:, None, :], rw, rb)
    return xo, u, idx[:, :TOP_K], wgt[:, :TOP_K]


MOE_ROW_TILE = 512


def _expert_changed(te_ref):
    t = pl.program_id(0)
    return jnp.logical_or(t == 0, te_ref[t] != te_ref[jnp.maximum(t - 1, 0)])


def _moe_gu_kernel(te_ref, tv_ref, x_ref, w_ref, b_ref, h_ref, wbf_ref):
    @pl.when(_expert_changed(te_ref))
    def _():
        wbf_ref[...] = w_ref[0].astype(BF16)

    @pl.when(tv_ref[pl.program_id(0)] > 0)
    def _():
        x = _load_row_tiled(x_ref, 0, x_ref.shape[0] // ROW_PARTS).astype(BF16)
        gu = _dot(x, wbf_ref[...]) + b_ref[0]
        gl = jnp.minimum(gu[:, :EXPERT_FF], SWIGLU_LIMIT)
        up = jnp.clip(gu[:, EXPERT_FF:], -SWIGLU_LIMIT, SWIGLU_LIMIT)
        h_ref[...] = ((up + 1.0) * gl * _sigmoid(gl * SWIGLU_ALPHA)).astype(h_ref.dtype)

    @pl.when(tv_ref[pl.program_id(0)] == 0)
    def _():
        h_ref[...] = jnp.zeros(h_ref.shape, h_ref.dtype)


def _moe_down_kernel(te_ref, tv_ref, h_ref, w_ref, b_ref, y_ref, wbf_ref):
    @pl.when(_expert_changed(te_ref))
    def _():
        wbf_ref[...] = w_ref[0].astype(BF16)

    @pl.when(tv_ref[pl.program_id(0)] > 0)
    def _():
        _store_row_tiled(y_ref, 0, _dot(h_ref[...], wbf_ref[...]) + b_ref[0])

    @pl.when(tv_ref[pl.program_id(0)] == 0)
    def _():
        y_ref[...] = jnp.zeros(y_ref.shape, y_ref.dtype)


def _route_metadata(top_idx, tm):
    T = top_idx.shape[0]
    P = T * TOP_K
    n_tiles = (P + N_EXPERTS * (tm - 1)) // tm
    e_flat = top_idx.reshape(P)
    onehot = (e_flat[:, None] == jnp.arange(N_EXPERTS)[None, :]).astype(jnp.int32)
    csum = jnp.cumsum(onehot, axis=0)
    counts = csum[-1]
    rank = jnp.sum(csum * onehot, axis=1) - 1
    padded = ((counts + tm - 1) // tm) * tm
    ends_p = jnp.cumsum(padded)
    starts_p = ends_p - padded
    dest = (starts_p[e_flat] + rank).astype(jnp.int32)
    tile_start = jnp.arange(n_tiles, dtype=jnp.int32) * tm
    tile_valid = (tile_start < ends_p[-1]).astype(jnp.int32)
    tile_expert = jnp.sum((tile_start[:, None] >= ends_p[None, :]).astype(jnp.int32), axis=1)
    last_expert = jnp.max(jnp.where(counts > 0, jnp.arange(N_EXPERTS), 0))
    tile_expert = jnp.where(tile_valid > 0, tile_expert, last_expert).astype(jnp.int32)
    return dest, tile_expert, tile_valid, n_tiles


DISPATCH_ROWS = 2048


def _dispatch_kernel(dest_ref, u_hbm, init_hbm, xs_hbm, sem):
    del init_hbm
    R = dest_ref.shape[-1]
    base = pl.program_id(0) * (R // TOP_K)

    def row(ref, i):
        return ref.at[pl.ds(pl.multiple_of(i * ROW_PARTS, ROW_PARTS), ROW_PARTS)]

    def issue(t, carry):
        for k in range(TOP_K):
            pltpu.make_async_copy(row(u_hbm, base + t), row(xs_hbm, dest_ref[0, 0, t * TOP_K + k]), sem).start()
        return carry

    lax.fori_loop(0, R // TOP_K, issue, 0, unroll=2)
    n = R * ROW_PARTS
    pltpu.make_async_copy(u_hbm.at[pl.ds(0, n)], xs_hbm.at[pl.ds(0, n)], sem).wait()


def _dispatch(u, dest, rows):
    P = dest.shape[0]
    R = DISPATCH_ROWS
    shape = (rows * ROW_PARTS, u.shape[1])
    return pl.pallas_call(
        _dispatch_kernel,
        grid=(P // R,),
        in_specs=[pl.BlockSpec((1, 1, R), lambda s: (s, 0, 0), memory_space=pltpu.SMEM),
                  pl.BlockSpec(memory_space=pl.ANY), pl.BlockSpec(memory_space=pl.ANY)],
        out_specs=pl.BlockSpec(memory_space=pl.ANY),
        out_shape=jax.ShapeDtypeStruct(shape, u.dtype),
        scratch_shapes=[pltpu.SemaphoreType.DMA(())],
        input_output_aliases={2: 0},
        compiler_params=_params("arbitrary"), name="moe_dispatch",
    )(dest.reshape(P // R, 1, R), u, jnp.zeros(shape, u.dtype))


def _moe_ffn(u, top_idx, w_gu, b_gu, w_down, b_down):
    D = u.shape[1] * ROW_PARTS
    tm = MOE_ROW_TILE
    dest, tile_expert, tile_valid, n_tiles = _route_metadata(top_idx, tm)
    rows = n_tiles * tm
    x_sorted = _dispatch(u, dest, rows)
    ff2 = 2 * EXPERT_FF
    tiled = pl.BlockSpec((tm * ROW_PARTS, D // ROW_PARTS), lambda t, te, tv: (t, 0))
    h = pl.pallas_call(
        _moe_gu_kernel,
        grid_spec=pltpu.PrefetchScalarGridSpec(
            num_scalar_prefetch=2, grid=(n_tiles,),
            in_specs=[tiled,
                      pl.BlockSpec((1, D, ff2), lambda t, te, tv: (te[t], 0, 0)),
                      pl.BlockSpec((1, 1, ff2), lambda t, te, tv: (te[t], 0, 0))],
            out_specs=pl.BlockSpec((tm, EXPERT_FF), lambda t, te, tv: (t, 0)),
            scratch_shapes=[pltpu.VMEM((D, ff2), BF16)]),
        out_shape=jax.ShapeDtypeStruct((rows, EXPERT_FF), BF16),
        compiler_params=_params("arbitrary"), name="moe_gate_up",
    )(tile_expert, tile_valid, x_sorted, w_gu, b_gu[:, None, :])
    y = pl.pallas_call(
        _moe_down_kernel,
        grid_spec=pltpu.PrefetchScalarGridSpec(
            num_scalar_prefetch=2, grid=(n_tiles,),
            in_specs=[pl.BlockSpec((tm, EXPERT_FF), lambda t, te, tv: (t, 0)),
                      pl.BlockSpec((1, EXPERT_FF, D), lambda t, te, tv: (te[t], 0, 0)),
                      pl.BlockSpec((1, 1, D), lambda t, te, tv: (te[t], 0, 0))],
            out_specs=tiled,
            scratch_shapes=[pltpu.VMEM((EXPERT_FF, D), BF16)]),
        out_shape=jax.ShapeDtypeStruct((rows * ROW_PARTS, D // ROW_PARTS), F32),
        compiler_params=_params("arbitrary"), name="moe_down",
    )(tile_expert, tile_valid, h, w_down, b_down[:, None, :])
    return y, dest


COMBINE_TILE = 256
COMBINE_SUB = 64


def _combine_ln_kernel(alpha, dest_ref, y_hbm, x_ref, wgt_ref, gate_ref, g_ref, b_ref, sc_ref, sh_ref,
                       xo_ref, u_ref, buf_ref, sems):
    n_sub = COMBINE_TILE // COMBINE_SUB

    def row(ref, i):
        return ref.at[pl.ds(pl.multiple_of(i * ROW_PARTS, ROW_PARTS), ROW_PARTS)]

    for j in range(n_sub):
        def issue(t, carry, j=j):
            tok = j * COMBINE_SUB + t
            for k in range(TOP_K):
                pltpu.make_async_copy(row(y_hbm, dest_ref[0, 0, tok * TOP_K + k]),
                                      row(buf_ref, k * COMBINE_TILE + tok), sems.at[j]).start()
            return carry

        lax.fori_loop(0, COMBINE_SUB, issue, 0, unroll=2)
    for j in range(n_sub):
        lo = j * COMBINE_SUB
        n = COMBINE_SUB * TOP_K * ROW_PARTS
        pltpu.make_async_copy(y_hbm.at[pl.ds(0, n)], buf_ref.at[pl.ds(0, n)], sems.at[j]).wait()
        wgt = wgt_ref[pl.ds(lo, COMBINE_SUB), :]
        f = sum(_load_row_tiled(buf_ref, k * COMBINE_TILE + lo, COMBINE_SUB) * wgt[:, k:k + 1] for k in range(TOP_K))
        xn = _deepnorm(alpha, x_ref[pl.ds(lo, COMBINE_SUB), :], f, gate_ref[0], g_ref[...], b_ref[...])
        xo_ref[pl.ds(lo, COMBINE_SUB), :] = xn
        u_ref[pl.ds(lo, COMBINE_SUB), :] = (xn * (1.0 + sc_ref[0]) + sh_ref[0]).astype(u_ref.dtype)


def _combine_ln(x, y, dest, top_w, gate, ln_g, ln_b, sc, sh, S, alpha):
    T, D = x.shape
    tr = COMBINE_TILE
    vec, row, par = _row_specs(D, S, tr)
    n_sub = tr // COMBINE_SUB
    return pl.pallas_call(
        functools.partial(_combine_ln_kernel, alpha), grid=(T // tr,),
        in_specs=[pl.BlockSpec((1, 1, tr * TOP_K), lambda i: (i, 0, 0), memory_space=pltpu.SMEM),
                  pl.BlockSpec(memory_space=pl.ANY), row,
                  pl.BlockSpec((tr, TOP_K), lambda i: (i, 0)), vec, par, par, vec, vec],
        out_specs=[row, row],
        out_shape=[jax.ShapeDtypeStruct((T, D), F32), jax.ShapeDtypeStruct((T, D), BF16)],
        scratch_shapes=[pltpu.VMEM((tr * TOP_K * ROW_PARTS, D // ROW_PARTS), F32), pltpu.SemaphoreType.DMA((n_sub,))],
        compiler_params=_params("arbitrary"), name="moe_combine_ln",
    )(dest.reshape(T // tr, 1, tr * TOP_K), y, x, top_w, gate[:, None, :], ln_g[None, :], ln_b[None, :],
      sc[:, None, :], sh[:, None, :])


ATTN_TILE = 256


def _softmax_block(s, m, l, acc, v):
    m_new = jnp.maximum(m, jnp.max(s, -1, keepdims=True))
    a = jnp.exp(m - m_new)
    p = jnp.exp(s - m_new)
    return m_new, a * l + jnp.sum(p, -1, keepdims=True), a * acc + _dot(p.astype(BF16), v)


def _mla_attn_kernel(scale, qn_ref, qr_ref, kv_ref, kr_ref, o_ref):
    qi = pl.program_id(2)
    t = ATTN_TILE
    row = _iota((t, t), 0)
    col = _iota((t, t), 1)
    for hh in range(2):
        qn = qn_ref[0, :, hh * MLA_NOPE:(hh + 1) * MLA_NOPE]
        qr = qr_ref[0, :, hh * MLA_ROPE:(hh + 1) * MLA_ROPE]
        c0 = hh * (MLA_NOPE + MLA_V)

        def scores(start):
            kn = kv_ref[0, pl.ds(start, t), c0:c0 + MLA_NOPE]
            kr = kr_ref[0, pl.ds(start, t), :]
            return (_dot_nt(qn, kn) + _dot_nt(qr, kr)) * scale

        def values(start):
            return kv_ref[0, pl.ds(start, t), c0 + MLA_NOPE:c0 + MLA_NOPE + MLA_V]

        d0 = pl.multiple_of(qi * t, t)
        s = jnp.where(col <= row, scores(d0), NEG_INF)
        m = jnp.max(s, -1, keepdims=True)
        p = jnp.exp(s - m)
        carry = (m, jnp.sum(p, -1, keepdims=True), _dot(p.astype(BF16), values(d0)))

        def body(j, carry):
            start = pl.multiple_of(j * t, t)
            return _softmax_block(scores(start), *carry, values(start))

        m, l, acc = lax.fori_loop(0, qi, body, carry)
        o_ref[0, :, hh * MLA_V:(hh + 1) * MLA_V] = (acc / l).astype(o_ref.dtype)


def _rope(x, cos, sin):
    half = x.shape[-1] // 2
    x1, x2 = x[..., :half], x[..., half:]
    return jnp.concatenate([x1 * cos - x2 * sin, x2 * cos + x1 * sin], -1)


def _mla_mixer(u, B, S, w_in, q_norm, kv_norm, w_qb, w_kvb, w_o):
    H = MLA_HEADS
    n_lat = MLA_Q_LORA + MLA_KV_LORA
    lat = _mm(u, w_in, F32, n_cols=n_lat)
    k_rope = _mm(u, w_in, F32, col0=n_lat, n_cols=V7X_LANES)[:, :MLA_ROPE]
    wq = w_qb.reshape(MLA_Q_LORA, H, MLA_NOPE + MLA_ROPE)
    wq = jnp.concatenate([wq[:, :, :MLA_NOPE].reshape(MLA_Q_LORA, H * MLA_NOPE),
                          wq[:, :, MLA_NOPE:].reshape(MLA_Q_LORA, H * MLA_ROPE)], -1)
    q = _mm(lat, wq, F32, rms_gain=q_norm, x_col=0)
    kv = _mm(lat, w_kvb, BF16, rms_gain=kv_norm, x_col=1)
    inv_freq = ROPE_THETA ** (-jnp.arange(MLA_ROPE // 2, dtype=F32) / (MLA_ROPE // 2))
    ang = jnp.arange(S, dtype=F32)[:, None] * inv_freq[None, :]
    cos, sin = jnp.cos(ang), jnp.sin(ang)
    q_nope = q[:, :H * MLA_NOPE].astype(BF16).reshape(B, S, H * MLA_NOPE)
    q_rope = _rope(q[:, H * MLA_NOPE:].reshape(B, S, H, MLA_ROPE), cos[:, None, :], sin[:, None, :])
    q_rope = q_rope.astype(BF16).reshape(B, S, H * MLA_ROPE)
    k_rope = _rope(k_rope.reshape(B, S, MLA_ROPE), cos, sin).astype(BF16)
    t = ATTN_TILE
    o = pl.pallas_call(
        functools.partial(_mla_attn_kernel, (MLA_NOPE + MLA_ROPE) ** -0.5),
        grid=(B, H // 2, S // t),
        in_specs=[pl.BlockSpec((1, t, 2 * MLA_NOPE), lambda b, h, i: (b, i, h)),
                  pl.BlockSpec((1, t, 2 * MLA_ROPE), lambda b, h, i: (b, i, h)),
                  pl.BlockSpec((1, S, 2 * (MLA_NOPE + MLA_V)), lambda b, h, i: (b, 0, h)),
                  pl.BlockSpec((1, S, MLA_ROPE), lambda b, h, i: (b, 0, 0))],
        out_specs=pl.BlockSpec((1, t, 2 * MLA_V), lambda b, h, i: (b, i, h)),
        out_shape=jax.ShapeDtypeStruct((B, S, H * MLA_V), BF16),
        compiler_params=_params("parallel", "parallel", "arbitrary"), name="mla_attention",
    )(q_nope, q_rope, kv.reshape(B, S, -1), k_rope)
    return _mm(o.reshape(B * S, H * MLA_V), w_o, F32)


def _t5_bucket(dist):
    n = jnp.maximum(dist, 0)
    max_exact = REL_BUCKETS // 2
    large = max_exact + (jnp.log(jnp.maximum(n, 1).astype(F32) / max_exact)
                         / math.log(REL_MAX_DIST / max_exact) * (REL_BUCKETS - max_exact)).astype(jnp.int32)
    large = jnp.minimum(large, REL_BUCKETS - 1)
    return jnp.where(n < max_exact, n, large)


def _moba_kernel(scale, n_sel, q_ref, k_ref, v_ref, bias_ref, o_ref, kbf_ref, vbf_ref, kmean_ref):
    qi = pl.program_id(2)
    L = MOBA_BLOCK
    S = k_ref.shape[1]
    n_blk = S // L

    @pl.when(qi == 0)
    def _():
        k = k_ref[0].astype(F32)
        kbf_ref[...] = k.astype(BF16)
        vbf_ref[...] = v_ref[0].astype(BF16)
        kmean_ref[...] = jnp.zeros(kmean_ref.shape, F32)
        kmean_ref[0:n_blk, :] = jnp.mean(k.reshape(n_blk, L, k.shape[-1]), axis=1)

    q = q_ref[0].astype(F32)
    qb = q.astype(BF16)
    lane = _iota((L, V7X_LANES), 1)
    gate = jnp.where(lane < qi, _dot_nt(q, kmean_ref[...], precision=HIGHEST), NEG_INF)
    row = _iota((L, L), 0)
    col = _iota((L, L), 1)
    d0 = pl.multiple_of(qi * L, L)
    s = _dot_nt(qb, kbf_ref[pl.ds(d0, L), :]) * scale + bias_ref[0, 0]
    s = jnp.where(col <= row, s, NEG_INF)
    m = jnp.max(s, -1, keepdims=True)
    p = jnp.exp(s - m)
    carry = (m, jnp.sum(p, -1, keepdims=True), _dot(p.astype(BF16), vbf_ref[pl.ds(d0, L), :]))

    def body(j, carry):
        gj = jnp.sum(jnp.where(lane == j, gate, 0.0), -1, keepdims=True)
        beats = jnp.logical_or(gate > gj, jnp.logical_and(gate == gj, lane < j))
        sel = jnp.sum(jnp.where(beats, 1.0, 0.0), -1, keepdims=True) < n_sel
        start = pl.multiple_of(j * L, L)
        bias = bias_ref[0, jnp.minimum(qi - j, 2)]
        s = _dot_nt(qb, kbf_ref[pl.ds(start, L), :]) * scale + bias
        s = jnp.where(sel, s, NEG_INF)
        return _softmax_block(s, *carry, vbf_ref[pl.ds(start, L), :])

    m, l, acc = lax.fori_loop(0, qi, body, carry)
    o_ref[0] = (acc / l).astype(o_ref.dtype)


def _moba_mixer(u, B, S, w_in, w_o, rel_bias):
    H, Dh, L = MOBA_HEADS, u.shape[1] // MOBA_HEADS, MOBA_BLOCK
    assert S % L == 0 and Dh == V7X_LANES and REL_MAX_DIST <= L
    n_blk = S // L
    n_sel = max(min(MOBA_TOPK, n_blk - 1), 1)
    qkv = _mm(u, w_in, BF16).reshape(B, S, 3 * H * Dh)
    qk = jnp.arange(L)[:, None] - jnp.arange(L)[None, :]
    dist = jnp.stack([qk, qk + L, qk + 2 * L])
    bias = jnp.transpose(rel_bias[_t5_bucket(dist)], (3, 0, 1, 2))
    o = pl.pallas_call(
        functools.partial(_moba_kernel, Dh ** -0.5, n_sel),
        grid=(B, H, n_blk),
        in_specs=[pl.BlockSpec((1, L, Dh), lambda b, h, i: (b, i, h)),
                  pl.BlockSpec((1, S, Dh), lambda b, h, i: (b, 0, H + h)),
                  pl.BlockSpec((1, S, Dh), lambda b, h, i: (b, 0, 2 * H + h)),
                  pl.BlockSpec((1, 3, L, L), lambda b, h, i: (h, 0, 0, 0))],
        out_specs=pl.BlockSpec((1, L, Dh), lambda b, h, i: (b, i, h)),
        out_shape=jax.ShapeDtypeStruct((B, S, H * Dh), BF16),
        scratch_shapes=[pltpu.VMEM((S, Dh), BF16), pltpu.VMEM((S, Dh), BF16), pltpu.VMEM((V7X_LANES, Dh), F32)],
        compiler_params=_params("parallel", "parallel", "arbitrary"), name="moba_attention",
    )(qkv, qkv, qkv, bias)
    return _mm(o.reshape(B * S, H * Dh), w_o, F32)


GDN_HEAD_GROUP = 8


def _l2norm(x):
    return x * lax.rsqrt(jnp.sum(x * x, -1, keepdims=True) + L2_EPS)


def _split_bf16(x):
    hi = x.astype(BF16)
    return hi, (x - hi.astype(F32)).astype(BF16)


def _dot3(a, b):
    ah, al = _split_bf16(a)
    bh, bl = _split_bf16(b)
    return _dot(ah, bh) + (_dot(ah, bl) + _dot(al, bh))


def _unit_lower_inverse(a_lows, block):
    C = a_lows[0].shape[0]
    r = _iota((C, C), 0)
    c = _iota((C, C), 1)
    eye = jnp.where(r == c, 1.0, 0.0)
    same = (r // block) == (c // block)
    a_d = [jnp.where(same, a, 0.0) for a in a_lows]
    a_off = [a - d for a, d in zip(a_lows, a_d)]
    inv_d = [eye - d for d in a_d]
    pw = a_d
    k = 2
    while k < block:
        pw = [_dot3(p, p) for p in pw]
        inv_d = [_dot3(i, eye + p) for i, p in zip(inv_d, pw)]
        k *= 2
    n = [_dot3(i, o) for i, o in zip(inv_d, a_off)]
    inv_n = [eye - x for x in n]
    pw = n
    k = 2
    while k < C // block:
        pw = [_dot3(p, p) for p in pw]
        inv_n = [_dot3(i, eye + p) for i, p in zip(inv_n, pw)]
        k *= 2
    return [_dot3(i, d) for i, d in zip(inv_n, inv_d)]


GDN_HIST = 8


def _causal_conv_silu(x_ref, w_ref, hist_ref):
    C = x_ref.shape[1]
    hist_ref[GDN_HIST:GDN_HIST + C, :] = x_ref[0].astype(F32)
    first = GDN_HIST - (GDN_CONV - 1)
    y = sum(hist_ref[first + i:first + i + C, :] * w_ref[i:i + 1, :] for i in range(GDN_CONV))
    hist_ref[0:GDN_HIST, :] = hist_ref[C:C + GDN_HIST, :]
    return _silu(y)


def _gdn_kernel(q_ref, k_ref, v_ref, z_ref, cwq_ref, cwk_ref, cwv_ref, gc_ref, gct_ref, beta_ref, ng_ref,
                o_ref, state_ref, hq_ref, hk_ref, hv_ref):
    G, C, DK, DV = GDN_HEAD_GROUP, GDN_CHUNK, GDN_DK, GDN_DV
    rep = GDN_V_HEADS // GDN_K_HEADS
    heads = range(G)

    @pl.when(pl.program_id(2) == 0)
    def _():
        state_ref[...] = jnp.zeros(state_ref.shape, F32)
        for hist in (hq_ref, hk_ref, hv_ref):
            hist[0:GDN_HIST, :] = jnp.zeros((GDN_HIST, hist.shape[1]), F32)

    qc = _causal_conv_silu(q_ref, cwq_ref, hq_ref)
    kc = _causal_conv_silu(k_ref, cwk_ref, hk_ref)
    vc = _causal_conv_silu(v_ref, cwv_ref, hv_ref)
    r = _iota((C, C), 0)
    c = _iota((C, C), 1)
    tri = c <= r
    strict = c < r
    gc = gc_ref[0, 0]
    gct = gct_ref[0, 0, 0]
    beta = beta_ref[0, 0]
    q = [_l2norm(qc[:, i * DK:(i + 1) * DK]) * (DK ** -0.5) for i in range(G // rep)]
    k = [_l2norm(kc[:, i * DK:(i + 1) * DK]) for i in range(G // rep)]
    kb = [x.astype(BF16) for x in k]
    qk = [_dot_nt(a.astype(BF16), b) for a, b in zip(q, kb)]
    kk = [_dot_nt(b, b) for b in kb]
    g_col = [gc[:, h:h + 1] for h in heads]
    b_col = [beta[:, h:h + 1] for h in heads]
    decay = [jnp.where(tri, jnp.exp(jnp.where(tri, g_col[h] - gct[h:h + 1, :], 0.0)), 0.0) for h in heads]
    t_inv = _unit_lower_inverse([jnp.where(strict, kk[h // rep] * b_col[h] * decay[h], 0.0) for h in heads], 16)
    e_g = [jnp.exp(g) for g in g_col]
    rhs = [jnp.concatenate([vc[:, h * DV:(h + 1) * DV] * b_col[h], k[h // rep] * (b_col[h] * e_g[h])], -1)
           for h in heads]
    sol = [_dot(t_inv[h].astype(BF16), rhs[h].astype(BF16)) for h in heads]
    state = [state_ref[h] for h in heads]
    state_b = [s.astype(BF16) for s in state]
    v_new = [sol[h][:, :DV] - _dot(sol[h][:, DV:].astype(BF16), state_b[h]) for h in heads]
    v_new_b = [x.astype(BF16) for x in v_new]
    attn = [jnp.where(tri, qk[h // rep] * decay[h], 0.0).astype(BF16) for h in heads]
    o = [_dot((q[h // rep] * e_g[h]).astype(BF16), state_b[h]) + _dot(attn[h], v_new_b[h]) for h in heads]
    g_last = [g[C - 1:C, :] for g in g_col]
    k_tail = [(k[h // rep] * jnp.exp(g_last[h] - g_col[h])).astype(BF16) for h in heads]
    new_state = [state[h] * jnp.exp(g_last[h]) + _dot_tn(k_tail[h], v_new_b[h]) for h in heads]
    o = [x * lax.rsqrt(jnp.mean(x * x, -1, keepdims=True) + RMS_EPS) * ng_ref[...] for x in o]
    o = [o[h] * _silu(z_ref[0, :, h * DV:(h + 1) * DV].astype(F32)) for h in heads]
    for h in heads:
        state_ref[h] = new_state[h]
    o_ref[0] = jnp.concatenate(o, -1).astype(o_ref.dtype)


def _gdn_mixer(u, B, S, w_in, conv_w, a_log, dt_bias, norm_g, w_o):
    HK, HV, DK, DV, C, G = GDN_K_HEADS, GDN_V_HEADS, GDN_DK, GDN_DV, GDN_CHUNK, GDN_HEAD_GROUP
    qk_dim, v_dim = HK * DK, HV * DV
    n_main = 2 * qk_dim + 2 * v_dim
    proj = _mm(u, w_in, BF16, n_cols=n_main).reshape(B, S, n_main)
    ba = _mm(u, w_in, F32, col0=n_main, n_cols=V7X_LANES).reshape(B, S, -1)
    n_conv = 2 * qk_dim + v_dim
    beta = jax.nn.sigmoid(ba[:, :, :HV])
    g = -jnp.exp(a_log) * jax.nn.softplus(ba[:, :, HV:2 * HV] + dt_bias)
    N = S // C
    gc = jnp.cumsum(g.reshape(B, N, C, HV), axis=2)
    HG = HV // G
    gc_g = gc.reshape(B, N, C, HG, G).transpose(0, 3, 1, 2, 4).reshape(B, HG, S, G)
    gct_g = gc.reshape(B, N, C, HG, G).transpose(0, 3, 1, 4, 2)
    beta_g = beta.reshape(B, S, HG, G).transpose(0, 2, 1, 3)
    kw = (G // (HV // HK)) * DK
    vw = G * DV
    k_blk, v_blk, z_blk = qk_dim // kw, 2 * qk_dim // vw, n_conv // vw
    o = pl.pallas_call(
        _gdn_kernel,
        grid=(B, HG, N),
        in_specs=[pl.BlockSpec((1, C, kw), lambda b, h, n: (b, n, h)),
                  pl.BlockSpec((1, C, kw), lambda b, h, n: (b, n, k_blk + h)),
                  pl.BlockSpec((1, C, vw), lambda b, h, n: (b, n, v_blk + h)),
                  pl.BlockSpec((1, C, vw), lambda b, h, n: (b, n, z_blk + h)),
                  pl.BlockSpec((GDN_CONV, kw), lambda b, h, n: (0, h)),
                  pl.BlockSpec((GDN_CONV, kw), lambda b, h, n: (0, k_blk + h)),
                  pl.BlockSpec((GDN_CONV, vw), lambda b, h, n: (0, v_blk + h)),
                  pl.BlockSpec((1, 1, C, G), lambda b, h, n: (b, h, n, 0)),
                  pl.BlockSpec((1, 1, 1, G, C), lambda b, h, n: (b, h, n, 0, 0)),
                  pl.BlockSpec((1, 1, C, G), lambda b, h, n: (b, h, n, 0)),
                  pl.BlockSpec((1, DV), lambda b, h, n: (0, 0))],
        out_specs=pl.BlockSpec((1, C, vw), lambda b, h, n: (b, n, h)),
        out_shape=jax.ShapeDtypeStruct((B, S, v_dim), BF16),
        scratch_shapes=[pltpu.VMEM((G, DK, DV), F32), pltpu.VMEM((GDN_HIST + C, kw), F32),
                        pltpu.VMEM((GDN_HIST + C, kw), F32), pltpu.VMEM((GDN_HIST + C, vw), F32)],
        compiler_params=_params("parallel", "parallel", "arbitrary"), name="gdn_chunked",
    )(proj, proj, proj, proj, conv_w, conv_w, conv_w, gc_g, gct_g, beta_g, norm_g[None, :])
    return _mm(o.reshape(B * S, v_dim), w_o, F32)


def _gla_kernel(scale, q_ref, k_ref, v_ref, og_ref, gk_ref, wgk_ref, bgk_ref, ng_ref, o_ref, state_ref):
    C = GLA_CHUNK

    @pl.when(pl.program_id(2) == 0)
    def _():
        state_ref[...] = jnp.zeros(state_ref.shape, F32)

    x = _dot(gk_ref[0].astype(BF16), wgk_ref[...].astype(BF16)) + bgk_ref[...]
    log_alpha = (jnp.minimum(x, 0.0) - jnp.log(1.0 + jnp.exp(-jnp.abs(x)))) / GLA_GATE_NORMALIZER
    r = _iota((C, C), 0)
    c = _iota((C, C), 1)
    causal = c <= r
    b = _dot(jnp.where(causal, 1.0, 0.0), log_alpha, precision=HIGHEST)
    b_last = b[C - 1:C, :]
    q = q_ref[0].astype(F32) * scale
    k = k_ref[0].astype(F32)
    v = v_ref[0].astype(BF16)
    q_dec = (q * jnp.exp(b)).astype(BF16)
    k_inv = (k * jnp.exp(-b)).astype(BF16)
    k_tail = (k * jnp.exp(b_last - b)).astype(BF16)
    attn = jnp.where(causal, _dot_nt(q_dec, k_inv), 0.0)
    state_t = state_ref[...]
    o = _dot(attn.astype(BF16), v) + _dot_nt(q_dec, state_t.astype(BF16))
    state_ref[...] = state_t * jnp.exp(b_last) + _dot_tn(v, k_tail)
    o = o * lax.rsqrt(jnp.mean(o * o, -1, keepdims=True) + RMS_EPS) * ng_ref[...]
    o_ref[0] = (o * _silu(og_ref[0].astype(F32))).astype(o_ref.dtype)


def _gla_mixer(u, B, S, w_in, w_gk, b_gk, norm_g, w_o):
    D = u.shape[1]
    H, C = GLA_HEADS, GLA_CHUNK
    key_dim, val_dim = D // 2, D
    dk, dv = key_dim // H, val_dim // H
    n_main = 2 * key_dim + 2 * val_dim
    proj = _mm(u, w_in, BF16, n_cols=n_main).reshape(B, S, n_main)
    gk = _mm(u, w_in, F32, col0=n_main, n_cols=V7X_LANES).reshape(B, S, -1)
    wgk = jnp.pad(w_gk, ((0, gk.shape[-1] - GLA_GATE_RANK), (0, 0)))
    o = pl.pallas_call(
        functools.partial(_gla_kernel, dk ** -0.5),
        grid=(B, H, S // C),
        in_specs=[pl.BlockSpec((1, C, dk), lambda b, h, n: (b, n, h)),
                  pl.BlockSpec((1, C, dk), lambda b, h, n: (b, n, H + h)),
                  pl.BlockSpec((1, C, dv), lambda b, h, n: (b, n, 2 * key_dim // dv + h)),
                  pl.BlockSpec((1, C, dv), lambda b, h, n: (b, n, (2 * key_dim + val_dim) // dv + h)),
                  pl.BlockSpec((1, C, gk.shape[-1]), lambda b, h, n: (b, n, 0)),
                  pl.BlockSpec((gk.shape[-1], dk), lambda b, h, n: (0, h)),
                  pl.BlockSpec((1, dk), lambda b, h, n: (0, h)),
                  pl.BlockSpec((1, dv), lambda b, h, n: (0, 0))],
        out_specs=pl.BlockSpec((1, C, dv), lambda b, h, n: (b, n, h)),
        out_shape=jax.ShapeDtypeStruct((B, S, val_dim), BF16),
        scratch_shapes=[pltpu.VMEM((dv, dk), F32)],
        compiler_params=_params("parallel", "parallel", "arbitrary"), name="gla_chunked",
    )(proj, proj, proj, proj, gk, wgk, b_gk[None, :], norm_g[None, :])
    return _mm(o.reshape(B * S, val_dim), w_o, F32)


def kernel(x, c, rel_bias, mla_w_in, mla_q_norm, mla_kv_norm, mla_w_qb, mla_w_kvb, mla_w_o, gdn_w_in, gdn_conv_w, gdn_a_log, gdn_dt_bias, gdn_norm, gdn_w_o, gla_w_in, gla_w_gk, gla_b_gk, gla_norm, gla_w_o, moba_w_in, moba_w_o, ada_w, ada_b, ln_g, ln_b, router_w, router_b, moe_w_gu, moe_b_gu, moe_w_down, moe_b_down):
    B, S, D = x.shape
    assert D == D_MODEL
    depth = ada_w.shape[0]
    alpha = (2 * depth) ** 0.25
    mod = _ada_mod(c, ada_w, ada_b)
    sh_a, sc_a, g_a, sh_f, sc_f, g_f = (mod[:, :, k * D:(k + 1) * D] for k in range(6))
    xt = x.reshape(B * S, D)
    u = _modulate(xt, sc_a[0], sh_a[0], S)
    for i in range(depth):
        m, j = i % N_MIXERS, i // N_MIXERS
        if m == 0:
            h = _mla_mixer(u, B, S, mla_w_in[j], mla_q_norm[j], mla_kv_norm[j], mla_w_qb[j], mla_w_kvb[j], mla_w_o[j])
        elif m == 1:
            h = _gdn_mixer(u, B, S, gdn_w_in[j], gdn_conv_w[j], gdn_a_log[j], gdn_dt_bias[j], gdn_norm[j], gdn_w_o[j])
        elif m == 2:
            h = _gla_mixer(u, B, S, gla_w_in[j], gla_w_gk[j], gla_b_gk[j], gla_norm[j], gla_w_o[j])
        else:
            h = _moba_mixer(u, B, S, moba_w_in[j], moba_w_o[j], rel_bias)
        xt, u, top_idx, top_w = _ln_router(xt, h, g_a[i], ln_g[i, 0], ln_b[i, 0], sc_f[i], sh_f[i],
                                           router_w[i], router_b[i], S, alpha)
        y, dest = _moe_ffn(u, top_idx, moe_w_gu[i], moe_b_gu[i], moe_w_down[i], moe_b_down[i])
        nxt = (i + 1) % depth
        xt, u = _combine_ln(xt, y, dest, top_w, g_f[i], ln_g[i, 1], ln_b[i, 1], sc_a[nxt], sh_a[nxt], S, alpha)
    return xt.reshape(B, S, D)
```

```python
import functools
import math

import jax
import jax.numpy as jnp
from jax import lax
from jax.experimental import pallas as pl
from jax.experimental.pallas import tpu as pltpu

D_MODEL = 2048
N_MIXERS = 4
MLA_HEADS, MLA_Q_LORA, MLA_KV_LORA, MLA_NOPE, MLA_ROPE, MLA_V = 16, 512, 512, 128, 64, 128
ROPE_THETA = 10000.0
GDN_K_HEADS, GDN_V_HEADS, GDN_DK, GDN_DV, GDN_CONV, GDN_CHUNK = 16, 32, 128, 128, 4, 64
GLA_HEADS, GLA_GATE_RANK, GLA_GATE_NORMALIZER, GLA_CHUNK = 4, 16, 16.0, 64
MOBA_HEADS, MOBA_BLOCK, MOBA_TOPK = 16, 256, 3
REL_BUCKETS, REL_MAX_DIST = 32, 128
N_EXPERTS, TOP_K, EXPERT_FF = 32, 4, 768
SWIGLU_LIMIT, SWIGLU_ALPHA = 7.0, 1.702
LN_EPS, RMS_EPS, L2_EPS = 1e-5, 1e-6, 1e-6

V7X_LANES = 128
V7X_VMEM_LIMIT_BYTES = 56 * 1024 * 1024

F32 = jnp.float32
BF16 = jnp.bfloat16
HIGHEST = lax.Precision.HIGHEST
NEG_INF = float("-inf")


def _params(*sem):
    return pltpu.CompilerParams(dimension_semantics=sem, vmem_limit_bytes=V7X_VMEM_LIMIT_BYTES)


def _dot(a, b, dims=None, precision=None):
    if dims is None:
        dims = (((a.ndim - 1,), (0,)), ((), ()))
    return lax.dot_general(a, b, dims, precision=precision, preferred_element_type=F32)


def _dot_nt(a, b, precision=None):
    return _dot(a, b, (((1,), (1,)), ((), ())), precision)


def _dot_tn(a, b, precision=None):
    return _dot(a, b, (((0,), (0,)), ((), ())), precision)


def _sigmoid(x):
    return 1.0 / (1.0 + jnp.exp(-x))


def _silu(x):
    return x * _sigmoid(x)


def _iota(shape, dim):
    return lax.broadcasted_iota(jnp.int32, shape, dim)


def _mm_kernel(valid_cols, rms, x_ref, *refs):
    g_ref = refs[0] if rms else None
    w_ref, o_ref, wbf_ref = refs[-3:]

    @pl.when(pl.program_id(1) == 0)
    def _():
        w = w_ref[...]
        if valid_cols is not None:
            w = jnp.where(_iota(w.shape, 1) < valid_cols, w, 0.0)
        wbf_ref[...] = w.astype(BF16)

    x = x_ref[...]
    if rms:
        x = x.astype(F32)
        x = x * lax.rsqrt(jnp.mean(x * x, -1, keepdims=True) + RMS_EPS) * g_ref[...]
    o_ref[...] = _dot(x.astype(BF16), wbf_ref[...]).astype(o_ref.dtype)


def _mm_tiles(M, K, N):
    tm = 1024 if M % 1024 == 0 else M
    tn = N
    for cand in (1024, 512, 256, 128):
        if N % cand == 0 and K * cand * 4 <= 8 * 1024 * 1024:
            tn = cand
            break
    return tm, tn


def _mm(x, w, out_dtype, rms_gain=None, x_col=0, col0=0, n_cols=None):
    M = x.shape[0]
    K = w.shape[0]
    N = w.shape[1] if n_cols is None else n_cols
    tm, tn = _mm_tiles(M, K, N)
    assert col0 % tn == 0
    valid_cols = w.shape[1] - col0 if col0 + N > w.shape[1] else None
    assert valid_cols is None or N == tn
    c0 = col0 // tn
    x_spec = pl.BlockSpec((tm, K), lambda n, m: (m, x_col))
    w_spec = pl.BlockSpec((K, tn), lambda n, m: (0, c0 + n))
    if rms_gain is None:
        in_specs, args = [x_spec, w_spec], (x, w)
    else:
        in_specs, args = [x_spec, pl.BlockSpec((1, K), lambda n, m: (0, 0)), w_spec], (x, rms_gain[None, :], w)
    return pl.pallas_call(
        functools.partial(_mm_kernel, valid_cols, rms_gain is not None),
        grid=(N // tn, M // tm),
        in_specs=in_specs,
        out_specs=pl.BlockSpec((tm, tn), lambda n, m: (m, n)),
        out_shape=jax.ShapeDtypeStruct((M, N), out_dtype),
        scratch_shapes=[pltpu.VMEM((K, tn), BF16)],
        compiler_params=_params("parallel", "arbitrary"),
        name="proj_matmul",
    )(*args)


def _pad_cols(w, mult=V7X_LANES):
    pad = (-w.shape[-1]) % mult
    return jnp.pad(w, ((0, 0), (0, pad))) if pad else w


def _ada_kernel(c_ref, w_ref, b_ref, o_ref):
    c = _silu(c_ref[...]).astype(BF16)
    o_ref[0] = _dot(c, w_ref[0].astype(BF16)) + b_ref[0]


def _ada_mod(c, ada_w, ada_b):
    depth, D, N = ada_w.shape
    B = c.shape[0]
    rows = 8
    c_pad = jnp.pad(c, ((0, rows - B), (0, 0)))
    tn = 1024
    out = pl.pallas_call(
        _ada_kernel,
        grid=(depth, N // tn),
        in_specs=[pl.BlockSpec((rows, D), lambda i, n: (0, 0)),
                  pl.BlockSpec((1, D, tn), lambda i, n: (i, 0, n)),
                  pl.BlockSpec((1, 1, tn), lambda i, n: (i, 0, n))],
        out_specs=pl.BlockSpec((1, rows, tn), lambda i, n: (i, 0, n)),
        out_shape=jax.ShapeDtypeStruct((depth, rows, N), F32),
        compiler_params=_params("parallel", "parallel"),
        name="ada_mod",
    )(c_pad, ada_w, ada_b.reshape(depth, 1, N))
    return out[:, :B]


ROW_TILE = 256


def _modulate_kernel(x_ref, sc_ref, sh_ref, u_ref):
    u_ref[...] = (x_ref[...] * (1.0 + sc_ref[0]) + sh_ref[0]).astype(u_ref.dtype)


def _row_specs(D, S, tr):
    vec = pl.BlockSpec((1, 1, D), lambda i: ((i * tr) // S, 0, 0))
    row = pl.BlockSpec((tr, D), lambda i: (i, 0))
    par = pl.BlockSpec((1, D), lambda i: (0, 0))
    return vec, row, par


def _modulate(x, sc, sh, S):
    T, D = x.shape
    tr = ROW_TILE
    vec, row, _ = _row_specs(D, S, tr)
    return pl.pallas_call(
        _modulate_kernel, grid=(T // tr,), in_specs=[row, vec, vec], out_specs=row,
        out_shape=jax.ShapeDtypeStruct((T, D), BF16),
        compiler_params=_params("parallel"), name="modulate",
    )(x, sc[:, None, :], sh[:, None, :])


def _deepnorm(alpha, x, h, gate, g, b):
    y = alpha * x + (1.0 + gate) * h
    mu = jnp.mean(y, -1, keepdims=True)
    yc = y - mu
    var = jnp.mean(yc * yc, -1, keepdims=True)
    return yc * lax.rsqrt(var + LN_EPS) * g + b


ROW_PARTS = D_MODEL // V7X_LANES


def _store_row_tiled(ref, row0, value):
    n, d = value.shape
    w = d // ROW_PARTS
    for j in range(ROW_PARTS):
        ref[pl.ds(row0 * ROW_PARTS + j, n, stride=ROW_PARTS), :] = value[:, j * w:(j + 1) * w]


def _load_row_tiled(ref, row0, n):
    return jnp.concatenate([ref[pl.ds(row0 * ROW_PARTS + j, n, stride=ROW_PARTS), :] for j in range(ROW_PARTS)], -1)


def _ln_router_kernel(alpha, x_ref, h_ref, gate_ref, g_ref, b_ref, sc_ref, sh_ref, rw_ref, rb_ref,
                      xo_ref, u_ref, idx_ref, wgt_ref):
    xn = _deepnorm(alpha, x_ref[...], h_ref[...].astype(F32), gate_ref[0], g_ref[...], b_ref[...])
    xo_ref[...] = xn
    u = xn * (1.0 + sc_ref[0]) + sh_ref[0]
    _store_row_tiled(u_ref, 0, u)
    logits = _dot(u, rw_ref[...], precision=HIGHEST) + rb_ref[...]
    lane = _iota(logits.shape, 1)
    vals = jnp.where(lane < N_EXPERTS, logits, NEG_INF)
    top_v, top_i = [], []
    for _ in range(TOP_K):
        m = jnp.max(vals, -1, keepdims=True)
        i = jnp.min(jnp.where(vals == m, lane, V7X_LANES), -1, keepdims=True)
        top_v.append(m)
        top_i.append(i)
        vals = jnp.where(lane == i, NEG_INF, vals)
    exps = [jnp.exp(v - top_v[0]) for v in top_v]
    denom = functools.reduce(lambda a, b: a + b, exps)
    idx = jnp.zeros(logits.shape, jnp.int32)
    wgt = jnp.zeros(logits.shape, F32)
    for k in range(TOP_K):
        idx = jnp.where(lane == k, top_i[k], idx)
        wgt = jnp.where(lane == k, exps[k] / denom, wgt)
    idx_ref[...] = idx
    wgt_ref[...] = wgt


def _ln_router(x, h, gate, ln_g, ln_b, sc, sh, router_w, router_b, S, alpha):
    T, D = x.shape
    tr = ROW_TILE
    vec, row, par = _row_specs(D, S, tr)
    lane_row = pl.BlockSpec((tr, V7X_LANES), lambda i: (i, 0))
    tiled_row = pl.BlockSpec((tr * ROW_PARTS, D // ROW_PARTS), lambda i: (i, 0))
    rw = _pad_cols(router_w)
    rb = _pad_cols(router_b[None, :])
    xo, u, idx, wgt = pl.pallas_call(
        functools.partial(_ln_router_kernel, alpha), grid=(T // tr,),
        in_specs=[row, row, vec, par, par, vec, vec,
                  pl.BlockSpec((D, V7X_LANES), lambda i: (0, 0)), pl.BlockSpec((1, V7X_LANES), lambda i: (0, 0))],
        out_specs=[row, tiled_row, lane_row, lane_row],
        out_shape=[jax.ShapeDtypeStruct((T, D), F32), jax.ShapeDtypeStruct((T * ROW_PARTS, D // ROW_PARTS), F32),
                   jax.ShapeDtypeStruct((T, V7X_LANES), jnp.int32), jax.ShapeDtypeStruct((T, V7X_LANES), F32)],
        compiler_params=_params("parallel"), name="deepnorm_ln_router",
    )(x, h, gate[:, None, :], ln_g[None, :], ln_b[None, :], sc[:, None, :], sh[:, None, :], rw, rb)
    return xo, u, idx[:, :TOP_K], wgt[:, :TOP_K]


MOE_ROW_TILE = 512


def _expert_changed(te_ref):
    t = pl.program_id(0)
    return jnp.logical_or(t == 0, te_ref[t] != te_ref[jnp.maximum(t - 1, 0)])


def _moe_gu_kernel(te_ref, tv_ref, x_ref, w_ref, b_ref, h_ref, wbf_ref):
    @pl.when(_expert_changed(te_ref))
    def _():
        wbf_ref[...] = w_ref[0].astype(BF16)

    @pl.when(tv_ref[pl.program_id(0)] > 0)
    def _():
        x = _load_row_tiled(x_ref, 0, x_ref.shape[0] // ROW_PARTS).astype(BF16)
        gu = _dot(x, wbf_ref[...]) + b_ref[0]
        gl = jnp.minimum(gu[:, :EXPERT_FF], SWIGLU_LIMIT)
        up = jnp.clip(gu[:, EXPERT_FF:], -SWIGLU_LIMIT, SWIGLU_LIMIT)
        h_ref[...] = ((up + 1.0) * gl * _sigmoid(gl * SWIGLU_ALPHA)).astype(h_ref.dtype)

    @pl.when(tv_ref[pl.program_id(0)] == 0)
    def _():
        h_ref[...] = jnp.zeros(h_ref.shape, h_ref.dtype)


def _moe_down_kernel(te_ref, tv_ref, h_ref, w_ref, b_ref, y_ref, wbf_ref):
    @pl.when(_expert_changed(te_ref))
    def _():
        wbf_ref[...] = w_ref[0].astype(BF16)

    @pl.when(tv_ref[pl.program_id(0)] > 0)
    def _():
        _store_row_tiled(y_ref, 0, _dot(h_ref[...], wbf_ref[...]) + b_ref[0])

    @pl.when(tv_ref[pl.program_id(0)] == 0)
    def _():
        y_ref[...] = jnp.zeros(y_ref.shape, y_ref.dtype)


def _route_metadata(top_idx, tm):
    T = top_idx.shape[0]
    P = T * TOP_K
    n_tiles = (P + N_EXPERTS * (tm - 1)) // tm
    e_flat = top_idx.reshape(P)
    onehot = (e_flat[:, None] == jnp.arange(N_EXPERTS)[None, :]).astype(jnp.int32)
    csum = jnp.cumsum(onehot, axis=0)
    counts = csum[-1]
    rank = jnp.sum(csum * onehot, axis=1) - 1
    padded = ((counts + tm - 1) // tm) * tm
    ends_p = jnp.cumsum(padded)
    starts_p = ends_p - padded
    dest = (starts_p[e_flat] + rank).astype(jnp.int32)
    tile_start = jnp.arange(n_tiles, dtype=jnp.int32) * tm
    tile_valid = (tile_start < ends_p[-1]).astype(jnp.int32)
    tile_expert = jnp.sum((tile_start[:, None] >= ends_p[None, :]).astype(jnp.int32), axis=1)
    last_expert = jnp.max(jnp.where(counts > 0, jnp.arange(N_EXPERTS), 0))
    tile_expert = jnp.where(tile_valid > 0, tile_expert, last_expert).astype(jnp.int32)
    return dest, tile_expert, tile_valid, n_tiles


DISPATCH_ROWS = 2048


def _dispatch_kernel(dest_ref, u_ref, init_hbm, xs_hbm, sem):
    del init_hbm
    R = dest_ref.shape[-1]

    def row(ref, i):
        return ref.at[pl.ds(pl.multiple_of(i * ROW_PARTS, ROW_PARTS), ROW_PARTS)]

    def issue(t, carry):
        for k in range(TOP_K):
            pltpu.make_async_copy(row(u_ref, t), row(xs_hbm, dest_ref[0, 0, t * TOP_K + k]), sem).start()
        return carry

    lax.fori_loop(0, R // TOP_K, issue, 0, unroll=2)
    n = (R // TOP_K) * ROW_PARTS
    for _ in range(TOP_K):
        pltpu.make_async_copy(u_ref, xs_hbm.at[pl.ds(0, n)], sem).wait()


def _dispatch(u, dest, rows):
    P = dest.shape[0]
    R = DISPATCH_ROWS
    shape = (rows * ROW_PARTS, u.shape[1])
    return pl.pallas_call(
        _dispatch_kernel,
        grid=(P // R,),
        in_specs=[pl.BlockSpec((1, 1, R), lambda s: (s, 0, 0), memory_space=pltpu.SMEM),
                  pl.BlockSpec(((R // TOP_K) * ROW_PARTS, u.shape[1]), lambda s: (s, 0)),
                  pl.BlockSpec(memory_space=pl.ANY)],
        out_specs=pl.BlockSpec(memory_space=pl.ANY),
        out_shape=jax.ShapeDtypeStruct(shape, u.dtype),
        scratch_shapes=[pltpu.SemaphoreType.DMA(())],
        input_output_aliases={2: 0},
        compiler_params=_params("arbitrary"), name="moe_dispatch",
    )(dest.reshape(P // R, 1, R), u, jnp.zeros(shape, u.dtype))


def _moe_ffn(u, top_idx, layer, w_gu, b_gu, w_down, b_down):
    D = u.shape[1] * ROW_PARTS
    tm = MOE_ROW_TILE
    dest, tile_expert, tile_valid, n_tiles = _route_metadata(top_idx, tm)
    tile_expert = tile_expert + layer * N_EXPERTS
    w_gu, w_down = (w.reshape((-1,) + w.shape[2:]) for w in (w_gu, w_down))
    b_gu, b_down = (b.reshape(-1, b.shape[-1]) for b in (b_gu, b_down))
    rows = n_tiles * tm
    x_sorted = _dispatch(u, dest, rows)
    ff2 = 2 * EXPERT_FF
    tiled = pl.BlockSpec((tm * ROW_PARTS, D // ROW_PARTS), lambda t, te, tv: (t, 0))
    h = pl.pallas_call(
        _moe_gu_kernel,
        grid_spec=pltpu.PrefetchScalarGridSpec(
            num_scalar_prefetch=2, grid=(n_tiles,),
            in_specs=[tiled,
                      pl.BlockSpec((1, D, ff2), lambda t, te, tv: (te[t], 0, 0)),
                      pl.BlockSpec((1, 1, ff2), lambda t, te, tv: (te[t], 0, 0))],
            out_specs=pl.BlockSpec((tm, EXPERT_FF), lambda t, te, tv: (t, 0)),
            scratch_shapes=[pltpu.VMEM((D, ff2), BF16)]),
        out_shape=jax.ShapeDtypeStruct((rows, EXPERT_FF), BF16),
        compiler_params=_params("arbitrary"), name="moe_gate_up",
    )(tile_expert, tile_valid, x_sorted, w_gu, b_gu[:, None, :])
    y = pl.pallas_call(
        _moe_down_kernel,
        grid_spec=pltpu.PrefetchScalarGridSpec(
            num_scalar_prefetch=2, grid=(n_tiles,),
            in_specs=[pl.BlockSpec((tm, EXPERT_FF), lambda t, te, tv: (t, 0)),
                      pl.BlockSpec((1, EXPERT_FF, D), lambda t, te, tv: (te[t], 0, 0)),
                      pl.BlockSpec((1, 1, D), lambda t, te, tv: (te[t], 0, 0))],
            out_specs=tiled,
            scratch_shapes=[pltpu.VMEM((EXPERT_FF, D), BF16)]),
        out_shape=jax.ShapeDtypeStruct((rows * ROW_PARTS, D // ROW_PARTS), F32),
        compiler_params=_params("arbitrary"), name="moe_down",
    )(tile_expert, tile_valid, h, w_down, b_down[:, None, :])
    return y, dest


COMBINE_TILE = 256
COMBINE_SUB = 64


def _combine_ln_kernel(alpha, dest_ref, y_hbm, x_ref, wgt_ref, gate_ref, g_ref, b_ref, sc_ref, sh_ref,
                       xo_ref, u_ref, buf_ref, sems):
    n_sub = COMBINE_TILE // COMBINE_SUB

    def row(ref, i):
        return ref.at[pl.ds(pl.multiple_of(i * ROW_PARTS, ROW_PARTS), ROW_PARTS)]

    for j in range(n_sub):
        def issue(t, carry, j=j):
            tok = j * COMBINE_SUB + t
            for k in range(TOP_K):
                pltpu.make_async_copy(row(y_hbm, dest_ref[0, 0, tok * TOP_K + k]),
                                      row(buf_ref, k * COMBINE_TILE + tok), sems.at[j]).start()
            return carry

        lax.fori_loop(0, COMBINE_SUB, issue, 0, unroll=2)
    for j in range(n_sub):
        lo = j * COMBINE_SUB
        n = COMBINE_SUB * TOP_K * ROW_PARTS
        pltpu.make_async_copy(y_hbm.at[pl.ds(0, n)], buf_ref.at[pl.ds(0, n)], sems.at[j]).wait()
        wgt = wgt_ref[pl.ds(lo, COMBINE_SUB), :]
        f = sum(_load_row_tiled(buf_ref, k * COMBINE_TILE + lo, COMBINE_SUB) * wgt[:, k:k + 1] for k in range(TOP_K))
        xn = _deepnorm(alpha, x_ref[pl.ds(lo, COMBINE_SUB), :], f, gate_ref[0], g_ref[...], b_ref[...])
        xo_ref[pl.ds(lo, COMBINE_SUB), :] = xn
        u_ref[pl.ds(lo, COMBINE_SUB), :] = (xn * (1.0 + sc_ref[0]) + sh_ref[0]).astype(u_ref.dtype)


def _combine_ln(x, y, dest, top_w, gate, ln_g, ln_b, sc, sh, S, alpha):
    T, D = x.shape
    tr = COMBINE_TILE
    vec, row, par = _row_specs(D, S, tr)
    n_sub = tr // COMBINE_SUB
    return pl.pallas_call(
        functools.partial(_combine_ln_kernel, alpha), grid=(T // tr,),
        in_specs=[pl.BlockSpec((1, 1, tr * TOP_K), lambda i: (i, 0, 0), memory_space=pltpu.SMEM),
                  pl.BlockSpec(memory_space=pl.ANY), row,
                  pl.BlockSpec((tr, TOP_K), lambda i: (i, 0)), vec, par, par, vec, vec],
        out_specs=[row, row],
        out_shape=[jax.ShapeDtypeStruct((T, D), F32), jax.ShapeDtypeStruct((T, D), BF16)],
        scratch_shapes=[pltpu.VMEM((tr * TOP_K * ROW_PARTS, D // ROW_PARTS), F32), pltpu.SemaphoreType.DMA((n_sub,))],
        compiler_params=_params("arbitrary"), name="moe_combine_ln",
    )(dest.reshape(T // tr, 1, tr * TOP_K), y, x, top_w, gate[:, None, :], ln_g[None, :], ln_b[None, :],
      sc[:, None, :], sh[:, None, :])


ATTN_TILE = 256


def _softmax_pv(s, v):
    p = jnp.exp(s - jnp.max(s, -1, keepdims=True))
    return _dot(p.astype(BF16), v) / jnp.sum(p, -1, keepdims=True)


def _causal_widths(n_tiles):
    half = n_tiles // 2
    return [(half, half), (n_tiles, n_tiles)] if half > 0 else [(n_tiles, n_tiles)]


def _mla_attn_kernel(scale, qn_ref, qr_ref, kv_ref, kr_ref, o_ref):
    qi = pl.program_id(2)
    t = ATTN_TILE
    n_tiles = kv_ref.shape[1] // t

    def attend(n_kv):
        w = n_kv * t
        visible = _iota((t, w), 1) <= _iota((t, w), 0) + qi * t
        outs = []
        for hh in range(2):
            qn = qn_ref[0, :, hh * MLA_NOPE:(hh + 1) * MLA_NOPE]
            qr = qr_ref[0, :, hh * MLA_ROPE:(hh + 1) * MLA_ROPE]
            c0 = hh * (MLA_NOPE + MLA_V)
            s = (_dot_nt(qn, kv_ref[0, 0:w, c0:c0 + MLA_NOPE]) + _dot_nt(qr, kr_ref[0, 0:w, :])) * scale
            s = jnp.where(visible, s, NEG_INF)
            outs.append(_softmax_pv(s, kv_ref[0, 0:w, c0 + MLA_NOPE:c0 + MLA_NOPE + MLA_V]))
        o_ref[0] = jnp.concatenate(outs, -1).astype(o_ref.dtype)

    lo = 0
    for hi, n_kv in _causal_widths(n_tiles):
        pl.when(jnp.logical_and(qi >= lo, qi < hi))(functools.partial(attend, n_kv))
        lo = hi


def _rope(x, cos, sin):
    half = x.shape[-1] // 2
    x1, x2 = x[..., :half], x[..., half:]
    return jnp.concatenate([x1 * cos - x2 * sin, x2 * cos + x1 * sin], -1)


def _mla_mixer(u, B, S, w_in, q_norm, kv_norm, w_qb, w_kvb, w_o):
    H = MLA_HEADS
    n_lat = MLA_Q_LORA + MLA_KV_LORA
    lat = _mm(u, w_in, F32, n_cols=n_lat)
    k_rope = _mm(u, w_in, F32, col0=n_lat, n_cols=V7X_LANES)[:, :MLA_ROPE]
    wq = w_qb.reshape(MLA_Q_LORA, H, MLA_NOPE + MLA_ROPE)
    wq = jnp.concatenate([wq[:, :, :MLA_NOPE].reshape(MLA_Q_LORA, H * MLA_NOPE),
                          wq[:, :, MLA_NOPE:].reshape(MLA_Q_LORA, H * MLA_ROPE)], -1)
    q = _mm(lat, wq, F32, rms_gain=q_norm, x_col=0)
    kv = _mm(lat, w_kvb, BF16, rms_gain=kv_norm, x_col=1)
    inv_freq = ROPE_THETA ** (-jnp.arange(MLA_ROPE // 2, dtype=F32) / (MLA_ROPE // 2))
    ang = jnp.arange(S, dtype=F32)[:, None] * inv_freq[None, :]
    cos, sin = jnp.cos(ang), jnp.sin(ang)
    q_nope = q[:, :H * MLA_NOPE].astype(BF16).reshape(B, S, H * MLA_NOPE)
    q_rope = _rope(q[:, H * MLA_NOPE:].reshape(B, S, H, MLA_ROPE), cos[:, None, :], sin[:, None, :])
    q_rope = q_rope.astype(BF16).reshape(B, S, H * MLA_ROPE)
    k_rope = _rope(k_rope.reshape(B, S, MLA_ROPE), cos, sin).astype(BF16)
    t = ATTN_TILE
    o = pl.pallas_call(
        functools.partial(_mla_attn_kernel, (MLA_NOPE + MLA_ROPE) ** -0.5),
        grid=(B, H // 2, S // t),
        in_specs=[pl.BlockSpec((1, t, 2 * MLA_NOPE), lambda b, h, i: (b, i, h)),
                  pl.BlockSpec((1, t, 2 * MLA_ROPE), lambda b, h, i: (b, i, h)),
                  pl.BlockSpec((1, S, 2 * (MLA_NOPE + MLA_V)), lambda b, h, i: (b, 0, h)),
                  pl.BlockSpec((1, S, MLA_ROPE), lambda b, h, i: (b, 0, 0))],
        out_specs=pl.BlockSpec((1, t, 2 * MLA_V), lambda b, h, i: (b, i, h)),
        out_shape=jax.ShapeDtypeStruct((B, S, H * MLA_V), BF16),
        compiler_params=_params("parallel", "parallel", "arbitrary"), name="mla_attention",
    )(q_nope, q_rope, kv.reshape(B, S, -1), k_rope)
    return _mm(o.reshape(B * S, H * MLA_V), w_o, F32)


def _t5_bucket(dist):
    n = jnp.maximum(dist, 0)
    max_exact = REL_BUCKETS // 2
    large = max_exact + (jnp.log(jnp.maximum(n, 1).astype(F32) / max_exact)
                         / math.log(REL_MAX_DIST / max_exact) * (REL_BUCKETS - max_exact)).astype(jnp.int32)
    large = jnp.minimum(large, REL_BUCKETS - 1)
    return jnp.where(n < max_exact, n, large)


MOBA_FAR = 2


def _moba_kernel(scale, n_sel, tab_ref, q_ref, k_ref, v_ref, bkt_ref, o_ref, kbf_ref, vbf_ref, kmean_ref, bias_ref):
    h, b, qi = pl.program_id(0), pl.program_id(1), pl.program_id(2)
    L = MOBA_BLOCK
    S = k_ref.shape[1]
    n_blk = S // L

    @pl.when(jnp.logical_and(b == 0, qi == 0))
    def _():
        for d in range(MOBA_FAR + 1):
            bucket = bkt_ref[d]
            tile = jnp.zeros((L, L), F32)
            for e in range(REL_BUCKETS):
                tile = jnp.where(bucket == e, tab_ref[e, h], tile)
            bias_ref[d] = tile

    @pl.when(qi == 0)
    def _():
        k = k_ref[0].astype(F32)
        kbf_ref[...] = k.astype(BF16)
        vbf_ref[...] = v_ref[0].astype(BF16)
        kmean_ref[...] = jnp.zeros(kmean_ref.shape, F32)
        kmean_ref[0:n_blk, :] = jnp.mean(k.reshape(n_blk, L, k.shape[-1]), axis=1)

    q = q_ref[0].astype(F32)
    qb = q.astype(BF16)
    lane = _iota((L, V7X_LANES), 1)
    gate = jnp.where(lane < qi, _dot_nt(q, kmean_ref[...], precision=HIGHEST), NEG_INF)
    causal_add = jnp.where(_iota((L, L), 1) <= _iota((L, L), 0), 0.0, NEG_INF)
    past_add = []
    for j in range(n_blk - 1):
        gj = gate[:, j:j + 1]
        beats = jnp.logical_or(gate > gj, jnp.logical_and(gate == gj, lane < j))
        in_topk = jnp.sum(jnp.where(beats, 1.0, 0.0), -1, keepdims=True) < n_sel
        past_add.append(jnp.where(jnp.logical_and(in_topk, j < qi), 0.0, NEG_INF))
    past_add.append(jnp.full((L, 1), NEG_INF, F32))

    def attend(n_kv):
        w = n_kv * L
        s = _dot_nt(qb, kbf_ref[0:w, :]) * scale
        parts = []
        for j in range(n_kv):
            bias = bias_ref[jnp.clip(qi - j, 0, MOBA_FAR)]
            parts.append(s[:, j * L:(j + 1) * L] + bias + jnp.where(j == qi, causal_add, past_add[j]))
        o_ref[0] = _softmax_pv(jnp.concatenate(parts, -1), vbf_ref[0:w, :]).astype(o_ref.dtype)

    lo = 0
    for hi, n_kv in _causal_widths(n_blk):
        pl.when(jnp.logical_and(qi >= lo, qi < hi))(functools.partial(attend, n_kv))
        lo = hi


def _moba_mixer(u, B, S, w_in, w_o, rel_bias):
    H, Dh, L = MOBA_HEADS, u.shape[1] // MOBA_HEADS, MOBA_BLOCK
    n_blk = S // L
    assert S % L == 0 and Dh == V7X_LANES and (MOBA_FAR - 1) * L >= REL_MAX_DIST
    n_sel = max(min(MOBA_TOPK, n_blk - 1), 1)
    qkv = _mm(u, w_in, BF16).reshape(B, S, 3 * H * Dh)
    qk = jnp.arange(L)[:, None] - jnp.arange(L)[None, :]
    bucket = _t5_bucket(jnp.stack([qk + d * L for d in range(MOBA_FAR + 1)])).astype(jnp.int32)
    o = pl.pallas_call(
        functools.partial(_moba_kernel, Dh ** -0.5, n_sel),
        grid=(H, B, n_blk),
        in_specs=[pl.BlockSpec(memory_space=pltpu.SMEM),
                  pl.BlockSpec((1, L, Dh), lambda h, b, i: (b, i, h)),
                  pl.BlockSpec((1, S, Dh), lambda h, b, i: (b, 0, H + h)),
                  pl.BlockSpec((1, S, Dh), lambda h, b, i: (b, 0, 2 * H + h)),
                  pl.BlockSpec((MOBA_FAR + 1, L, L), lambda h, b, i: (0, 0, 0))],
        out_specs=pl.BlockSpec((1, L, Dh), lambda h, b, i: (b, i, h)),
        out_shape=jax.ShapeDtypeStruct((B, S, H * Dh), BF16),
        scratch_shapes=[pltpu.VMEM((S, Dh), BF16), pltpu.VMEM((S, Dh), BF16), pltpu.VMEM((V7X_LANES, Dh), F32),
                        pltpu.VMEM((MOBA_FAR + 1, L, L), F32)],
        compiler_params=_params("parallel", "arbitrary", "arbitrary"), name="moba_attention",
    )(rel_bias, qkv, qkv, qkv, bucket)
    return _mm(o.reshape(B * S, H * Dh), w_o, F32)


GDN_HEAD_GROUP = 8


def _l2norm(x):
    return x * lax.rsqrt(jnp.sum(x * x, -1, keepdims=True) + L2_EPS)


def _split_bf16(x):
    hi = x.astype(BF16)
    return hi, (x - hi.astype(F32)).astype(BF16)


def _dot3(a, b):
    ah, al = _split_bf16(a)
    bh, bl = _split_bf16(b)
    return _dot(ah, bh) + (_dot(ah, bl) + _dot(al, bh))


def _unit_lower_inverse(a_lows, block):
    C = a_lows[0].shape[0]
    r = _iota((C, C), 0)
    c = _iota((C, C), 1)
    eye = jnp.where(r == c, 1.0, 0.0)
    same = (r // block) == (c // block)
    a_d = [jnp.where(same, a, 0.0) for a in a_lows]
    a_off = [a - d for a, d in zip(a_lows, a_d)]
    inv_d = [eye - d for d in a_d]
    pw = a_d
    k = 2
    while k < block:
        pw = [_dot3(p, p) for p in pw]
        inv_d = [_dot3(i, eye + p) for i, p in zip(inv_d, pw)]
        k *= 2
    n = [_dot3(i, o) for i, o in zip(inv_d, a_off)]
    inv_n = [eye - x for x in n]
    pw = n
    k = 2
    while k < C // block:
        pw = [_dot3(p, p) for p in pw]
        inv_n = [_dot3(i, eye + p) for i, p in zip(inv_n, pw)]
        k *= 2
    return [_dot3(i, d) for i, d in zip(inv_n, inv_d)]


GDN_HIST = 8


def _causal_conv_silu(x_ref, w_ref, hist_ref):
    C = x_ref.shape[1]
    hist_ref[GDN_HIST:GDN_HIST + C, :] = x_ref[0].astype(F32)
    first = GDN_HIST - (GDN_CONV - 1)
    y = sum(hist_ref[first + i:first + i + C, :] * w_ref[i:i + 1, :] for i in range(GDN_CONV))
    hist_ref[0:GDN_HIST, :] = hist_ref[C:C + GDN_HIST, :]
    return _silu(y)


def _gdn_kernel(q_ref, k_ref, v_ref, z_ref, cwq_ref, cwk_ref, cwv_ref, gc_ref, gct_ref, beta_ref, ng_ref,
                o_ref, state_ref, hq_ref, hk_ref, hv_ref):
    G, C, DK, DV = GDN_HEAD_GROUP, GDN_CHUNK, GDN_DK, GDN_DV
    rep = GDN_V_HEADS // GDN_K_HEADS
    heads = range(G)

    @pl.when(pl.program_id(2) == 0)
    def _():
        state_ref[...] = jnp.zeros(state_ref.shape, F32)
        for hist in (hq_ref, hk_ref, hv_ref):
            hist[0:GDN_HIST, :] = jnp.zeros((GDN_HIST, hist.shape[1]), F32)

    qc = _causal_conv_silu(q_ref, cwq_ref, hq_ref)
    kc = _causal_conv_silu(k_ref, cwk_ref, hk_ref)
    vc = _causal_conv_silu(v_ref, cwv_ref, hv_ref)
    r = _iota((C, C), 0)
    c = _iota((C, C), 1)
    tri = c <= r
    strict = c < r
    gc = gc_ref[0, 0]
    gct = gct_ref[0, 0, 0]
    beta = beta_ref[0, 0]
    q = [_l2norm(qc[:, i * DK:(i + 1) * DK]) * (DK ** -0.5) for i in range(G // rep)]
    k = [_l2norm(kc[:, i * DK:(i + 1) * DK]) for i in range(G // rep)]
    kb = [x.astype(BF16) for x in k]
    qk = [_dot_nt(a.astype(BF16), b) for a, b in zip(q, kb)]
    kk = [_dot_nt(b, b) for b in kb]
    g_col = [gc[:, h:h + 1] for h in heads]
    b_col = [beta[:, h:h + 1] for h in heads]
    decay = [jnp.where(tri, jnp.exp(jnp.where(tri, g_col[h] - gct[h:h + 1, :], 0.0)), 0.0) for h in heads]
    t_inv = _unit_lower_inverse([jnp.where(strict, kk[h // rep] * b_col[h] * decay[h], 0.0) for h in heads], 16)
    e_g = [jnp.exp(g) for g in g_col]
    rhs = [jnp.concatenate([vc[:, h * DV:(h + 1) * DV] * b_col[h], k[h // rep] * (b_col[h] * e_g[h])], -1)
           for h in heads]
    sol = [_dot(t_inv[h].astype(BF16), rhs[h].astype(BF16)) for h in heads]
    state = [state_ref[h] for h in heads]
    state_b = [s.astype(BF16) for s in state]
    v_new = [sol[h][:, :DV] - _dot(sol[h][:, DV:].astype(BF16), state_b[h]) for h in heads]
    v_new_b = [x.astype(BF16) for x in v_new]
    attn = [jnp.where(tri, qk[h // rep] * decay[h], 0.0).astype(BF16) for h in heads]
    o = [_dot((q[h // rep] * e_g[h]).astype(BF16), state_b[h]) + _dot(attn[h], v_new_b[h]) for h in heads]
    g_last = [g[C - 1:C, :] for g in g_col]
    k_tail = [(k[h // rep] * jnp.exp(g_last[h] - g_col[h])).astype(BF16) for h in heads]
    new_state = [state[h] * jnp.exp(g_last[h]) + _dot_tn(k_tail[h], v_new_b[h]) for h in heads]
    o = [x * lax.rsqrt(jnp.mean(x * x, -1, keepdims=True) + RMS_EPS) * ng_ref[...] for x in o]
    o = [o[h] * _silu(z_ref[0, :, h * DV:(h + 1) * DV].astype(F32)) for h in heads]
    for h in heads:
        state_ref[h] = new_state[h]
    o_ref[0] = jnp.concatenate(o, -1).astype(o_ref.dtype)


def _gdn_mixer(u, B, S, w_in, conv_w, a_log, dt_bias, norm_g, w_o):
    HK, HV, DK, DV, C, G = GDN_K_HEADS, GDN_V_HEADS, GDN_DK, GDN_DV, GDN_CHUNK, GDN_HEAD_GROUP
    qk_dim, v_dim = HK * DK, HV * DV
    n_main = 2 * qk_dim + 2 * v_dim
    proj = _mm(u, w_in, BF16, n_cols=n_main).reshape(B, S, n_main)
    ba = _mm(u, w_in, F32, col0=n_main, n_cols=V7X_LANES).reshape(B, S, -1)
    n_conv = 2 * qk_dim + v_dim
    beta = jax.nn.sigmoid(ba[:, :, :HV])
    g = -jnp.exp(a_log) * jax.nn.softplus(ba[:, :, HV:2 * HV] + dt_bias)
    N = S // C
    gc = jnp.cumsum(g.reshape(B, N, C, HV), axis=2)
    HG = HV // G
    gc_g = gc.reshape(B, N, C, HG, G).transpose(0, 3, 1, 2, 4).reshape(B, HG, S, G)
    gct_g = gc.reshape(B, N, C, HG, G).transpose(0, 3, 1, 4, 2)
    beta_g = beta.reshape(B, S, HG, G).transpose(0, 2, 1, 3)
    kw = (G // (HV // HK)) * DK
    vw = G * DV
    k_blk, v_blk, z_blk = qk_dim // kw, 2 * qk_dim // vw, n_conv // vw
    o = pl.pallas_call(
        _gdn_kernel,
        grid=(B, HG, N),
        in_specs=[pl.BlockSpec((1, C, kw), lambda b, h, n: (b, n, h)),
                  pl.BlockSpec((1, C, kw), lambda b, h, n: (b, n, k_blk + h)),
                  pl.BlockSpec((1, C, vw), lambda b, h, n: (b, n, v_blk + h)),
                  pl.BlockSpec((1, C, vw), lambda b, h, n: (b, n, z_blk + h)),
                  pl.BlockSpec((GDN_CONV, kw), lambda b, h, n: (0, h)),
                  pl.BlockSpec((GDN_CONV, kw), lambda b, h, n: (0, k_blk + h)),
                  pl.BlockSpec((GDN_CONV, vw), lambda b, h, n: (0, v_blk + h)),
                  pl.BlockSpec((1, 1, C, G), lambda b, h, n: (b, h, n, 0)),
                  pl.BlockSpec((1, 1, 1, G, C), lambda b, h, n: (b, h, n, 0, 0)),
                  pl.BlockSpec((1, 1, C, G), lambda b, h, n: (b, h, n, 0)),
                  pl.BlockSpec((1, DV), lambda b, h, n: (0, 0))],
        out_specs=pl.BlockSpec((1, C, vw), lambda b, h, n: (b, n, h)),
        out_shape=jax.ShapeDtypeStruct((B, S, v_dim), BF16),
        scratch_shapes=[pltpu.VMEM((G, DK, DV), F32), pltpu.VMEM((GDN_HIST + C, kw), F32),
                        pltpu.VMEM((GDN_HIST + C, kw), F32), pltpu.VMEM((GDN_HIST + C, vw), F32)],
        compiler_params=_params("parallel", "parallel", "arbitrary"), name="gdn_chunked",
    )(proj, proj, proj, proj, conv_w, conv_w, conv_w, gc_g, gct_g, beta_g, norm_g[None, :])
    return _mm(o.reshape(B * S, v_dim), w_o, F32)


def _gla_kernel(scale, q_ref, k_ref, v_ref, og_ref, gk_ref, wgk_ref, bgk_ref, ng_ref, o_ref, state_ref):
    C = GLA_CHUNK

    @pl.when(pl.program_id(2) == 0)
    def _():
        state_ref[...] = jnp.zeros(state_ref.shape, F32)

    x = _dot(gk_ref[0].astype(BF16), wgk_ref[...].astype(BF16)) + bgk_ref[...]
    log_alpha = (jnp.minimum(x, 0.0) - jnp.log(1.0 + jnp.exp(-jnp.abs(x)))) / GLA_GATE_NORMALIZER
    r = _iota((C, C), 0)
    c = _iota((C, C), 1)
    causal = c <= r
    b = _dot(jnp.where(causal, 1.0, 0.0), log_alpha, precision=HIGHEST)
    b_last = b[C - 1:C, :]
    q = q_ref[0].astype(F32) * scale
    k = k_ref[0].astype(F32)
    v = v_ref[0].astype(BF16)
    q_dec = (q * jnp.exp(b)).astype(BF16)
    k_inv = (k * jnp.exp(-b)).astype(BF16)
    k_tail = (k * jnp.exp(b_last - b)).astype(BF16)
    attn = jnp.where(causal, _dot_nt(q_dec, k_inv), 0.0)
    state_t = state_ref[...]
    o = _dot(attn.astype(BF16), v) + _dot_nt(q_dec, state_t.astype(BF16))
    state_ref[...] = state_t * jnp.exp(b_last) + _dot_tn(v, k_tail)
    o = o * lax.rsqrt(jnp.mean(o * o, -1, keepdims=True) + RMS_EPS) * ng_ref[...]
    o_ref[0] = (o * _silu(og_ref[0].astype(F32))).astype(o_ref.dtype)


def _gla_mixer(u, B, S, w_in, w_gk, b_gk, norm_g, w_o):
    D = u.shape[1]
    H, C = GLA_HEADS, GLA_CHUNK
    key_dim, val_dim = D // 2, D
    dk, dv = key_dim // H, val_dim // H
    n_main = 2 * key_dim + 2 * val_dim
    proj = _mm(u, w_in, BF16, n_cols=n_main).reshape(B, S, n_main)
    gk = _mm(u, w_in, F32, col0=n_main, n_cols=V7X_LANES).reshape(B, S, -1)
    wgk = jnp.pad(w_gk, ((0, gk.shape[-1] - GLA_GATE_RANK), (0, 0)))
    o = pl.pallas_call(
        functools.partial(_gla_kernel, dk ** -0.5),
        grid=(B, H, S // C),
        in_specs=[pl.BlockSpec((1, C, dk), lambda b, h, n: (b, n, h)),
                  pl.BlockSpec((1, C, dk), lambda b, h, n: (b, n, H + h)),
                  pl.BlockSpec((1, C, dv), lambda b, h, n: (b, n, 2 * key_dim // dv + h)),
                  pl.BlockSpec((1, C, dv), lambda b, h, n: (b, n, (2 * key_dim + val_dim) // dv + h)),
                  pl.BlockSpec((1, C, gk.shape[-1]), lambda b, h, n: (b, n, 0)),
                  pl.BlockSpec((gk.shape[-1], dk), lambda b, h, n: (0, h)),
                  pl.BlockSpec((1, dk), lambda b, h, n: (0, h)),
                  pl.BlockSpec((1, dv), lambda b, h, n: (0, 0))],
        out_specs=pl.BlockSpec((1, C, dv), lambda b, h, n: (b, n, h)),
        out_shape=jax.ShapeDtypeStruct((B, S, val_dim), BF16),
        scratch_shapes=[pltpu.VMEM((dv, dk), F32)],
        compiler_params=_params("parallel", "parallel", "arbitrary"), name="gla_chunked",
    )(proj, proj, proj, proj, gk, wgk, b_gk[None, :], norm_g[None, :])
    return _mm(o.reshape(B * S, val_dim), w_o, F32)


def kernel(x, c, rel_bias, mla_w_in, mla_q_norm, mla_kv_norm, mla_w_qb, mla_w_kvb, mla_w_o, gdn_w_in, gdn_conv_w, gdn_a_log, gdn_dt_bias, gdn_norm, gdn_w_o, gla_w_in, gla_w_gk, gla_b_gk, gla_norm, gla_w_o, moba_w_in, moba_w_o, ada_w, ada_b, ln_g, ln_b, router_w, router_b, moe_w_gu, moe_b_gu, moe_w_down, moe_b_down):
    B, S, D = x.shape
    assert D == D_MODEL
    depth = ada_w.shape[0]
    alpha = (2 * depth) ** 0.25
    mod = _ada_mod(c, ada_w, ada_b)
    sh_a, sc_a, g_a, sh_f, sc_f, g_f = (mod[:, :, k * D:(k + 1) * D] for k in range(6))
    xt = x.reshape(B * S, D)
    u = _modulate(xt, sc_a[0], sh_a[0], S)
    for i in range(depth):
        m, j = i % N_MIXERS, i // N_MIXERS
        if m == 0:
            h = _mla_mixer(u, B, S, mla_w_in[j], mla_q_norm[j], mla_kv_norm[j], mla_w_qb[j], mla_w_kvb[j], mla_w_o[j])
        elif m == 1:
            h = _gdn_mixer(u, B, S, gdn_w_in[j], gdn_conv_w[j], gdn_a_log[j], gdn_dt_bias[j], gdn_norm[j], gdn_w_o[j])
        elif m == 2:
            h = _gla_mixer(u, B, S, gla_w_in[j], gla_w_gk[j], gla_b_gk[j], gla_norm[j], gla_w_o[j])
        else:
            h = _moba_mixer(u, B, S, moba_w_in[j], moba_w_o[j], rel_bias)
        xt, u, top_idx, top_w = _ln_router(xt, h, g_a[i], ln_g[i, 0], ln_b[i, 0], sc_f[i], sh_f[i],
                                           router_w[i], router_b[i], S, alpha)
        y, dest = _moe_ffn(u, top_idx, i, moe_w_gu, moe_b_gu, moe_w_down, moe_b_down)
        nxt = (i + 1) % depth
        xt, u = _combine_ln(xt, y, dest, top_w, g_f[i], ln_g[i, 1], ln_b[i, 1], sc_a[nxt], sh_a[nxt], S, alpha)
    return xt.reshape(B, S, D)
```

```python
import functools
import math

import jax
import jax.numpy as jnp
from jax import lax
from jax.experimental import pallas as pl
from jax.experimental.pallas import tpu as pltpu

D_MODEL = 2048
N_MIXERS = 4
MLA_HEADS, MLA_Q_LORA, MLA_KV_LORA, MLA_NOPE, MLA_ROPE, MLA_V = 16, 512, 512, 128, 64, 128
ROPE_THETA = 10000.0
GDN_K_HEADS, GDN_V_HEADS, GDN_DK, GDN_DV, GDN_CONV, GDN_CHUNK = 16, 32, 128, 128, 4, 64
GLA_HEADS, GLA_GATE_RANK, GLA_GATE_NORMALIZER, GLA_CHUNK = 4, 16, 16.0, 64
MOBA_HEADS, MOBA_BLOCK, MOBA_TOPK = 16, 256, 3
REL_BUCKETS, REL_MAX_DIST = 32, 128
N_EXPERTS, TOP_K, EXPERT_FF = 32, 4, 768
SWIGLU_LIMIT, SWIGLU_ALPHA = 7.0, 1.702
LN_EPS, RMS_EPS, L2_EPS = 1e-5, 1e-6, 1e-6

V7X_LANES = 128
V7X_VMEM_LIMIT_BYTES = 56 * 1024 * 1024

F32 = jnp.float32
BF16 = jnp.bfloat16
HIGHEST = lax.Precision.HIGHEST
NEG_INF = float("-inf")


def _params(*sem):
    return pltpu.CompilerParams(dimension_semantics=sem, vmem_limit_bytes=V7X_VMEM_LIMIT_BYTES)


def _dot(a, b, dims=None, precision=None):
    if dims is None:
        dims = (((a.ndim - 1,), (0,)), ((), ()))
    return lax.dot_general(a, b, dims, precision=precision, preferred_element_type=F32)


def _dot_nt(a, b, precision=None):
    return _dot(a, b, (((1,), (1,)), ((), ())), precision)


def _dot_tn(a, b, precision=None):
    return _dot(a, b, (((0,), (0,)), ((), ())), precision)


def _sigmoid(x):
    return 1.0 / (1.0 + jnp.exp(-x))


def _silu(x):
    return x * _sigmoid(x)


def _iota(shape, dim):
    return lax.broadcasted_iota(jnp.int32, shape, dim)


def _mm_kernel(valid_cols, rms, x_ref, *refs):
    g_ref = refs[0] if rms else None
    w_ref, o_ref, wbf_ref = refs[-3:]

    @pl.when(pl.program_id(1) == 0)
    def _():
        w = w_ref[...]
        if valid_cols is not None:
            w = jnp.where(_iota(w.shape, 1) < valid_cols, w, 0.0)
        wbf_ref[...] = w.astype(BF16)

    x = x_ref[...]
    if rms:
        x = x.astype(F32)
        x = x * lax.rsqrt(jnp.mean(x * x, -1, keepdims=True) + RMS_EPS) * g_ref[...]
    o_ref[...] = _dot(x.astype(BF16), wbf_ref[...]).astype(o_ref.dtype)


def _mm_tiles(M, K, N):
    tm = 1024 if M % 1024 == 0 else M
    tn = N
    for cand in (1024, 512, 256, 128):
        if N % cand == 0 and K * cand * 4 <= 8 * 1024 * 1024:
            tn = cand
            break
    return tm, tn


def _mm(x, w, out_dtype, rms_gain=None, x_col=0, col0=0, n_cols=None):
    M = x.shape[0]
    K = w.shape[0]
    N = w.shape[1] if n_cols is None else n_cols
    tm, tn = _mm_tiles(M, K, N)
    assert col0 % tn == 0
    valid_cols = w.shape[1] - col0 if col0 + N > w.shape[1] else None
    assert valid_cols is None or N == tn
    c0 = col0 // tn
    x_spec = pl.BlockSpec((tm, K), lambda n, m: (m, x_col))
    w_spec = pl.BlockSpec((K, tn), lambda n, m: (0, c0 + n))
    if rms_gain is None:
        in_specs, args = [x_spec, w_spec], (x, w)
    else:
        in_specs, args = [x_spec, pl.BlockSpec((1, K), lambda n, m: (0, 0)), w_spec], (x, rms_gain[None, :], w)
    return pl.pallas_call(
        functools.partial(_mm_kernel, valid_cols, rms_gain is not None),
        grid=(N // tn, M // tm),
        in_specs=in_specs,
        out_specs=pl.BlockSpec((tm, tn), lambda n, m: (m, n)),
        out_shape=jax.ShapeDtypeStruct((M, N), out_dtype),
        scratch_shapes=[pltpu.VMEM((K, tn), BF16)],
        compiler_params=_params("parallel", "arbitrary"),
        name="proj_matmul",
    )(*args)


def _pad_cols(w, mult=V7X_LANES):
    pad = (-w.shape[-1]) % mult
    return jnp.pad(w, ((0, 0), (0, pad))) if pad else w


def _ada_kernel(c_ref, w_ref, b_ref, o_ref):
    c = _silu(c_ref[...]).astype(BF16)
    o_ref[0] = _dot(c, w_ref[0].astype(BF16)) + b_ref[0]


def _ada_mod(c, ada_w, ada_b):
    depth, D, N = ada_w.shape
    B = c.shape[0]
    rows = 8
    c_pad = jnp.pad(c, ((0, rows - B), (0, 0)))
    tn = 1024
    out = pl.pallas_call(
        _ada_kernel,
        grid=(depth, N // tn),
        in_specs=[pl.BlockSpec((rows, D), lambda i, n: (0, 0)),
                  pl.BlockSpec((1, D, tn), lambda i, n: (i, 0, n)),
                  pl.BlockSpec((1, 1, tn), lambda i, n: (i, 0, n))],
        out_specs=pl.BlockSpec((1, rows, tn), lambda i, n: (i, 0, n)),
        out_shape=jax.ShapeDtypeStruct((depth, rows, N), F32),
        compiler_params=_params("parallel", "parallel"),
        name="ada_mod",
    )(c_pad, ada_w, ada_b.reshape(depth, 1, N))
    return out[:, :B]


ROW_TILE = 256


def _modulate_kernel(x_ref, sc_ref, sh_ref, u_ref):
    u_ref[...] = (x_ref[...] * (1.0 + sc_ref[0]) + sh_ref[0]).astype(u_ref.dtype)


def _row_specs(D, S, tr):
    vec = pl.BlockSpec((1, 1, D), lambda i: ((i * tr) // S, 0, 0))
    row = pl.BlockSpec((tr, D), lambda i: (i, 0))
    par = pl.BlockSpec((1, D), lambda i: (0, 0))
    return vec, row, par


def _modulate(x, sc, sh, S):
    T, D = x.shape
    tr = ROW_TILE
    vec, row, _ = _row_specs(D, S, tr)
    return pl.pallas_call(
        _modulate_kernel, grid=(T // tr,), in_specs=[row, vec, vec], out_specs=row,
        out_shape=jax.ShapeDtypeStruct((T, D), BF16),
        compiler_params=_params("parallel"), name="modulate",
    )(x, sc[:, None, :], sh[:, None, :])


def _deepnorm(alpha, x, h, gate, g, b):
    y = alpha * x + (1.0 + gate) * h
    mu = jnp.mean(y, -1, keepdims=True)
    yc = y - mu
    var = jnp.mean(yc * yc, -1, keepdims=True)
    return yc * lax.rsqrt(var + LN_EPS) * g + b


ROW_PARTS = D_MODEL // V7X_LANES


def _store_row_tiled(ref, row0, value):
    n, d = value.shape
    w = d // ROW_PARTS
    for j in range(ROW_PARTS):
        ref[pl.ds(row0 * ROW_PARTS + j, n, stride=ROW_PARTS), :] = value[:, j * w:(j + 1) * w]


def _load_row_tiled(ref, row0, n):
    return jnp.concatenate([ref[pl.ds(row0 * ROW_PARTS + j, n, stride=ROW_PARTS), :] for j in range(ROW_PARTS)], -1)


def _ln_router_kernel(alpha, x_ref, h_ref, gate_ref, g_ref, b_ref, sc_ref, sh_ref, rw_ref, rb_ref,
                      xo_ref, u_ref, idx_ref, wgt_ref):
    xn = _deepnorm(alpha, x_ref[...], h_ref[...].astype(F32), gate_ref[0], g_ref[...], b_ref[...])
    xo_ref[...] = xn
    u = xn * (1.0 + sc_ref[0]) + sh_ref[0]
    _store_row_tiled(u_ref, 0, u)
    logits = _dot(u, rw_ref[...], precision=HIGHEST) + rb_ref[...]
    lane = _iota(logits.shape, 1)
    vals = jnp.where(lane < N_EXPERTS, logits, NEG_INF)
    top_v, top_i = [], []
    for _ in range(TOP_K):
        m = jnp.max(vals, -1, keepdims=True)
        i = jnp.min(jnp.where(vals == m, lane, V7X_LANES), -1, keepdims=True)
        top_v.append(m)
        top_i.append(i)
        vals = jnp.where(lane == i, NEG_INF, vals)
    exps = [jnp.exp(v - top_v[0]) for v in top_v]
    denom = functools.reduce(lambda a, b: a + b, exps)
    idx = jnp.zeros(logits.shape, jnp.int32)
    wgt = jnp.zeros(logits.shape, F32)
    for k in range(TOP_K):
        idx = jnp.where(lane == k, top_i[k], idx)
        wgt = jnp.where(lane == k, exps[k] / denom, wgt)
    idx_ref[...] = idx
    wgt_ref[...] = wgt


def _ln_router(x, h, gate, ln_g, ln_b, sc, sh, router_w, router_b, S, alpha):
    T, D = x.shape
    tr = ROW_TILE
    vec, row, par = _row_specs(D, S, tr)
    lane_row = pl.BlockSpec((tr, V7X_LANES), lambda i: (i, 0))
    tiled_row = pl.BlockSpec((tr * ROW_PARTS, D // ROW_PARTS), lambda i: (i, 0))
    rw = _pad_cols(router_w)
    rb = _pad_cols(router_b[None, :])
    xo, u, idx, wgt = pl.pallas_call(
        functools.partial(_ln_router_kernel, alpha), grid=(T // tr,),
        in_specs=[row, row, vec, par, par, vec, vec,
                  pl.BlockSpec((D, V7X_LANES), lambda i: (0, 0)), pl.BlockSpec((1, V7X_LANES), lambda i: (0, 0))],
        out_specs=[row, tiled_row, lane_row, lane_row],
        out_shape=[jax.ShapeDtypeStruct((T, D), F32), jax.ShapeDtypeStruct((T * ROW_PARTS, D // ROW_PARTS), F32),
                   jax.ShapeDtypeStruct((T, V7X_LANES), jnp.int32), jax.ShapeDtypeStruct((T, V7X_LANES), F32)],
        compiler_params=_params("parallel"), name="deepnorm_ln_router",
    )(x, h, gate[:, None, :], ln_g[None, :], ln_b[None, :], sc[:, None, :], sh[:, None, :], rw, rb)
    return xo, u, idx[:, :TOP_K], wgt[:, :TOP_K]


MOE_ROW_TILE = 512


def _expert_changed(te_ref):
    t = pl.program_id(0)
    return jnp.logical_or(t == 0, te_ref[t] != te_ref[jnp.maximum(t - 1, 0)])


def _moe_gu_kernel(te_ref, tv_ref, x_ref, w_ref, b_ref, h_ref, wbf_ref):
    @pl.when(_expert_changed(te_ref))
    def _():
        wbf_ref[...] = w_ref[0].astype(BF16)

    @pl.when(tv_ref[pl.program_id(0)] > 0)
    def _():
        x = _load_row_tiled(x_ref, 0, x_ref.shape[0] // ROW_PARTS).astype(BF16)
        gu = _dot(x, wbf_ref[...]) + b_ref[0]
        gl = jnp.minimum(gu[:, :EXPERT_FF], SWIGLU_LIMIT)
        up = jnp.clip(gu[:, EXPERT_FF:], -SWIGLU_LIMIT, SWIGLU_LIMIT)
        h_ref[...] = ((up + 1.0) * gl * _sigmoid(gl * SWIGLU_ALPHA)).astype(h_ref.dtype)

    @pl.when(tv_ref[pl.program_id(0)] == 0)
    def _():
        h_ref[...] = jnp.zeros(h_ref.shape, h_ref.dtype)


def _moe_down_kernel(te_ref, tv_ref, h_ref, w_ref, b_ref, y_ref, wbf_ref):
    @pl.when(_expert_changed(te_ref))
    def _():
        wbf_ref[...] = w_ref[0].astype(BF16)

    @pl.when(tv_ref[pl.program_id(0)] > 0)
    def _():
        _store_row_tiled(y_ref, 0, _dot(h_ref[...], wbf_ref[...]) + b_ref[0])

    @pl.when(tv_ref[pl.program_id(0)] == 0)
    def _():
        y_ref[...] = jnp.zeros(y_ref.shape, y_ref.dtype)


def _route_metadata(top_idx, tm):
    T = top_idx.shape[0]
    P = T * TOP_K
    n_tiles = (P + N_EXPERTS * (tm - 1)) // tm
    e_flat = top_idx.reshape(P)
    onehot = (e_flat[:, None] == jnp.arange(N_EXPERTS)[None, :]).astype(jnp.int32)
    csum = jnp.cumsum(onehot, axis=0)
    counts = csum[-1]
    rank = jnp.sum(csum * onehot, axis=1) - 1
    padded = ((counts + tm - 1) // tm) * tm
    ends_p = jnp.cumsum(padded)
    starts_p = ends_p - padded
    dest = (starts_p[e_flat] + rank).astype(jnp.int32)
    tile_start = jnp.arange(n_tiles, dtype=jnp.int32) * tm
    tile_valid = (tile_start < ends_p[-1]).astype(jnp.int32)
    tile_expert = jnp.sum((tile_start[:, None] >= ends_p[None, :]).astype(jnp.int32), axis=1)
    last_expert = jnp.max(jnp.where(counts > 0, jnp.arange(N_EXPERTS), 0))
    tile_expert = jnp.where(tile_valid > 0, tile_expert, last_expert).astype(jnp.int32)
    return dest, tile_expert, tile_valid, n_tiles


DISPATCH_ROWS = 2048


def _dispatch_kernel(dest_ref, u_ref, init_hbm, xs_hbm, sem):
    del init_hbm
    R = dest_ref.shape[-1]

    def row(ref, i):
        return ref.at[pl.ds(pl.multiple_of(i * ROW_PARTS, ROW_PARTS), ROW_PARTS)]

    def issue(t, carry):
        for k in range(TOP_K):
            pltpu.make_async_copy(row(u_ref, t), row(xs_hbm, dest_ref[0, 0, t * TOP_K + k]), sem).start()
        return carry

    lax.fori_loop(0, R // TOP_K, issue, 0, unroll=2)
    n = (R // TOP_K) * ROW_PARTS
    for _ in range(TOP_K):
        pltpu.make_async_copy(u_ref, xs_hbm.at[pl.ds(0, n)], sem).wait()


def _dispatch(u, dest, rows):
    P = dest.shape[0]
    R = DISPATCH_ROWS
    shape = (rows * ROW_PARTS, u.shape[1])
    return pl.pallas_call(
        _dispatch_kernel,
        grid=(P // R,),
        in_specs=[pl.BlockSpec((1, 1, R), lambda s: (s, 0, 0), memory_space=pltpu.SMEM),
                  pl.BlockSpec(((R // TOP_K) * ROW_PARTS, u.shape[1]), lambda s: (s, 0)),
                  pl.BlockSpec(memory_space=pl.ANY)],
        out_specs=pl.BlockSpec(memory_space=pl.ANY),
        out_shape=jax.ShapeDtypeStruct(shape, u.dtype),
        scratch_shapes=[pltpu.SemaphoreType.DMA(())],
        input_output_aliases={2: 0},
        compiler_params=_params("arbitrary"), name="moe_dispatch",
    )(dest.reshape(P // R, 1, R), u, jnp.zeros(shape, u.dtype))


def _moe_ffn(u, top_idx, layer, w_gu, b_gu, w_down, b_down):
    D = u.shape[1] * ROW_PARTS
    tm = MOE_ROW_TILE
    dest, tile_expert, tile_valid, n_tiles = _route_metadata(top_idx, tm)
    tile_expert = tile_expert + layer * N_EXPERTS
    w_gu, w_down = (w.reshape((-1,) + w.shape[2:]) for w in (w_gu, w_down))
    b_gu, b_down = (b.reshape(-1, b.shape[-1]) for b in (b_gu, b_down))
    rows = n_tiles * tm
    x_sorted = _dispatch(u, dest, rows)
    ff2 = 2 * EXPERT_FF
    tiled = pl.BlockSpec((tm * ROW_PARTS, D // ROW_PARTS), lambda t, te, tv: (t, 0))
    h = pl.pallas_call(
        _moe_gu_kernel,
        grid_spec=pltpu.PrefetchScalarGridSpec(
            num_scalar_prefetch=2, grid=(n_tiles,),
            in_specs=[tiled,
                      pl.BlockSpec((1, D, ff2), lambda t, te, tv: (te[t], 0, 0)),
                      pl.BlockSpec((1, 1, ff2), lambda t, te, tv: (te[t], 0, 0))],
            out_specs=pl.BlockSpec((tm, EXPERT_FF), lambda t, te, tv: (t, 0)),
            scratch_shapes=[pltpu.VMEM((D, ff2), BF16)]),
        out_shape=jax.ShapeDtypeStruct((rows, EXPERT_FF), BF16),
        compiler_params=_params("arbitrary"), name="moe_gate_up",
    )(tile_expert, tile_valid, x_sorted, w_gu, b_gu[:, None, :])
    y = pl.pallas_call(
        _moe_down_kernel,
        grid_spec=pltpu.PrefetchScalarGridSpec(
            num_scalar_prefetch=2, grid=(n_tiles,),
            in_specs=[pl.BlockSpec((tm, EXPERT_FF), lambda t, te, tv: (t, 0)),
                      pl.BlockSpec((1, EXPERT_FF, D), lambda t, te, tv: (te[t], 0, 0)),
                      pl.BlockSpec((1, 1, D), lambda t, te, tv: (te[t], 0, 0))],
            out_specs=tiled,
            scratch_shapes=[pltpu.VMEM((EXPERT_FF, D), BF16)]),
        out_shape=jax.ShapeDtypeStruct((rows * ROW_PARTS, D // ROW_PARTS), F32),
        compiler_params=_params("arbitrary"), name="moe_down",
    )(tile_expert, tile_valid, h, w_down, b_down[:, None, :])
    return y, dest


COMBINE_TILE = 256
COMBINE_SUB = 64


def _combine_ln_kernel(alpha, dest_ref, y_hbm, x_ref, wgt_ref, gate_ref, g_ref, b_ref, sc_ref, sh_ref,
                       xo_ref, u_ref, buf_ref, sems):
    n_sub = COMBINE_TILE // COMBINE_SUB

    def row(ref, i):
        return ref.at[pl.ds(pl.multiple_of(i * ROW_PARTS, ROW_PARTS), ROW_PARTS)]

    for j in range(n_sub):
        def issue(t, carry, j=j):
            tok = j * COMBINE_SUB + t
            for k in range(TOP_K):
                pltpu.make_async_copy(row(y_hbm, dest_ref[0, 0, tok * TOP_K + k]),
                                      row(buf_ref, k * COMBINE_TILE + tok), sems.at[j]).start()
            return carry

        lax.fori_loop(0, COMBINE_SUB, issue, 0, unroll=2)
    for j in range(n_sub):
        lo = j * COMBINE_SUB
        n = COMBINE_SUB * TOP_K * ROW_PARTS
        pltpu.make_async_copy(y_hbm.at[pl.ds(0, n)], buf_ref.at[pl.ds(0, n)], sems.at[j]).wait()
        wgt = wgt_ref[pl.ds(lo, COMBINE_SUB), :]
        f = sum(_load_row_tiled(buf_ref, k * COMBINE_TILE + lo, COMBINE_SUB) * wgt[:, k:k + 1] for k in range(TOP_K))
        xn = _deepnorm(alpha, x_ref[pl.ds(lo, COMBINE_SUB), :], f, gate_ref[0], g_ref[...], b_ref[...])
        xo_ref[pl.ds(lo, COMBINE_SUB), :] = xn
        u_ref[pl.ds(lo, COMBINE_SUB), :] = (xn * (1.0 + sc_ref[0]) + sh_ref[0]).astype(u_ref.dtype)


def _combine_ln(x, y, dest, top_w, gate, ln_g, ln_b, sc, sh, S, alpha):
    T, D = x.shape
    tr = COMBINE_TILE
    vec, row, par = _row_specs(D, S, tr)
    n_sub = tr // COMBINE_SUB
    return pl.pallas_call(
        functools.partial(_combine_ln_kernel, alpha), grid=(T // tr,),
        in_specs=[pl.BlockSpec((1, 1, tr * TOP_K), lambda i: (i, 0, 0), memory_space=pltpu.SMEM),
                  pl.BlockSpec(memory_space=pl.ANY), row,
                  pl.BlockSpec((tr, TOP_K), lambda i: (i, 0)), vec, par, par, vec, vec],
        out_specs=[row, row],
        out_shape=[jax.ShapeDtypeStruct((T, D), F32), jax.ShapeDtypeStruct((T, D), BF16)],
        scratch_shapes=[pltpu.VMEM((tr * TOP_K * ROW_PARTS, D // ROW_PARTS), F32), pltpu.SemaphoreType.DMA((n_sub,))],
        compiler_params=_params("arbitrary"), name="moe_combine_ln",
    )(dest.reshape(T // tr, 1, tr * TOP_K), y, x, top_w, gate[:, None, :], ln_g[None, :], ln_b[None, :],
      sc[:, None, :], sh[:, None, :])


ATTN_TILE = 256


def _softmax_pv(s, v):
    p = jnp.exp(s - jnp.max(s, -1, keepdims=True))
    return _dot(p.astype(BF16), v) / jnp.sum(p, -1, keepdims=True)


def _causal_widths(n_tiles):
    half = n_tiles // 2
    return [(half, half), (n_tiles, n_tiles)] if half > 0 else [(n_tiles, n_tiles)]


def _mla_attn_kernel(scale, qn_ref, qr_ref, kv_ref, kr_ref, o_ref):
    qi = pl.program_id(2)
    t = ATTN_TILE
    n_tiles = kv_ref.shape[1] // t

    def attend(n_kv):
        w = n_kv * t
        visible = _iota((t, w), 1) <= _iota((t, w), 0) + qi * t
        outs = []
        for hh in range(2):
            qn = qn_ref[0, :, hh * MLA_NOPE:(hh + 1) * MLA_NOPE]
            qr = qr_ref[0, :, hh * MLA_ROPE:(hh + 1) * MLA_ROPE]
            c0 = hh * (MLA_NOPE + MLA_V)
            s = (_dot_nt(qn, kv_ref[0, 0:w, c0:c0 + MLA_NOPE]) + _dot_nt(qr, kr_ref[0, 0:w, :])) * scale
            s = jnp.where(visible, s, NEG_INF)
            outs.append(_softmax_pv(s, kv_ref[0, 0:w, c0 + MLA_NOPE:c0 + MLA_NOPE + MLA_V]))
        o_ref[0] = jnp.concatenate(outs, -1).astype(o_ref.dtype)

    lo = 0
    for hi, n_kv in _causal_widths(n_tiles):
        pl.when(jnp.logical_and(qi >= lo, qi < hi))(functools.partial(attend, n_kv))
        lo = hi


def _rope(x, cos, sin):
    half = x.shape[-1] // 2
    x1, x2 = x[..., :half], x[..., half:]
    return jnp.concatenate([x1 * cos - x2 * sin, x2 * cos + x1 * sin], -1)


def _mla_mixer(u, B, S, w_in, q_norm, kv_norm, w_qb, w_kvb, w_o):
    H = MLA_HEADS
    n_lat = MLA_Q_LORA + MLA_KV_LORA
    lat = _mm(u, w_in, F32, n_cols=n_lat)
    k_rope = _mm(u, w_in, F32, col0=n_lat, n_cols=V7X_LANES)[:, :MLA_ROPE]
    wq = w_qb.reshape(MLA_Q_LORA, H, MLA_NOPE + MLA_ROPE)
    wq = jnp.concatenate([wq[:, :, :MLA_NOPE].reshape(MLA_Q_LORA, H * MLA_NOPE),
                          wq[:, :, MLA_NOPE:].reshape(MLA_Q_LORA, H * MLA_ROPE)], -1)
    q = _mm(lat, wq, F32, rms_gain=q_norm, x_col=0)
    kv = _mm(lat, w_kvb, BF16, rms_gain=kv_norm, x_col=1)
    inv_freq = ROPE_THETA ** (-jnp.arange(MLA_ROPE // 2, dtype=F32) / (MLA_ROPE // 2))
    ang = jnp.arange(S, dtype=F32)[:, None] * inv_freq[None, :]
    cos, sin = jnp.cos(ang), jnp.sin(ang)
    q_nope = q[:, :H * MLA_NOPE].astype(BF16).reshape(B, S, H * MLA_NOPE)
    q_rope = _rope(q[:, H * MLA_NOPE:].reshape(B, S, H, MLA_ROPE), cos[:, None, :], sin[:, None, :])
    q_rope = q_rope.astype(BF16).reshape(B, S, H * MLA_ROPE)
    k_rope = _rope(k_rope.reshape(B, S, MLA_ROPE), cos, sin).astype(BF16)
    t = ATTN_TILE
    o = pl.pallas_call(
        functools.partial(_mla_attn_kernel, (MLA_NOPE + MLA_ROPE) ** -0.5),
        grid=(B, H // 2, S // t),
        in_specs=[pl.BlockSpec((1, t, 2 * MLA_NOPE), lambda b, h, i: (b, i, h)),
                  pl.BlockSpec((1, t, 2 * MLA_ROPE), lambda b, h, i: (b, i, h)),
                  pl.BlockSpec((1, S, 2 * (MLA_NOPE + MLA_V)), lambda b, h, i: (b, 0, h)),
                  pl.BlockSpec((1, S, MLA_ROPE), lambda b, h, i: (b, 0, 0))],
        out_specs=pl.BlockSpec((1, t, 2 * MLA_V), lambda b, h, i: (b, i, h)),
        out_shape=jax.ShapeDtypeStruct((B, S, H * MLA_V), BF16),
        compiler_params=_params("parallel", "parallel", "arbitrary"), name="mla_attention",
    )(q_nope, q_rope, kv.reshape(B, S, -1), k_rope)
    return _mm(o.reshape(B * S, H * MLA_V), w_o, F32)


def _t5_bucket(dist):
    n = jnp.maximum(dist, 0)
    max_exact = REL_BUCKETS // 2
    large = max_exact + (jnp.log(jnp.maximum(n, 1).astype(F32) / max_exact)
                         / math.log(REL_MAX_DIST / max_exact) * (REL_BUCKETS - max_exact)).astype(jnp.int32)
    large = jnp.minimum(large, REL_BUCKETS - 1)
    return jnp.where(n < max_exact, n, large)


MOBA_FAR = 2


def _moba_kernel(scale, n_sel, tab_ref, q_ref, k_ref, v_ref, bkt_ref, o_ref, kbf_ref, vbf_ref, kmean_ref, bias_ref):
    h, b, qi = pl.program_id(0), pl.program_id(1), pl.program_id(2)
    L = MOBA_BLOCK
    S = k_ref.shape[1]
    n_blk = S // L

    @pl.when(jnp.logical_and(b == 0, qi == 0))
    def _():
        for d in range(MOBA_FAR + 1):
            bucket = bkt_ref[d]
            tile = jnp.zeros((L, L), F32)
            for e in range(REL_BUCKETS):
                tile = jnp.where(bucket == e, tab_ref[e, h], tile)
            bias_ref[d] = tile

    @pl.when(qi == 0)
    def _():
        k = k_ref[0].astype(F32)
        kbf_ref[...] = k.astype(BF16)
        vbf_ref[...] = v_ref[0].astype(BF16)
        kmean_ref[...] = jnp.zeros(kmean_ref.shape, F32)
        kmean_ref[0:n_blk, :] = jnp.mean(k.reshape(n_blk, L, k.shape[-1]), axis=1)

    q = q_ref[0].astype(F32)
    qb = q.astype(BF16)
    nb = kmean_ref.shape[0]
    blk = _iota((nb, L), 0)
    gate = jnp.where(blk < qi, _dot_nt(kmean_ref[...], q, precision=HIGHEST), NEG_INF)
    picked = jnp.zeros((nb, L), F32)
    for j in range(n_blk - 1):
        gj = gate[j:j + 1, :]
        beats = jnp.logical_or(gate > gj, jnp.logical_and(gate == gj, blk < j))
        in_topk = jnp.sum(jnp.where(beats, 1.0, 0.0), 0, keepdims=True) < n_sel
        picked = jnp.where(blk == j, jnp.where(jnp.logical_and(in_topk, j < qi), 1.0, 0.0), picked)
    picked = _dot_tn(picked, jnp.where(_iota((nb, nb), 0) == _iota((nb, nb), 1), 1.0, 0.0))
    causal_add = jnp.where(_iota((L, L), 1) <= _iota((L, L), 0), 0.0, NEG_INF)
    past_add = [jnp.where(picked[:, j:j + 1] > 0.5, 0.0, NEG_INF) for j in range(n_blk)]

    def attend(n_kv):
        w = n_kv * L
        s = _dot_nt(qb, kbf_ref[0:w, :]) * scale
        parts = []
        for j in range(n_kv):
            bias = bias_ref[jnp.clip(qi - j, 0, MOBA_FAR)]
            parts.append(s[:, j * L:(j + 1) * L] + bias + jnp.where(j == qi, causal_add, past_add[j]))
        o_ref[0] = _softmax_pv(jnp.concatenate(parts, -1), vbf_ref[0:w, :]).astype(o_ref.dtype)

    lo = 0
    for hi, n_kv in _causal_widths(n_blk):
        pl.when(jnp.logical_and(qi >= lo, qi < hi))(functools.partial(attend, n_kv))
        lo = hi


def _moba_mixer(u, B, S, w_in, w_o, rel_bias):
    H, Dh, L = MOBA_HEADS, u.shape[1] // MOBA_HEADS, MOBA_BLOCK
    n_blk = S // L
    assert S % L == 0 and Dh == V7X_LANES and (MOBA_FAR - 1) * L >= REL_MAX_DIST
    n_sel = max(min(MOBA_TOPK, n_blk - 1), 1)
    qkv = _mm(u, w_in, BF16).reshape(B, S, 3 * H * Dh)
    qk = jnp.arange(L)[:, None] - jnp.arange(L)[None, :]
    bucket = _t5_bucket(jnp.stack([qk + d * L for d in range(MOBA_FAR + 1)])).astype(jnp.int32)
    o = pl.pallas_call(
        functools.partial(_moba_kernel, Dh ** -0.5, n_sel),
        grid=(H, B, n_blk),
        in_specs=[pl.BlockSpec(memory_space=pltpu.SMEM),
                  pl.BlockSpec((1, L, Dh), lambda h, b, i: (b, i, h)),
                  pl.BlockSpec((1, S, Dh), lambda h, b, i: (b, 0, H + h)),
                  pl.BlockSpec((1, S, Dh), lambda h, b, i: (b, 0, 2 * H + h)),
                  pl.BlockSpec((MOBA_FAR + 1, L, L), lambda h, b, i: (0, 0, 0))],
        out_specs=pl.BlockSpec((1, L, Dh), lambda h, b, i: (b, i, h)),
        out_shape=jax.ShapeDtypeStruct((B, S, H * Dh), BF16),
        scratch_shapes=[pltpu.VMEM((S, Dh), BF16), pltpu.VMEM((S, Dh), BF16),
                        pltpu.VMEM((-(-n_blk // 8) * 8, Dh), F32),
                        pltpu.VMEM((MOBA_FAR + 1, L, L), F32)],
        compiler_params=_params("parallel", "arbitrary", "arbitrary"), name="moba_attention",
    )(rel_bias, qkv, qkv, qkv, bucket)
    return _mm(o.reshape(B * S, H * Dh), w_o, F32)


GDN_HEAD_GROUP = 16


def _lane_sum(x):
    return _dot(x.astype(BF16), jnp.ones((x.shape[1], x.shape[1]), BF16))


def _l2norm(x):
    return x * lax.rsqrt(_lane_sum(x * x) + L2_EPS)


def _dotb(a, b):
    return _dot(a.astype(BF16), b.astype(BF16))


def _unit_lower_inverse(a_lows, block):
    C = a_lows[0].shape[0]
    r = _iota((C, C), 0)
    c = _iota((C, C), 1)
    eye = jnp.where(r == c, 1.0, 0.0)
    same = (r // block) == (c // block)
    a_d = [jnp.where(same, a, 0.0) for a in a_lows]
    a_off = [a - d for a, d in zip(a_lows, a_d)]
    inv_d = [eye - d for d in a_d]
    pw = a_d
    k = 2
    while k < block:
        pw = [_dotb(p, p) for p in pw]
        inv_d = [_dotb(i, eye + p) for i, p in zip(inv_d, pw)]
        k *= 2
    n = [_dotb(i, o) for i, o in zip(inv_d, a_off)]
    inv_n = [eye - x for x in n]
    pw = n
    k = 2
    while k < C // block:
        pw = [_dotb(p, p) for p in pw]
        inv_n = [_dotb(i, eye + p) for i, p in zip(inv_n, pw)]
        k *= 2
    return [_dotb(i, d) for i, d in zip(inv_n, inv_d)]


GDN_HIST = 16


def _causal_conv_silu(x_ref, w_ref, hist_ref):
    C = x_ref.shape[1]
    x = x_ref[0]
    hist_ref[GDN_HIST:GDN_HIST + C, :] = x
    hist = hist_ref[...]
    t = _iota((C, GDN_HIST + C), 0)
    r = _iota((C, GDN_HIST + C), 1)
    y = x.astype(F32) * w_ref[GDN_CONV - 1:GDN_CONV, :]
    for i in range(GDN_CONV - 1):
        shift = jnp.where(r == t + (GDN_HIST - (GDN_CONV - 1) + i), 1.0, 0.0).astype(BF16)
        y = y + _dot(shift, hist) * w_ref[i:i + 1, :]
    hist_ref[0:GDN_HIST, :] = hist_ref[C:C + GDN_HIST, :]
    return _silu(y)


def _gdn_kernel(q_ref, k_ref, v_ref, z_ref, cwq_ref, cwk_ref, cwv_ref, gc_ref, gct_ref, beta_ref, ng_ref,
                o_ref, state_ref, hq_ref, hk_ref, hv_ref):
    G, C, DK, DV = GDN_HEAD_GROUP, GDN_CHUNK, GDN_DK, GDN_DV
    rep = GDN_V_HEADS // GDN_K_HEADS
    heads = range(G)

    @pl.when(pl.program_id(2) == 0)
    def _():
        state_ref[...] = jnp.zeros(state_ref.shape, F32)
        for hist in (hq_ref, hk_ref, hv_ref):
            hist[0:GDN_HIST, :] = jnp.zeros((GDN_HIST, hist.shape[1]), hist.dtype)

    qc = _causal_conv_silu(q_ref, cwq_ref, hq_ref)
    kc = _causal_conv_silu(k_ref, cwk_ref, hk_ref)
    vc = _causal_conv_silu(v_ref, cwv_ref, hv_ref)
    r = _iota((C, C), 0)
    c = _iota((C, C), 1)
    tri = c <= r
    strict = c < r
    gc = gc_ref[0, 0]
    gct = gct_ref[0, 0, 0]
    beta = beta_ref[0, 0]
    q = [_l2norm(qc[:, i * DK:(i + 1) * DK]) * (DK ** -0.5) for i in range(G // rep)]
    k = [_l2norm(kc[:, i * DK:(i + 1) * DK]) for i in range(G // rep)]
    assert DK == V7X_LANES and DV == V7X_LANES
    kb = [x.astype(BF16) for x in k]
    qk = [_dot_nt(a.astype(BF16), b) for a, b in zip(q, kb)]
    kk = [_dot_nt(b, b) for b in kb]
    g_col = [gc[:, h:h + 1] for h in heads]
    b_col = [beta[:, h:h + 1] for h in heads]
    decay = [jnp.where(tri, jnp.exp(jnp.where(tri, g_col[h] - gct[h:h + 1, :], 0.0)), 0.0) for h in heads]
    t_inv = _unit_lower_inverse([jnp.where(strict, kk[h // rep] * b_col[h] * decay[h], 0.0) for h in heads], 16)
    e_g = [jnp.exp(g) for g in g_col]
    rhs = [jnp.concatenate([vc[:, h * DV:(h + 1) * DV] * b_col[h], k[h // rep] * (b_col[h] * e_g[h])], -1)
           for h in heads]
    sol = [_dot(t_inv[h].astype(BF16), rhs[h].astype(BF16)) for h in heads]
    state = [state_ref[h] for h in heads]
    state_b = [s.astype(BF16) for s in state]
    v_new = [sol[h][:, :DV] - _dot(sol[h][:, DV:].astype(BF16), state_b[h]) for h in heads]
    v_new_b = [x.astype(BF16) for x in v_new]
    attn = [jnp.where(tri, qk[h // rep] * decay[h], 0.0).astype(BF16) for h in heads]
    o = [_dot((q[h // rep] * e_g[h]).astype(BF16), state_b[h]) + _dot(attn[h], v_new_b[h]) for h in heads]
    g_last = [g[C - 1:C, :] for g in g_col]
    k_tail = [(k[h // rep] * jnp.exp(g_last[h] - g_col[h])).astype(BF16) for h in heads]
    new_state = [state[h] * jnp.exp(g_last[h]) + _dot_tn(k_tail[h], v_new_b[h]) for h in heads]
    o = [x * lax.rsqrt(_lane_sum(x * x) * (1.0 / DV) + RMS_EPS) * ng_ref[...] for x in o]
    o = [o[h] * _silu(z_ref[0, :, h * DV:(h + 1) * DV].astype(F32)) for h in heads]
    for h in heads:
        state_ref[h] = new_state[h]
    o_ref[0] = jnp.concatenate(o, -1).astype(o_ref.dtype)


def _gdn_mixer(u, B, S, w_in, conv_w, a_log, dt_bias, norm_g, w_o):
    HK, HV, DK, DV, C, G = GDN_K_HEADS, GDN_V_HEADS, GDN_DK, GDN_DV, GDN_CHUNK, GDN_HEAD_GROUP
    qk_dim, v_dim = HK * DK, HV * DV
    n_main = 2 * qk_dim + 2 * v_dim
    proj = _mm(u, w_in, BF16, n_cols=n_main).reshape(B, S, n_main)
    ba = _mm(u, w_in, F32, col0=n_main, n_cols=V7X_LANES).reshape(B, S, -1)
    n_conv = 2 * qk_dim + v_dim
    beta = jax.nn.sigmoid(ba[:, :, :HV])
    g = -jnp.exp(a_log) * jax.nn.softplus(ba[:, :, HV:2 * HV] + dt_bias)
    N = S // C
    gc = jnp.cumsum(g.reshape(B, N, C, HV), axis=2)
    HG = HV // G
    gc_g = gc.reshape(B, N, C, HG, G).transpose(0, 3, 1, 2, 4).reshape(B, HG, S, G)
    gct_g = gc.reshape(B, N, C, HG, G).transpose(0, 3, 1, 4, 2)
    beta_g = beta.reshape(B, S, HG, G).transpose(0, 2, 1, 3)
    kw = (G // (HV // HK)) * DK
    vw = G * DV
    k_blk, v_blk, z_blk = qk_dim // kw, 2 * qk_dim // vw, n_conv // vw
    o = pl.pallas_call(
        _gdn_kernel,
        grid=(B, HG, N),
        in_specs=[pl.BlockSpec((1, C, kw), lambda b, h, n: (b, n, h)),
                  pl.BlockSpec((1, C, kw), lambda b, h, n: (b, n, k_blk + h)),
                  pl.BlockSpec((1, C, vw), lambda b, h, n: (b, n, v_blk + h)),
                  pl.BlockSpec((1, C, vw), lambda b, h, n: (b, n, z_blk + h)),
                  pl.BlockSpec((GDN_CONV, kw), lambda b, h, n: (0, h)),
                  pl.BlockSpec((GDN_CONV, kw), lambda b, h, n: (0, k_blk + h)),
                  pl.BlockSpec((GDN_CONV, vw), lambda b, h, n: (0, v_blk + h)),
                  pl.BlockSpec((1, 1, C, G), lambda b, h, n: (b, h, n, 0)),
                  pl.BlockSpec((1, 1, 1, G, C), lambda b, h, n: (b, h, n, 0, 0)),
                  pl.BlockSpec((1, 1, C, G), lambda b, h, n: (b, h, n, 0)),
                  pl.BlockSpec((1, DV), lambda b, h, n: (0, 0))],
        out_specs=pl.BlockSpec((1, C, vw), lambda b, h, n: (b, n, h)),
        out_shape=jax.ShapeDtypeStruct((B, S, v_dim), BF16),
        scratch_shapes=[pltpu.VMEM((G, DK, DV), F32), pltpu.VMEM((GDN_HIST + C, kw), BF16),
                        pltpu.VMEM((GDN_HIST + C, kw), BF16), pltpu.VMEM((GDN_HIST + C, vw), BF16)],
        compiler_params=_params("parallel", "parallel", "arbitrary"), name="gdn_chunked",
    )(proj, proj, proj, proj, conv_w, conv_w, conv_w, gc_g, gct_g, beta_g, norm_g[None, :])
    return _mm(o.reshape(B * S, v_dim), w_o, F32)


def _gla_kernel(scale, q_ref, k_ref, v_ref, og_ref, gk_ref, wgk_ref, bgk_ref, ng_ref, o_ref, state_ref):
    C = GLA_CHUNK
    H, dv, dk = state_ref.shape
    heads = range(H)

    @pl.when(pl.program_id(1) == 0)
    def _():
        state_ref[...] = jnp.zeros(state_ref.shape, F32)

    x = _dot(gk_ref[0].astype(BF16), wgk_ref[...].astype(BF16)) + bgk_ref[...]
    log_alpha = (jnp.minimum(x, 0.0) - jnp.log(1.0 + jnp.exp(-jnp.abs(x)))) / GLA_GATE_NORMALIZER
    r = _iota((C, C), 0)
    c = _iota((C, C), 1)
    causal = c <= r
    b = _dot(jnp.where(causal, 1.0, 0.0), log_alpha, precision=HIGHEST)
    b_last = b[C - 1:C, :]
    q = q_ref[0].astype(F32) * scale
    k = k_ref[0].astype(F32)
    q_dec = (q * jnp.exp(b)).astype(BF16)
    k_inv = (k * jnp.exp(-b)).astype(BF16)
    k_tail = (k * jnp.exp(b_last - b)).astype(BF16)
    decay = jnp.exp(b_last)
    ks = [slice(h * dk, (h + 1) * dk) for h in heads]
    v = [v_ref[0, :, h * dv:(h + 1) * dv] for h in heads]
    attn = [jnp.where(causal, _dot_nt(q_dec[:, ks[h]], k_inv[:, ks[h]]), 0.0).astype(BF16) for h in heads]
    state_t = [state_ref[h] for h in heads]
    o = [_dot(attn[h], v[h]) + _dot_nt(q_dec[:, ks[h]], state_t[h].astype(BF16)) for h in heads]
    new_state = [state_t[h] * decay[:, ks[h]] + _dot_tn(v[h], k_tail[:, ks[h]]) for h in heads]
    o = [x * lax.rsqrt(jnp.mean(x * x, -1, keepdims=True) + RMS_EPS) * ng_ref[...] for x in o]
    for h in heads:
        state_ref[h] = new_state[h]
    o_ref[0] = (jnp.concatenate(o, -1) * _silu(og_ref[0].astype(F32))).astype(o_ref.dtype)


def _gla_mixer(u, B, S, w_in, w_gk, b_gk, norm_g, w_o):
    D = u.shape[1]
    H, C = GLA_HEADS, GLA_CHUNK
    key_dim, val_dim = D // 2, D
    dk, dv = key_dim // H, val_dim // H
    n_main = 2 * key_dim + 2 * val_dim
    proj = _mm(u, w_in, BF16, n_cols=n_main).reshape(B, S, n_main)
    gk = _mm(u, w_in, F32, col0=n_main, n_cols=V7X_LANES).reshape(B, S, -1)
    wgk = jnp.pad(w_gk, ((0, gk.shape[-1] - GLA_GATE_RANK), (0, 0)))
    o = pl.pallas_call(
        functools.partial(_gla_kernel, dk ** -0.5),
        grid=(B, S // C),
        in_specs=[pl.BlockSpec((1, C, key_dim), lambda b, n: (b, n, 0)),
                  pl.BlockSpec((1, C, key_dim), lambda b, n: (b, n, 1)),
                  pl.BlockSpec((1, C, val_dim), lambda b, n: (b, n, 2 * key_dim // val_dim)),
                  pl.BlockSpec((1, C, val_dim), lambda b, n: (b, n, 2 * key_dim // val_dim + 1)),
                  pl.BlockSpec((1, C, gk.shape[-1]), lambda b, n: (b, n, 0)),
                  pl.BlockSpec((gk.shape[-1], key_dim), lambda b, n: (0, 0)),
                  pl.BlockSpec((1, key_dim), lambda b, n: (0, 0)),
                  pl.BlockSpec((1, dv), lambda b, n: (0, 0))],
        out_specs=pl.BlockSpec((1, C, val_dim), lambda b, n: (b, n, 0)),
        out_shape=jax.ShapeDtypeStruct((B, S, val_dim), BF16),
        scratch_shapes=[pltpu.VMEM((H, dv, dk), F32)],
        compiler_params=_params("parallel", "arbitrary"), name="gla_chunked",
    )(proj, proj, proj, proj, gk, wgk, b_gk[None, :], norm_g[None, :])
    return _mm(o.reshape(B * S, val_dim), w_o, F32)


def kernel(x, c, rel_bias, mla_w_in, mla_q_norm, mla_kv_norm, mla_w_qb, mla_w_kvb, mla_w_o, gdn_w_in, gdn_conv_w, gdn_a_log, gdn_dt_bias, gdn_norm, gdn_w_o, gla_w_in, gla_w_gk, gla_b_gk, gla_norm, gla_w_o, moba_w_in, moba_w_o, ada_w, ada_b, ln_g, ln_b, router_w, router_b, moe_w_gu, moe_b_gu, moe_w_down, moe_b_down):
    B, S, D = x.shape
    assert D == D_MODEL
    depth = ada_w.shape[0]
    alpha = (2 * depth) ** 0.25
    mod = _ada_mod(c, ada_w, ada_b)
    sh_a, sc_a, g_a, sh_f, sc_f, g_f = (mod[:, :, k * D:(k + 1) * D] for k in range(6))
    xt = x.reshape(B * S, D)
    u = _modulate(xt, sc_a[0], sh_a[0], S)
    for i in range(depth):
        m, j = i % N_MIXERS, i // N_MIXERS
        if m == 0:
            h = _mla_mixer(u, B, S, mla_w_in[j], mla_q_norm[j], mla_kv_norm[j], mla_w_qb[j], mla_w_kvb[j], mla_w_o[j])
        elif m == 1:
            h = _gdn_mixer(u, B, S, gdn_w_in[j], gdn_conv_w[j], gdn_a_log[j], gdn_dt_bias[j], gdn_norm[j], gdn_w_o[j])
        elif m == 2:
            h = _gla_mixer(u, B, S, gla_w_in[j], gla_w_gk[j], gla_b_gk[j], gla_norm[j], gla_w_o[j])
        else:
            h = _moba_mixer(u, B, S, moba_w_in[j], moba_w_o[j], rel_bias)
        xt, u, top_idx, top_w = _ln_router(xt, h, g_a[i], ln_g[i, 0], ln_b[i, 0], sc_f[i], sh_f[i],
                                           router_w[i], router_b[i], S, alpha)
        y, dest = _moe_ffn(u, top_idx, i, moe_w_gu, moe_b_gu, moe_w_down, moe_b_down)
        nxt = (i + 1) % depth
        xt, u = _combine_ln(xt, y, dest, top_w, g_f[i], ln_g[i, 1], ln_b[i, 1], sc_a[nxt], sh_a[nxt], S, alpha)
    return xt.reshape(B, S, D)
```

```python
import functools
import math

import jax
import jax.numpy as jnp
from jax import lax
from jax.experimental import pallas as pl
from jax.experimental.pallas import tpu as pltpu

D_MODEL = 2048
N_MIXERS = 4
MLA_HEADS, MLA_Q_LORA, MLA_KV_LORA, MLA_NOPE, MLA_ROPE, MLA_V = 16, 512, 512, 128, 64, 128
ROPE_THETA = 10000.0
GDN_K_HEADS, GDN_V_HEADS, GDN_DK, GDN_DV, GDN_CONV, GDN_CHUNK = 16, 32, 128, 128, 4, 64
GLA_HEADS, GLA_GATE_RANK, GLA_GATE_NORMALIZER, GLA_CHUNK = 4, 16, 16.0, 64
MOBA_HEADS, MOBA_BLOCK, MOBA_TOPK = 16, 256, 3
REL_BUCKETS, REL_MAX_DIST = 32, 128
N_EXPERTS, TOP_K, EXPERT_FF = 32, 4, 768
SWIGLU_LIMIT, SWIGLU_ALPHA = 7.0, 1.702
LN_EPS, RMS_EPS, L2_EPS = 1e-5, 1e-6, 1e-6

V7X_LANES = 128
V7X_VMEM_LIMIT_BYTES = 56 * 1024 * 1024

F32 = jnp.float32
BF16 = jnp.bfloat16
HIGHEST = lax.Precision.HIGHEST
NEG_INF = float("-inf")


def _params(*sem):
    return pltpu.CompilerParams(dimension_semantics=sem, vmem_limit_bytes=V7X_VMEM_LIMIT_BYTES)


def _dot(a, b, dims=None, precision=None):
    if dims is None:
        dims = (((a.ndim - 1,), (0,)), ((), ()))
    return lax.dot_general(a, b, dims, precision=precision, preferred_element_type=F32)


def _dot_nt(a, b, precision=None):
    return _dot(a, b, (((1,), (1,)), ((), ())), precision)


def _dot_tn(a, b, precision=None):
    return _dot(a, b, (((0,), (0,)), ((), ())), precision)


def _sigmoid(x):
    return 1.0 / (1.0 + jnp.exp(-x))


def _silu(x):
    return x * _sigmoid(x)


def _iota(shape, dim):
    return lax.broadcasted_iota(jnp.int32, shape, dim)


def _mm_kernel(valid_cols, rms, x_ref, *refs):
    g_ref = refs[0] if rms else None
    w_ref, o_ref, wbf_ref = refs[-3:]

    @pl.when(pl.program_id(1) == 0)
    def _():
        w = w_ref[...]
        if valid_cols is not None:
            w = jnp.where(_iota(w.shape, 1) < valid_cols, w, 0.0)
        wbf_ref[...] = w.astype(BF16)

    x = x_ref[...]
    if rms:
        x = x.astype(F32)
        x = x * lax.rsqrt(jnp.mean(x * x, -1, keepdims=True) + RMS_EPS) * g_ref[...]
    o_ref[...] = _dot(x.astype(BF16), wbf_ref[...]).astype(o_ref.dtype)


def _mm_tiles(M, K, N):
    tm = 1024 if M % 1024 == 0 else M
    tn = N
    for cand in (1024, 512, 256, 128):
        if N % cand == 0 and K * cand * 4 <= 8 * 1024 * 1024:
            tn = cand
            break
    return tm, tn


def _mm(x, w, out_dtype, rms_gain=None, x_col=0, col0=0, n_cols=None):
    M = x.shape[0]
    K = w.shape[0]
    N = w.shape[1] if n_cols is None else n_cols
    tm, tn = _mm_tiles(M, K, N)
    assert col0 % tn == 0
    valid_cols = w.shape[1] - col0 if col0 + N > w.shape[1] else None
    assert valid_cols is None or N == tn
    c0 = col0 // tn
    x_spec = pl.BlockSpec((tm, K), lambda n, m: (m, x_col))
    w_spec = pl.BlockSpec((K, tn), lambda n, m: (0, c0 + n))
    if rms_gain is None:
        in_specs, args = [x_spec, w_spec], (x, w)
    else:
        in_specs, args = [x_spec, pl.BlockSpec((1, K), lambda n, m: (0, 0)), w_spec], (x, rms_gain[None, :], w)
    return pl.pallas_call(
        functools.partial(_mm_kernel, valid_cols, rms_gain is not None),
        grid=(N // tn, M // tm),
        in_specs=in_specs,
        out_specs=pl.BlockSpec((tm, tn), lambda n, m: (m, n)),
        out_shape=jax.ShapeDtypeStruct((M, N), out_dtype),
        scratch_shapes=[pltpu.VMEM((K, tn), BF16)],
        compiler_params=_params("parallel", "arbitrary"),
        name="proj_matmul",
    )(*args)


def _pad_cols(w, mult=V7X_LANES):
    pad = (-w.shape[-1]) % mult
    return jnp.pad(w, ((0, 0), (0, pad))) if pad else w


def _ada_kernel(c_ref, w_ref, b_ref, o_ref):
    c = _silu(c_ref[...]).astype(BF16)
    o_ref[0] = _dot(c, w_ref[0].astype(BF16)) + b_ref[0]


def _ada_mod(c, ada_w, ada_b):
    depth, D, N = ada_w.shape
    B = c.shape[0]
    rows = 8
    c_pad = jnp.pad(c, ((0, rows - B), (0, 0)))
    tn = 1024
    out = pl.pallas_call(
        _ada_kernel,
        grid=(depth, N // tn),
        in_specs=[pl.BlockSpec((rows, D), lambda i, n: (0, 0)),
                  pl.BlockSpec((1, D, tn), lambda i, n: (i, 0, n)),
                  pl.BlockSpec((1, 1, tn), lambda i, n: (i, 0, n))],
        out_specs=pl.BlockSpec((1, rows, tn), lambda i, n: (i, 0, n)),
        out_shape=jax.ShapeDtypeStruct((depth, rows, N), F32),
        compiler_params=_params("parallel", "parallel"),
        name="ada_mod",
    )(c_pad, ada_w, ada_b.reshape(depth, 1, N))
    return out[:, :B]


ROW_TILE = 256


def _modulate_kernel(x_ref, sc_ref, sh_ref, u_ref):
    u_ref[...] = (x_ref[...] * (1.0 + sc_ref[0]) + sh_ref[0]).astype(u_ref.dtype)


def _row_specs(D, S, tr):
    vec = pl.BlockSpec((1, 1, D), lambda i: ((i * tr) // S, 0, 0))
    row = pl.BlockSpec((tr, D), lambda i: (i, 0))
    par = pl.BlockSpec((1, D), lambda i: (0, 0))
    return vec, row, par


def _modulate(x, sc, sh, S):
    T, D = x.shape
    tr = ROW_TILE
    vec, row, _ = _row_specs(D, S, tr)
    return pl.pallas_call(
        _modulate_kernel, grid=(T // tr,), in_specs=[row, vec, vec], out_specs=row,
        out_shape=jax.ShapeDtypeStruct((T, D), BF16),
        compiler_params=_params("parallel"), name="modulate",
    )(x, sc[:, None, :], sh[:, None, :])


def _deepnorm(alpha, x, h, gate, g, b):
    y = alpha * x + (1.0 + gate) * h
    mu = jnp.mean(y, -1, keepdims=True)
    yc = y - mu
    var = jnp.mean(yc * yc, -1, keepdims=True)
    return yc * lax.rsqrt(var + LN_EPS) * g + b


U32 = jnp.uint32
ROW_PARTS = D_MODEL // (2 * V7X_LANES)
assert ROW_PARTS == 8


def _bits(x):
    return lax.bitcast_convert_type(x.astype(BF16).astype(F32), U32)


def _store_row_tiled(ref, row0, value):
    n = value.shape[0]
    for j in range(ROW_PARTS):
        lo = _bits(value[:, (2 * j) * V7X_LANES:(2 * j + 1) * V7X_LANES])
        hi = _bits(value[:, (2 * j + 1) * V7X_LANES:(2 * j + 2) * V7X_LANES])
        ref[pl.ds(row0 * ROW_PARTS + j, n, stride=ROW_PARTS), :] = lax.shift_right_logical(lo, U32(16)) | hi


def _load_row_tiled(ref, row0, n):
    parts = []
    for j in range(ROW_PARTS):
        w = ref[pl.ds(row0 * ROW_PARTS + j, n, stride=ROW_PARTS), :]
        parts.append(lax.bitcast_convert_type(lax.shift_left(w, U32(16)), F32))
        parts.append(lax.bitcast_convert_type(w & U32(0xFFFF0000), F32))
    return jnp.concatenate(parts, -1)


def _ln_router_kernel(alpha, x_ref, h_ref, gate_ref, g_ref, b_ref, sc_ref, sh_ref, rw_ref, rb_ref,
                      xo_ref, u_ref, idx_ref, wgt_ref):
    xn = _deepnorm(alpha, x_ref[...], h_ref[...].astype(F32), gate_ref[0], g_ref[...], b_ref[...])
    xo_ref[...] = xn
    u = xn * (1.0 + sc_ref[0]) + sh_ref[0]
    _store_row_tiled(u_ref, 0, u)
    logits = _dot(u, rw_ref[...], precision=HIGHEST) + rb_ref[...]
    lane = _iota(logits.shape, 1)
    vals = jnp.where(lane < N_EXPERTS, logits, NEG_INF)
    top_v, top_i = [], []
    for _ in range(TOP_K):
        m = jnp.max(vals, -1, keepdims=True)
        i = jnp.min(jnp.where(vals == m, lane, V7X_LANES), -1, keepdims=True)
        top_v.append(m)
        top_i.append(i)
        vals = jnp.where(lane == i, NEG_INF, vals)
    exps = [jnp.exp(v - top_v[0]) for v in top_v]
    denom = functools.reduce(lambda a, b: a + b, exps)
    idx = jnp.zeros(logits.shape, jnp.int32)
    wgt = jnp.zeros(logits.shape, F32)
    for k in range(TOP_K):
        idx = jnp.where(lane == k, top_i[k], idx)
        wgt = jnp.where(lane == k, exps[k] / denom, wgt)
    idx_ref[...] = idx
    wgt_ref[...] = wgt


def _ln_router(x, h, gate, ln_g, ln_b, sc, sh, router_w, router_b, S, alpha):
    T, D = x.shape
    tr = ROW_TILE
    vec, row, par = _row_specs(D, S, tr)
    lane_row = pl.BlockSpec((tr, V7X_LANES), lambda i: (i, 0))
    tiled_row = pl.BlockSpec((tr * ROW_PARTS, V7X_LANES), lambda i: (i, 0))
    rw = _pad_cols(router_w)
    rb = _pad_cols(router_b[None, :])
    xo, u, idx, wgt = pl.pallas_call(
        functools.partial(_ln_router_kernel, alpha), grid=(T // tr,),
        in_specs=[row, row, vec, par, par, vec, vec,
                  pl.BlockSpec((D, V7X_LANES), lambda i: (0, 0)), pl.BlockSpec((1, V7X_LANES), lambda i: (0, 0))],
        out_specs=[row, tiled_row, lane_row, lane_row],
        out_shape=[jax.ShapeDtypeStruct((T, D), F32), jax.ShapeDtypeStruct((T * ROW_PARTS, V7X_LANES), U32),
                   jax.ShapeDtypeStruct((T, V7X_LANES), jnp.int32), jax.ShapeDtypeStruct((T, V7X_LANES), F32)],
        compiler_params=_params("parallel"), name="deepnorm_ln_router",
    )(x, h, gate[:, None, :], ln_g[None, :], ln_b[None, :], sc[:, None, :], sh[:, None, :], rw, rb)
    return xo, u, idx[:, :TOP_K], wgt[:, :TOP_K]


MOE_ROW_TILE = 512


def _expert_changed(te_ref):
    t = pl.program_id(0)
    return jnp.logical_or(t == 0, te_ref[t] != te_ref[jnp.maximum(t - 1, 0)])


def _moe_gu_kernel(te_ref, tv_ref, x_ref, w_ref, b_ref, h_ref, wbf_ref):
    @pl.when(_expert_changed(te_ref))
    def _():
        wbf_ref[...] = w_ref[0].astype(BF16)

    @pl.when(tv_ref[pl.program_id(0)] > 0)
    def _():
        x = _load_row_tiled(x_ref, 0, x_ref.shape[0] // ROW_PARTS).astype(BF16)
        gu = _dot(x, wbf_ref[...]) + b_ref[0]
        gl = jnp.minimum(gu[:, :EXPERT_FF], SWIGLU_LIMIT)
        up = jnp.clip(gu[:, EXPERT_FF:], -SWIGLU_LIMIT, SWIGLU_LIMIT)
        h_ref[...] = ((up + 1.0) * gl * _sigmoid(gl * SWIGLU_ALPHA)).astype(h_ref.dtype)

    @pl.when(tv_ref[pl.program_id(0)] == 0)
    def _():
        h_ref[...] = jnp.zeros(h_ref.shape, h_ref.dtype)


def _moe_down_kernel(te_ref, tv_ref, h_ref, w_ref, b_ref, y_ref, wbf_ref):
    @pl.when(_expert_changed(te_ref))
    def _():
        wbf_ref[...] = w_ref[0].astype(BF16)

    @pl.when(tv_ref[pl.program_id(0)] > 0)
    def _():
        _store_row_tiled(y_ref, 0, _dot(h_ref[...], wbf_ref[...]) + b_ref[0])

    @pl.when(tv_ref[pl.program_id(0)] == 0)
    def _():
        y_ref[...] = jnp.zeros(y_ref.shape, y_ref.dtype)


def _route_metadata(top_idx, tm):
    T = top_idx.shape[0]
    P = T * TOP_K
    n_tiles = (P + N_EXPERTS * (tm - 1)) // tm
    e_flat = top_idx.reshape(P)
    onehot = (e_flat[:, None] == jnp.arange(N_EXPERTS)[None, :]).astype(jnp.int32)
    csum = jnp.cumsum(onehot, axis=0)
    counts = csum[-1]
    rank = jnp.sum(csum * onehot, axis=1) - 1
    padded = ((counts + tm - 1) // tm) * tm
    ends_p = jnp.cumsum(padded)
    starts_p = ends_p - padded
    dest = (starts_p[e_flat] + rank).astype(jnp.int32)
    tile_start = jnp.arange(n_tiles, dtype=jnp.int32) * tm
    tile_valid = (tile_start < ends_p[-1]).astype(jnp.int32)
    tile_expert = jnp.sum((tile_start[:, None] >= ends_p[None, :]).astype(jnp.int32), axis=1)
    last_expert = jnp.max(jnp.where(counts > 0, jnp.arange(N_EXPERTS), 0))
    tile_expert = jnp.where(tile_valid > 0, tile_expert, last_expert).astype(jnp.int32)
    return dest, tile_expert, tile_valid, n_tiles


DISPATCH_ROWS = 2048


def _dispatch_kernel(dest_ref, u_ref, init_hbm, xs_hbm, sem):
    del init_hbm
    R = dest_ref.shape[-1]

    def row(ref, i):
        return ref.at[pl.ds(pl.multiple_of(i * ROW_PARTS, ROW_PARTS), ROW_PARTS)]

    def issue(t, carry):
        for k in range(TOP_K):
            pltpu.make_async_copy(row(u_ref, t), row(xs_hbm, dest_ref[0, 0, t * TOP_K + k]), sem).start()
        return carry

    lax.fori_loop(0, R // TOP_K, issue, 0, unroll=2)
    n = (R // TOP_K) * ROW_PARTS
    for _ in range(TOP_K):
        pltpu.make_async_copy(u_ref, xs_hbm.at[pl.ds(0, n)], sem).wait()


def _dispatch(u, dest, rows):
    P = dest.shape[0]
    R = DISPATCH_ROWS
    shape = (rows * ROW_PARTS, u.shape[1])
    return pl.pallas_call(
        _dispatch_kernel,
        grid=(P // R,),
        in_specs=[pl.BlockSpec((1, 1, R), lambda s: (s, 0, 0), memory_space=pltpu.SMEM),
                  pl.BlockSpec(((R // TOP_K) * ROW_PARTS, u.shape[1]), lambda s: (s, 0)),
                  pl.BlockSpec(memory_space=pl.ANY)],
        out_specs=pl.BlockSpec(memory_space=pl.ANY),
        out_shape=jax.ShapeDtypeStruct(shape, u.dtype),
        scratch_shapes=[pltpu.SemaphoreType.DMA(())],
        input_output_aliases={2: 0},
        compiler_params=_params("arbitrary"), name="moe_dispatch",
    )(dest.reshape(P // R, 1, R), u, jnp.zeros(shape, u.dtype))


def _moe_ffn(u, top_idx, layer, w_gu, b_gu, w_down, b_down):
    D = D_MODEL
    tm = MOE_ROW_TILE
    dest, tile_expert, tile_valid, n_tiles = _route_metadata(top_idx, tm)
    tile_expert = tile_expert + layer * N_EXPERTS
    w_gu, w_down = (w.reshape((-1,) + w.shape[2:]) for w in (w_gu, w_down))
    b_gu, b_down = (b.reshape(-1, b.shape[-1]) for b in (b_gu, b_down))
    rows = n_tiles * tm
    x_sorted = _dispatch(u, dest, rows)
    ff2 = 2 * EXPERT_FF
    tiled = pl.BlockSpec((tm * ROW_PARTS, V7X_LANES), lambda t, te, tv: (t, 0))
    h = pl.pallas_call(
        _moe_gu_kernel,
        grid_spec=pltpu.PrefetchScalarGridSpec(
            num_scalar_prefetch=2, grid=(n_tiles,),
            in_specs=[tiled,
                      pl.BlockSpec((1, D, ff2), lambda t, te, tv: (te[t], 0, 0)),
                      pl.BlockSpec((1, 1, ff2), lambda t, te, tv: (te[t], 0, 0))],
            out_specs=pl.BlockSpec((tm, EXPERT_FF), lambda t, te, tv: (t, 0)),
            scratch_shapes=[pltpu.VMEM((D, ff2), BF16)]),
        out_shape=jax.ShapeDtypeStruct((rows, EXPERT_FF), BF16),
        compiler_params=_params("arbitrary"), name="moe_gate_up",
    )(tile_expert, tile_valid, x_sorted, w_gu, b_gu[:, None, :])
    y = pl.pallas_call(
        _moe_down_kernel,
        grid_spec=pltpu.PrefetchScalarGridSpec(
            num_scalar_prefetch=2, grid=(n_tiles,),
            in_specs=[pl.BlockSpec((tm, EXPERT_FF), lambda t, te, tv: (t, 0)),
                      pl.BlockSpec((1, EXPERT_FF, D), lambda t, te, tv: (te[t], 0, 0)),
                      pl.BlockSpec((1, 1, D), lambda t, te, tv: (te[t], 0, 0))],
            out_specs=tiled,
            scratch_shapes=[pltpu.VMEM((EXPERT_FF, D), BF16)]),
        out_shape=jax.ShapeDtypeStruct((rows * ROW_PARTS, V7X_LANES), U32),
        compiler_params=_params("arbitrary"), name="moe_down",
    )(tile_expert, tile_valid, h, w_down, b_down[:, None, :])
    return y, dest


COMBINE_TILE = 256
COMBINE_SUB = 64


def _combine_ln_kernel(alpha, dest_ref, y_hbm, x_ref, wgt_ref, gate_ref, g_ref, b_ref, sc_ref, sh_ref,
                       xo_ref, u_ref, buf_ref, sems):
    n_sub = COMBINE_TILE // COMBINE_SUB

    def row(ref, i):
        return ref.at[pl.ds(pl.multiple_of(i * ROW_PARTS, ROW_PARTS), ROW_PARTS)]

    for j in range(n_sub):
        def issue(t, carry, j=j):
            tok = j * COMBINE_SUB + t
            for k in range(TOP_K):
                pltpu.make_async_copy(row(y_hbm, dest_ref[0, 0, tok * TOP_K + k]),
                                      row(buf_ref, k * COMBINE_TILE + tok), sems.at[j]).start()
            return carry

        lax.fori_loop(0, COMBINE_SUB, issue, 0, unroll=2)
    for j in range(n_sub):
        lo = j * COMBINE_SUB
        n = COMBINE_SUB * TOP_K * ROW_PARTS
        pltpu.make_async_copy(y_hbm.at[pl.ds(0, n)], buf_ref.at[pl.ds(0, n)], sems.at[j]).wait()
        wgt = wgt_ref[pl.ds(lo, COMBINE_SUB), :]
        f = sum(_load_row_tiled(buf_ref, k * COMBINE_TILE + lo, COMBINE_SUB) * wgt[:, k:k + 1] for k in range(TOP_K))
        xn = _deepnorm(alpha, x_ref[pl.ds(lo, COMBINE_SUB), :], f, gate_ref[0], g_ref[...], b_ref[...])
        xo_ref[pl.ds(lo, COMBINE_SUB), :] = xn
        u_ref[pl.ds(lo, COMBINE_SUB), :] = (xn * (1.0 + sc_ref[0]) + sh_ref[0]).astype(u_ref.dtype)


def _combine_ln(x, y, dest, top_w, gate, ln_g, ln_b, sc, sh, S, alpha):
    T, D = x.shape
    tr = COMBINE_TILE
    vec, row, par = _row_specs(D, S, tr)
    n_sub = tr // COMBINE_SUB
    return pl.pallas_call(
        functools.partial(_combine_ln_kernel, alpha), grid=(T // tr,),
        in_specs=[pl.BlockSpec((1, 1, tr * TOP_K), lambda i: (i, 0, 0), memory_space=pltpu.SMEM),
                  pl.BlockSpec(memory_space=pl.ANY), row,
                  pl.BlockSpec((tr, TOP_K), lambda i: (i, 0)), vec, par, par, vec, vec],
        out_specs=[row, row],
        out_shape=[jax.ShapeDtypeStruct((T, D), F32), jax.ShapeDtypeStruct((T, D), BF16)],
        scratch_shapes=[pltpu.VMEM((tr * TOP_K * ROW_PARTS, V7X_LANES), U32), pltpu.SemaphoreType.DMA((n_sub,))],
        compiler_params=_params("arbitrary"), name="moe_combine_ln",
    )(dest.reshape(T // tr, 1, tr * TOP_K), y, x, top_w, gate[:, None, :], ln_g[None, :], ln_b[None, :],
      sc[:, None, :], sh[:, None, :])


ATTN_TILE = 256


def _softmax_pv(s, v):
    p = jnp.exp(s - jnp.max(s, -1, keepdims=True))
    return _dot(p.astype(BF16), v) / jnp.sum(p, -1, keepdims=True)


def _causal_widths(n_tiles):
    half = n_tiles // 2
    return [(half, half), (n_tiles, n_tiles)] if half > 0 else [(n_tiles, n_tiles)]


def _mla_attn_kernel(scale, qn_ref, qr_ref, kv_ref, kr_ref, o_ref):
    qi = pl.program_id(2)
    t = ATTN_TILE
    n_tiles = kv_ref.shape[1] // t

    def attend(n_kv):
        w = n_kv * t
        visible = _iota((t, w), 1) <= _iota((t, w), 0) + qi * t
        outs = []
        for hh in range(2):
            qn = qn_ref[0, :, hh * MLA_NOPE:(hh + 1) * MLA_NOPE]
            qr = qr_ref[0, :, hh * MLA_ROPE:(hh + 1) * MLA_ROPE]
            c0 = hh * (MLA_NOPE + MLA_V)
            s = (_dot_nt(qn, kv_ref[0, 0:w, c0:c0 + MLA_NOPE]) + _dot_nt(qr, kr_ref[0, 0:w, :])) * scale
            s = jnp.where(visible, s, NEG_INF)
            outs.append(_softmax_pv(s, kv_ref[0, 0:w, c0 + MLA_NOPE:c0 + MLA_NOPE + MLA_V]))
        o_ref[0] = jnp.concatenate(outs, -1).astype(o_ref.dtype)

    lo = 0
    for hi, n_kv in _causal_widths(n_tiles):
        pl.when(jnp.logical_and(qi >= lo, qi < hi))(functools.partial(attend, n_kv))
        lo = hi


def _rope(x, cos, sin):
    half = x.shape[-1] // 2
    x1, x2 = x[..., :half], x[..., half:]
    return jnp.concatenate([x1 * cos - x2 * sin, x2 * cos + x1 * sin], -1)


def _mla_mixer(u, B, S, w_in, q_norm, kv_norm, w_qb, w_kvb, w_o):
    H = MLA_HEADS
    n_lat = MLA_Q_LORA + MLA_KV_LORA
    lat = _mm(u, w_in, F32, n_cols=n_lat)
    k_rope = _mm(u, w_in, F32, col0=n_lat, n_cols=V7X_LANES)[:, :MLA_ROPE]
    wq = w_qb.reshape(MLA_Q_LORA, H, MLA_NOPE + MLA_ROPE)
    wq = jnp.concatenate([wq[:, :, :MLA_NOPE].reshape(MLA_Q_LORA, H * MLA_NOPE),
                          wq[:, :, MLA_NOPE:].reshape(MLA_Q_LORA, H * MLA_ROPE)], -1)
    q = _mm(lat, wq, F32, rms_gain=q_norm, x_col=0)
    kv = _mm(lat, w_kvb, BF16, rms_gain=kv_norm, x_col=1)
    inv_freq = ROPE_THETA ** (-jnp.arange(MLA_ROPE // 2, dtype=F32) / (MLA_ROPE // 2))
    ang = jnp.arange(S, dtype=F32)[:, None] * inv_freq[None, :]
    cos, sin = jnp.cos(ang), jnp.sin(ang)
    q_nope = q[:, :H * MLA_NOPE].astype(BF16).reshape(B, S, H * MLA_NOPE)
    q_rope = _rope(q[:, H * MLA_NOPE:].reshape(B, S, H, MLA_ROPE), cos[:, None, :], sin[:, None, :])
    q_rope = q_rope.astype(BF16).reshape(B, S, H * MLA_ROPE)
    k_rope = _rope(k_rope.reshape(B, S, MLA_ROPE), cos, sin).astype(BF16)
    t = ATTN_TILE
    o = pl.pallas_call(
        functools.partial(_mla_attn_kernel, (MLA_NOPE + MLA_ROPE) ** -0.5),
        grid=(B, H // 2, S // t),
        in_specs=[pl.BlockSpec((1, t, 2 * MLA_NOPE), lambda b, h, i: (b, i, h)),
                  pl.BlockSpec((1, t, 2 * MLA_ROPE), lambda b, h, i: (b, i, h)),
                  pl.BlockSpec((1, S, 2 * (MLA_NOPE + MLA_V)), lambda b, h, i: (b, 0, h)),
                  pl.BlockSpec((1, S, MLA_ROPE), lambda b, h, i: (b, 0, 0))],
        out_specs=pl.BlockSpec((1, t, 2 * MLA_V), lambda b, h, i: (b, i, h)),
        out_shape=jax.ShapeDtypeStruct((B, S, H * MLA_V), BF16),
        compiler_params=_params("parallel", "parallel", "arbitrary"), name="mla_attention",
    )(q_nope, q_rope, kv.reshape(B, S, -1), k_rope)
    return _mm(o.reshape(B * S, H * MLA_V), w_o, F32)


def _t5_bucket(dist):
    n = jnp.maximum(dist, 0)
    max_exact = REL_BUCKETS // 2
    large = max_exact + (jnp.log(jnp.maximum(n, 1).astype(F32) / max_exact)
                         / math.log(REL_MAX_DIST / max_exact) * (REL_BUCKETS - max_exact)).astype(jnp.int32)
    large = jnp.minimum(large, REL_BUCKETS - 1)
    return jnp.where(n < max_exact, n, large)


MOBA_FAR = 2


def _moba_kernel(scale, n_sel, tab_ref, q_ref, k_ref, v_ref, bkt_ref, o_ref, kbf_ref, vbf_ref, kmean_ref, bias_ref):
    h, b, qi = pl.program_id(0), pl.program_id(1), pl.program_id(2)
    L = MOBA_BLOCK
    S = k_ref.shape[1]
    n_blk = S // L

    @pl.when(jnp.logical_and(b == 0, qi == 0))
    def _():
        for d in range(MOBA_FAR + 1):
            bucket = bkt_ref[d]
            tile = jnp.zeros((L, L), F32)
            for e in range(REL_BUCKETS):
                tile = jnp.where(bucket == e, tab_ref[e, h], tile)
            bias_ref[d] = tile

    @pl.when(qi == 0)
    def _():
        k = k_ref[0].astype(F32)
        kbf_ref[...] = k.astype(BF16)
        vbf_ref[...] = v_ref[0].astype(BF16)
        kmean_ref[...] = jnp.zeros(kmean_ref.shape, F32)
        kmean_ref[0:n_blk, :] = jnp.mean(k.reshape(n_blk, L, k.shape[-1]), axis=1)

    q = q_ref[0].astype(F32)
    qb = q.astype(BF16)
    nb = kmean_ref.shape[0]
    blk = _iota((nb, L), 0)
    gate = jnp.where(blk < qi, _dot_nt(kmean_ref[...], q, precision=HIGHEST), NEG_INF)
    picked = jnp.zeros((nb, L), F32)
    for j in range(n_blk - 1):
        gj = gate[j:j + 1, :]
        beats = jnp.logical_or(gate > gj, jnp.logical_and(gate == gj, blk < j))
        in_topk = jnp.sum(jnp.where(beats, 1.0, 0.0), 0, keepdims=True) < n_sel
        picked = jnp.where(blk == j, jnp.where(jnp.logical_and(in_topk, j < qi), 1.0, 0.0), picked)
    picked = _dot_tn(picked, jnp.where(_iota((nb, nb), 0) == _iota((nb, nb), 1), 1.0, 0.0))
    causal_add = jnp.where(_iota((L, L), 1) <= _iota((L, L), 0), 0.0, NEG_INF)
    past_add = [jnp.where(picked[:, j:j + 1] > 0.5, 0.0, NEG_INF) for j in range(n_blk)]

    def attend(n_kv):
        w = n_kv * L
        s = _dot_nt(qb, kbf_ref[0:w, :]) * scale
        parts = []
        for j in range(n_kv):
            bias = bias_ref[jnp.clip(qi - j, 0, MOBA_FAR)]
            parts.append(s[:, j * L:(j + 1) * L] + bias + jnp.where(j == qi, causal_add, past_add[j]))
        o_ref[0] = _softmax_pv(jnp.concatenate(parts, -1), vbf_ref[0:w, :]).astype(o_ref.dtype)

    lo = 0
    for hi, n_kv in _causal_widths(n_blk):
        pl.when(jnp.logical_and(qi >= lo, qi < hi))(functools.partial(attend, n_kv))
        lo = hi


def _moba_mixer(u, B, S, w_in, w_o, rel_bias):
    H, Dh, L = MOBA_HEADS, u.shape[1] // MOBA_HEADS, MOBA_BLOCK
    n_blk = S // L
    assert S % L == 0 and Dh == V7X_LANES and (MOBA_FAR - 1) * L >= REL_MAX_DIST
    n_sel = max(min(MOBA_TOPK, n_blk - 1), 1)
    qkv = _mm(u, w_in, BF16).reshape(B, S, 3 * H * Dh)
    qk = jnp.arange(L)[:, None] - jnp.arange(L)[None, :]
    bucket = _t5_bucket(jnp.stack([qk + d * L for d in range(MOBA_FAR + 1)])).astype(jnp.int32)
    o = pl.pallas_call(
        functools.partial(_moba_kernel, Dh ** -0.5, n_sel),
        grid=(H, B, n_blk),
        in_specs=[pl.BlockSpec(memory_space=pltpu.SMEM),
                  pl.BlockSpec((1, L, Dh), lambda h, b, i: (b, i, h)),
                  pl.BlockSpec((1, S, Dh), lambda h, b, i: (b, 0, H + h)),
                  pl.BlockSpec((1, S, Dh), lambda h, b, i: (b, 0, 2 * H + h)),
                  pl.BlockSpec((MOBA_FAR + 1, L, L), lambda h, b, i: (0, 0, 0))],
        out_specs=pl.BlockSpec((1, L, Dh), lambda h, b, i: (b, i, h)),
        out_shape=jax.ShapeDtypeStruct((B, S, H * Dh), BF16),
        scratch_shapes=[pltpu.VMEM((S, Dh), BF16), pltpu.VMEM((S, Dh), BF16),
                        pltpu.VMEM((-(-n_blk // 8) * 8, Dh), F32),
                        pltpu.VMEM((MOBA_FAR + 1, L, L), F32)],
        compiler_params=_params("parallel", "arbitrary", "arbitrary"), name="moba_attention",
    )(rel_bias, qkv, qkv, qkv, bucket)
    return _mm(o.reshape(B * S, H * Dh), w_o, F32)


GDN_HEAD_GROUP = 16


def _lane_sum(x):
    return _dot(x.astype(BF16), jnp.ones((x.shape[1], x.shape[1]), BF16))


def _l2norm(x):
    return x * lax.rsqrt(_lane_sum(x * x) + L2_EPS)


def _dotb(a, b):
    return _dot(a.astype(BF16), b.astype(BF16))


def _unit_lower_inverse(a_lows, block):
    C = a_lows[0].shape[0]
    r = _iota((C, C), 0)
    c = _iota((C, C), 1)
    eye = jnp.where(r == c, 1.0, 0.0)
    same = (r // block) == (c // block)
    a_d = [jnp.where(same, a, 0.0) for a in a_lows]
    a_off = [a - d for a, d in zip(a_lows, a_d)]
    inv_d = [eye - d for d in a_d]
    pw = a_d
    k = 2
    while k < block:
        pw = [_dotb(p, p) for p in pw]
        inv_d = [_dotb(i, eye + p) for i, p in zip(inv_d, pw)]
        k *= 2
    n = [_dotb(i, o) for i, o in zip(inv_d, a_off)]
    inv_n = [eye - x for x in n]
    pw = n
    k = 2
    while k < C // block:
        pw = [_dotb(p, p) for p in pw]
        inv_n = [_dotb(i, eye + p) for i, p in zip(inv_n, pw)]
        k *= 2
    return [_dotb(i, d) for i, d in zip(inv_n, inv_d)]


GDN_HIST = 16


def _causal_conv_silu(x_ref, w_ref, hist_ref):
    C = x_ref.shape[1]
    x = x_ref[0]
    hist_ref[GDN_HIST:GDN_HIST + C, :] = x
    hist = hist_ref[...]
    t = _iota((C, GDN_HIST + C), 0)
    r = _iota((C, GDN_HIST + C), 1)
    y = x.astype(F32) * w_ref[GDN_CONV - 1:GDN_CONV, :]
    for i in range(GDN_CONV - 1):
        shift = jnp.where(r == t + (GDN_HIST - (GDN_CONV - 1) + i), 1.0, 0.0).astype(BF16)
        y = y + _dot(shift, hist) * w_ref[i:i + 1, :]
    hist_ref[0:GDN_HIST, :] = hist_ref[C:C + GDN_HIST, :]
    return _silu(y)


def _gdn_kernel(q_ref, k_ref, v_ref, z_ref, cwq_ref, cwk_ref, cwv_ref, gc_ref, gct_ref, beta_ref, ng_ref,
                o_ref, state_ref, hq_ref, hk_ref, hv_ref):
    G, C, DK, DV = GDN_HEAD_GROUP, GDN_CHUNK, GDN_DK, GDN_DV
    rep = GDN_V_HEADS // GDN_K_HEADS
    heads = range(G)

    @pl.when(pl.program_id(2) == 0)
    def _():
        state_ref[...] = jnp.zeros(state_ref.shape, F32)
        for hist in (hq_ref, hk_ref, hv_ref):
            hist[0:GDN_HIST, :] = jnp.zeros((GDN_HIST, hist.shape[1]), hist.dtype)

    qc = _causal_conv_silu(q_ref, cwq_ref, hq_ref)
    kc = _causal_conv_silu(k_ref, cwk_ref, hk_ref)
    vc = _causal_conv_silu(v_ref, cwv_ref, hv_ref)
    r = _iota((C, C), 0)
    c = _iota((C, C), 1)
    tri = c <= r
    strict = c < r
    gc = gc_ref[0, 0]
    gct = gct_ref[0, 0, 0]
    beta = beta_ref[0, 0]
    q = [_l2norm(qc[:, i * DK:(i + 1) * DK]) * (DK ** -0.5) for i in range(G // rep)]
    k = [_l2norm(kc[:, i * DK:(i + 1) * DK]) for i in range(G // rep)]
    assert DK == V7X_LANES and DV == V7X_LANES
    kb = [x.astype(BF16) for x in k]
    qk = [_dot_nt(a.astype(BF16), b) for a, b in zip(q, kb)]
    kk = [_dot_nt(b, b) for b in kb]
    g_col = [gc[:, h:h + 1] for h in heads]
    b_col = [beta[:, h:h + 1] for h in heads]
    decay = [jnp.where(tri, jnp.exp(jnp.where(tri, g_col[h] - gct[h:h + 1, :], 0.0)), 0.0) for h in heads]
    t_inv = _unit_lower_inverse([jnp.where(strict, kk[h // rep] * b_col[h] * decay[h], 0.0) for h in heads], 16)
    e_g = [jnp.exp(g) for g in g_col]
    rhs = [jnp.concatenate([vc[:, h * DV:(h + 1) * DV] * b_col[h], k[h // rep] * (b_col[h] * e_g[h])], -1)
           for h in heads]
    sol = [_dot(t_inv[h].astype(BF16), rhs[h].astype(BF16)) for h in heads]
    state = [state_ref[h] for h in heads]
    state_b = [s.astype(BF16) for s in state]
    v_new = [sol[h][:, :DV] - _dot(sol[h][:, DV:].astype(BF16), state_b[h]) for h in heads]
    v_new_b = [x.astype(BF16) for x in v_new]
    attn = [jnp.where(tri, qk[h // rep] * decay[h], 0.0).astype(BF16) for h in heads]
    o = [_dot((q[h // rep] * e_g[h]).astype(BF16), state_b[h]) + _dot(attn[h], v_new_b[h]) for h in heads]
    g_last = [g[C - 1:C, :] for g in g_col]
    k_tail = [(k[h // rep] * jnp.exp(g_last[h] - g_col[h])).astype(BF16) for h in heads]
    new_state = [state[h] * jnp.exp(g_last[h]) + _dot_tn(k_tail[h], v_new_b[h]) for h in heads]
    o = [x * lax.rsqrt(_lane_sum(x * x) * (1.0 / DV) + RMS_EPS) * ng_ref[...] for x in o]
    o = [o[h] * _silu(z_ref[0, :, h * DV:(h + 1) * DV].astype(F32)) for h in heads]
    for h in heads:
        state_ref[h] = new_state[h]
    o_ref[0] = jnp.concatenate(o, -1).astype(o_ref.dtype)


def _gdn_mixer(u, B, S, w_in, conv_w, a_log, dt_bias, norm_g, w_o):
    HK, HV, DK, DV, C, G = GDN_K_HEADS, GDN_V_HEADS, GDN_DK, GDN_DV, GDN_CHUNK, GDN_HEAD_GROUP
    qk_dim, v_dim = HK * DK, HV * DV
    n_main = 2 * qk_dim + 2 * v_dim
    proj = _mm(u, w_in, BF16, n_cols=n_main).reshape(B, S, n_main)
    ba = _mm(u, w_in, F32, col0=n_main, n_cols=V7X_LANES).reshape(B, S, -1)
    n_conv = 2 * qk_dim + v_dim
    beta = jax.nn.sigmoid(ba[:, :, :HV])
    g = -jnp.exp(a_log) * jax.nn.softplus(ba[:, :, HV:2 * HV] + dt_bias)
    N = S // C
    gc = jnp.cumsum(g.reshape(B, N, C, HV), axis=2)
    HG = HV // G
    gc_g = gc.reshape(B, N, C, HG, G).transpose(0, 3, 1, 2, 4).reshape(B, HG, S, G)
    gct_g = gc.reshape(B, N, C, HG, G).transpose(0, 3, 1, 4, 2)
    beta_g = beta.reshape(B, S, HG, G).transpose(0, 2, 1, 3)
    kw = (G // (HV // HK)) * DK
    vw = G * DV
    k_blk, v_blk, z_blk = qk_dim // kw, 2 * qk_dim // vw, n_conv // vw
    o = pl.pallas_call(
        _gdn_kernel,
        grid=(B, HG, N),
        in_specs=[pl.BlockSpec((1, C, kw), lambda b, h, n: (b, n, h)),
                  pl.BlockSpec((1, C, kw), lambda b, h, n: (b, n, k_blk + h)),
                  pl.BlockSpec((1, C, vw), lambda b, h, n: (b, n, v_blk + h)),
                  pl.BlockSpec((1, C, vw), lambda b, h, n: (b, n, z_blk + h)),
                  pl.BlockSpec((GDN_CONV, kw), lambda b, h, n: (0, h)),
                  pl.BlockSpec((GDN_CONV, kw), lambda b, h, n: (0, k_blk + h)),
                  pl.BlockSpec((GDN_CONV, vw), lambda b, h, n: (0, v_blk + h)),
                  pl.BlockSpec((1, 1, C, G), lambda b, h, n: (b, h, n, 0)),
                  pl.BlockSpec((1, 1, 1, G, C), lambda b, h, n: (b, h, n, 0, 0)),
                  pl.BlockSpec((1, 1, C, G), lambda b, h, n: (b, h, n, 0)),
                  pl.BlockSpec((1, DV), lambda b, h, n: (0, 0))],
        out_specs=pl.BlockSpec((1, C, vw), lambda b, h, n: (b, n, h)),
        out_shape=jax.ShapeDtypeStruct((B, S, v_dim), BF16),
        scratch_shapes=[pltpu.VMEM((G, DK, DV), F32), pltpu.VMEM((GDN_HIST + C, kw), BF16),
                        pltpu.VMEM((GDN_HIST + C, kw), BF16), pltpu.VMEM((GDN_HIST + C, vw), BF16)],
        compiler_params=_params("parallel", "parallel", "arbitrary"), name="gdn_chunked",
    )(proj, proj, proj, proj, conv_w, conv_w, conv_w, gc_g, gct_g, beta_g, norm_g[None, :])
    return _mm(o.reshape(B * S, v_dim), w_o, F32)


def _gla_kernel(scale, q_ref, k_ref, v_ref, og_ref, gk_ref, wgk_ref, bgk_ref, ng_ref, o_ref, state_ref):
    C = GLA_CHUNK
    H, dv, dk = state_ref.shape
    heads = range(H)

    @pl.when(pl.program_id(1) == 0)
    def _():
        state_ref[...] = jnp.zeros(state_ref.shape, F32)

    x = _dot(gk_ref[0].astype(BF16), wgk_ref[...].astype(BF16)) + bgk_ref[...]
    log_alpha = (jnp.minimum(x, 0.0) - jnp.log(1.0 + jnp.exp(-jnp.abs(x)))) / GLA_GATE_NORMALIZER
    r = _iota((C, C), 0)
    c = _iota((C, C), 1)
    causal = c <= r
    b = _dot(jnp.where(causal, 1.0, 0.0), log_alpha, precision=HIGHEST)
    b_last = b[C - 1:C, :]
    q = q_ref[0].astype(F32) * scale
    k = k_ref[0].astype(F32)
    q_dec = (q * jnp.exp(b)).astype(BF16)
    k_inv = (k * jnp.exp(-b)).astype(BF16)
    k_tail = (k * jnp.exp(b_last - b)).astype(BF16)
    decay = jnp.exp(b_last)
    ks = [slice(h * dk, (h + 1) * dk) for h in heads]
    v = [v_ref[0, :, h * dv:(h + 1) * dv] for h in heads]
    attn = [jnp.where(causal, _dot_nt(q_dec[:, ks[h]], k_inv[:, ks[h]]), 0.0).astype(BF16) for h in heads]
    state_t = [state_ref[h] for h in heads]
    o = [_dot(attn[h], v[h]) + _dot_nt(q_dec[:, ks[h]], state_t[h].astype(BF16)) for h in heads]
    new_state = [state_t[h] * decay[:, ks[h]] + _dot_tn(v[h], k_tail[:, ks[h]]) for h in heads]
    o = [x * lax.rsqrt(jnp.mean(x * x, -1, keepdims=True) + RMS_EPS) * ng_ref[...] for x in o]
    for h in heads:
        state_ref[h] = new_state[h]
    o_ref[0] = (jnp.concatenate(o, -1) * _silu(og_ref[0].astype(F32))).astype(o_ref.dtype)


def _gla_mixer(u, B, S, w_in, w_gk, b_gk, norm_g, w_o):
    D = u.shape[1]
    H, C = GLA_HEADS, GLA_CHUNK
    key_dim, val_dim = D // 2, D
    dk, dv = key_dim // H, val_dim // H
    n_main = 2 * key_dim + 2 * val_dim
    proj = _mm(u, w_in, BF16, n_cols=n_main).reshape(B, S, n_main)
    gk = _mm(u, w_in, F32, col0=n_main, n_cols=V7X_LANES).reshape(B, S, -1)
    wgk = jnp.pad(w_gk, ((0, gk.shape[-1] - GLA_GATE_RANK), (0, 0)))
    o = pl.pallas_call(
        functools.partial(_gla_kernel, dk ** -0.5),
        grid=(B, S // C),
        in_specs=[pl.BlockSpec((1, C, key_dim), lambda b, n: (b, n, 0)),
                  pl.BlockSpec((1, C, key_dim), lambda b, n: (b, n, 1)),
                  pl.BlockSpec((1, C, val_dim), lambda b, n: (b, n, 2 * key_dim // val_dim)),
                  pl.BlockSpec((1, C, val_dim), lambda b, n: (b, n, 2 * key_dim // val_dim + 1)),
                  pl.BlockSpec((1, C, gk.shape[-1]), lambda b, n: (b, n, 0)),
                  pl.BlockSpec((gk.shape[-1], key_dim), lambda b, n: (0, 0)),
                  pl.BlockSpec((1, key_dim), lambda b, n: (0, 0)),
                  pl.BlockSpec((1, dv), lambda b, n: (0, 0))],
        out_specs=pl.BlockSpec((1, C, val_dim), lambda b, n: (b, n, 0)),
        out_shape=jax.ShapeDtypeStruct((B, S, val_dim), BF16),
        scratch_shapes=[pltpu.VMEM((H, dv, dk), F32)],
        compiler_params=_params("parallel", "arbitrary"), name="gla_chunked",
    )(proj, proj, proj, proj, gk, wgk, b_gk[None, :], norm_g[None, :])
    return _mm(o.reshape(B * S, val_dim), w_o, F32)


def kernel(x, c, rel_bias, mla_w_in, mla_q_norm, mla_kv_norm, mla_w_qb, mla_w_kvb, mla_w_o, gdn_w_in, gdn_conv_w, gdn_a_log, gdn_dt_bias, gdn_norm, gdn_w_o, gla_w_in, gla_w_gk, gla_b_gk, gla_norm, gla_w_o, moba_w_in, moba_w_o, ada_w, ada_b, ln_g, ln_b, router_w, router_b, moe_w_gu, moe_b_gu, moe_w_down, moe_b_down):
    B, S, D = x.shape
    assert D == D_MODEL
    depth = ada_w.shape[0]
    alpha = (2 * depth) ** 0.25
    mod = _ada_mod(c, ada_w, ada_b)
    sh_a, sc_a, g_a, sh_f, sc_f, g_f = (mod[:, :, k * D:(k + 1) * D] for k in range(6))
    xt = x.reshape(B * S, D)
    u = _modulate(xt, sc_a[0], sh_a[0], S)
    for i in range(depth):
        m, j = i % N_MIXERS, i // N_MIXERS
        if m == 0:
            h = _mla_mixer(u, B, S, mla_w_in[j], mla_q_norm[j], mla_kv_norm[j], mla_w_qb[j], mla_w_kvb[j], mla_w_o[j])
        elif m == 1:
            h = _gdn_mixer(u, B, S, gdn_w_in[j], gdn_conv_w[j], gdn_a_log[j], gdn_dt_bias[j], gdn_norm[j], gdn_w_o[j])
        elif m == 2:
            h = _gla_mixer(u, B, S, gla_w_in[j], gla_w_gk[j], gla_b_gk[j], gla_norm[j], gla_w_o[j])
        else:
            h = _moba_mixer(u, B, S, moba_w_in[j], moba_w_o[j], rel_bias)
        xt, u, top_idx, top_w = _ln_router(xt, h, g_a[i], ln_g[i, 0], ln_b[i, 0], sc_f[i], sh_f[i],
                                           router_w[i], router_b[i], S, alpha)
        y, dest = _moe_ffn(u, top_idx, i, moe_w_gu, moe_b_gu, moe_w_down, moe_b_down)
        nxt = (i + 1) % depth
        xt, u = _combine_ln(xt, y, dest, top_w, g_f[i], ln_g[i, 1], ln_b[i, 1], sc_a[nxt], sh_a[nxt], S, alpha)
    return xt.reshape(B, S, D)
```

```python
import functools
import math

import jax
import jax.numpy as jnp
from jax import lax
from jax.experimental import pallas as pl
from jax.experimental.pallas import tpu as pltpu

D_MODEL = 2048
N_MIXERS = 4
MLA_HEADS, MLA_Q_LORA, MLA_KV_LORA, MLA_NOPE, MLA_ROPE, MLA_V = 16, 512, 512, 128, 64, 128
ROPE_THETA = 10000.0
GDN_K_HEADS, GDN_V_HEADS, GDN_DK, GDN_DV, GDN_CONV, GDN_CHUNK = 16, 32, 128, 128, 4, 64
GLA_HEADS, GLA_GATE_RANK, GLA_GATE_NORMALIZER, GLA_CHUNK = 4, 16, 16.0, 64
MOBA_HEADS, MOBA_BLOCK, MOBA_TOPK = 16, 256, 3
REL_BUCKETS, REL_MAX_DIST = 32, 128
N_EXPERTS, TOP_K, EXPERT_FF = 32, 4, 768
SWIGLU_LIMIT, SWIGLU_ALPHA = 7.0, 1.702
LN_EPS, RMS_EPS, L2_EPS = 1e-5, 1e-6, 1e-6

V7X_LANES = 128
V7X_VMEM_LIMIT_BYTES = 56 * 1024 * 1024

F32 = jnp.float32
BF16 = jnp.bfloat16
HIGHEST = lax.Precision.HIGHEST
NEG_INF = float("-inf")


def _params(*sem):
    return pltpu.CompilerParams(dimension_semantics=sem, vmem_limit_bytes=V7X_VMEM_LIMIT_BYTES)


def _dot(a, b, dims=None, precision=None):
    if dims is None:
        dims = (((a.ndim - 1,), (0,)), ((), ()))
    return lax.dot_general(a, b, dims, precision=precision, preferred_element_type=F32)


def _dot_nt(a, b, precision=None):
    return _dot(a, b, (((1,), (1,)), ((), ())), precision)


def _dot_tn(a, b, precision=None):
    return _dot(a, b, (((0,), (0,)), ((), ())), precision)


def _sigmoid(x):
    return 1.0 / (1.0 + jnp.exp(-x))


def _silu(x):
    return x * _sigmoid(x)


def _iota(shape, dim):
    return lax.broadcasted_iota(jnp.int32, shape, dim)


def _mm_kernel(valid_cols, rms, x_ref, *refs):
    g_ref = refs[0] if rms else None
    w_ref, o_ref, wbf_ref = refs[-3:]

    @pl.when(pl.program_id(1) == 0)
    def _():
        w = w_ref[...]
        if valid_cols is not None:
            w = jnp.where(_iota(w.shape, 1) < valid_cols, w, 0.0)
        wbf_ref[...] = w.astype(BF16)

    x = x_ref[...]
    if rms:
        x = x.astype(F32)
        x = x * lax.rsqrt(jnp.mean(x * x, -1, keepdims=True) + RMS_EPS) * g_ref[...]
    o_ref[...] = _dot(x.astype(BF16), wbf_ref[...]).astype(o_ref.dtype)


def _mm_tiles(M, K, N):
    tm = 1024 if M % 1024 == 0 else M
    tn = N
    for cand in (1024, 512, 256, 128):
        if N % cand == 0 and K * cand * 4 <= 8 * 1024 * 1024:
            tn = cand
            break
    return tm, tn


def _mm(x, w, out_dtype, rms_gain=None, x_col=0, col0=0, n_cols=None):
    M = x.shape[0]
    K = w.shape[0]
    N = w.shape[1] if n_cols is None else n_cols
    tm, tn = _mm_tiles(M, K, N)
    assert col0 % tn == 0
    valid_cols = w.shape[1] - col0 if col0 + N > w.shape[1] else None
    assert valid_cols is None or N == tn
    c0 = col0 // tn
    x_spec = pl.BlockSpec((tm, K), lambda n, m: (m, x_col))
    w_spec = pl.BlockSpec((K, tn), lambda n, m: (0, c0 + n))
    if rms_gain is None:
        in_specs, args = [x_spec, w_spec], (x, w)
    else:
        in_specs, args = [x_spec, pl.BlockSpec((1, K), lambda n, m: (0, 0)), w_spec], (x, rms_gain[None, :], w)
    return pl.pallas_call(
        functools.partial(_mm_kernel, valid_cols, rms_gain is not None),
        grid=(N // tn, M // tm),
        in_specs=in_specs,
        out_specs=pl.BlockSpec((tm, tn), lambda n, m: (m, n)),
        out_shape=jax.ShapeDtypeStruct((M, N), out_dtype),
        scratch_shapes=[pltpu.VMEM((K, tn), BF16)],
        compiler_params=_params("parallel", "arbitrary"),
        name="proj_matmul",
    )(*args)


def _pad_cols(w, mult=V7X_LANES):
    pad = (-w.shape[-1]) % mult
    return jnp.pad(w, ((0, 0), (0, pad))) if pad else w


def _ada_kernel(c_ref, w_ref, b_ref, o_ref):
    c = _silu(c_ref[...]).astype(BF16)
    o_ref[0] = _dot(c, w_ref[0].astype(BF16)) + b_ref[0]


def _ada_mod(c, ada_w, ada_b):
    depth, D, N = ada_w.shape
    B = c.shape[0]
    rows = 8
    c_pad = jnp.pad(c, ((0, rows - B), (0, 0)))
    tn = 1024
    out = pl.pallas_call(
        _ada_kernel,
        grid=(depth, N // tn),
        in_specs=[pl.BlockSpec((rows, D), lambda i, n: (0, 0)),
                  pl.BlockSpec((1, D, tn), lambda i, n: (i, 0, n)),
                  pl.BlockSpec((1, 1, tn), lambda i, n: (i, 0, n))],
        out_specs=pl.BlockSpec((1, rows, tn), lambda i, n: (i, 0, n)),
        out_shape=jax.ShapeDtypeStruct((depth, rows, N), F32),
        compiler_params=_params("parallel", "parallel"),
        name="ada_mod",
    )(c_pad, ada_w, ada_b.reshape(depth, 1, N))
    return out[:, :B]


ROW_TILE = 256


def _modulate_kernel(x_ref, sc_ref, sh_ref, u_ref):
    u_ref[...] = (x_ref[...] * (1.0 + sc_ref[0]) + sh_ref[0]).astype(u_ref.dtype)


def _row_specs(D, S, tr):
    vec = pl.BlockSpec((1, 1, D), lambda i: ((i * tr) // S, 0, 0))
    row = pl.BlockSpec((tr, D), lambda i: (i, 0))
    par = pl.BlockSpec((1, D), lambda i: (0, 0))
    return vec, row, par


def _modulate(x, sc, sh, S):
    T, D = x.shape
    tr = ROW_TILE
    vec, row, _ = _row_specs(D, S, tr)
    return pl.pallas_call(
        _modulate_kernel, grid=(T // tr,), in_specs=[row, vec, vec], out_specs=row,
        out_shape=jax.ShapeDtypeStruct((T, D), BF16),
        compiler_params=_params("parallel"), name="modulate",
    )(x, sc[:, None, :], sh[:, None, :])


def _deepnorm(alpha, x, h, gate, g, b):
    y = alpha * x + (1.0 + gate) * h
    mu = jnp.mean(y, -1, keepdims=True)
    yc = y - mu
    var = jnp.mean(yc * yc, -1, keepdims=True)
    return yc * lax.rsqrt(var + LN_EPS) * g + b


U32 = jnp.uint32
ROW_PARTS = D_MODEL // (2 * V7X_LANES)
assert ROW_PARTS == 8


def _bits(x):
    return lax.bitcast_convert_type(x.astype(BF16).astype(F32), U32)


def _store_row_tiled(ref, row0, value):
    n = value.shape[0]
    for j in range(ROW_PARTS):
        lo = _bits(value[:, (2 * j) * V7X_LANES:(2 * j + 1) * V7X_LANES])
        hi = _bits(value[:, (2 * j + 1) * V7X_LANES:(2 * j + 2) * V7X_LANES])
        ref[pl.ds(row0 * ROW_PARTS + j, n, stride=ROW_PARTS), :] = lax.shift_right_logical(lo, U32(16)) | hi


def _load_row_tiled(ref, row0, n):
    parts = []
    for j in range(ROW_PARTS):
        w = ref[pl.ds(row0 * ROW_PARTS + j, n, stride=ROW_PARTS), :]
        parts.append(lax.bitcast_convert_type(lax.shift_left(w, U32(16)), F32))
        parts.append(lax.bitcast_convert_type(w & U32(0xFFFF0000), F32))
    return jnp.concatenate(parts, -1)


def _ln_router_kernel(alpha, x_ref, h_ref, gate_ref, g_ref, b_ref, sc_ref, sh_ref, rw_ref, rb_ref,
                      xo_ref, u_ref, idx_ref, wgt_ref):
    xn = _deepnorm(alpha, x_ref[...], h_ref[...].astype(F32), gate_ref[0], g_ref[...], b_ref[...])
    xo_ref[...] = xn
    u = xn * (1.0 + sc_ref[0]) + sh_ref[0]
    _store_row_tiled(u_ref, 0, u)
    logits = _dot(u, rw_ref[...], precision=HIGHEST) + rb_ref[...]
    lane = _iota(logits.shape, 1)
    vals = jnp.where(lane < N_EXPERTS, logits, NEG_INF)
    top_v, top_i = [], []
    for _ in range(TOP_K):
        m = jnp.max(vals, -1, keepdims=True)
        i = jnp.min(jnp.where(vals == m, lane, V7X_LANES), -1, keepdims=True)
        top_v.append(m)
        top_i.append(i)
        vals = jnp.where(lane == i, NEG_INF, vals)
    exps = [jnp.exp(v - top_v[0]) for v in top_v]
    denom = functools.reduce(lambda a, b: a + b, exps)
    idx = jnp.zeros(logits.shape, jnp.int32)
    wgt = jnp.zeros(logits.shape, F32)
    for k in range(TOP_K):
        idx = jnp.where(lane == k, top_i[k], idx)
        wgt = jnp.where(lane == k, exps[k] / denom, wgt)
    idx_ref[...] = idx
    wgt_ref[...] = wgt


def _ln_router(x, h, gate, ln_g, ln_b, sc, sh, router_w, router_b, S, alpha):
    T, D = x.shape
    tr = ROW_TILE
    vec, row, par = _row_specs(D, S, tr)
    lane_row = pl.BlockSpec((tr, V7X_LANES), lambda i: (i, 0))
    tiled_row = pl.BlockSpec((tr * ROW_PARTS, V7X_LANES), lambda i: (i, 0))
    rw = _pad_cols(router_w)
    rb = _pad_cols(router_b[None, :])
    xo, u, idx, wgt = pl.pallas_call(
        functools.partial(_ln_router_kernel, alpha), grid=(T // tr,),
        in_specs=[row, row, vec, par, par, vec, vec,
                  pl.BlockSpec((D, V7X_LANES), lambda i: (0, 0)), pl.BlockSpec((1, V7X_LANES), lambda i: (0, 0))],
        out_specs=[row, tiled_row, lane_row, lane_row],
        out_shape=[jax.ShapeDtypeStruct((T, D), F32), jax.ShapeDtypeStruct((T * ROW_PARTS, V7X_LANES), U32),
                   jax.ShapeDtypeStruct((T, V7X_LANES), jnp.int32), jax.ShapeDtypeStruct((T, V7X_LANES), F32)],
        compiler_params=_params("parallel"), name="deepnorm_ln_router",
    )(x, h, gate[:, None, :], ln_g[None, :], ln_b[None, :], sc[:, None, :], sh[:, None, :], rw, rb)
    return xo, u, idx[:, :TOP_K], wgt[:, :TOP_K]


MOE_ROW_TILE = 512


def _expert_changed(te_ref):
    t = pl.program_id(0)
    return jnp.logical_or(t == 0, te_ref[t] != te_ref[jnp.maximum(t - 1, 0)])


def _moe_gu_kernel(te_ref, tv_ref, x_ref, w_ref, b_ref, h_ref, wbf_ref):
    @pl.when(_expert_changed(te_ref))
    def _():
        wbf_ref[...] = w_ref[0].astype(BF16)

    @pl.when(tv_ref[pl.program_id(0)] > 0)
    def _():
        x = _load_row_tiled(x_ref, 0, x_ref.shape[0] // ROW_PARTS).astype(BF16)
        gu = _dot(x, wbf_ref[...]) + b_ref[0]
        gl = jnp.minimum(gu[:, :EXPERT_FF], SWIGLU_LIMIT)
        up = jnp.clip(gu[:, EXPERT_FF:], -SWIGLU_LIMIT, SWIGLU_LIMIT)
        h_ref[...] = ((up + 1.0) * gl * _sigmoid(gl * SWIGLU_ALPHA)).astype(h_ref.dtype)

    @pl.when(tv_ref[pl.program_id(0)] == 0)
    def _():
        h_ref[...] = jnp.zeros(h_ref.shape, h_ref.dtype)


def _moe_down_kernel(te_ref, tv_ref, h_ref, w_ref, b_ref, y_ref, wbf_ref):
    @pl.when(_expert_changed(te_ref))
    def _():
        wbf_ref[...] = w_ref[0].astype(BF16)

    @pl.when(tv_ref[pl.program_id(0)] > 0)
    def _():
        _store_row_tiled(y_ref, 0, _dot(h_ref[...], wbf_ref[...]) + b_ref[0])

    @pl.when(tv_ref[pl.program_id(0)] == 0)
    def _():
        y_ref[...] = jnp.zeros(y_ref.shape, y_ref.dtype)


def _route_metadata(top_idx, tm):
    T = top_idx.shape[0]
    P = T * TOP_K
    n_tiles = (P + N_EXPERTS * (tm - 1)) // tm
    e_flat = top_idx.reshape(P)
    onehot = (e_flat[:, None] == jnp.arange(N_EXPERTS)[None, :]).astype(jnp.int32)
    csum = jnp.cumsum(onehot, axis=0)
    counts = csum[-1]
    rank = jnp.sum(csum * onehot, axis=1) - 1
    padded = ((counts + tm - 1) // tm) * tm
    ends_p = jnp.cumsum(padded)
    starts_p = ends_p - padded
    dest = (starts_p[e_flat] + rank).astype(jnp.int32)
    tile_start = jnp.arange(n_tiles, dtype=jnp.int32) * tm
    tile_valid = (tile_start < ends_p[-1]).astype(jnp.int32)
    tile_expert = jnp.sum((tile_start[:, None] >= ends_p[None, :]).astype(jnp.int32), axis=1)
    last_expert = jnp.max(jnp.where(counts > 0, jnp.arange(N_EXPERTS), 0))
    tile_expert = jnp.where(tile_valid > 0, tile_expert, last_expert).astype(jnp.int32)
    return dest, tile_expert, tile_valid, n_tiles


DISPATCH_ROWS = 2048


def _dispatch_kernel(dest_ref, u_ref, init_hbm, xs_hbm, sem):
    del init_hbm
    R = dest_ref.shape[-1]

    def row(ref, i):
        return ref.at[pl.ds(pl.multiple_of(i * ROW_PARTS, ROW_PARTS), ROW_PARTS)]

    def issue(t, carry):
        for k in range(TOP_K):
            pltpu.make_async_copy(row(u_ref, t), row(xs_hbm, dest_ref[0, 0, t * TOP_K + k]), sem).start(priority=k % 2)
        return carry

    lax.fori_loop(0, R // TOP_K, issue, 0, unroll=2)
    n = (R // TOP_K) * ROW_PARTS
    for _ in range(TOP_K):
        pltpu.make_async_copy(u_ref, xs_hbm.at[pl.ds(0, n)], sem).wait()


def _dispatch(u, dest, rows):
    P = dest.shape[0]
    R = DISPATCH_ROWS
    shape = (rows * ROW_PARTS, u.shape[1])
    return pl.pallas_call(
        _dispatch_kernel,
        grid=(P // R,),
        in_specs=[pl.BlockSpec((1, 1, R), lambda s: (s, 0, 0), memory_space=pltpu.SMEM),
                  pl.BlockSpec(((R // TOP_K) * ROW_PARTS, u.shape[1]), lambda s: (s, 0)),
                  pl.BlockSpec(memory_space=pl.ANY)],
        out_specs=pl.BlockSpec(memory_space=pl.ANY),
        out_shape=jax.ShapeDtypeStruct(shape, u.dtype),
        scratch_shapes=[pltpu.SemaphoreType.DMA(())],
        input_output_aliases={2: 0},
        compiler_params=_params("arbitrary"), name="moe_dispatch",
    )(dest.reshape(P // R, 1, R), u, jnp.zeros(shape, u.dtype))


def _moe_ffn(u, top_idx, layer, w_gu, b_gu, w_down, b_down):
    D = D_MODEL
    tm = MOE_ROW_TILE
    dest, tile_expert, tile_valid, n_tiles = _route_metadata(top_idx, tm)
    tile_expert = tile_expert + layer * N_EXPERTS
    w_gu, w_down = (w.reshape((-1,) + w.shape[2:]) for w in (w_gu, w_down))
    b_gu, b_down = (b.reshape(-1, b.shape[-1]) for b in (b_gu, b_down))
    rows = n_tiles * tm
    x_sorted = _dispatch(u, dest, rows)
    ff2 = 2 * EXPERT_FF
    tiled = pl.BlockSpec((tm * ROW_PARTS, V7X_LANES), lambda t, te, tv: (t, 0))
    h = pl.pallas_call(
        _moe_gu_kernel,
        grid_spec=pltpu.PrefetchScalarGridSpec(
            num_scalar_prefetch=2, grid=(n_tiles,),
            in_specs=[tiled,
                      pl.BlockSpec((1, D, ff2), lambda t, te, tv: (te[t], 0, 0)),
                      pl.BlockSpec((1, 1, ff2), lambda t, te, tv: (te[t], 0, 0))],
            out_specs=pl.BlockSpec((tm, EXPERT_FF), lambda t, te, tv: (t, 0)),
            scratch_shapes=[pltpu.VMEM((D, ff2), BF16)]),
        out_shape=jax.ShapeDtypeStruct((rows, EXPERT_FF), BF16),
        compiler_params=_params("arbitrary"), name="moe_gate_up",
    )(tile_expert, tile_valid, x_sorted, w_gu, b_gu[:, None, :])
    y = pl.pallas_call(
        _moe_down_kernel,
        grid_spec=pltpu.PrefetchScalarGridSpec(
            num_scalar_prefetch=2, grid=(n_tiles,),
            in_specs=[pl.BlockSpec((tm, EXPERT_FF), lambda t, te, tv: (t, 0)),
                      pl.BlockSpec((1, EXPERT_FF, D), lambda t, te, tv: (te[t], 0, 0)),
                      pl.BlockSpec((1, 1, D), lambda t, te, tv: (te[t], 0, 0))],
            out_specs=tiled,
            scratch_shapes=[pltpu.VMEM((EXPERT_FF, D), BF16)]),
        out_shape=jax.ShapeDtypeStruct((rows * ROW_PARTS, V7X_LANES), U32),
        compiler_params=_params("arbitrary"), name="moe_down",
    )(tile_expert, tile_valid, h, w_down, b_down[:, None, :])
    return y, dest


COMBINE_TILE = 256
COMBINE_SUB = 64


def _combine_ln_kernel(alpha, dest_ref, y_hbm, x_ref, wgt_ref, gate_ref, g_ref, b_ref, sc_ref, sh_ref,
                       xo_ref, u_ref, buf_ref, sems):
    n_sub = COMBINE_TILE // COMBINE_SUB

    def row(ref, i):
        return ref.at[pl.ds(pl.multiple_of(i * ROW_PARTS, ROW_PARTS), ROW_PARTS)]

    for j in range(n_sub):
        def issue(t, carry, j=j):
            tok = j * COMBINE_SUB + t
            for k in range(TOP_K):
                pltpu.make_async_copy(row(y_hbm, dest_ref[0, 0, tok * TOP_K + k]),
                                      row(buf_ref, k * COMBINE_TILE + tok), sems.at[j]).start(priority=k % 2)
            return carry

        lax.fori_loop(0, COMBINE_SUB, issue, 0, unroll=2)
    for j in range(n_sub):
        lo = j * COMBINE_SUB
        n = COMBINE_SUB * TOP_K * ROW_PARTS
        pltpu.make_async_copy(y_hbm.at[pl.ds(0, n)], buf_ref.at[pl.ds(0, n)], sems.at[j]).wait()
        wgt = wgt_ref[pl.ds(lo, COMBINE_SUB), :]
        f = sum(_load_row_tiled(buf_ref, k * COMBINE_TILE + lo, COMBINE_SUB) * wgt[:, k:k + 1] for k in range(TOP_K))
        xn = _deepnorm(alpha, x_ref[pl.ds(lo, COMBINE_SUB), :], f, gate_ref[0], g_ref[...], b_ref[...])
        xo_ref[pl.ds(lo, COMBINE_SUB), :] = xn
        u_ref[pl.ds(lo, COMBINE_SUB), :] = (xn * (1.0 + sc_ref[0]) + sh_ref[0]).astype(u_ref.dtype)


def _combine_ln(x, y, dest, top_w, gate, ln_g, ln_b, sc, sh, S, alpha):
    T, D = x.shape
    tr = COMBINE_TILE
    vec, row, par = _row_specs(D, S, tr)
    n_sub = tr // COMBINE_SUB
    return pl.pallas_call(
        functools.partial(_combine_ln_kernel, alpha), grid=(T // tr,),
        in_specs=[pl.BlockSpec((1, 1, tr * TOP_K), lambda i: (i, 0, 0), memory_space=pltpu.SMEM),
                  pl.BlockSpec(memory_space=pl.ANY), row,
                  pl.BlockSpec((tr, TOP_K), lambda i: (i, 0)), vec, par, par, vec, vec],
        out_specs=[row, row],
        out_shape=[jax.ShapeDtypeStruct((T, D), F32), jax.ShapeDtypeStruct((T, D), BF16)],
        scratch_shapes=[pltpu.VMEM((tr * TOP_K * ROW_PARTS, V7X_LANES), U32), pltpu.SemaphoreType.DMA((n_sub,))],
        compiler_params=_params("arbitrary"), name="moe_combine_ln",
    )(dest.reshape(T // tr, 1, tr * TOP_K), y, x, top_w, gate[:, None, :], ln_g[None, :], ln_b[None, :],
      sc[:, None, :], sh[:, None, :])


ATTN_TILE = 256


def _softmax_pv(s, v):
    p = jnp.exp(s - jnp.max(s, -1, keepdims=True))
    return _dot(p.astype(BF16), v) / jnp.sum(p, -1, keepdims=True)


CAUSAL_VARIANTS = 4


def _causal_widths(n_tiles):
    step = max(n_tiles // CAUSAL_VARIANTS, 1)
    bounds = list(range(step, n_tiles, step)) + [n_tiles]
    return [(n, n) for n in bounds]


def _mla_attn_kernel(scale, qn_ref, qr_ref, cos_ref, sin_ref, kv_ref, kr_ref, o_ref):
    qi = pl.program_id(2)
    t = ATTN_TILE
    n_tiles = kv_ref.shape[1] // t
    x = qr_ref[0].astype(F32)
    half = MLA_ROPE // 2
    first_half = _iota(x.shape, 1) % MLA_ROPE < half
    swapped = jnp.where(first_half, pltpu.roll(x, x.shape[1] - half, 1), pltpu.roll(x, half, 1))
    qr_all = (x * cos_ref[...] + swapped * sin_ref[...]).astype(BF16)

    def attend(n_kv):
        w = n_kv * t
        visible = _iota((t, w), 1) <= _iota((t, w), 0) + qi * t
        outs = []
        for hh in range(2):
            qn = qn_ref[0, :, hh * MLA_NOPE:(hh + 1) * MLA_NOPE]
            qr = qr_all[:, hh * MLA_ROPE:(hh + 1) * MLA_ROPE]
            c0 = hh * (MLA_NOPE + MLA_V)
            s = (_dot_nt(qn, kv_ref[0, 0:w, c0:c0 + MLA_NOPE]) + _dot_nt(qr, kr_ref[0, 0:w, :])) * scale
            s = jnp.where(visible, s, NEG_INF)
            outs.append(_softmax_pv(s, kv_ref[0, 0:w, c0 + MLA_NOPE:c0 + MLA_NOPE + MLA_V]))
        o_ref[0] = jnp.concatenate(outs, -1).astype(o_ref.dtype)

    lo = 0
    for hi, n_kv in _causal_widths(n_tiles):
        pl.when(jnp.logical_and(qi >= lo, qi < hi))(functools.partial(attend, n_kv))
        lo = hi


def _rope(x, cos, sin):
    half = x.shape[-1] // 2
    x1, x2 = x[..., :half], x[..., half:]
    return jnp.concatenate([x1 * cos - x2 * sin, x2 * cos + x1 * sin], -1)


def _mla_mixer(u, B, S, w_in, q_norm, kv_norm, w_qb, w_kvb, w_o):
    H = MLA_HEADS
    n_lat = MLA_Q_LORA + MLA_KV_LORA
    lat = _mm(u, w_in, F32, n_cols=n_lat)
    k_rope = _mm(u, w_in, F32, col0=n_lat, n_cols=V7X_LANES)[:, :MLA_ROPE]
    wq = w_qb.reshape(MLA_Q_LORA, H, MLA_NOPE + MLA_ROPE)
    q_nope = _mm(lat, wq[:, :, :MLA_NOPE].reshape(MLA_Q_LORA, H * MLA_NOPE), BF16, rms_gain=q_norm, x_col=0)
    q_rope = _mm(lat, wq[:, :, MLA_NOPE:].reshape(MLA_Q_LORA, H * MLA_ROPE), F32, rms_gain=q_norm, x_col=0)
    kv = _mm(lat, w_kvb, BF16, rms_gain=kv_norm, x_col=1)
    inv_freq = ROPE_THETA ** (-jnp.arange(MLA_ROPE // 2, dtype=F32) / (MLA_ROPE // 2))
    ang = jnp.arange(S, dtype=F32)[:, None] * inv_freq[None, :]
    cos, sin = jnp.cos(ang), jnp.sin(ang)
    k_rope = _rope(k_rope.reshape(B, S, MLA_ROPE), cos, sin).astype(BF16)
    cos2 = jnp.tile(jnp.concatenate([cos, cos], -1), (1, 2))
    sin2 = jnp.tile(jnp.concatenate([-sin, sin], -1), (1, 2))
    t = ATTN_TILE
    o = pl.pallas_call(
        functools.partial(_mla_attn_kernel, (MLA_NOPE + MLA_ROPE) ** -0.5),
        grid=(B, H // 2, S // t),
        in_specs=[pl.BlockSpec((1, t, 2 * MLA_NOPE), lambda b, h, i: (b, i, h)),
                  pl.BlockSpec((1, t, 2 * MLA_ROPE), lambda b, h, i: (b, i, h)),
                  pl.BlockSpec((t, 2 * MLA_ROPE), lambda b, h, i: (i, 0)),
                  pl.BlockSpec((t, 2 * MLA_ROPE), lambda b, h, i: (i, 0)),
                  pl.BlockSpec((1, S, 2 * (MLA_NOPE + MLA_V)), lambda b, h, i: (b, 0, h)),
                  pl.BlockSpec((1, S, MLA_ROPE), lambda b, h, i: (b, 0, 0))],
        out_specs=pl.BlockSpec((1, t, 2 * MLA_V), lambda b, h, i: (b, i, h)),
        out_shape=jax.ShapeDtypeStruct((B, S, H * MLA_V), BF16),
        compiler_params=_params("parallel", "parallel", "arbitrary"), name="mla_attention",
    )(q_nope.reshape(B, S, -1), q_rope.reshape(B, S, -1), cos2, sin2, kv.reshape(B, S, -1), k_rope)
    return _mm(o.reshape(B * S, H * MLA_V), w_o, F32)


def _t5_bucket(dist):
    n = jnp.maximum(dist, 0)
    max_exact = REL_BUCKETS // 2
    large = max_exact + (jnp.log(jnp.maximum(n, 1).astype(F32) / max_exact)
                         / math.log(REL_MAX_DIST / max_exact) * (REL_BUCKETS - max_exact)).astype(jnp.int32)
    large = jnp.minimum(large, REL_BUCKETS - 1)
    return jnp.where(n < max_exact, n, large)


MOBA_FAR = 2


def _moba_kernel(scale, n_sel, tab_ref, q_ref, k_ref, v_ref, bkt_ref, o_ref, kbf_ref, vbf_ref, kmean_ref, bias_ref):
    h, b, qi = pl.program_id(0), pl.program_id(1), pl.program_id(2)
    L = MOBA_BLOCK
    S = k_ref.shape[1]
    n_blk = S // L

    @pl.when(jnp.logical_and(b == 0, qi == 0))
    def _():
        for d in range(MOBA_FAR + 1):
            bucket = bkt_ref[d]
            tile = jnp.zeros((L, L), F32)
            for e in range(REL_BUCKETS):
                tile = jnp.where(bucket == e, tab_ref[e, h], tile)
            bias_ref[d] = tile

    @pl.when(qi == 0)
    def _():
        k = k_ref[0].astype(F32)
        kbf_ref[...] = k.astype(BF16)
        vbf_ref[...] = v_ref[0].astype(BF16)
        kmean_ref[...] = jnp.zeros(kmean_ref.shape, F32)
        kmean_ref[0:n_blk, :] = jnp.mean(k.reshape(n_blk, L, k.shape[-1]), axis=1)

    q = q_ref[0].astype(F32)
    qb = q.astype(BF16)
    nb = kmean_ref.shape[0]
    blk = _iota((nb, L), 0)
    gate = jnp.where(blk < qi, _dot_nt(kmean_ref[...], q, precision=HIGHEST), NEG_INF)
    picked = jnp.zeros((nb, L), F32)
    for j in range(n_blk - 1):
        gj = gate[j:j + 1, :]
        beats = jnp.logical_or(gate > gj, jnp.logical_and(gate == gj, blk < j))
        in_topk = jnp.sum(jnp.where(beats, 1.0, 0.0), 0, keepdims=True) < n_sel
        picked = jnp.where(blk == j, jnp.where(jnp.logical_and(in_topk, j < qi), 1.0, 0.0), picked)
    picked = _dot_tn(picked, jnp.where(_iota((nb, nb), 0) == _iota((nb, nb), 1), 1.0, 0.0))
    causal_add = jnp.where(_iota((L, L), 1) <= _iota((L, L), 0), 0.0, NEG_INF)
    past_add = [jnp.where(picked[:, j:j + 1] > 0.5, 0.0, NEG_INF) for j in range(n_blk)]

    def attend(n_kv):
        w = n_kv * L
        s = _dot_nt(qb, kbf_ref[0:w, :]) * scale
        parts = []
        for j in range(n_kv):
            bias = bias_ref[jnp.clip(qi - j, 0, MOBA_FAR)]
            parts.append(s[:, j * L:(j + 1) * L] + bias + jnp.where(j == qi, causal_add, past_add[j]))
        o_ref[0] = _softmax_pv(jnp.concatenate(parts, -1), vbf_ref[0:w, :]).astype(o_ref.dtype)

    lo = 0
    for hi, n_kv in _causal_widths(n_blk):
        pl.when(jnp.logical_and(qi >= lo, qi < hi))(functools.partial(attend, n_kv))
        lo = hi


def _moba_mixer(u, B, S, w_in, w_o, rel_bias):
    H, Dh, L = MOBA_HEADS, u.shape[1] // MOBA_HEADS, MOBA_BLOCK
    n_blk = S // L
    assert S % L == 0 and Dh == V7X_LANES and (MOBA_FAR - 1) * L >= REL_MAX_DIST
    n_sel = max(min(MOBA_TOPK, n_blk - 1), 1)
    qkv = _mm(u, w_in, BF16).reshape(B, S, 3 * H * Dh)
    qk = jnp.arange(L)[:, None] - jnp.arange(L)[None, :]
    bucket = _t5_bucket(jnp.stack([qk + d * L for d in range(MOBA_FAR + 1)])).astype(jnp.int32)
    o = pl.pallas_call(
        functools.partial(_moba_kernel, Dh ** -0.5, n_sel),
        grid=(H, B, n_blk),
        in_specs=[pl.BlockSpec(memory_space=pltpu.SMEM),
                  pl.BlockSpec((1, L, Dh), lambda h, b, i: (b, i, h)),
                  pl.BlockSpec((1, S, Dh), lambda h, b, i: (b, 0, H + h)),
                  pl.BlockSpec((1, S, Dh), lambda h, b, i: (b, 0, 2 * H + h)),
                  pl.BlockSpec((MOBA_FAR + 1, L, L), lambda h, b, i: (0, 0, 0))],
        out_specs=pl.BlockSpec((1, L, Dh), lambda h, b, i: (b, i, h)),
        out_shape=jax.ShapeDtypeStruct((B, S, H * Dh), BF16),
        scratch_shapes=[pltpu.VMEM((S, Dh), BF16), pltpu.VMEM((S, Dh), BF16),
                        pltpu.VMEM((-(-n_blk // 8) * 8, Dh), F32),
                        pltpu.VMEM((MOBA_FAR + 1, L, L), F32)],
        compiler_params=_params("parallel", "arbitrary", "arbitrary"), name="moba_attention",
    )(rel_bias, qkv, qkv, qkv, bucket)
    return _mm(o.reshape(B * S, H * Dh), w_o, F32)


GDN_HEAD_GROUP = 16


def _lane_sum(x):
    return _dot(x.astype(BF16), jnp.ones((x.shape[1], x.shape[1]), BF16))


def _l2norm(x):
    return x * lax.rsqrt(_lane_sum(x * x) + L2_EPS)


def _dotb(a, b):
    return _dot(a.astype(BF16), b.astype(BF16))


def _unit_lower_inverse(a_lows, block):
    C = a_lows[0].shape[0]
    r = _iota((C, C), 0)
    c = _iota((C, C), 1)
    eye = jnp.where(r == c, 1.0, 0.0)
    same = (r // block) == (c // block)
    a_d = [jnp.where(same, a, 0.0) for a in a_lows]
    a_off = [a - d for a, d in zip(a_lows, a_d)]
    inv_d = [eye - d for d in a_d]
    pw = a_d
    k = 2
    while k < block:
        pw = [_dotb(p, p) for p in pw]
        inv_d = [_dotb(i, eye + p) for i, p in zip(inv_d, pw)]
        k *= 2
    n = [_dotb(i, o) for i, o in zip(inv_d, a_off)]
    inv_n = [eye - x for x in n]
    pw = n
    k = 2
    while k < C // block:
        pw = [_dotb(p, p) for p in pw]
        inv_n = [_dotb(i, eye + p) for i, p in zip(inv_n, pw)]
        k *= 2
    return [_dotb(i, d) for i, d in zip(inv_n, inv_d)]


GDN_HIST = 16


def _causal_conv_silu(x_ref, w_ref, hist_ref):
    C = x_ref.shape[1]
    x = x_ref[0]
    hist_ref[GDN_HIST:GDN_HIST + C, :] = x
    hist = hist_ref[...]
    t = _iota((C, GDN_HIST + C), 0)
    r = _iota((C, GDN_HIST + C), 1)
    y = x.astype(F32) * w_ref[GDN_CONV - 1:GDN_CONV, :]
    for i in range(GDN_CONV - 1):
        shift = jnp.where(r == t + (GDN_HIST - (GDN_CONV - 1) + i), 1.0, 0.0).astype(BF16)
        y = y + _dot(shift, hist) * w_ref[i:i + 1, :]
    hist_ref[0:GDN_HIST, :] = hist_ref[C:C + GDN_HIST, :]
    return _silu(y)


def _gdn_kernel(q_ref, k_ref, v_ref, z_ref, cwq_ref, cwk_ref, cwv_ref, gc_ref, gct_ref, beta_ref, ng_ref,
                o_ref, state_ref, hq_ref, hk_ref, hv_ref):
    G, C, DK, DV = GDN_HEAD_GROUP, GDN_CHUNK, GDN_DK, GDN_DV
    rep = GDN_V_HEADS // GDN_K_HEADS
    heads = range(G)

    @pl.when(pl.program_id(2) == 0)
    def _():
        state_ref[...] = jnp.zeros(state_ref.shape, F32)
        for hist in (hq_ref, hk_ref, hv_ref):
            hist[0:GDN_HIST, :] = jnp.zeros((GDN_HIST, hist.shape[1]), hist.dtype)

    qc = _causal_conv_silu(q_ref, cwq_ref, hq_ref)
    kc = _causal_conv_silu(k_ref, cwk_ref, hk_ref)
    vc = _causal_conv_silu(v_ref, cwv_ref, hv_ref)
    r = _iota((C, C), 0)
    c = _iota((C, C), 1)
    tri = c <= r
    strict = c < r
    gc = gc_ref[0, 0]
    gct = gct_ref[0, 0, 0]
    beta = beta_ref[0, 0]
    q = [_l2norm(qc[:, i * DK:(i + 1) * DK]) * (DK ** -0.5) for i in range(G // rep)]
    k = [_l2norm(kc[:, i * DK:(i + 1) * DK]) for i in range(G // rep)]
    assert DK == V7X_LANES and DV == V7X_LANES
    kb = [x.astype(BF16) for x in k]
    qk = [_dot_nt(a.astype(BF16), b) for a, b in zip(q, kb)]
    kk = [_dot_nt(b, b) for b in kb]
    g_col = [gc[:, h:h + 1] for h in heads]
    b_col = [beta[:, h:h + 1] for h in heads]
    decay = [jnp.where(tri, jnp.exp(jnp.where(tri, g_col[h] - gct[h:h + 1, :], 0.0)), 0.0) for h in heads]
    t_inv = _unit_lower_inverse([jnp.where(strict, kk[h // rep] * b_col[h] * decay[h], 0.0) for h in heads], 16)
    e_g = [jnp.exp(g) for g in g_col]
    rhs = [jnp.concatenate([vc[:, h * DV:(h + 1) * DV] * b_col[h], k[h // rep] * (b_col[h] * e_g[h])], -1)
           for h in heads]
    sol = [_dot(t_inv[h].astype(BF16), rhs[h].astype(BF16)) for h in heads]
    state = [state_ref[h] for h in heads]
    state_b = [s.astype(BF16) for s in state]
    v_new = [sol[h][:, :DV] - _dot(sol[h][:, DV:].astype(BF16), state_b[h]) for h in heads]
    v_new_b = [x.astype(BF16) for x in v_new]
    attn = [jnp.where(tri, qk[h // rep] * decay[h], 0.0).astype(BF16) for h in heads]
    o = [_dot((q[h // rep] * e_g[h]).astype(BF16), state_b[h]) + _dot(attn[h], v_new_b[h]) for h in heads]
    g_last = [g[C - 1:C, :] for g in g_col]
    k_tail = [(k[h // rep] * jnp.exp(g_last[h] - g_col[h])).astype(BF16) for h in heads]
    new_state = [state[h] * jnp.exp(g_last[h]) + _dot_tn(k_tail[h], v_new_b[h]) for h in heads]
    o = [x * lax.rsqrt(_lane_sum(x * x) * (1.0 / DV) + RMS_EPS) * ng_ref[...] for x in o]
    o = [o[h] * _silu(z_ref[0, :, h * DV:(h + 1) * DV].astype(F32)) for h in heads]
    for h in heads:
        state_ref[h] = new_state[h]
    o_ref[0] = jnp.concatenate(o, -1).astype(o_ref.dtype)


def _gdn_mixer(u, B, S, w_in, conv_w, a_log, dt_bias, norm_g, w_o):
    HK, HV, DK, DV, C, G = GDN_K_HEADS, GDN_V_HEADS, GDN_DK, GDN_DV, GDN_CHUNK, GDN_HEAD_GROUP
    qk_dim, v_dim = HK * DK, HV * DV
    n_main = 2 * qk_dim + 2 * v_dim
    proj = _mm(u, w_in, BF16, n_cols=n_main).reshape(B, S, n_main)
    ba = _mm(u, w_in, F32, col0=n_main, n_cols=V7X_LANES).reshape(B, S, -1)
    n_conv = 2 * qk_dim + v_dim
    beta = jax.nn.sigmoid(ba[:, :, :HV])
    g = -jnp.exp(a_log) * jax.nn.softplus(ba[:, :, HV:2 * HV] + dt_bias)
    N = S // C
    gc = jnp.cumsum(g.reshape(B, N, C, HV), axis=2)
    HG = HV // G
    gc_g = gc.reshape(B, N, C, HG, G).transpose(0, 3, 1, 2, 4).reshape(B, HG, S, G)
    gct_g = gc.reshape(B, N, C, HG, G).transpose(0, 3, 1, 4, 2)
    beta_g = beta.reshape(B, S, HG, G).transpose(0, 2, 1, 3)
    kw = (G // (HV // HK)) * DK
    vw = G * DV
    k_blk, v_blk, z_blk = qk_dim // kw, 2 * qk_dim // vw, n_conv // vw
    o = pl.pallas_call(
        _gdn_kernel,
        grid=(B, HG, N),
        in_specs=[pl.BlockSpec((1, C, kw), lambda b, h, n: (b, n, h)),
                  pl.BlockSpec((1, C, kw), lambda b, h, n: (b, n, k_blk + h)),
                  pl.BlockSpec((1, C, vw), lambda b, h, n: (b, n, v_blk + h)),
                  pl.BlockSpec((1, C, vw), lambda b, h, n: (b, n, z_blk + h)),
                  pl.BlockSpec((GDN_CONV, kw), lambda b, h, n: (0, h)),
                  pl.BlockSpec((GDN_CONV, kw), lambda b, h, n: (0, k_blk + h)),
                  pl.BlockSpec((GDN_CONV, vw), lambda b, h, n: (0, v_blk + h)),
                  pl.BlockSpec((1, 1, C, G), lambda b, h, n: (b, h, n, 0)),
                  pl.BlockSpec((1, 1, 1, G, C), lambda b, h, n: (b, h, n, 0, 0)),
                  pl.BlockSpec((1, 1, C, G), lambda b, h, n: (b, h, n, 0)),
                  pl.BlockSpec((1, DV), lambda b, h, n: (0, 0))],
        out_specs=pl.BlockSpec((1, C, vw), lambda b, h, n: (b, n, h)),
        out_shape=jax.ShapeDtypeStruct((B, S, v_dim), BF16),
        scratch_shapes=[pltpu.VMEM((G, DK, DV), F32), pltpu.VMEM((GDN_HIST + C, kw), BF16),
                        pltpu.VMEM((GDN_HIST + C, kw), BF16), pltpu.VMEM((GDN_HIST + C, vw), BF16)],
        compiler_params=_params("parallel", "parallel", "arbitrary"), name="gdn_chunked",
    )(proj, proj, proj, proj, conv_w, conv_w, conv_w, gc_g, gct_g, beta_g, norm_g[None, :])
    return _mm(o.reshape(B * S, v_dim), w_o, F32)


def _gla_kernel(scale, q_ref, k_ref, v_ref, og_ref, gk_ref, wgk_ref, bgk_ref, ng_ref, o_ref, state_ref):
    C = GLA_CHUNK
    H, dv, dk = state_ref.shape
    heads = range(H)

    @pl.when(pl.program_id(1) == 0)
    def _():
        state_ref[...] = jnp.zeros(state_ref.shape, F32)

    x = _dot(gk_ref[0].astype(BF16), wgk_ref[...].astype(BF16)) + bgk_ref[...]
    log_alpha = (jnp.minimum(x, 0.0) - jnp.log(1.0 + jnp.exp(-jnp.abs(x)))) / GLA_GATE_NORMALIZER
    r = _iota((C, C), 0)
    c = _iota((C, C), 1)
    causal = c <= r
    b = _dot(jnp.where(causal, 1.0, 0.0), log_alpha, precision=HIGHEST)
    b_last = b[C - 1:C, :]
    q = q_ref[0].astype(F32) * scale
    k = k_ref[0].astype(F32)
    q_dec = (q * jnp.exp(b)).astype(BF16)
    k_inv = (k * jnp.exp(-b)).astype(BF16)
    k_tail = (k * jnp.exp(b_last - b)).astype(BF16)
    decay = jnp.exp(b_last)
    ks = [slice(h * dk, (h + 1) * dk) for h in heads]
    v = [v_ref[0, :, h * dv:(h + 1) * dv] for h in heads]
    attn = [jnp.where(causal, _dot_nt(q_dec[:, ks[h]], k_inv[:, ks[h]]), 0.0).astype(BF16) for h in heads]
    state_t = [state_ref[h] for h in heads]
    o = [_dot(attn[h], v[h]) + _dot_nt(q_dec[:, ks[h]], state_t[h].astype(BF16)) for h in heads]
    new_state = [state_t[h] * decay[:, ks[h]] + _dot_tn(v[h], k_tail[:, ks[h]]) for h in heads]
    o = [x * lax.rsqrt(jnp.mean(x * x, -1, keepdims=True) + RMS_EPS) * ng_ref[...] for x in o]
    for h in heads:
        state_ref[h] = new_state[h]
    o_ref[0] = (jnp.concatenate(o, -1) * _silu(og_ref[0].astype(F32))).astype(o_ref.dtype)


def _gla_mixer(u, B, S, w_in, w_gk, b_gk, norm_g, w_o):
    D = u.shape[1]
    H, C = GLA_HEADS, GLA_CHUNK
    key_dim, val_dim = D // 2, D
    dk, dv = key_dim // H, val_dim // H
    n_main = 2 * key_dim + 2 * val_dim
    proj = _mm(u, w_in, BF16, n_cols=n_main).reshape(B, S, n_main)
    gk = _mm(u, w_in, F32, col0=n_main, n_cols=V7X_LANES).reshape(B, S, -1)
    wgk = jnp.pad(w_gk, ((0, gk.shape[-1] - GLA_GATE_RANK), (0, 0)))
    o = pl.pallas_call(
        functools.partial(_gla_kernel, dk ** -0.5),
        grid=(B, S // C),
        in_specs=[pl.BlockSpec((1, C, key_dim), lambda b, n: (b, n, 0)),
                  pl.BlockSpec((1, C, key_dim), lambda b, n: (b, n, 1)),
                  pl.BlockSpec((1, C, val_dim), lambda b, n: (b, n, 2 * key_dim // val_dim)),
                  pl.BlockSpec((1, C, val_dim), lambda b, n: (b, n, 2 * key_dim // val_dim + 1)),
                  pl.BlockSpec((1, C, gk.shape[-1]), lambda b, n: (b, n, 0)),
                  pl.BlockSpec((gk.shape[-1], key_dim), lambda b, n: (0, 0)),
                  pl.BlockSpec((1, key_dim), lambda b, n: (0, 0)),
                  pl.BlockSpec((1, dv), lambda b, n: (0, 0))],
        out_specs=pl.BlockSpec((1, C, val_dim), lambda b, n: (b, n, 0)),
        out_shape=jax.ShapeDtypeStruct((B, S, val_dim), BF16),
        scratch_shapes=[pltpu.VMEM((H, dv, dk), F32)],
        compiler_params=_params("parallel", "arbitrary"), name="gla_chunked",
    )(proj, proj, proj, proj, gk, wgk, b_gk[None, :], norm_g[None, :])
    return _mm(o.reshape(B * S, val_dim), w_o, F32)


def kernel(x, c, rel_bias, mla_w_in, mla_q_norm, mla_kv_norm, mla_w_qb, mla_w_kvb, mla_w_o, gdn_w_in, gdn_conv_w, gdn_a_log, gdn_dt_bias, gdn_norm, gdn_w_o, gla_w_in, gla_w_gk, gla_b_gk, gla_norm, gla_w_o, moba_w_in, moba_w_o, ada_w, ada_b, ln_g, ln_b, router_w, router_b, moe_w_gu, moe_b_gu, moe_w_down, moe_b_down):
    B, S, D = x.shape
    assert D == D_MODEL
    depth = ada_w.shape[0]
    alpha = (2 * depth) ** 0.25
    mod = _ada_mod(c, ada_w, ada_b)
    sh_a, sc_a, g_a, sh_f, sc_f, g_f = (mod[:, :, k * D:(k + 1) * D] for k in range(6))
    xt = x.reshape(B * S, D)
    u = _modulate(xt, sc_a[0], sh_a[0], S)
    for i in range(depth):
        m, j = i % N_MIXERS, i // N_MIXERS
        if m == 0:
            h = _mla_mixer(u, B, S, mla_w_in[j], mla_q_norm[j], mla_kv_norm[j], mla_w_qb[j], mla_w_kvb[j], mla_w_o[j])
        elif m == 1:
            h = _gdn_mixer(u, B, S, gdn_w_in[j], gdn_conv_w[j], gdn_a_log[j], gdn_dt_bias[j], gdn_norm[j], gdn_w_o[j])
        elif m == 2:
            h = _gla_mixer(u, B, S, gla_w_in[j], gla_w_gk[j], gla_b_gk[j], gla_norm[j], gla_w_o[j])
        else:
            h = _moba_mixer(u, B, S, moba_w_in[j], moba_w_o[j], rel_bias)
        xt, u, top_idx, top_w = _ln_router(xt, h, g_a[i], ln_g[i, 0], ln_b[i, 0], sc_f[i], sh_f[i],
                                           router_w[i], router_b[i], S, alpha)
        y, dest = _moe_ffn(u, top_idx, i, moe_w_gu, moe_b_gu, moe_w_down, moe_b_down)
        nxt = (i + 1) % depth
        xt, u = _combine_ln(xt, y, dest, top_w, g_f[i], ln_g[i, 1], ln_b[i, 1], sc_a[nxt], sh_a[nxt], S, alpha)
    return xt.reshape(B, S, D)
```

```python
import functools
import math

import jax
import jax.numpy as jnp
from jax import lax
from jax.experimental import pallas as pl
from jax.experimental.pallas import tpu as pltpu

D_MODEL = 2048
N_MIXERS = 4
MLA_HEADS, MLA_Q_LORA, MLA_KV_LORA, MLA_NOPE, MLA_ROPE, MLA_V = 16, 512, 512, 128, 64, 128
ROPE_THETA = 10000.0
GDN_K_HEADS, GDN_V_HEADS, GDN_DK, GDN_DV, GDN_CONV, GDN_CHUNK = 16, 32, 128, 128, 4, 64
GLA_HEADS, GLA_GATE_RANK, GLA_GATE_NORMALIZER, GLA_CHUNK = 4, 16, 16.0, 64
MOBA_HEADS, MOBA_BLOCK, MOBA_TOPK = 16, 256, 3
REL_BUCKETS, REL_MAX_DIST = 32, 128
N_EXPERTS, TOP_K, EXPERT_FF = 32, 4, 768
SWIGLU_LIMIT, SWIGLU_ALPHA = 7.0, 1.702
LN_EPS, RMS_EPS, L2_EPS = 1e-5, 1e-6, 1e-6

V7X_LANES = 128
V7X_VMEM_LIMIT_BYTES = 56 * 1024 * 1024

F32 = jnp.float32
BF16 = jnp.bfloat16
HIGHEST = lax.Precision.HIGHEST
NEG_INF = float("-inf")


def _params(*sem):
    return pltpu.CompilerParams(dimension_semantics=sem, vmem_limit_bytes=V7X_VMEM_LIMIT_BYTES)


def _dot(a, b, dims=None, precision=None):
    if dims is None:
        dims = (((a.ndim - 1,), (0,)), ((), ()))
    return lax.dot_general(a, b, dims, precision=precision, preferred_element_type=F32)


def _dot_nt(a, b, precision=None):
    return _dot(a, b, (((1,), (1,)), ((), ())), precision)


def _dot_tn(a, b, precision=None):
    return _dot(a, b, (((0,), (0,)), ((), ())), precision)


def _sigmoid(x):
    return 1.0 / (1.0 + jnp.exp(-x))


def _silu(x):
    return x * _sigmoid(x)


def _iota(shape, dim):
    return lax.broadcasted_iota(jnp.int32, shape, dim)


def _mm_kernel(valid_cols, rms, w_transposed, x_ref, *refs):
    g_ref = refs[0] if rms else None
    w_ref, o_ref, wbf_ref = refs[-3:]

    @pl.when(pl.program_id(1) == 0)
    def _():
        w = w_ref[...]
        if valid_cols is not None:
            w = jnp.where(_iota(w.shape, 0 if w_transposed else 1) < valid_cols, w, 0.0)
        wbf_ref[...] = w.astype(BF16)

    x = x_ref[...]
    if rms:
        x = x.astype(F32)
        x = x * lax.rsqrt(jnp.mean(x * x, -1, keepdims=True) + RMS_EPS) * g_ref[...]
    dot = _dot_nt if w_transposed else _dot
    o_ref[...] = dot(x.astype(BF16), wbf_ref[...]).astype(o_ref.dtype)


def _mm_tiles(M, K, N):
    tm = 1024 if M % 1024 == 0 else M
    tn = N
    for cand in (1024, 512, 256, 128):
        if N % cand == 0 and K * cand * 4 <= 8 * 1024 * 1024:
            tn = cand
            break
    return tm, tn


def _mm(x, w, out_dtype, rms_gain=None, x_col=0, col0=0, n_cols=None, w_transposed=False):
    M = x.shape[0]
    K, n_w = (w.shape[1], w.shape[0]) if w_transposed else w.shape
    N = n_w if n_cols is None else n_cols
    tm, tn = _mm_tiles(M, K, N)
    assert col0 % tn == 0
    valid_cols = n_w - col0 if col0 + N > n_w else None
    assert valid_cols is None or N == tn
    c0 = col0 // tn
    x_spec = pl.BlockSpec((tm, K), lambda n, m: (m, x_col))
    if w_transposed:
        w_spec, w_tile = pl.BlockSpec((tn, K), lambda n, m: (c0 + n, 0)), (tn, K)
    else:
        w_spec, w_tile = pl.BlockSpec((K, tn), lambda n, m: (0, c0 + n)), (K, tn)
    if rms_gain is None:
        in_specs, args = [x_spec, w_spec], (x, w)
    else:
        in_specs, args = [x_spec, pl.BlockSpec((1, K), lambda n, m: (0, 0)), w_spec], (x, rms_gain[None, :], w)
    return pl.pallas_call(
        functools.partial(_mm_kernel, valid_cols, rms_gain is not None, w_transposed),
        grid=(N // tn, M // tm),
        in_specs=in_specs,
        out_specs=pl.BlockSpec((tm, tn), lambda n, m: (m, n)),
        out_shape=jax.ShapeDtypeStruct((M, N), out_dtype),
        scratch_shapes=[pltpu.VMEM(w_tile, BF16)],
        compiler_params=_params("parallel", "arbitrary"),
        name="proj_matmul",
    )(*args)


def _ada_kernel(c_ref, w_ref, b_ref, o_ref):
    c = _silu(c_ref[...]).astype(BF16)
    o_ref[0] = _dot(c, w_ref[0].astype(BF16)) + b_ref[0]


def _ada_mod(c, ada_w, ada_b):
    depth, D, N = ada_w.shape
    B = c.shape[0]
    rows = 8
    c_pad = jnp.pad(c, ((0, rows - B), (0, 0)))
    tn = 1024
    out = pl.pallas_call(
        _ada_kernel,
        grid=(depth, N // tn),
        in_specs=[pl.BlockSpec((rows, D), lambda i, n: (0, 0)),
                  pl.BlockSpec((1, D, tn), lambda i, n: (i, 0, n)),
                  pl.BlockSpec((1, 1, tn), lambda i, n: (i, 0, n))],
        out_specs=pl.BlockSpec((1, rows, tn), lambda i, n: (i, 0, n)),
        out_shape=jax.ShapeDtypeStruct((depth, rows, N), F32),
        compiler_params=_params("parallel", "parallel"),
        name="ada_mod",
    )(c_pad, ada_w, ada_b.reshape(depth, 1, N))
    return out[:, :B]


ROW_TILE = 256


def _modulate_kernel(x_ref, sc_ref, sh_ref, u_ref):
    u_ref[...] = (x_ref[...] * (1.0 + sc_ref[0]) + sh_ref[0]).astype(u_ref.dtype)


def _row_specs(D, S, tr):
    vec = pl.BlockSpec((1, 1, D), lambda i: ((i * tr) // S, 0, 0))
    row = pl.BlockSpec((tr, D), lambda i: (i, 0))
    par = pl.BlockSpec((1, D), lambda i: (0, 0))
    return vec, row, par


def _modulate(x, sc, sh, S):
    T, D = x.shape
    tr = ROW_TILE
    vec, row, _ = _row_specs(D, S, tr)
    return pl.pallas_call(
        _modulate_kernel, grid=(T // tr,), in_specs=[row, vec, vec], out_specs=row,
        out_shape=jax.ShapeDtypeStruct((T, D), BF16),
        compiler_params=_params("parallel"), name="modulate",
    )(x, sc[:, None, :], sh[:, None, :])


def _deepnorm(alpha, x, h, gate, g, b):
    y = alpha * x + (1.0 + gate) * h
    mu = jnp.mean(y, -1, keepdims=True)
    yc = y - mu
    var = jnp.mean(yc * yc, -1, keepdims=True)
    return yc * lax.rsqrt(var + LN_EPS) * g + b


U32 = jnp.uint32
ROW_PARTS = D_MODEL // (2 * V7X_LANES)
assert ROW_PARTS == 8


def _bits(x):
    return lax.bitcast_convert_type(x.astype(BF16).astype(F32), U32)


def _store_row_tiled(ref, row0, value):
    n = value.shape[0]
    for j in range(ROW_PARTS):
        lo = _bits(value[:, (2 * j) * V7X_LANES:(2 * j + 1) * V7X_LANES])
        hi = _bits(value[:, (2 * j + 1) * V7X_LANES:(2 * j + 2) * V7X_LANES])
        ref[pl.ds(row0 * ROW_PARTS + j, n, stride=ROW_PARTS), :] = lax.shift_right_logical(lo, U32(16)) | hi


def _load_row_tiled(ref, row0, n):
    parts = []
    for j in range(ROW_PARTS):
        w = ref[pl.ds(row0 * ROW_PARTS + j, n, stride=ROW_PARTS), :]
        parts.append(lax.bitcast_convert_type(lax.shift_left(w, U32(16)), F32))
        parts.append(lax.bitcast_convert_type(w & U32(0xFFFF0000), F32))
    return jnp.concatenate(parts, -1)


def _ln_router_kernel(alpha, x_ref, h_ref, gate_ref, g_ref, b_ref, sc_ref, sh_ref, rwh_ref, rwl_ref, rb_ref,
                      xo_ref, u_ref, idx_ref, wgt_ref):
    xn = _deepnorm(alpha, x_ref[...], h_ref[...].astype(F32), gate_ref[0], g_ref[...], b_ref[...])
    xo_ref[...] = xn
    u = xn * (1.0 + sc_ref[0]) + sh_ref[0]
    _store_row_tiled(u_ref, 0, u)
    u_hi = u.astype(BF16)
    u_lo = (u - u_hi.astype(F32)).astype(BF16)
    logits = (_dot_nt(rwh_ref[...], u_hi) + (_dot_nt(rwh_ref[...], u_lo) + _dot_nt(rwl_ref[...], u_hi))) + rb_ref[...]
    expert = _iota(logits.shape, 0)
    vals = logits
    top_v, top_i = [], []
    for _ in range(TOP_K):
        m = jnp.max(vals, 0, keepdims=True)
        i = jnp.min(jnp.where(vals == m, expert, N_EXPERTS), 0, keepdims=True)
        top_v.append(m)
        top_i.append(i)
        vals = jnp.where(expert == i, NEG_INF, vals)
    exps = [jnp.exp(v - top_v[0]) for v in top_v]
    denom = functools.reduce(lambda a, b: a + b, exps)
    slot = _iota(idx_ref.shape, 0)
    idx = jnp.zeros(idx_ref.shape, jnp.int32)
    wgt = jnp.zeros(wgt_ref.shape, F32)
    for k in range(TOP_K):
        idx = jnp.where(slot == k, top_i[k], idx)
        wgt = jnp.where(slot == k, exps[k] / denom, wgt)
    idx_ref[...] = idx
    wgt_ref[...] = wgt


def _ln_router(x, h, gate, ln_g, ln_b, sc, sh, router_w, router_b, S, alpha):
    T, D = x.shape
    tr = ROW_TILE
    vec, row, par = _row_specs(D, S, tr)
    slots = 8
    choice = pl.BlockSpec((slots, tr), lambda i: (0, i))
    tiled_row = pl.BlockSpec((tr * ROW_PARTS, V7X_LANES), lambda i: (i, 0))
    rwt = router_w.T
    rw_hi = rwt.astype(BF16)
    rw_lo = (rwt - rw_hi.astype(F32)).astype(BF16)
    rw_spec = pl.BlockSpec((N_EXPERTS, D), lambda i: (0, 0))
    xo, u, idx, wgt = pl.pallas_call(
        functools.partial(_ln_router_kernel, alpha), grid=(T // tr,),
        in_specs=[row, row, vec, par, par, vec, vec, rw_spec, rw_spec,
                  pl.BlockSpec((N_EXPERTS, 1), lambda i: (0, 0))],
        out_specs=[row, tiled_row, choice, choice],
        out_shape=[jax.ShapeDtypeStruct((T, D), F32), jax.ShapeDtypeStruct((T * ROW_PARTS, V7X_LANES), U32),
                   jax.ShapeDtypeStruct((slots, T), jnp.int32), jax.ShapeDtypeStruct((slots, T), F32)],
        compiler_params=_params("parallel"), name="deepnorm_ln_router",
    )(x, h, gate[:, None, :], ln_g[None, :], ln_b[None, :], sc[:, None, :], sh[:, None, :], rw_hi, rw_lo,
      router_b[:, None])
    return xo, u, idx[:TOP_K].T, wgt[:TOP_K].T


MOE_ROW_TILE = 512


def _expert_changed(te_ref):
    t = pl.program_id(0)
    return jnp.logical_or(t == 0, te_ref[t] != te_ref[jnp.maximum(t - 1, 0)])


def _moe_gu_kernel(te_ref, tv_ref, x_ref, w_ref, b_ref, h_ref, wbf_ref):
    @pl.when(_expert_changed(te_ref))
    def _():
        wbf_ref[...] = w_ref[0].astype(BF16)

    @pl.when(tv_ref[pl.program_id(0)] > 0)
    def _():
        x = _load_row_tiled(x_ref, 0, x_ref.shape[0] // ROW_PARTS).astype(BF16)
        gu = _dot(x, wbf_ref[...]) + b_ref[0]
        gl = jnp.minimum(gu[:, :EXPERT_FF], SWIGLU_LIMIT)
        up = jnp.clip(gu[:, EXPERT_FF:], -SWIGLU_LIMIT, SWIGLU_LIMIT)
        h_ref[...] = ((up + 1.0) * gl * _sigmoid(gl * SWIGLU_ALPHA)).astype(h_ref.dtype)

    @pl.when(tv_ref[pl.program_id(0)] == 0)
    def _():
        h_ref[...] = jnp.zeros(h_ref.shape, h_ref.dtype)


def _moe_down_kernel(te_ref, tv_ref, h_ref, w_ref, b_ref, y_ref, wbf_ref):
    @pl.when(_expert_changed(te_ref))
    def _():
        wbf_ref[...] = w_ref[0].astype(BF16)

    @pl.when(tv_ref[pl.program_id(0)] > 0)
    def _():
        _store_row_tiled(y_ref, 0, _dot(h_ref[...], wbf_ref[...]) + b_ref[0])

    @pl.when(tv_ref[pl.program_id(0)] == 0)
    def _():
        y_ref[...] = jnp.zeros(y_ref.shape, y_ref.dtype)


def _route_metadata(top_idx, tm):
    T = top_idx.shape[0]
    P = T * TOP_K
    n_tiles = (P + N_EXPERTS * (tm - 1)) // tm
    e_flat = top_idx.reshape(P)
    onehot = (e_flat[:, None] == jnp.arange(N_EXPERTS)[None, :]).astype(jnp.int32)
    csum = jnp.cumsum(onehot, axis=0)
    counts = csum[-1]
    rank = jnp.sum(csum * onehot, axis=1) - 1
    padded = ((counts + tm - 1) // tm) * tm
    ends_p = jnp.cumsum(padded)
    starts_p = ends_p - padded
    dest = (starts_p[e_flat] + rank).astype(jnp.int32)
    tile_start = jnp.arange(n_tiles, dtype=jnp.int32) * tm
    tile_valid = (tile_start < ends_p[-1]).astype(jnp.int32)
    tile_expert = jnp.sum((tile_start[:, None] >= ends_p[None, :]).astype(jnp.int32), axis=1)
    last_expert = jnp.max(jnp.where(counts > 0, jnp.arange(N_EXPERTS), 0))
    tile_expert = jnp.where(tile_valid > 0, tile_expert, last_expert).astype(jnp.int32)
    return dest, tile_expert, tile_valid, n_tiles


DISPATCH_ROWS = 2048


def _dispatch_kernel(dest_ref, u_ref, init_hbm, xs_hbm, sem):
    del init_hbm
    R = dest_ref.shape[-1]

    def row(ref, i):
        return ref.at[pl.ds(pl.multiple_of(i * ROW_PARTS, ROW_PARTS), ROW_PARTS)]

    def issue(t, carry):
        for k in range(TOP_K):
            pltpu.make_async_copy(row(u_ref, t), row(xs_hbm, dest_ref[0, 0, t * TOP_K + k]), sem).start(priority=k % 2)
        return carry

    lax.fori_loop(0, R // TOP_K, issue, 0, unroll=2)
    n = (R // TOP_K) * ROW_PARTS
    for _ in range(TOP_K):
        pltpu.make_async_copy(u_ref, xs_hbm.at[pl.ds(0, n)], sem).wait()


def _dispatch(u, dest, rows):
    P = dest.shape[0]
    R = DISPATCH_ROWS
    shape = (rows * ROW_PARTS, u.shape[1])
    return pl.pallas_call(
        _dispatch_kernel,
        grid=(P // R,),
        in_specs=[pl.BlockSpec((1, 1, R), lambda s: (s, 0, 0), memory_space=pltpu.SMEM),
                  pl.BlockSpec(((R // TOP_K) * ROW_PARTS, u.shape[1]), lambda s: (s, 0)),
                  pl.BlockSpec(memory_space=pl.ANY)],
        out_specs=pl.BlockSpec(memory_space=pl.ANY),
        out_shape=jax.ShapeDtypeStruct(shape, u.dtype),
        scratch_shapes=[pltpu.SemaphoreType.DMA(())],
        input_output_aliases={2: 0},
        compiler_params=_params("arbitrary"), name="moe_dispatch",
    )(dest.reshape(P // R, 1, R), u, jnp.zeros(shape, u.dtype))


def _moe_ffn(u, top_idx, layer, w_gu, b_gu, w_down, b_down):
    D = D_MODEL
    tm = MOE_ROW_TILE
    dest, tile_expert, tile_valid, n_tiles = _route_metadata(top_idx, tm)
    tile_expert = tile_expert + layer * N_EXPERTS
    w_gu, w_down = (w.reshape((-1,) + w.shape[2:]) for w in (w_gu, w_down))
    b_gu, b_down = (b.reshape(-1, b.shape[-1]) for b in (b_gu, b_down))
    rows = n_tiles * tm
    x_sorted = _dispatch(u, dest, rows)
    ff2 = 2 * EXPERT_FF
    tiled = pl.BlockSpec((tm * ROW_PARTS, V7X_LANES), lambda t, te, tv: (t, 0))
    h = pl.pallas_call(
        _moe_gu_kernel,
        grid_spec=pltpu.PrefetchScalarGridSpec(
            num_scalar_prefetch=2, grid=(n_tiles,),
            in_specs=[tiled,
                      pl.BlockSpec((1, D, ff2), lambda t, te, tv: (te[t], 0, 0)),
                      pl.BlockSpec((1, 1, ff2), lambda t, te, tv: (te[t], 0, 0))],
            out_specs=pl.BlockSpec((tm, EXPERT_FF), lambda t, te, tv: (t, 0)),
            scratch_shapes=[pltpu.VMEM((D, ff2), BF16)]),
        out_shape=jax.ShapeDtypeStruct((rows, EXPERT_FF), BF16),
        compiler_params=_params("arbitrary"), name="moe_gate_up",
    )(tile_expert, tile_valid, x_sorted, w_gu, b_gu[:, None, :])
    y = pl.pallas_call(
        _moe_down_kernel,
        grid_spec=pltpu.PrefetchScalarGridSpec(
            num_scalar_prefetch=2, grid=(n_tiles,),
            in_specs=[pl.BlockSpec((tm, EXPERT_FF), lambda t, te, tv: (t, 0)),
                      pl.BlockSpec((1, EXPERT_FF, D), lambda t, te, tv: (te[t], 0, 0)),
                      pl.BlockSpec((1, 1, D), lambda t, te, tv: (te[t], 0, 0))],
            out_specs=tiled,
            scratch_shapes=[pltpu.VMEM((EXPERT_FF, D), BF16)]),
        out_shape=jax.ShapeDtypeStruct((rows * ROW_PARTS, V7X_LANES), U32),
        compiler_params=_params("arbitrary"), name="moe_down",
    )(tile_expert, tile_valid, h, w_down, b_down[:, None, :])
    return y, dest


COMBINE_TILE = 256
COMBINE_SUB = 64


def _combine_ln_kernel(alpha, dest_ref, y_hbm, x_ref, wgt_ref, gate_ref, g_ref, b_ref, sc_ref, sh_ref,
                       xo_ref, u_ref, buf_ref, sems):
    n_sub = COMBINE_TILE // COMBINE_SUB

    def row(ref, i):
        return ref.at[pl.ds(pl.multiple_of(i * ROW_PARTS, ROW_PARTS), ROW_PARTS)]

    for j in range(n_sub):
        def issue(t, carry, j=j):
            tok = j * COMBINE_SUB + t
            for k in range(TOP_K):
                pltpu.make_async_copy(row(y_hbm, dest_ref[0, 0, tok * TOP_K + k]),
                                      row(buf_ref, k * COMBINE_TILE + tok), sems.at[j]).start(priority=k % 2)
            return carry

        lax.fori_loop(0, COMBINE_SUB, issue, 0, unroll=2)
    for j in range(n_sub):
        lo = j * COMBINE_SUB
        n = COMBINE_SUB * TOP_K * ROW_PARTS
        pltpu.make_async_copy(y_hbm.at[pl.ds(0, n)], buf_ref.at[pl.ds(0, n)], sems.at[j]).wait()
        wgt = wgt_ref[pl.ds(lo, COMBINE_SUB), :]
        f = sum(_load_row_tiled(buf_ref, k * COMBINE_TILE + lo, COMBINE_SUB) * wgt[:, k:k + 1] for k in range(TOP_K))
        xn = _deepnorm(alpha, x_ref[pl.ds(lo, COMBINE_SUB), :], f, gate_ref[0], g_ref[...], b_ref[...])
        xo_ref[pl.ds(lo, COMBINE_SUB), :] = xn
        u_ref[pl.ds(lo, COMBINE_SUB), :] = (xn * (1.0 + sc_ref[0]) + sh_ref[0]).astype(u_ref.dtype)


def _combine_ln(x, y, dest, top_w, gate, ln_g, ln_b, sc, sh, S, alpha):
    T, D = x.shape
    tr = COMBINE_TILE
    vec, row, par = _row_specs(D, S, tr)
    n_sub = tr // COMBINE_SUB
    return pl.pallas_call(
        functools.partial(_combine_ln_kernel, alpha), grid=(T // tr,),
        in_specs=[pl.BlockSpec((1, 1, tr * TOP_K), lambda i: (i, 0, 0), memory_space=pltpu.SMEM),
                  pl.BlockSpec(memory_space=pl.ANY), row,
                  pl.BlockSpec((tr, TOP_K), lambda i: (i, 0)), vec, par, par, vec, vec],
        out_specs=[row, row],
        out_shape=[jax.ShapeDtypeStruct((T, D), F32), jax.ShapeDtypeStruct((T, D), BF16)],
        scratch_shapes=[pltpu.VMEM((tr * TOP_K * ROW_PARTS, V7X_LANES), U32), pltpu.SemaphoreType.DMA((n_sub,))],
        compiler_params=_params("arbitrary"), name="moe_combine_ln",
    )(dest.reshape(T // tr, 1, tr * TOP_K), y, x, top_w, gate[:, None, :], ln_g[None, :], ln_b[None, :],
      sc[:, None, :], sh[:, None, :])


ATTN_TILE = 256


def _softmax_pv(s, v):
    p = jnp.exp(s - jnp.max(s, -1, keepdims=True))
    return _dot(p.astype(BF16), v) / jnp.sum(p, -1, keepdims=True)


CAUSAL_VARIANTS = 4


def _causal_widths(n_tiles):
    step = max(n_tiles // CAUSAL_VARIANTS, 1)
    bounds = list(range(step, n_tiles, step)) + [n_tiles]
    return [(n, n) for n in bounds]


def _mla_attn_kernel(scale, qn_ref, qr_ref, cos_ref, sin_ref, kv_ref, kr_ref, o_ref):
    qi = pl.program_id(2)
    t = ATTN_TILE
    n_tiles = kv_ref.shape[1] // t
    x = qr_ref[0].astype(F32)
    half = MLA_ROPE // 2
    first_half = _iota(x.shape, 1) % MLA_ROPE < half
    swapped = jnp.where(first_half, pltpu.roll(x, x.shape[1] - half, 1), pltpu.roll(x, half, 1))
    qr_all = (x * cos_ref[...] + swapped * sin_ref[...]).astype(BF16)

    def attend(n_kv):
        w = n_kv * t
        visible = _iota((t, w), 1) <= _iota((t, w), 0) + qi * t
        outs = []
        for hh in range(2):
            qn = qn_ref[0, :, hh * MLA_NOPE:(hh + 1) * MLA_NOPE]
            qr = qr_all[:, hh * MLA_ROPE:(hh + 1) * MLA_ROPE]
            c0 = hh * (MLA_NOPE + MLA_V)
            s = (_dot_nt(qn, kv_ref[0, 0:w, c0:c0 + MLA_NOPE]) + _dot_nt(qr, kr_ref[0, 0:w, :])) * scale
            s = jnp.where(visible, s, NEG_INF)
            outs.append(_softmax_pv(s, kv_ref[0, 0:w, c0 + MLA_NOPE:c0 + MLA_NOPE + MLA_V]))
        o_ref[0] = jnp.concatenate(outs, -1).astype(o_ref.dtype)

    lo = 0
    for hi, n_kv in _causal_widths(n_tiles):
        pl.when(jnp.logical_and(qi >= lo, qi < hi))(functools.partial(attend, n_kv))
        lo = hi


def _rope(x, cos, sin):
    half = x.shape[-1] // 2
    x1, x2 = x[..., :half], x[..., half:]
    return jnp.concatenate([x1 * cos - x2 * sin, x2 * cos + x1 * sin], -1)


def _mla_mixer(u, B, S, w_in, q_norm, kv_norm, w_qb, w_kvb, w_o):
    H = MLA_HEADS
    n_lat = MLA_Q_LORA + MLA_KV_LORA
    lat = _mm(u, w_in, F32, n_cols=n_lat)
    k_rope = _mm(u, w_in, F32, col0=n_lat, n_cols=V7X_LANES)[:, :MLA_ROPE]
    wq = w_qb.reshape(MLA_Q_LORA, H, MLA_NOPE + MLA_ROPE)
    q_nope = _mm(lat, wq[:, :, :MLA_NOPE].reshape(MLA_Q_LORA, H * MLA_NOPE), BF16, rms_gain=q_norm, x_col=0)
    q_rope = _mm(lat, wq[:, :, MLA_NOPE:].reshape(MLA_Q_LORA, H * MLA_ROPE), F32, rms_gain=q_norm, x_col=0)
    kv = _mm(lat, w_kvb, BF16, rms_gain=kv_norm, x_col=1)
    inv_freq = ROPE_THETA ** (-jnp.arange(MLA_ROPE // 2, dtype=F32) / (MLA_ROPE // 2))
    ang = jnp.arange(S, dtype=F32)[:, None] * inv_freq[None, :]
    cos, sin = jnp.cos(ang), jnp.sin(ang)
    k_rope = _rope(k_rope.reshape(B, S, MLA_ROPE), cos, sin).astype(BF16)
    cos2 = jnp.tile(jnp.concatenate([cos, cos], -1), (1, 2))
    sin2 = jnp.tile(jnp.concatenate([-sin, sin], -1), (1, 2))
    t = ATTN_TILE
    o = pl.pallas_call(
        functools.partial(_mla_attn_kernel, (MLA_NOPE + MLA_ROPE) ** -0.5),
        grid=(B, H // 2, S // t),
        in_specs=[pl.BlockSpec((1, t, 2 * MLA_NOPE), lambda b, h, i: (b, i, h)),
                  pl.BlockSpec((1, t, 2 * MLA_ROPE), lambda b, h, i: (b, i, h)),
                  pl.BlockSpec((t, 2 * MLA_ROPE), lambda b, h, i: (i, 0)),
                  pl.BlockSpec((t, 2 * MLA_ROPE), lambda b, h, i: (i, 0)),
                  pl.BlockSpec((1, S, 2 * (MLA_NOPE + MLA_V)), lambda b, h, i: (b, 0, h)),
                  pl.BlockSpec((1, S, MLA_ROPE), lambda b, h, i: (b, 0, 0))],
        out_specs=pl.BlockSpec((1, t, 2 * MLA_V), lambda b, h, i: (b, i, h)),
        out_shape=jax.ShapeDtypeStruct((B, S, H * MLA_V), BF16),
        compiler_params=_params("parallel", "parallel", "arbitrary"), name="mla_attention",
    )(q_nope.reshape(B, S, -1), q_rope.reshape(B, S, -1), cos2, sin2, kv.reshape(B, S, -1), k_rope)
    return _mm(o.reshape(B * S, H * MLA_V), w_o, F32)


def _t5_bucket(dist):
    n = jnp.maximum(dist, 0)
    max_exact = REL_BUCKETS // 2
    large = max_exact + (jnp.log(jnp.maximum(n, 1).astype(F32) / max_exact)
                         / math.log(REL_MAX_DIST / max_exact) * (REL_BUCKETS - max_exact)).astype(jnp.int32)
    large = jnp.minimum(large, REL_BUCKETS - 1)
    return jnp.where(n < max_exact, n, large)


MOBA_FAR = 2


def _moba_kernel(scale, n_sel, tab_ref, q_ref, k_ref, v_ref, bkt_ref, o_ref, kbf_ref, vbf_ref, kmean_ref, bias_ref):
    h, b, qi = pl.program_id(0), pl.program_id(1), pl.program_id(2)
    L = MOBA_BLOCK
    S = k_ref.shape[1]
    n_blk = S // L

    @pl.when(jnp.logical_and(b == 0, qi == 0))
    def _():
        for d in range(MOBA_FAR + 1):
            bucket = bkt_ref[d]
            tile = jnp.zeros((L, L), F32)
            for e in range(REL_BUCKETS):
                tile = jnp.where(bucket == e, tab_ref[e, h], tile)
            bias_ref[d] = tile

    @pl.when(qi == 0)
    def _():
        k = k_ref[0].astype(F32)
        kbf_ref[...] = k.astype(BF16)
        vbf_ref[...] = v_ref[0].astype(BF16)
        kmean_ref[...] = jnp.zeros(kmean_ref.shape, F32)
        kmean_ref[0:n_blk, :] = jnp.mean(k.reshape(n_blk, L, k.shape[-1]), axis=1)

    q = q_ref[0].astype(F32)
    qb = q.astype(BF16)
    nb = kmean_ref.shape[0]
    blk = _iota((nb, L), 0)
    gate = jnp.where(blk < qi, _dot_nt(kmean_ref[...], q, precision=HIGHEST), NEG_INF)
    picked = jnp.zeros((nb, L), F32)
    for j in range(n_blk - 1):
        gj = gate[j:j + 1, :]
        beats = jnp.logical_or(gate > gj, jnp.logical_and(gate == gj, blk < j))
        in_topk = jnp.sum(jnp.where(beats, 1.0, 0.0), 0, keepdims=True) < n_sel
        picked = jnp.where(blk == j, jnp.where(jnp.logical_and(in_topk, j < qi), 1.0, 0.0), picked)
    picked = _dot_tn(picked, jnp.where(_iota((nb, nb), 0) == _iota((nb, nb), 1), 1.0, 0.0))
    causal_add = jnp.where(_iota((L, L), 1) <= _iota((L, L), 0), 0.0, NEG_INF)
    past_add = [jnp.where(picked[:, j:j + 1] > 0.5, 0.0, NEG_INF) for j in range(n_blk)]

    def attend(n_kv):
        w = n_kv * L
        s = _dot_nt(qb, kbf_ref[0:w, :]) * scale
        parts = []
        for j in range(n_kv):
            bias = bias_ref[jnp.clip(qi - j, 0, MOBA_FAR)]
            parts.append(s[:, j * L:(j + 1) * L] + bias + jnp.where(j == qi, causal_add, past_add[j]))
        o_ref[0] = _softmax_pv(jnp.concatenate(parts, -1), vbf_ref[0:w, :]).astype(o_ref.dtype)

    lo = 0
    for hi, n_kv in _causal_widths(n_blk):
        pl.when(jnp.logical_and(qi >= lo, qi < hi))(functools.partial(attend, n_kv))
        lo = hi


def _moba_mixer(u, B, S, w_in, w_o, rel_bias):
    H, Dh, L = MOBA_HEADS, u.shape[1] // MOBA_HEADS, MOBA_BLOCK
    n_blk = S // L
    assert S % L == 0 and Dh == V7X_LANES and (MOBA_FAR - 1) * L >= REL_MAX_DIST
    n_sel = max(min(MOBA_TOPK, n_blk - 1), 1)
    qkv = _mm(u, w_in, BF16).reshape(B, S, 3 * H * Dh)
    qk = jnp.arange(L)[:, None] - jnp.arange(L)[None, :]
    bucket = _t5_bucket(jnp.stack([qk + d * L for d in range(MOBA_FAR + 1)])).astype(jnp.int32)
    o = pl.pallas_call(
        functools.partial(_moba_kernel, Dh ** -0.5, n_sel),
        grid=(H, B, n_blk),
        in_specs=[pl.BlockSpec(memory_space=pltpu.SMEM),
                  pl.BlockSpec((1, L, Dh), lambda h, b, i: (b, i, h)),
                  pl.BlockSpec((1, S, Dh), lambda h, b, i: (b, 0, H + h)),
                  pl.BlockSpec((1, S, Dh), lambda h, b, i: (b, 0, 2 * H + h)),
                  pl.BlockSpec((MOBA_FAR + 1, L, L), lambda h, b, i: (0, 0, 0))],
        out_specs=pl.BlockSpec((1, L, Dh), lambda h, b, i: (b, i, h)),
        out_shape=jax.ShapeDtypeStruct((B, S, H * Dh), BF16),
        scratch_shapes=[pltpu.VMEM((S, Dh), BF16), pltpu.VMEM((S, Dh), BF16),
                        pltpu.VMEM((-(-n_blk // 8) * 8, Dh), F32),
                        pltpu.VMEM((MOBA_FAR + 1, L, L), F32)],
        compiler_params=_params("parallel", "arbitrary", "arbitrary"), name="moba_attention",
    )(rel_bias, qkv, qkv, qkv, bucket)
    return _mm(o.reshape(B * S, H * Dh), w_o, F32)


GDN_HEAD_GROUP = 16


def _lane_sum(x):
    return _dot(x.astype(BF16), jnp.ones((x.shape[1], x.shape[1]), BF16))


def _l2norm(x):
    return x * lax.rsqrt(_lane_sum(x * x) + L2_EPS)


def _dotb(a, b):
    return _dot(a.astype(BF16), b.astype(BF16))


def _unit_lower_inverse(a_lows, block):
    C = a_lows[0].shape[0]
    r = _iota((C, C), 0)
    c = _iota((C, C), 1)
    eye = jnp.where(r == c, 1.0, 0.0)
    same = (r // block) == (c // block)
    a_d = [jnp.where(same, a, 0.0) for a in a_lows]
    a_off = [a - d for a, d in zip(a_lows, a_d)]
    inv_d = [eye - d for d in a_d]
    pw = a_d
    k = 2
    while k < block:
        pw = [_dotb(p, p) for p in pw]
        inv_d = [_dotb(i, eye + p) for i, p in zip(inv_d, pw)]
        k *= 2
    n = [_dotb(i, o) for i, o in zip(inv_d, a_off)]
    inv_n = [eye - x for x in n]
    pw = n
    k = 2
    while k < C // block:
        pw = [_dotb(p, p) for p in pw]
        inv_n = [_dotb(i, eye + p) for i, p in zip(inv_n, pw)]
        k *= 2
    return [_dotb(i, d) for i, d in zip(inv_n, inv_d)]


GDN_HIST = 16


def _causal_conv_silu(x_ref, w_ref, hist_ref):
    C = x_ref.shape[1]
    x = x_ref[0]
    hist_ref[GDN_HIST:GDN_HIST + C, :] = x
    taps = GDN_CONV - 1
    o = _iota((taps * C, GDN_HIST + C), 0)
    r = _iota((taps * C, GDN_HIST + C), 1)
    shift = jnp.where(r == o % C + o // C + (GDN_HIST - taps), 1.0, 0.0).astype(BF16)
    shifted = _dot(shift, hist_ref[...])
    y = x.astype(F32) * w_ref[taps:taps + 1, :]
    for i in range(taps):
        y = y + shifted[i * C:(i + 1) * C, :] * w_ref[i:i + 1, :]
    hist_ref[0:GDN_HIST, :] = hist_ref[C:C + GDN_HIST, :]
    return _silu(y)


def _gdn_kernel(q_ref, k_ref, v_ref, z_ref, cwq_ref, cwk_ref, cwv_ref, gc_ref, gct_ref, beta_ref, ng_ref,
                o_ref, state_ref, hq_ref, hk_ref, hv_ref):
    G, C, DK, DV = GDN_HEAD_GROUP, GDN_CHUNK, GDN_DK, GDN_DV
    rep = GDN_V_HEADS // GDN_K_HEADS
    heads = range(G)

    @pl.when(pl.program_id(2) == 0)
    def _():
        state_ref[...] = jnp.zeros(state_ref.shape, F32)
        for hist in (hq_ref, hk_ref, hv_ref):
            hist[0:GDN_HIST, :] = jnp.zeros((GDN_HIST, hist.shape[1]), hist.dtype)

    qc = _causal_conv_silu(q_ref, cwq_ref, hq_ref)
    kc = _causal_conv_silu(k_ref, cwk_ref, hk_ref)
    vc = _causal_conv_silu(v_ref, cwv_ref, hv_ref)
    r = _iota((C, C), 0)
    c = _iota((C, C), 1)
    tri = c <= r
    strict = c < r
    gc = gc_ref[0, 0]
    gct = gct_ref[0, 0, 0]
    beta = beta_ref[0, 0]
    q = [_l2norm(qc[:, i * DK:(i + 1) * DK]) * (DK ** -0.5) for i in range(G // rep)]
    k = [_l2norm(kc[:, i * DK:(i + 1) * DK]) for i in range(G // rep)]
    assert DK == V7X_LANES and DV == V7X_LANES
    kb = [x.astype(BF16) for x in k]
    qk = [_dot_nt(a.astype(BF16), b) for a, b in zip(q, kb)]
    kk = [_dot_nt(b, b) for b in kb]
    g_col = [gc[:, h:h + 1] for h in heads]
    b_col = [beta[:, h:h + 1] for h in heads]
    decay = [jnp.where(tri, jnp.exp(jnp.where(tri, g_col[h] - gct[h:h + 1, :], 0.0)), 0.0) for h in heads]
    t_inv = _unit_lower_inverse([jnp.where(strict, kk[h // rep] * b_col[h] * decay[h], 0.0) for h in heads], 16)
    e_g = [jnp.exp(g) for g in g_col]
    rhs = [jnp.concatenate([vc[:, h * DV:(h + 1) * DV] * b_col[h], k[h // rep] * (b_col[h] * e_g[h])], -1)
           for h in heads]
    sol = [_dot(t_inv[h].astype(BF16), rhs[h].astype(BF16)) for h in heads]
    state = [state_ref[h] for h in heads]
    state_b = [s.astype(BF16) for s in state]
    v_new = [sol[h][:, :DV] - _dot(sol[h][:, DV:].astype(BF16), state_b[h]) for h in heads]
    v_new_b = [x.astype(BF16) for x in v_new]
    attn = [jnp.where(tri, qk[h // rep] * decay[h], 0.0).astype(BF16) for h in heads]
    o = [_dot((q[h // rep] * e_g[h]).astype(BF16), state_b[h]) + _dot(attn[h], v_new_b[h]) for h in heads]
    g_last = [g[C - 1:C, :] for g in g_col]
    k_tail = [(k[h // rep] * jnp.exp(g_last[h] - g_col[h])).astype(BF16) for h in heads]
    new_state = [state[h] * jnp.exp(g_last[h]) + _dot_tn(k_tail[h], v_new_b[h]) for h in heads]
    o = [x * lax.rsqrt(_lane_sum(x * x) * (1.0 / DV) + RMS_EPS) * ng_ref[...] for x in o]
    o = [o[h] * _silu(z_ref[0, :, h * DV:(h + 1) * DV].astype(F32)) for h in heads]
    for h in heads:
        state_ref[h] = new_state[h]
    o_ref[0] = jnp.concatenate(o, -1).astype(o_ref.dtype)


def _gdn_mixer(u, B, S, w_in, conv_w, a_log, dt_bias, norm_g, w_o):
    HK, HV, DK, DV, C, G = GDN_K_HEADS, GDN_V_HEADS, GDN_DK, GDN_DV, GDN_CHUNK, GDN_HEAD_GROUP
    qk_dim, v_dim = HK * DK, HV * DV
    n_main = 2 * qk_dim + 2 * v_dim
    w_t = w_in.T
    proj = _mm(u, w_t, BF16, n_cols=n_main, w_transposed=True).reshape(B, S, n_main)
    ba = _mm(u, w_t, F32, col0=n_main, n_cols=V7X_LANES, w_transposed=True).reshape(B, S, -1)
    n_conv = 2 * qk_dim + v_dim
    beta = jax.nn.sigmoid(ba[:, :, :HV])
    g = -jnp.exp(a_log) * jax.nn.softplus(ba[:, :, HV:2 * HV] + dt_bias)
    N = S // C
    gc = jnp.cumsum(g.reshape(B, N, C, HV), axis=2)
    HG = HV // G
    gc_g = gc.reshape(B, N, C, HG, G).transpose(0, 3, 1, 2, 4).reshape(B, HG, S, G)
    gct_g = gc.reshape(B, N, C, HG, G).transpose(0, 3, 1, 4, 2)
    beta_g = beta.reshape(B, S, HG, G).transpose(0, 2, 1, 3)
    kw = (G // (HV // HK)) * DK
    vw = G * DV
    k_blk, v_blk, z_blk = qk_dim // kw, 2 * qk_dim // vw, n_conv // vw
    o = pl.pallas_call(
        _gdn_kernel,
        grid=(B, HG, N),
        in_specs=[pl.BlockSpec((1, C, kw), lambda b, h, n: (b, n, h)),
                  pl.BlockSpec((1, C, kw), lambda b, h, n: (b, n, k_blk + h)),
                  pl.BlockSpec((1, C, vw), lambda b, h, n: (b, n, v_blk + h)),
                  pl.BlockSpec((1, C, vw), lambda b, h, n: (b, n, z_blk + h)),
                  pl.BlockSpec((GDN_CONV, kw), lambda b, h, n: (0, h)),
                  pl.BlockSpec((GDN_CONV, kw), lambda b, h, n: (0, k_blk + h)),
                  pl.BlockSpec((GDN_CONV, vw), lambda b, h, n: (0, v_blk + h)),
                  pl.BlockSpec((1, 1, C, G), lambda b, h, n: (b, h, n, 0)),
                  pl.BlockSpec((1, 1, 1, G, C), lambda b, h, n: (b, h, n, 0, 0)),
                  pl.BlockSpec((1, 1, C, G), lambda b, h, n: (b, h, n, 0)),
                  pl.BlockSpec((1, DV), lambda b, h, n: (0, 0))],
        out_specs=pl.BlockSpec((1, C, vw), lambda b, h, n: (b, n, h)),
        out_shape=jax.ShapeDtypeStruct((B, S, v_dim), BF16),
        scratch_shapes=[pltpu.VMEM((G, DK, DV), F32), pltpu.VMEM((GDN_HIST + C, kw), BF16),
                        pltpu.VMEM((GDN_HIST + C, kw), BF16), pltpu.VMEM((GDN_HIST + C, vw), BF16)],
        compiler_params=_params("parallel", "parallel", "arbitrary"), name="gdn_chunked",
    )(proj, proj, proj, proj, conv_w, conv_w, conv_w, gc_g, gct_g, beta_g, norm_g[None, :])
    return _mm(o.reshape(B * S, v_dim), w_o, F32)


def _gla_kernel(scale, q_ref, k_ref, v_ref, og_ref, gk_ref, wgk_ref, bgk_ref, ng_ref, o_ref, state_ref):
    C = GLA_CHUNK
    H, dv, dk = state_ref.shape
    heads = range(H)

    @pl.when(pl.program_id(1) == 0)
    def _():
        state_ref[...] = jnp.zeros(state_ref.shape, F32)

    x = _dot(gk_ref[0].astype(BF16), wgk_ref[...].astype(BF16)) + bgk_ref[...]
    log_alpha = (jnp.minimum(x, 0.0) - jnp.log(1.0 + jnp.exp(-jnp.abs(x)))) / GLA_GATE_NORMALIZER
    r = _iota((C, C), 0)
    c = _iota((C, C), 1)
    causal = c <= r
    b = _dot(jnp.where(causal, 1.0, 0.0), log_alpha, precision=HIGHEST)
    b_last = b[C - 1:C, :]
    q = q_ref[0].astype(F32) * scale
    k = k_ref[0].astype(F32)
    q_dec = (q * jnp.exp(b)).astype(BF16)
    k_inv = (k * jnp.exp(-b)).astype(BF16)
    k_tail = (k * jnp.exp(b_last - b)).astype(BF16)
    decay = jnp.exp(b_last)
    ks = [slice(h * dk, (h + 1) * dk) for h in heads]
    v = [v_ref[0, :, h * dv:(h + 1) * dv] for h in heads]
    attn = [jnp.where(causal, _dot_nt(q_dec[:, ks[h]], k_inv[:, ks[h]]), 0.0).astype(BF16) for h in heads]
    state_t = [state_ref[h] for h in heads]
    o = [_dot(attn[h], v[h]) + _dot_nt(q_dec[:, ks[h]], state_t[h].astype(BF16)) for h in heads]
    new_state = [state_t[h] * decay[:, ks[h]] + _dot_tn(v[h], k_tail[:, ks[h]]) for h in heads]
    o = [x * lax.rsqrt(jnp.mean(x * x, -1, keepdims=True) + RMS_EPS) * ng_ref[...] for x in o]
    for h in heads:
        state_ref[h] = new_state[h]
    o_ref[0] = (jnp.concatenate(o, -1) * _silu(og_ref[0].astype(F32))).astype(o_ref.dtype)


def _gla_mixer(u, B, S, w_in, w_gk, b_gk, norm_g, w_o):
    D = u.shape[1]
    H, C = GLA_HEADS, GLA_CHUNK
    key_dim, val_dim = D // 2, D
    dk, dv = key_dim // H, val_dim // H
    n_main = 2 * key_dim + 2 * val_dim
    w_t = w_in.T
    proj = _mm(u, w_t, BF16, n_cols=n_main, w_transposed=True).reshape(B, S, n_main)
    gk = _mm(u, w_t, F32, col0=n_main, n_cols=V7X_LANES, w_transposed=True).reshape(B, S, -1)
    wgk = jnp.pad(w_gk, ((0, gk.shape[-1] - GLA_GATE_RANK), (0, 0)))
    o = pl.pallas_call(
        functools.partial(_gla_kernel, dk ** -0.5),
        grid=(B, S // C),
        in_specs=[pl.BlockSpec((1, C, key_dim), lambda b, n: (b, n, 0)),
                  pl.BlockSpec((1, C, key_dim), lambda b, n: (b, n, 1)),
                  pl.BlockSpec((1, C, val_dim), lambda b, n: (b, n, 2 * key_dim // val_dim)),
                  pl.BlockSpec((1, C, val_dim), lambda b, n: (b, n, 2 * key_dim // val_dim + 1)),
                  pl.BlockSpec((1, C, gk.shape[-1]), lambda b, n: (b, n, 0)),
                  pl.BlockSpec((gk.shape[-1], key_dim), lambda b, n: (0, 0)),
                  pl.BlockSpec((1, key_dim), lambda b, n: (0, 0)),
                  pl.BlockSpec((1, dv), lambda b, n: (0, 0))],
        out_specs=pl.BlockSpec((1, C, val_dim), lambda b, n: (b, n, 0)),
        out_shape=jax.ShapeDtypeStruct((B, S, val_dim), BF16),
        scratch_shapes=[pltpu.VMEM((H, dv, dk), F32)],
        compiler_params=_params("parallel", "arbitrary"), name="gla_chunked",
    )(proj, proj, proj, proj, gk, wgk, b_gk[None, :], norm_g[None, :])
    return _mm(o.reshape(B * S, val_dim), w_o, F32)


def kernel(x, c, rel_bias, mla_w_in, mla_q_norm, mla_kv_norm, mla_w_qb, mla_w_kvb, mla_w_o, gdn_w_in, gdn_conv_w, gdn_a_log, gdn_dt_bias, gdn_norm, gdn_w_o, gla_w_in, gla_w_gk, gla_b_gk, gla_norm, gla_w_o, moba_w_in, moba_w_o, ada_w, ada_b, ln_g, ln_b, router_w, router_b, moe_w_gu, moe_b_gu, moe_w_down, moe_b_down):
    B, S, D = x.shape
    assert D == D_MODEL
    depth = ada_w.shape[0]
    alpha = (2 * depth) ** 0.25
    mod = _ada_mod(c, ada_w, ada_b)
    sh_a, sc_a, g_a, sh_f, sc_f, g_f = (mod[:, :, k * D:(k + 1) * D] for k in range(6))
    xt = x.reshape(B * S, D)
    u = _modulate(xt, sc_a[0], sh_a[0], S)
    for i in range(depth):
        m, j = i % N_MIXERS, i // N_MIXERS
        if m == 0:
            h = _mla_mixer(u, B, S, mla_w_in[j], mla_q_norm[j], mla_kv_norm[j], mla_w_qb[j], mla_w_kvb[j], mla_w_o[j])
        elif m == 1:
            h = _gdn_mixer(u, B, S, gdn_w_in[j], gdn_conv_w[j], gdn_a_log[j], gdn_dt_bias[j], gdn_norm[j], gdn_w_o[j])
        elif m == 2:
            h = _gla_mixer(u, B, S, gla_w_in[j], gla_w_gk[j], gla_b_gk[j], gla_norm[j], gla_w_o[j])
        else:
            h = _moba_mixer(u, B, S, moba_w_in[j], moba_w_o[j], rel_bias)
        xt, u, top_idx, top_w = _ln_router(xt, h, g_a[i], ln_g[i, 0], ln_b[i, 0], sc_f[i], sh_f[i],
                                           router_w[i], router_b[i], S, alpha)
        y, dest = _moe_ffn(u, top_idx, i, moe_w_gu, moe_b_gu, moe_w_down, moe_b_down)
        nxt = (i + 1) % depth
        xt, u = _combine_ln(xt, y, dest, top_w, g_f[i], ln_g[i, 1], ln_b[i, 1], sc_a[nxt], sh_a[nxt], S, alpha)
    return xt.reshape(B, S, D)
```

```python
import functools
import math

import jax
import jax.numpy as jnp
from jax import lax
from jax.experimental import pallas as pl
from jax.experimental.pallas import tpu as pltpu

D_MODEL = 2048
N_MIXERS = 4
MLA_HEADS, MLA_Q_LORA, MLA_KV_LORA, MLA_NOPE, MLA_ROPE, MLA_V = 16, 512, 512, 128, 64, 128
ROPE_THETA = 10000.0
GDN_K_HEADS, GDN_V_HEADS, GDN_DK, GDN_DV, GDN_CONV, GDN_CHUNK = 16, 32, 128, 128, 4, 64
GLA_HEADS, GLA_GATE_RANK, GLA_GATE_NORMALIZER, GLA_CHUNK = 4, 16, 16.0, 64
MOBA_HEADS, MOBA_BLOCK, MOBA_TOPK = 16, 256, 3
REL_BUCKETS, REL_MAX_DIST = 32, 128
N_EXPERTS, TOP_K, EXPERT_FF = 32, 4, 768
SWIGLU_LIMIT, SWIGLU_ALPHA = 7.0, 1.702
LN_EPS, RMS_EPS, L2_EPS = 1e-5, 1e-6, 1e-6

V7X_LANES = 128
V7X_VMEM_LIMIT_BYTES = 56 * 1024 * 1024

F32 = jnp.float32
BF16 = jnp.bfloat16
HIGHEST = lax.Precision.HIGHEST
NEG_INF = float("-inf")


def _params(*sem):
    return pltpu.CompilerParams(dimension_semantics=sem, vmem_limit_bytes=V7X_VMEM_LIMIT_BYTES)


def _dot(a, b, dims=None, precision=None):
    if dims is None:
        dims = (((a.ndim - 1,), (0,)), ((), ()))
    return lax.dot_general(a, b, dims, precision=precision, preferred_element_type=F32)


def _dot_nt(a, b, precision=None):
    return _dot(a, b, (((1,), (1,)), ((), ())), precision)


def _dot_tn(a, b, precision=None):
    return _dot(a, b, (((0,), (0,)), ((), ())), precision)


def _sigmoid(x):
    return 1.0 / (1.0 + jnp.exp(-x))


def _silu(x):
    return x * _sigmoid(x)


def _iota(shape, dim):
    return lax.broadcasted_iota(jnp.int32, shape, dim)


def _mm_kernel(valid_cols, rms, w_transposed, x_ref, *refs):
    g_ref = refs[0] if rms else None
    w_ref, o_ref, wbf_ref = refs[-3:]

    @pl.when(pl.program_id(1) == 0)
    def _():
        w = w_ref[...]
        if valid_cols is not None:
            w = jnp.where(_iota(w.shape, 0 if w_transposed else 1) < valid_cols, w, 0.0)
        wbf_ref[...] = w.astype(BF16)

    x = x_ref[...]
    if rms:
        x = x.astype(F32)
        x = x * lax.rsqrt(jnp.mean(x * x, -1, keepdims=True) + RMS_EPS) * g_ref[...]
    dot = _dot_nt if w_transposed else _dot
    o_ref[...] = dot(x.astype(BF16), wbf_ref[...]).astype(o_ref.dtype)


def _mm_tiles(M, K, N):
    tm = 1024 if M % 1024 == 0 else M
    tn = N
    for cand in (1024, 512, 256, 128):
        if N % cand == 0 and K * cand * 4 <= 8 * 1024 * 1024:
            tn = cand
            break
    return tm, tn


def _mm(x, w, out_dtype, rms_gain=None, x_col=0, col0=0, n_cols=None, w_transposed=False):
    M = x.shape[0]
    K, n_w = (w.shape[1], w.shape[0]) if w_transposed else w.shape
    N = n_w if n_cols is None else n_cols
    tm, tn = _mm_tiles(M, K, N)
    assert col0 % tn == 0
    valid_cols = n_w - col0 if col0 + N > n_w else None
    assert valid_cols is None or N == tn
    c0 = col0 // tn
    x_spec = pl.BlockSpec((tm, K), lambda n, m: (m, x_col))
    if w_transposed:
        w_spec, w_tile = pl.BlockSpec((tn, K), lambda n, m: (c0 + n, 0)), (tn, K)
    else:
        w_spec, w_tile = pl.BlockSpec((K, tn), lambda n, m: (0, c0 + n)), (K, tn)
    if rms_gain is None:
        in_specs, args = [x_spec, w_spec], (x, w)
    else:
        in_specs, args = [x_spec, pl.BlockSpec((1, K), lambda n, m: (0, 0)), w_spec], (x, rms_gain[None, :], w)
    return pl.pallas_call(
        functools.partial(_mm_kernel, valid_cols, rms_gain is not None, w_transposed),
        grid=(N // tn, M // tm),
        in_specs=in_specs,
        out_specs=pl.BlockSpec((tm, tn), lambda n, m: (m, n)),
        out_shape=jax.ShapeDtypeStruct((M, N), out_dtype),
        scratch_shapes=[pltpu.VMEM(w_tile, BF16)],
        compiler_params=_params("parallel", "arbitrary"),
        name="proj_matmul",
    )(*args)


def _ada_kernel(c_ref, w_ref, b_ref, o_ref):
    c = _silu(c_ref[...]).astype(BF16)
    o_ref[0] = _dot(c, w_ref[0].astype(BF16)) + b_ref[0]


def _ada_mod(c, ada_w, ada_b):
    depth, D, N = ada_w.shape
    B = c.shape[0]
    rows = 8
    c_pad = jnp.pad(c, ((0, rows - B), (0, 0)))
    tn = 1024
    out = pl.pallas_call(
        _ada_kernel,
        grid=(depth, N // tn),
        in_specs=[pl.BlockSpec((rows, D), lambda i, n: (0, 0)),
                  pl.BlockSpec((1, D, tn), lambda i, n: (i, 0, n)),
                  pl.BlockSpec((1, 1, tn), lambda i, n: (i, 0, n))],
        out_specs=pl.BlockSpec((1, rows, tn), lambda i, n: (i, 0, n)),
        out_shape=jax.ShapeDtypeStruct((depth, rows, N), F32),
        compiler_params=_params("parallel", "parallel"),
        name="ada_mod",
    )(c_pad, ada_w, ada_b.reshape(depth, 1, N))
    return out[:, :B]


ROW_TILE = 256


def _modulate_kernel(x_ref, sc_ref, sh_ref, u_ref):
    u_ref[...] = (x_ref[...] * (1.0 + sc_ref[0]) + sh_ref[0]).astype(u_ref.dtype)


def _row_specs(D, S, tr):
    vec = pl.BlockSpec((1, 1, D), lambda i: ((i * tr) // S, 0, 0))
    row = pl.BlockSpec((tr, D), lambda i: (i, 0))
    par = pl.BlockSpec((1, D), lambda i: (0, 0))
    return vec, row, par


def _modulate(x, sc, sh, S):
    T, D = x.shape
    tr = ROW_TILE
    vec, row, _ = _row_specs(D, S, tr)
    return pl.pallas_call(
        _modulate_kernel, grid=(T // tr,), in_specs=[row, vec, vec], out_specs=row,
        out_shape=jax.ShapeDtypeStruct((T, D), BF16),
        compiler_params=_params("parallel"), name="modulate",
    )(x, sc[:, None, :], sh[:, None, :])


def _deepnorm(alpha, x, h, gate, g, b):
    y = alpha * x + (1.0 + gate) * h
    mu = jnp.mean(y, -1, keepdims=True)
    yc = y - mu
    var = jnp.mean(yc * yc, -1, keepdims=True)
    return yc * lax.rsqrt(var + LN_EPS) * g + b


U32 = jnp.uint32
ROW_PARTS = D_MODEL // (2 * V7X_LANES)
assert ROW_PARTS == 8


def _bits(x):
    return lax.bitcast_convert_type(x.astype(BF16).astype(F32), U32)


def _store_row_tiled(ref, row0, value):
    n = value.shape[0]
    for j in range(ROW_PARTS):
        lo = _bits(value[:, (2 * j) * V7X_LANES:(2 * j + 1) * V7X_LANES])
        hi = _bits(value[:, (2 * j + 1) * V7X_LANES:(2 * j + 2) * V7X_LANES])
        ref[pl.ds(row0 * ROW_PARTS + j, n, stride=ROW_PARTS), :] = lax.shift_right_logical(lo, U32(16)) | hi


def _load_row_tiled(ref, row0, n):
    parts = []
    for j in range(ROW_PARTS):
        w = ref[pl.ds(row0 * ROW_PARTS + j, n, stride=ROW_PARTS), :]
        parts.append(lax.bitcast_convert_type(lax.shift_left(w, U32(16)), F32))
        parts.append(lax.bitcast_convert_type(w & U32(0xFFFF0000), F32))
    return jnp.concatenate(parts, -1)


def _ln_router_kernel(alpha, x_ref, h_ref, gate_ref, g_ref, b_ref, sc_ref, sh_ref, rwh_ref, rwl_ref, rb_ref,
                      xo_ref, u_ref, idx_ref, wgt_ref):
    xn = _deepnorm(alpha, x_ref[...], h_ref[...].astype(F32), gate_ref[0], g_ref[...], b_ref[...])
    xo_ref[...] = xn
    u = xn * (1.0 + sc_ref[0]) + sh_ref[0]
    _store_row_tiled(u_ref, 0, u)
    u_hi = u.astype(BF16)
    u_lo = (u - u_hi.astype(F32)).astype(BF16)
    logits = (_dot_nt(rwh_ref[...], u_hi) + (_dot_nt(rwh_ref[...], u_lo) + _dot_nt(rwl_ref[...], u_hi))) + rb_ref[...]
    expert = _iota(logits.shape, 0)
    vals = logits
    top_v, top_i = [], []
    for _ in range(TOP_K):
        m = jnp.max(vals, 0, keepdims=True)
        i = jnp.min(jnp.where(vals == m, expert, N_EXPERTS), 0, keepdims=True)
        top_v.append(m)
        top_i.append(i)
        vals = jnp.where(expert == i, NEG_INF, vals)
    exps = [jnp.exp(v - top_v[0]) for v in top_v]
    denom = functools.reduce(lambda a, b: a + b, exps)
    slot = _iota(idx_ref.shape, 0)
    idx = jnp.zeros(idx_ref.shape, jnp.int32)
    wgt = jnp.zeros(wgt_ref.shape, F32)
    for k in range(TOP_K):
        idx = jnp.where(slot == k, top_i[k], idx)
        wgt = jnp.where(slot == k, exps[k] / denom, wgt)
    idx_ref[...] = idx
    wgt_ref[...] = wgt


def _ln_router(x, h, gate, ln_g, ln_b, sc, sh, router_w, router_b, S, alpha):
    T, D = x.shape
    tr = ROW_TILE
    vec, row, par = _row_specs(D, S, tr)
    slots = 8
    choice = pl.BlockSpec((slots, tr), lambda i: (0, i))
    tiled_row = pl.BlockSpec((tr * ROW_PARTS, V7X_LANES), lambda i: (i, 0))
    rwt = router_w.T
    rw_hi = rwt.astype(BF16)
    rw_lo = (rwt - rw_hi.astype(F32)).astype(BF16)
    rw_spec = pl.BlockSpec((N_EXPERTS, D), lambda i: (0, 0))
    xo, u, idx, wgt = pl.pallas_call(
        functools.partial(_ln_router_kernel, alpha), grid=(T // tr,),
        in_specs=[row, row, vec, par, par, vec, vec, rw_spec, rw_spec,
                  pl.BlockSpec((N_EXPERTS, 1), lambda i: (0, 0))],
        out_specs=[row, tiled_row, choice, choice],
        out_shape=[jax.ShapeDtypeStruct((T, D), F32), jax.ShapeDtypeStruct((T * ROW_PARTS, V7X_LANES), U32),
                   jax.ShapeDtypeStruct((slots, T), jnp.int32), jax.ShapeDtypeStruct((slots, T), F32)],
        compiler_params=_params("parallel"), name="deepnorm_ln_router",
    )(x, h, gate[:, None, :], ln_g[None, :], ln_b[None, :], sc[:, None, :], sh[:, None, :], rw_hi, rw_lo,
      router_b[:, None])
    return xo, u, idx[:TOP_K].T, wgt[:TOP_K].T


MOE_ROW_TILE = 512


def _expert_weights(te_ref, tn_ref, ts_ref, w_hbm, wbuf_ref, wbf_ref, sems):
    t = pl.program_id(0)
    slot = ts_ref[t]

    def fetch(expert, s):
        return pltpu.make_async_copy(w_hbm.at[expert], wbuf_ref.at[s], sems.at[s])

    @pl.when(t == 0)
    def _():
        fetch(te_ref[0], 0).start(priority=1)

    @pl.when(tn_ref[t] >= 0)
    def _():
        fetch(tn_ref[t], 1 - slot).start(priority=1)

    @pl.when(slot >= 0)
    def _():
        fetch(te_ref[t], slot).wait()
        wbf_ref[...] = wbuf_ref[slot].astype(BF16)


def _moe_gu_kernel(te_ref, tv_ref, tn_ref, ts_ref, x_ref, w_hbm, b_ref, h_ref, wbuf_ref, wbf_ref, sems):
    _expert_weights(te_ref, tn_ref, ts_ref, w_hbm, wbuf_ref, wbf_ref, sems)

    @pl.when(tv_ref[pl.program_id(0)] > 0)
    def _():
        x = _load_row_tiled(x_ref, 0, x_ref.shape[0] // ROW_PARTS).astype(BF16)
        gu = _dot(x, wbf_ref[...]) + b_ref[0]
        gl = jnp.minimum(gu[:, :EXPERT_FF], SWIGLU_LIMIT)
        up = jnp.clip(gu[:, EXPERT_FF:], -SWIGLU_LIMIT, SWIGLU_LIMIT)
        h_ref[...] = ((up + 1.0) * gl * _sigmoid(gl * SWIGLU_ALPHA)).astype(h_ref.dtype)

    @pl.when(tv_ref[pl.program_id(0)] == 0)
    def _():
        h_ref[...] = jnp.zeros(h_ref.shape, h_ref.dtype)


def _moe_down_kernel(te_ref, tv_ref, tn_ref, ts_ref, h_ref, w_hbm, b_ref, y_ref, wbuf_ref, wbf_ref, sems):
    _expert_weights(te_ref, tn_ref, ts_ref, w_hbm, wbuf_ref, wbf_ref, sems)

    @pl.when(tv_ref[pl.program_id(0)] > 0)
    def _():
        _store_row_tiled(y_ref, 0, _dot(h_ref[...], wbf_ref[...]) + b_ref[0])

    @pl.when(tv_ref[pl.program_id(0)] == 0)
    def _():
        y_ref[...] = jnp.zeros(y_ref.shape, y_ref.dtype)


def _route_metadata(top_idx, tm):
    T = top_idx.shape[0]
    P = T * TOP_K
    n_tiles = (P + N_EXPERTS * (tm - 1)) // tm
    e_flat = top_idx.reshape(P)
    onehot = (e_flat[:, None] == jnp.arange(N_EXPERTS)[None, :]).astype(jnp.int32)
    csum = jnp.cumsum(onehot, axis=0)
    counts = csum[-1]
    rank = jnp.sum(csum * onehot, axis=1) - 1
    padded = ((counts + tm - 1) // tm) * tm
    ends_p = jnp.cumsum(padded)
    starts_p = ends_p - padded
    dest = (starts_p[e_flat] + rank).astype(jnp.int32)
    tile_start = jnp.arange(n_tiles, dtype=jnp.int32) * tm
    tile_valid = (tile_start < ends_p[-1]).astype(jnp.int32)
    tile_expert = jnp.sum((tile_start[:, None] >= ends_p[None, :]).astype(jnp.int32), axis=1)
    last_expert = jnp.max(jnp.where(counts > 0, jnp.arange(N_EXPERTS), 0))
    tile_expert = jnp.where(tile_valid > 0, tile_expert, last_expert).astype(jnp.int32)
    first = jnp.concatenate([jnp.ones((1,), bool), tile_expert[1:] != tile_expert[:-1]])
    later = tile_expert[None, :] > tile_expert[:, None]
    nxt = jnp.min(jnp.where(later, tile_expert[None, :], N_EXPERTS), axis=1)
    tile_next = jnp.where(jnp.logical_and(first, nxt < N_EXPERTS), nxt, -1).astype(jnp.int32)
    tile_slot = jnp.where(first, (jnp.cumsum(first.astype(jnp.int32)) - 1) % 2, -1).astype(jnp.int32)
    return dest, tile_expert, tile_valid, tile_next, tile_slot, n_tiles


DISPATCH_ROWS = 2048


def _dispatch_kernel(dest_ref, u_ref, init_hbm, xs_hbm, sem):
    del init_hbm
    R = dest_ref.shape[-1]

    def row(ref, i):
        return ref.at[pl.ds(pl.multiple_of(i * ROW_PARTS, ROW_PARTS), ROW_PARTS)]

    def issue(t, carry):
        for k in range(TOP_K):
            pltpu.make_async_copy(row(u_ref, t), row(xs_hbm, dest_ref[0, 0, t * TOP_K + k]), sem).start(priority=k % 2)
        return carry

    lax.fori_loop(0, R // TOP_K, issue, 0, unroll=2)
    n = (R // TOP_K) * ROW_PARTS
    for _ in range(TOP_K):
        pltpu.make_async_copy(u_ref, xs_hbm.at[pl.ds(0, n)], sem).wait()


def _dispatch(u, dest, rows):
    P = dest.shape[0]
    R = DISPATCH_ROWS
    shape = (rows * ROW_PARTS, u.shape[1])
    return pl.pallas_call(
        _dispatch_kernel,
        grid=(P // R,),
        in_specs=[pl.BlockSpec((1, 1, R), lambda s: (s, 0, 0), memory_space=pltpu.SMEM),
                  pl.BlockSpec(((R // TOP_K) * ROW_PARTS, u.shape[1]), lambda s: (s, 0)),
                  pl.BlockSpec(memory_space=pl.ANY)],
        out_specs=pl.BlockSpec(memory_space=pl.ANY),
        out_shape=jax.ShapeDtypeStruct(shape, u.dtype),
        scratch_shapes=[pltpu.SemaphoreType.DMA(())],
        input_output_aliases={2: 0},
        compiler_params=_params("arbitrary"), name="moe_dispatch",
    )(dest.reshape(P // R, 1, R), u, jnp.zeros(shape, u.dtype))


def _moe_ffn(u, top_idx, layer, w_gu, b_gu, w_down, b_down):
    D = D_MODEL
    tm = MOE_ROW_TILE
    dest, tile_expert, tile_valid, tile_next, tile_slot, n_tiles = _route_metadata(top_idx, tm)
    tile_expert = tile_expert + layer * N_EXPERTS
    tile_next = jnp.where(tile_next >= 0, tile_next + layer * N_EXPERTS, -1)
    sched = (tile_expert, tile_valid, tile_next, tile_slot)
    w_gu, w_down = (w.reshape((-1,) + w.shape[2:]) for w in (w_gu, w_down))
    b_gu, b_down = (b.reshape(-1, b.shape[-1]) for b in (b_gu, b_down))
    rows = n_tiles * tm
    x_sorted = _dispatch(u, dest, rows)
    ff2 = 2 * EXPERT_FF
    tiled = pl.BlockSpec((tm * ROW_PARTS, V7X_LANES), lambda t, *_: (t, 0))
    hidden = pl.BlockSpec((tm, EXPERT_FF), lambda t, *_: (t, 0))
    weights = pl.BlockSpec(memory_space=pl.ANY)

    def bias(n):
        return pl.BlockSpec((1, 1, n), lambda t, te, *_: (te[t], 0, 0))

    def weight_scratch(k, n):
        return [pltpu.VMEM((2, k, n), F32), pltpu.VMEM((k, n), BF16), pltpu.SemaphoreType.DMA((2,))]

    h = pl.pallas_call(
        _moe_gu_kernel,
        grid_spec=pltpu.PrefetchScalarGridSpec(
            num_scalar_prefetch=4, grid=(n_tiles,),
            in_specs=[tiled, weights, bias(ff2)], out_specs=hidden, scratch_shapes=weight_scratch(D, ff2)),
        out_shape=jax.ShapeDtypeStruct((rows, EXPERT_FF), BF16),
        compiler_params=_params("arbitrary"), name="moe_gate_up",
    )(*sched, x_sorted, w_gu, b_gu[:, None, :])
    y = pl.pallas_call(
        _moe_down_kernel,
        grid_spec=pltpu.PrefetchScalarGridSpec(
            num_scalar_prefetch=4, grid=(n_tiles,),
            in_specs=[hidden, weights, bias(D)], out_specs=tiled, scratch_shapes=weight_scratch(EXPERT_FF, D)),
        out_shape=jax.ShapeDtypeStruct((rows * ROW_PARTS, V7X_LANES), U32),
        compiler_params=_params("arbitrary"), name="moe_down",
    )(*sched, h, w_down, b_down[:, None, :])
    return y, dest


COMBINE_TILE = 256
COMBINE_SUB = 64


def _combine_ln_kernel(alpha, dest_ref, next_ref, y_hbm, x_ref, wgt_ref, gate_ref, g_ref, b_ref, sc_ref, sh_ref,
                       xo_ref, u_ref, buf0_ref, buf1_ref, sems):
    i = pl.program_id(0)
    bufs = (buf0_ref, buf1_ref)

    def row(ref, r):
        return ref.at[pl.ds(pl.multiple_of(r * ROW_PARTS, ROW_PARTS), ROW_PARTS)]

    def gather(idx_ref, slot):
        def issue(tok, carry):
            for k in range(TOP_K):
                pltpu.make_async_copy(row(y_hbm, idx_ref[0, 0, tok * TOP_K + k]),
                                      row(bufs[slot], k * COMBINE_TILE + tok), sems.at[slot]).start(priority=k % 2)
            return carry

        lax.fori_loop(0, COMBINE_TILE, issue, 0, unroll=2)

    def finish(slot):
        buf = bufs[slot]
        pltpu.make_async_copy(y_hbm.at[pl.ds(0, buf.shape[0])], buf, sems.at[slot]).wait()
        for lo in range(0, COMBINE_TILE, COMBINE_SUB):
            wgt = wgt_ref[pl.ds(lo, COMBINE_SUB), :]
            f = sum(_load_row_tiled(buf, k * COMBINE_TILE + lo, COMBINE_SUB) * wgt[:, k:k + 1] for k in range(TOP_K))
            xn = _deepnorm(alpha, x_ref[pl.ds(lo, COMBINE_SUB), :], f, gate_ref[0], g_ref[...], b_ref[...])
            xo_ref[pl.ds(lo, COMBINE_SUB), :] = xn
            u_ref[pl.ds(lo, COMBINE_SUB), :] = (xn * (1.0 + sc_ref[0]) + sh_ref[0]).astype(u_ref.dtype)

    pl.when(i == 0)(functools.partial(gather, dest_ref, 0))
    has_next = i + 1 < pl.num_programs(0)
    for slot in range(2):
        mine = i % 2 == slot
        pl.when(jnp.logical_and(has_next, mine))(functools.partial(gather, next_ref, 1 - slot))
        pl.when(mine)(functools.partial(finish, slot))


def _combine_ln(x, y, dest, top_w, gate, ln_g, ln_b, sc, sh, S, alpha):
    T, D = x.shape
    tr = COMBINE_TILE
    vec, row, par = _row_specs(D, S, tr)
    n_steps = T // tr
    dest = dest.reshape(n_steps, 1, tr * TOP_K)
    gathered = pltpu.VMEM((tr * TOP_K * ROW_PARTS, V7X_LANES), U32)
    return pl.pallas_call(
        functools.partial(_combine_ln_kernel, alpha), grid=(n_steps,),
        in_specs=[pl.BlockSpec((1, 1, tr * TOP_K), lambda i: (i, 0, 0), memory_space=pltpu.SMEM),
                  pl.BlockSpec((1, 1, tr * TOP_K), lambda i: (jnp.minimum(i + 1, n_steps - 1), 0, 0),
                               memory_space=pltpu.SMEM),
                  pl.BlockSpec(memory_space=pl.ANY), row,
                  pl.BlockSpec((tr, TOP_K), lambda i: (i, 0)), vec, par, par, vec, vec],
        out_specs=[row, row],
        out_shape=[jax.ShapeDtypeStruct((T, D), F32), jax.ShapeDtypeStruct((T, D), BF16)],
        scratch_shapes=[gathered, gathered, pltpu.SemaphoreType.DMA((2,))],
        compiler_params=_params("arbitrary"), name="moe_combine_ln",
    )(dest, dest, y, x, top_w, gate[:, None, :], ln_g[None, :], ln_b[None, :], sc[:, None, :], sh[:, None, :])


ATTN_TILE = 256


LOG2E = math.log2(math.e)


def _softmax_pv(s2, v):
    p = jnp.exp2(s2 - jnp.max(s2, -1, keepdims=True))
    return _dot(p.astype(BF16), v) / jnp.sum(p, -1, keepdims=True)


CAUSAL_VARIANTS = 4


def _causal_widths(n_tiles):
    step = max(n_tiles // CAUSAL_VARIANTS, 1)
    bounds = list(range(step, n_tiles, step)) + [n_tiles]
    return [(n, n) for n in bounds]


def _mla_attn_kernel(scale, qn_ref, qr_ref, cos_ref, sin_ref, kv_ref, kr_ref, o_ref):
    qi = pl.program_id(2)
    t = ATTN_TILE
    n_tiles = kv_ref.shape[1] // t
    x = qr_ref[0].astype(F32)
    half = MLA_ROPE // 2
    first_half = _iota(x.shape, 1) % MLA_ROPE < half
    swapped = jnp.where(first_half, pltpu.roll(x, x.shape[1] - half, 1), pltpu.roll(x, half, 1))
    qr_all = (x * cos_ref[...] + swapped * sin_ref[...]).astype(BF16)

    def attend(n_kv):
        w = n_kv * t
        visible = _iota((t, w), 1) <= _iota((t, w), 0) + qi * t
        outs = []
        for hh in range(2):
            qn = qn_ref[0, :, hh * MLA_NOPE:(hh + 1) * MLA_NOPE]
            qr = qr_all[:, hh * MLA_ROPE:(hh + 1) * MLA_ROPE]
            c0 = hh * (MLA_NOPE + MLA_V)
            s = (_dot_nt(qn, kv_ref[0, 0:w, c0:c0 + MLA_NOPE]) + _dot_nt(qr, kr_ref[0, 0:w, :])) * scale
            s = jnp.where(visible, s, NEG_INF)
            outs.append(_softmax_pv(s, kv_ref[0, 0:w, c0 + MLA_NOPE:c0 + MLA_NOPE + MLA_V]))
        o_ref[0] = jnp.concatenate(outs, -1).astype(o_ref.dtype)

    lo = 0
    for hi, n_kv in _causal_widths(n_tiles):
        pl.when(jnp.logical_and(qi >= lo, qi < hi))(functools.partial(attend, n_kv))
        lo = hi


def _rope(x, cos, sin):
    half = x.shape[-1] // 2
    x1, x2 = x[..., :half], x[..., half:]
    return jnp.concatenate([x1 * cos - x2 * sin, x2 * cos + x1 * sin], -1)


def _mla_mixer(u, B, S, w_in, q_norm, kv_norm, w_qb, w_kvb, w_o):
    H = MLA_HEADS
    n_lat = MLA_Q_LORA + MLA_KV_LORA
    lat = _mm(u, w_in, F32, n_cols=n_lat)
    k_rope = _mm(u, w_in, F32, col0=n_lat, n_cols=V7X_LANES)[:, :MLA_ROPE]
    wq = w_qb.reshape(MLA_Q_LORA, H, MLA_NOPE + MLA_ROPE)
    q_nope = _mm(lat, wq[:, :, :MLA_NOPE].reshape(MLA_Q_LORA, H * MLA_NOPE), BF16, rms_gain=q_norm, x_col=0)
    q_rope = _mm(lat, wq[:, :, MLA_NOPE:].reshape(MLA_Q_LORA, H * MLA_ROPE), F32, rms_gain=q_norm, x_col=0)
    kv = _mm(lat, w_kvb, BF16, rms_gain=kv_norm, x_col=1)
    inv_freq = ROPE_THETA ** (-jnp.arange(MLA_ROPE // 2, dtype=F32) / (MLA_ROPE // 2))
    ang = jnp.arange(S, dtype=F32)[:, None] * inv_freq[None, :]
    cos, sin = jnp.cos(ang), jnp.sin(ang)
    k_rope = _rope(k_rope.reshape(B, S, MLA_ROPE), cos, sin).astype(BF16)
    cos2 = jnp.tile(jnp.concatenate([cos, cos], -1), (1, 2))
    sin2 = jnp.tile(jnp.concatenate([-sin, sin], -1), (1, 2))
    t = ATTN_TILE
    o = pl.pallas_call(
        functools.partial(_mla_attn_kernel, (MLA_NOPE + MLA_ROPE) ** -0.5 * LOG2E),
        grid=(B, H // 2, S // t),
        in_specs=[pl.BlockSpec((1, t, 2 * MLA_NOPE), lambda b, h, i: (b, i, h)),
                  pl.BlockSpec((1, t, 2 * MLA_ROPE), lambda b, h, i: (b, i, h)),
                  pl.BlockSpec((t, 2 * MLA_ROPE), lambda b, h, i: (i, 0)),
                  pl.BlockSpec((t, 2 * MLA_ROPE), lambda b, h, i: (i, 0)),
                  pl.BlockSpec((1, S, 2 * (MLA_NOPE + MLA_V)), lambda b, h, i: (b, 0, h)),
                  pl.BlockSpec((1, S, MLA_ROPE), lambda b, h, i: (b, 0, 0))],
        out_specs=pl.BlockSpec((1, t, 2 * MLA_V), lambda b, h, i: (b, i, h)),
        out_shape=jax.ShapeDtypeStruct((B, S, H * MLA_V), BF16),
        compiler_params=_params("parallel", "parallel", "arbitrary"), name="mla_attention",
    )(q_nope.reshape(B, S, -1), q_rope.reshape(B, S, -1), cos2, sin2, kv.reshape(B, S, -1), k_rope)
    return _mm(o.reshape(B * S, H * MLA_V), w_o, F32)


def _t5_bucket(dist):
    n = jnp.maximum(dist, 0)
    max_exact = REL_BUCKETS // 2
    large = max_exact + (jnp.log(jnp.maximum(n, 1).astype(F32) / max_exact)
                         / math.log(REL_MAX_DIST / max_exact) * (REL_BUCKETS - max_exact)).astype(jnp.int32)
    large = jnp.minimum(large, REL_BUCKETS - 1)
    return jnp.where(n < max_exact, n, large)


MOBA_FAR = 2


def _moba_kernel(scale, n_sel, tab_ref, q_ref, k_ref, v_ref, bkt_ref, o_ref, kbf_ref, vbf_ref, kmean_ref, bias_ref):
    h, b, qi = pl.program_id(0), pl.program_id(1), pl.program_id(2)
    L = MOBA_BLOCK
    S = k_ref.shape[1]
    n_blk = S // L

    @pl.when(jnp.logical_and(b == 0, qi == 0))
    def _():
        for d in range(MOBA_FAR + 1):
            bucket = bkt_ref[d]
            tile = jnp.zeros((L, L), F32)
            for e in range(REL_BUCKETS):
                tile = jnp.where(bucket == e, tab_ref[e, h] * LOG2E, tile)
            bias_ref[d] = tile

    @pl.when(qi == 0)
    def _():
        k = k_ref[0].astype(F32)
        kbf_ref[...] = k.astype(BF16)
        vbf_ref[...] = v_ref[0].astype(BF16)
        kmean_ref[...] = jnp.zeros(kmean_ref.shape, F32)
        kmean_ref[0:n_blk, :] = jnp.mean(k.reshape(n_blk, L, k.shape[-1]), axis=1)

    q = q_ref[0].astype(F32)
    qb = q.astype(BF16)
    nb = kmean_ref.shape[0]
    blk = _iota((nb, L), 0)
    gate = jnp.where(blk < qi, _dot_nt(kmean_ref[...], q, precision=HIGHEST), NEG_INF)
    picked = jnp.zeros((nb, L), F32)
    for j in range(n_blk - 1):
        gj = gate[j:j + 1, :]
        beats = jnp.logical_or(gate > gj, jnp.logical_and(gate == gj, blk < j))
        in_topk = jnp.sum(jnp.where(beats, 1.0, 0.0), 0, keepdims=True) < n_sel
        picked = jnp.where(blk == j, jnp.where(jnp.logical_and(in_topk, j < qi), 1.0, 0.0), picked)
    picked = _dot_tn(picked, jnp.where(_iota((nb, nb), 0) == _iota((nb, nb), 1), 1.0, 0.0))
    causal_add = jnp.where(_iota((L, L), 1) <= _iota((L, L), 0), 0.0, NEG_INF)
    past_add = [jnp.where(picked[:, j:j + 1] > 0.5, 0.0, NEG_INF) for j in range(n_blk)]

    def attend(n_kv):
        w = n_kv * L
        s = _dot_nt(qb, kbf_ref[0:w, :]) * scale
        parts = []
        for j in range(n_kv):
            bias = bias_ref[jnp.clip(qi - j, 0, MOBA_FAR)]
            parts.append(s[:, j * L:(j + 1) * L] + bias + jnp.where(j == qi, causal_add, past_add[j]))
        o_ref[0] = _softmax_pv(jnp.concatenate(parts, -1), vbf_ref[0:w, :]).astype(o_ref.dtype)

    lo = 0
    for hi, n_kv in _causal_widths(n_blk):
        pl.when(jnp.logical_and(qi >= lo, qi < hi))(functools.partial(attend, n_kv))
        lo = hi


def _moba_mixer(u, B, S, w_in, w_o, rel_bias):
    H, Dh, L = MOBA_HEADS, u.shape[1] // MOBA_HEADS, MOBA_BLOCK
    n_blk = S // L
    assert S % L == 0 and Dh == V7X_LANES and (MOBA_FAR - 1) * L >= REL_MAX_DIST
    n_sel = max(min(MOBA_TOPK, n_blk - 1), 1)
    qkv = _mm(u, w_in, BF16).reshape(B, S, 3 * H * Dh)
    qk = jnp.arange(L)[:, None] - jnp.arange(L)[None, :]
    bucket = _t5_bucket(jnp.stack([qk + d * L for d in range(MOBA_FAR + 1)])).astype(jnp.int32)
    o = pl.pallas_call(
        functools.partial(_moba_kernel, Dh ** -0.5 * LOG2E, n_sel),
        grid=(H, B, n_blk),
        in_specs=[pl.BlockSpec(memory_space=pltpu.SMEM),
                  pl.BlockSpec((1, L, Dh), lambda h, b, i: (b, i, h)),
                  pl.BlockSpec((1, S, Dh), lambda h, b, i: (b, 0, H + h)),
                  pl.BlockSpec((1, S, Dh), lambda h, b, i: (b, 0, 2 * H + h)),
                  pl.BlockSpec((MOBA_FAR + 1, L, L), lambda h, b, i: (0, 0, 0))],
        out_specs=pl.BlockSpec((1, L, Dh), lambda h, b, i: (b, i, h)),
        out_shape=jax.ShapeDtypeStruct((B, S, H * Dh), BF16),
        scratch_shapes=[pltpu.VMEM((S, Dh), BF16), pltpu.VMEM((S, Dh), BF16),
                        pltpu.VMEM((-(-n_blk // 8) * 8, Dh), F32),
                        pltpu.VMEM((MOBA_FAR + 1, L, L), F32)],
        compiler_params=_params("parallel", "arbitrary", "arbitrary"), name="moba_attention",
    )(rel_bias, qkv, qkv, qkv, bucket)
    return _mm(o.reshape(B * S, H * Dh), w_o, F32)


GDN_HEAD_GROUP = 16


def _lane_sum(x):
    return _dot(x.astype(BF16), jnp.ones((x.shape[1], x.shape[1]), BF16))


def _l2norm(x):
    return x * lax.rsqrt(_lane_sum(x * x) + L2_EPS)


def _dotb(a, b):
    return _dot(a.astype(BF16), b.astype(BF16))


def _unit_lower_inverse(a_lows, block):
    C = a_lows[0].shape[0]
    r = _iota((C, C), 0)
    c = _iota((C, C), 1)
    eye = jnp.where(r == c, 1.0, 0.0)
    same = (r // block) == (c // block)
    a_d = [jnp.where(same, a, 0.0) for a in a_lows]
    a_off = [a - d for a, d in zip(a_lows, a_d)]
    inv_d = [eye - d for d in a_d]
    pw = a_d
    k = 2
    while k < block:
        pw = [_dotb(p, p) for p in pw]
        inv_d = [_dotb(i, eye + p) for i, p in zip(inv_d, pw)]
        k *= 2
    n = [_dotb(i, o) for i, o in zip(inv_d, a_off)]
    inv_n = [eye - x for x in n]
    pw = n
    k = 2
    while k < C // block:
        pw = [_dotb(p, p) for p in pw]
        inv_n = [_dotb(i, eye + p) for i, p in zip(inv_n, pw)]
        k *= 2
    return [_dotb(i, d) for i, d in zip(inv_n, inv_d)]


GDN_HIST = 16


def _causal_conv_silu(x_ref, w_ref, hist_ref):
    C = x_ref.shape[1]
    x = x_ref[0]
    hist_ref[GDN_HIST:GDN_HIST + C, :] = x
    taps = GDN_CONV - 1
    o = _iota((taps * C, GDN_HIST + C), 0)
    r = _iota((taps * C, GDN_HIST + C), 1)
    shift = jnp.where(r == o % C + o // C + (GDN_HIST - taps), 1.0, 0.0).astype(BF16)
    shifted = _dot(shift, hist_ref[...])
    y = x.astype(F32) * w_ref[taps:taps + 1, :]
    for i in range(taps):
        y = y + shifted[i * C:(i + 1) * C, :] * w_ref[i:i + 1, :]
    hist_ref[0:GDN_HIST, :] = hist_ref[C:C + GDN_HIST, :]
    return _silu(y)


def _gdn_kernel(q_ref, k_ref, v_ref, z_ref, cwq_ref, cwk_ref, cwv_ref, gc_ref, gct_ref, beta_ref, ng_ref,
                o_ref, state_ref, hq_ref, hk_ref, hv_ref):
    G, C, DK, DV = GDN_HEAD_GROUP, GDN_CHUNK, GDN_DK, GDN_DV
    rep = GDN_V_HEADS // GDN_K_HEADS
    heads = range(G)

    @pl.when(pl.program_id(2) == 0)
    def _():
        state_ref[...] = jnp.zeros(state_ref.shape, F32)
        for hist in (hq_ref, hk_ref, hv_ref):
            hist[0:GDN_HIST, :] = jnp.zeros((GDN_HIST, hist.shape[1]), hist.dtype)

    qc = _causal_conv_silu(q_ref, cwq_ref, hq_ref)
    kc = _causal_conv_silu(k_ref, cwk_ref, hk_ref)
    vc = _causal_conv_silu(v_ref, cwv_ref, hv_ref)
    r = _iota((C, C), 0)
    c = _iota((C, C), 1)
    tri = c <= r
    strict = c < r
    gc = gc_ref[0, 0]
    gct = gct_ref[0, 0, 0]
    beta = beta_ref[0, 0]
    q = [_l2norm(qc[:, i * DK:(i + 1) * DK]) * (DK ** -0.5) for i in range(G // rep)]
    k = [_l2norm(kc[:, i * DK:(i + 1) * DK]) for i in range(G // rep)]
    assert DK == V7X_LANES and DV == V7X_LANES
    kb = [x.astype(BF16) for x in k]
    qk = [_dot_nt(a.astype(BF16), b) for a, b in zip(q, kb)]
    kk = [_dot_nt(b, b) for b in kb]
    g_col = [gc[:, h:h + 1] for h in heads]
    b_col = [beta[:, h:h + 1] for h in heads]
    decay = [jnp.where(tri, jnp.exp(jnp.where(tri, g_col[h] - gct[h:h + 1, :], 0.0)), 0.0) for h in heads]
    t_inv = _unit_lower_inverse([jnp.where(strict, kk[h // rep] * b_col[h] * decay[h], 0.0) for h in heads], 16)
    e_g = [jnp.exp(g) for g in g_col]
    rhs = [jnp.concatenate([vc[:, h * DV:(h + 1) * DV] * b_col[h], k[h // rep] * (b_col[h] * e_g[h])], -1)
           for h in heads]
    sol = [_dot(t_inv[h].astype(BF16), rhs[h].astype(BF16)) for h in heads]
    state = [state_ref[h] for h in heads]
    state_b = [s.astype(BF16) for s in state]
    v_new = [sol[h][:, :DV] - _dot(sol[h][:, DV:].astype(BF16), state_b[h]) for h in heads]
    v_new_b = [x.astype(BF16) for x in v_new]
    attn = [jnp.where(tri, qk[h // rep] * decay[h], 0.0).astype(BF16) for h in heads]
    o = [_dot((q[h // rep] * e_g[h]).astype(BF16), state_b[h]) + _dot(attn[h], v_new_b[h]) for h in heads]
    g_last = [g[C - 1:C, :] for g in g_col]
    k_tail = [(k[h // rep] * jnp.exp(g_last[h] - g_col[h])).astype(BF16) for h in heads]
    new_state = [state[h] * jnp.exp(g_last[h]) + _dot_tn(k_tail[h], v_new_b[h]) for h in heads]
    o = [x * lax.rsqrt(_lane_sum(x * x) * (1.0 / DV) + RMS_EPS) * ng_ref[...] for x in o]
    o = [o[h] * _silu(z_ref[0, :, h * DV:(h + 1) * DV].astype(F32)) for h in heads]
    for h in heads:
        state_ref[h] = new_state[h]
    o_ref[0] = jnp.concatenate(o, -1).astype(o_ref.dtype)


def _gdn_mixer(u, B, S, w_in, conv_w, a_log, dt_bias, norm_g, w_o):
    HK, HV, DK, DV, C, G = GDN_K_HEADS, GDN_V_HEADS, GDN_DK, GDN_DV, GDN_CHUNK, GDN_HEAD_GROUP
    qk_dim, v_dim = HK * DK, HV * DV
    n_main = 2 * qk_dim + 2 * v_dim
    w_t = w_in.T
    proj = _mm(u, w_t, BF16, n_cols=n_main, w_transposed=True).reshape(B, S, n_main)
    ba = _mm(u, w_t, F32, col0=n_main, n_cols=V7X_LANES, w_transposed=True).reshape(B, S, -1)
    n_conv = 2 * qk_dim + v_dim
    beta = jax.nn.sigmoid(ba[:, :, :HV])
    g = -jnp.exp(a_log) * jax.nn.softplus(ba[:, :, HV:2 * HV] + dt_bias)
    N = S // C
    gc = jnp.cumsum(g.reshape(B, N, C, HV), axis=2)
    HG = HV // G
    gc_g = gc.reshape(B, N, C, HG, G).transpose(0, 3, 1, 2, 4).reshape(B, HG, S, G)
    gct_g = gc.reshape(B, N, C, HG, G).transpose(0, 3, 1, 4, 2)
    beta_g = beta.reshape(B, S, HG, G).transpose(0, 2, 1, 3)
    kw = (G // (HV // HK)) * DK
    vw = G * DV
    k_blk, v_blk, z_blk = qk_dim // kw, 2 * qk_dim // vw, n_conv // vw
    o = pl.pallas_call(
        _gdn_kernel,
        grid=(B, HG, N),
        in_specs=[pl.BlockSpec((1, C, kw), lambda b, h, n: (b, n, h)),
                  pl.BlockSpec((1, C, kw), lambda b, h, n: (b, n, k_blk + h)),
                  pl.BlockSpec((1, C, vw), lambda b, h, n: (b, n, v_blk + h)),
                  pl.BlockSpec((1, C, vw), lambda b, h, n: (b, n, z_blk + h)),
                  pl.BlockSpec((GDN_CONV, kw), lambda b, h, n: (0, h)),
                  pl.BlockSpec((GDN_CONV, kw), lambda b, h, n: (0, k_blk + h)),
                  pl.BlockSpec((GDN_CONV, vw), lambda b, h, n: (0, v_blk + h)),
                  pl.BlockSpec((1, 1, C, G), lambda b, h, n: (b, h, n, 0)),
                  pl.BlockSpec((1, 1, 1, G, C), lambda b, h, n: (b, h, n, 0, 0)),
                  pl.BlockSpec((1, 1, C, G), lambda b, h, n: (b, h, n, 0)),
                  pl.BlockSpec((1, DV), lambda b, h, n: (0, 0))],
        out_specs=pl.BlockSpec((1, C, vw), lambda b, h, n: (b, n, h)),
        out_shape=jax.ShapeDtypeStruct((B, S, v_dim), BF16),
        scratch_shapes=[pltpu.VMEM((G, DK, DV), F32), pltpu.VMEM((GDN_HIST + C, kw), BF16),
                        pltpu.VMEM((GDN_HIST + C, kw), BF16), pltpu.VMEM((GDN_HIST + C, vw), BF16)],
        compiler_params=_params("parallel", "parallel", "arbitrary"), name="gdn_chunked",
    )(proj, proj, proj, proj, conv_w, conv_w, conv_w, gc_g, gct_g, beta_g, norm_g[None, :])
    return _mm(o.reshape(B * S, v_dim), w_o, F32)


def _gla_kernel(scale, q_ref, k_ref, v_ref, og_ref, gk_ref, wgk_ref, bgk_ref, ng_ref, o_ref, state_ref):
    C = GLA_CHUNK
    H, dv, dk = state_ref.shape
    heads = range(H)

    @pl.when(pl.program_id(1) == 0)
    def _():
        state_ref[...] = jnp.zeros(state_ref.shape, F32)

    x = _dot(gk_ref[0].astype(BF16), wgk_ref[...].astype(BF16)) + bgk_ref[...]
    log_alpha = (jnp.minimum(x, 0.0) - jnp.log(1.0 + jnp.exp(-jnp.abs(x)))) / GLA_GATE_NORMALIZER
    r = _iota((C, C), 0)
    c = _iota((C, C), 1)
    causal = c <= r
    b = _dot(jnp.where(causal, 1.0, 0.0), log_alpha, precision=HIGHEST)
    b_last = b[C - 1:C, :]
    q = q_ref[0].astype(F32) * scale
    k = k_ref[0].astype(F32)
    q_dec = (q * jnp.exp(b)).astype(BF16)
    k_inv = (k * jnp.exp(-b)).astype(BF16)
    k_tail = (k * jnp.exp(b_last - b)).astype(BF16)
    decay = jnp.exp(b_last)
    ks = [slice(h * dk, (h + 1) * dk) for h in heads]
    v = [v_ref[0, :, h * dv:(h + 1) * dv] for h in heads]
    attn = [jnp.where(causal, _dot_nt(q_dec[:, ks[h]], k_inv[:, ks[h]]), 0.0).astype(BF16) for h in heads]
    state_t = [state_ref[h] for h in heads]
    o = [_dot(attn[h], v[h]) + _dot_nt(q_dec[:, ks[h]], state_t[h].astype(BF16)) for h in heads]
    new_state = [state_t[h] * decay[:, ks[h]] + _dot_tn(v[h], k_tail[:, ks[h]]) for h in heads]
    o = [x * lax.rsqrt(jnp.mean(x * x, -1, keepdims=True) + RMS_EPS) * ng_ref[...] for x in o]
    for h in heads:
        state_ref[h] = new_state[h]
    o_ref[0] = (jnp.concatenate(o, -1) * _silu(og_ref[0].astype(F32))).astype(o_ref.dtype)


def _gla_mixer(u, B, S, w_in, w_gk, b_gk, norm_g, w_o):
    D = u.shape[1]
    H, C = GLA_HEADS, GLA_CHUNK
    key_dim, val_dim = D // 2, D
    dk, dv = key_dim // H, val_dim // H
    n_main = 2 * key_dim + 2 * val_dim
    w_t = w_in.T
    proj = _mm(u, w_t, BF16, n_cols=n_main, w_transposed=True).reshape(B, S, n_main)
    gk = _mm(u, w_t, F32, col0=n_main, n_cols=V7X_LANES, w_transposed=True).reshape(B, S, -1)
    wgk = jnp.pad(w_gk, ((0, gk.shape[-1] - GLA_GATE_RANK), (0, 0)))
    o = pl.pallas_call(
        functools.partial(_gla_kernel, dk ** -0.5),
        grid=(B, S // C),
        in_specs=[pl.BlockSpec((1, C, key_dim), lambda b, n: (b, n, 0)),
                  pl.BlockSpec((1, C, key_dim), lambda b, n: (b, n, 1)),
                  pl.BlockSpec((1, C, val_dim), lambda b, n: (b, n, 2 * key_dim // val_dim)),
                  pl.BlockSpec((1, C, val_dim), lambda b, n: (b, n, 2 * key_dim // val_dim + 1)),
                  pl.BlockSpec((1, C, gk.shape[-1]), lambda b, n: (b, n, 0)),
                  pl.BlockSpec((gk.shape[-1], key_dim), lambda b, n: (0, 0)),
                  pl.BlockSpec((1, key_dim), lambda b, n: (0, 0)),
                  pl.BlockSpec((1, dv), lambda b, n: (0, 0))],
        out_specs=pl.BlockSpec((1, C, val_dim), lambda b, n: (b, n, 0)),
        out_shape=jax.ShapeDtypeStruct((B, S, val_dim), BF16),
        scratch_shapes=[pltpu.VMEM((H, dv, dk), F32)],
        compiler_params=_params("parallel", "arbitrary"), name="gla_chunked",
    )(proj, proj, proj, proj, gk, wgk, b_gk[None, :], norm_g[None, :])
    return _mm(o.reshape(B * S, val_dim), w_o, F32)


def kernel(x, c, rel_bias, mla_w_in, mla_q_norm, mla_kv_norm, mla_w_qb, mla_w_kvb, mla_w_o, gdn_w_in, gdn_conv_w, gdn_a_log, gdn_dt_bias, gdn_norm, gdn_w_o, gla_w_in, gla_w_gk, gla_b_gk, gla_norm, gla_w_o, moba_w_in, moba_w_o, ada_w, ada_b, ln_g, ln_b, router_w, router_b, moe_w_gu, moe_b_gu, moe_w_down, moe_b_down):
    B, S, D = x.shape
    assert D == D_MODEL
    depth = ada_w.shape[0]
    alpha = (2 * depth) ** 0.25
    mod = _ada_mod(c, ada_w, ada_b)
    sh_a, sc_a, g_a, sh_f, sc_f, g_f = (mod[:, :, k * D:(k + 1) * D] for k in range(6))
    xt = x.reshape(B * S, D)
    u = _modulate(xt, sc_a[0], sh_a[0], S)
    for i in range(depth):
        m, j = i % N_MIXERS, i // N_MIXERS
        if m == 0:
            h = _mla_mixer(u, B, S, mla_w_in[j], mla_q_norm[j], mla_kv_norm[j], mla_w_qb[j], mla_w_kvb[j], mla_w_o[j])
        elif m == 1:
            h = _gdn_mixer(u, B, S, gdn_w_in[j], gdn_conv_w[j], gdn_a_log[j], gdn_dt_bias[j], gdn_norm[j], gdn_w_o[j])
        elif m == 2:
            h = _gla_mixer(u, B, S, gla_w_in[j], gla_w_gk[j], gla_b_gk[j], gla_norm[j], gla_w_o[j])
        else:
            h = _moba_mixer(u, B, S, moba_w_in[j], moba_w_o[j], rel_bias)
        xt, u, top_idx, top_w = _ln_router(xt, h, g_a[i], ln_g[i, 0], ln_b[i, 0], sc_f[i], sh_f[i],
                                           router_w[i], router_b[i], S, alpha)
        y, dest = _moe_ffn(u, top_idx, i, moe_w_gu, moe_b_gu, moe_w_down, moe_b_down)
        nxt = (i + 1) % depth
        xt, u = _combine_ln(xt, y, dest, top_w, g_f[i], ln_g[i, 1], ln_b[i, 1], sc_a[nxt], sh_a[nxt], S, alpha)
    return xt.reshape(B, S, D)
```

```python
import functools
import math

import jax
import jax.numpy as jnp
from jax import lax
from jax.experimental import pallas as pl
from jax.experimental.pallas import tpu as pltpu

D_MODEL = 2048
N_MIXERS = 4
MLA_HEADS, MLA_Q_LORA, MLA_KV_LORA, MLA_NOPE, MLA_ROPE, MLA_V = 16, 512, 512, 128, 64, 128
ROPE_THETA = 10000.0
GDN_K_HEADS, GDN_V_HEADS, GDN_DK, GDN_DV, GDN_CONV, GDN_CHUNK = 16, 32, 128, 128, 4, 64
GLA_HEADS, GLA_GATE_RANK, GLA_GATE_NORMALIZER, GLA_CHUNK = 4, 16, 16.0, 64
MOBA_HEADS, MOBA_BLOCK, MOBA_TOPK = 16, 256, 3
REL_BUCKETS, REL_MAX_DIST = 32, 128
N_EXPERTS, TOP_K, EXPERT_FF = 32, 4, 768
SWIGLU_LIMIT, SWIGLU_ALPHA = 7.0, 1.702
LN_EPS, RMS_EPS, L2_EPS = 1e-5, 1e-6, 1e-6

V7X_LANES = 128
V7X_VMEM_LIMIT_BYTES = 56 * 1024 * 1024

F32 = jnp.float32
BF16 = jnp.bfloat16
HIGHEST = lax.Precision.HIGHEST
NEG_INF = float("-inf")


def _params(*sem):
    return pltpu.CompilerParams(dimension_semantics=sem, vmem_limit_bytes=V7X_VMEM_LIMIT_BYTES)


def _dot(a, b, dims=None, precision=None):
    if dims is None:
        dims = (((a.ndim - 1,), (0,)), ((), ()))
    return lax.dot_general(a, b, dims, precision=precision, preferred_element_type=F32)


def _dot_nt(a, b, precision=None):
    return _dot(a, b, (((1,), (1,)), ((), ())), precision)


def _dot_tn(a, b, precision=None):
    return _dot(a, b, (((0,), (0,)), ((), ())), precision)


def _sigmoid(x):
    return 1.0 / (1.0 + jnp.exp(-x))


def _silu(x):
    return x * _sigmoid(x)


def _iota(shape, dim):
    return lax.broadcasted_iota(jnp.int32, shape, dim)


def _mm_kernel(valid_cols, rms, w_transposed, x_ref, *refs):
    g_ref = refs[0] if rms else None
    w_ref, o_ref, wbf_ref = refs[-3:]

    @pl.when(pl.program_id(1) == 0)
    def _():
        w = w_ref[...]
        if valid_cols is not None:
            w = jnp.where(_iota(w.shape, 0 if w_transposed else 1) < valid_cols, w, 0.0)
        wbf_ref[...] = w.astype(BF16)

    x = x_ref[...]
    if rms:
        x = x.astype(F32)
        x = x * lax.rsqrt(jnp.mean(x * x, -1, keepdims=True) + RMS_EPS) * g_ref[...]
    dot = _dot_nt if w_transposed else _dot
    o_ref[...] = dot(x.astype(BF16), wbf_ref[...]).astype(o_ref.dtype)


def _mm_tiles(M, K, N):
    tm = 1024 if M % 1024 == 0 else M
    tn = N
    for cand in (1024, 512, 256, 128):
        if N % cand == 0 and K * cand * 4 <= 8 * 1024 * 1024:
            tn = cand
            break
    return tm, tn


def _mm(x, w, out_dtype, rms_gain=None, x_col=0, col0=0, n_cols=None, w_transposed=False):
    M = x.shape[0]
    K, n_w = (w.shape[1], w.shape[0]) if w_transposed else w.shape
    N = n_w if n_cols is None else n_cols
    tm, tn = _mm_tiles(M, K, N)
    assert col0 % tn == 0
    valid_cols = n_w - col0 if col0 + N > n_w else None
    assert valid_cols is None or N == tn
    c0 = col0 // tn
    x_spec = pl.BlockSpec((tm, K), lambda n, m: (m, x_col))
    if w_transposed:
        w_spec, w_tile = pl.BlockSpec((tn, K), lambda n, m: (c0 + n, 0)), (tn, K)
    else:
        w_spec, w_tile = pl.BlockSpec((K, tn), lambda n, m: (0, c0 + n)), (K, tn)
    if rms_gain is None:
        in_specs, args = [x_spec, w_spec], (x, w)
    else:
        in_specs, args = [x_spec, pl.BlockSpec((1, K), lambda n, m: (0, 0)), w_spec], (x, rms_gain[None, :], w)
    return pl.pallas_call(
        functools.partial(_mm_kernel, valid_cols, rms_gain is not None, w_transposed),
        grid=(N // tn, M // tm),
        in_specs=in_specs,
        out_specs=pl.BlockSpec((tm, tn), lambda n, m: (m, n)),
        out_shape=jax.ShapeDtypeStruct((M, N), out_dtype),
        scratch_shapes=[pltpu.VMEM(w_tile, BF16)],
        compiler_params=_params("parallel", "arbitrary"),
        name="proj_matmul",
    )(*args)


def _ada_kernel(c_ref, w_ref, b_ref, o_ref):
    c = _silu(c_ref[...]).astype(BF16)
    o_ref[0] = _dot(c, w_ref[0].astype(BF16)) + b_ref[0]


def _ada_mod(c, ada_w, ada_b):
    depth, D, N = ada_w.shape
    B = c.shape[0]
    rows = 8
    c_pad = jnp.pad(c, ((0, rows - B), (0, 0)))
    tn = 1024
    out = pl.pallas_call(
        _ada_kernel,
        grid=(depth, N // tn),
        in_specs=[pl.BlockSpec((rows, D), lambda i, n: (0, 0)),
                  pl.BlockSpec((1, D, tn), lambda i, n: (i, 0, n)),
                  pl.BlockSpec((1, 1, tn), lambda i, n: (i, 0, n))],
        out_specs=pl.BlockSpec((1, rows, tn), lambda i, n: (i, 0, n)),
        out_shape=jax.ShapeDtypeStruct((depth, rows, N), F32),
        compiler_params=_params("parallel", "parallel"),
        name="ada_mod",
    )(c_pad, ada_w, ada_b.reshape(depth, 1, N))
    return out[:, :B]


ROW_TILE = 256


def _modulate_kernel(x_ref, sc_ref, sh_ref, u_ref):
    u_ref[...] = (x_ref[...] * (1.0 + sc_ref[0]) + sh_ref[0]).astype(u_ref.dtype)


def _row_specs(D, S, tr):
    vec = pl.BlockSpec((1, 1, D), lambda i: ((i * tr) // S, 0, 0))
    row = pl.BlockSpec((tr, D), lambda i: (i, 0))
    par = pl.BlockSpec((1, D), lambda i: (0, 0))
    return vec, row, par


def _modulate(x, sc, sh, S):
    T, D = x.shape
    tr = ROW_TILE
    vec, row, _ = _row_specs(D, S, tr)
    return pl.pallas_call(
        _modulate_kernel, grid=(T // tr,), in_specs=[row, vec, vec], out_specs=row,
        out_shape=jax.ShapeDtypeStruct((T, D), BF16),
        compiler_params=_params("parallel"), name="modulate",
    )(x, sc[:, None, :], sh[:, None, :])


def _deepnorm(alpha, x, h, gate, g, b):
    y = alpha * x + (1.0 + gate) * h
    mu = jnp.mean(y, -1, keepdims=True)
    yc = y - mu
    var = jnp.mean(yc * yc, -1, keepdims=True)
    return yc * lax.rsqrt(var + LN_EPS) * g + b


U32 = jnp.uint32
ROW_PARTS = D_MODEL // (2 * V7X_LANES)
assert ROW_PARTS == 8


def _bits(x):
    return lax.bitcast_convert_type(x.astype(BF16).astype(F32), U32)


def _store_row_tiled(ref, row0, value):
    n = value.shape[0]
    for j in range(ROW_PARTS):
        lo = _bits(value[:, (2 * j) * V7X_LANES:(2 * j + 1) * V7X_LANES])
        hi = _bits(value[:, (2 * j + 1) * V7X_LANES:(2 * j + 2) * V7X_LANES])
        ref[pl.ds(row0 * ROW_PARTS + j, n, stride=ROW_PARTS), :] = lax.shift_right_logical(lo, U32(16)) | hi


def _load_row_tiled(ref, row0, n):
    parts = []
    for j in range(ROW_PARTS):
        w = ref[pl.ds(row0 * ROW_PARTS + j, n, stride=ROW_PARTS), :]
        parts.append(lax.bitcast_convert_type(lax.shift_left(w, U32(16)), F32))
        parts.append(lax.bitcast_convert_type(w & U32(0xFFFF0000), F32))
    return jnp.concatenate(parts, -1)


def _ln_router_kernel(alpha, x_ref, h_ref, gate_ref, g_ref, b_ref, sc_ref, sh_ref, rwh_ref, rwl_ref, rb_ref,
                      xo_ref, u_ref, idx_ref, wgt_ref):
    xn = _deepnorm(alpha, x_ref[...], h_ref[...].astype(F32), gate_ref[0], g_ref[...], b_ref[...])
    xo_ref[...] = xn
    u = xn * (1.0 + sc_ref[0]) + sh_ref[0]
    _store_row_tiled(u_ref, 0, u)
    u_hi = u.astype(BF16)
    u_lo = (u - u_hi.astype(F32)).astype(BF16)
    logits = (_dot_nt(rwh_ref[...], u_hi) + (_dot_nt(rwh_ref[...], u_lo) + _dot_nt(rwl_ref[...], u_hi))) + rb_ref[...]
    expert = _iota(logits.shape, 0)
    vals = logits
    top_v, top_i = [], []
    for _ in range(TOP_K):
        m = jnp.max(vals, 0, keepdims=True)
        i = jnp.min(jnp.where(vals == m, expert, N_EXPERTS), 0, keepdims=True)
        top_v.append(m)
        top_i.append(i)
        vals = jnp.where(expert == i, NEG_INF, vals)
    exps = [jnp.exp(v - top_v[0]) for v in top_v]
    denom = functools.reduce(lambda a, b: a + b, exps)
    slot = _iota(idx_ref.shape, 0)
    idx = jnp.zeros(idx_ref.shape, jnp.int32)
    wgt = jnp.zeros(wgt_ref.shape, F32)
    for k in range(TOP_K):
        idx = jnp.where(slot == k, top_i[k], idx)
        wgt = jnp.where(slot == k, exps[k] / denom, wgt)
    idx_ref[...] = idx
    wgt_ref[...] = wgt


def _ln_router(x, h, gate, ln_g, ln_b, sc, sh, router_w, router_b, S, alpha):
    T, D = x.shape
    tr = ROW_TILE
    vec, row, par = _row_specs(D, S, tr)
    slots = 8
    choice = pl.BlockSpec((slots, tr), lambda i: (0, i))
    tiled_row = pl.BlockSpec((tr * ROW_PARTS, V7X_LANES), lambda i: (i, 0))
    rwt = router_w.T
    rw_hi = rwt.astype(BF16)
    rw_lo = (rwt - rw_hi.astype(F32)).astype(BF16)
    rw_spec = pl.BlockSpec((N_EXPERTS, D), lambda i: (0, 0))
    xo, u, idx, wgt = pl.pallas_call(
        functools.partial(_ln_router_kernel, alpha), grid=(T // tr,),
        in_specs=[row, row, vec, par, par, vec, vec, rw_spec, rw_spec,
                  pl.BlockSpec((N_EXPERTS, 1), lambda i: (0, 0))],
        out_specs=[row, tiled_row, choice, choice],
        out_shape=[jax.ShapeDtypeStruct((T, D), F32), jax.ShapeDtypeStruct((T * ROW_PARTS, V7X_LANES), U32),
                   jax.ShapeDtypeStruct((slots, T), jnp.int32), jax.ShapeDtypeStruct((slots, T), F32)],
        compiler_params=_params("parallel"), name="deepnorm_ln_router",
    )(x, h, gate[:, None, :], ln_g[None, :], ln_b[None, :], sc[:, None, :], sh[:, None, :], rw_hi, rw_lo,
      router_b[:, None])
    return xo, u, idx[:TOP_K].T, wgt[:TOP_K].T


MOE_ROW_TILE = 512


def _expert_weights(te_ref, tn_ref, ts_ref, w_hbm, wbuf_ref, wbf_ref, sems):
    t = pl.program_id(0)
    slot = ts_ref[t]

    def fetch(expert, s):
        return pltpu.make_async_copy(w_hbm.at[expert], wbuf_ref.at[s], sems.at[s])

    @pl.when(t == 0)
    def _():
        fetch(te_ref[0], 0).start(priority=1)

    @pl.when(tn_ref[t] >= 0)
    def _():
        fetch(tn_ref[t], 1 - slot).start(priority=1)

    @pl.when(slot >= 0)
    def _():
        fetch(te_ref[t], slot).wait()
        wbf_ref[...] = wbuf_ref[slot].astype(BF16)


def _moe_gu_kernel(te_ref, tv_ref, tn_ref, ts_ref, x_ref, w_hbm, b_ref, h_ref, wbuf_ref, wbf_ref, sems):
    _expert_weights(te_ref, tn_ref, ts_ref, w_hbm, wbuf_ref, wbf_ref, sems)

    @pl.when(tv_ref[pl.program_id(0)] > 0)
    def _():
        x = _load_row_tiled(x_ref, 0, x_ref.shape[0] // ROW_PARTS).astype(BF16)
        gu = _dot(x, wbf_ref[...]) + b_ref[0]
        gl = jnp.minimum(gu[:, :EXPERT_FF], SWIGLU_LIMIT)
        up = jnp.clip(gu[:, EXPERT_FF:], -SWIGLU_LIMIT, SWIGLU_LIMIT)
        h_ref[...] = ((up + 1.0) * gl * _sigmoid(gl * SWIGLU_ALPHA)).astype(h_ref.dtype)

    @pl.when(tv_ref[pl.program_id(0)] == 0)
    def _():
        h_ref[...] = jnp.zeros(h_ref.shape, h_ref.dtype)


def _moe_down_kernel(te_ref, tv_ref, tn_ref, ts_ref, h_ref, w_hbm, b_ref, y_ref, wbuf_ref, wbf_ref, sems):
    _expert_weights(te_ref, tn_ref, ts_ref, w_hbm, wbuf_ref, wbf_ref, sems)

    @pl.when(tv_ref[pl.program_id(0)] > 0)
    def _():
        _store_row_tiled(y_ref, 0, _dot(h_ref[...], wbf_ref[...]) + b_ref[0])

    @pl.when(tv_ref[pl.program_id(0)] == 0)
    def _():
        y_ref[...] = jnp.zeros(y_ref.shape, y_ref.dtype)


def _route_metadata(top_idx, tm):
    T = top_idx.shape[0]
    P = T * TOP_K
    n_tiles = (P + N_EXPERTS * (tm - 1)) // tm
    e_flat = top_idx.reshape(P)
    onehot = (e_flat[:, None] == jnp.arange(N_EXPERTS)[None, :]).astype(jnp.int32)
    csum = jnp.cumsum(onehot, axis=0)
    counts = csum[-1]
    rank = jnp.sum(csum * onehot, axis=1) - 1
    padded = ((counts + tm - 1) // tm) * tm
    ends_p = jnp.cumsum(padded)
    starts_p = ends_p - padded
    dest = (starts_p[e_flat] + rank).astype(jnp.int32)
    tile_start = jnp.arange(n_tiles, dtype=jnp.int32) * tm
    tile_valid = (tile_start < ends_p[-1]).astype(jnp.int32)
    tile_expert = jnp.sum((tile_start[:, None] >= ends_p[None, :]).astype(jnp.int32), axis=1)
    last_expert = jnp.max(jnp.where(counts > 0, jnp.arange(N_EXPERTS), 0))
    tile_expert = jnp.where(tile_valid > 0, tile_expert, last_expert).astype(jnp.int32)
    first = jnp.concatenate([jnp.ones((1,), bool), tile_expert[1:] != tile_expert[:-1]])
    later = tile_expert[None, :] > tile_expert[:, None]
    nxt = jnp.min(jnp.where(later, tile_expert[None, :], N_EXPERTS), axis=1)
    tile_next = jnp.where(jnp.logical_and(first, nxt < N_EXPERTS), nxt, -1).astype(jnp.int32)
    tile_slot = jnp.where(first, (jnp.cumsum(first.astype(jnp.int32)) - 1) % 2, -1).astype(jnp.int32)
    pads = jnp.stack([jnp.concatenate([starts_p + counts, ends_p[-1:]]),
                      jnp.concatenate([padded - counts, n_tiles - ends_p[-1:] // tm])]).astype(jnp.int32)
    return dest, tile_expert, tile_valid, tile_next, tile_slot, pads, n_tiles


DISPATCH_ROWS = 2048


def _zero_padding_rows(tm, n_pad_rows, pads_ref, xs_hbm, zero_ref, zsem):
    zero_ref[...] = jnp.zeros(zero_ref.shape, zero_ref.dtype)

    def put(first_row, n_rows):
        dst = xs_hbm.at[pl.ds(pl.multiple_of(first_row * ROW_PARTS, ROW_PARTS), n_rows * ROW_PARTS)]
        pltpu.make_async_copy(zero_ref.at[pl.ds(0, n_rows * ROW_PARTS)], dst, zsem).start()

    for e in range(N_EXPERTS):
        start, length = pads_ref[0, e], pads_ref[1, e]
        size = tm // 2
        while size >= 1:
            pl.when((length & size) != 0)(functools.partial(put, start + (length & ~(2 * size - 1)), size))
            size //= 2
    max_tail = (n_pad_rows + tm - 1) // tm
    for t in range(max_tail):
        pl.when(t < pads_ref[1, N_EXPERTS])(functools.partial(put, pads_ref[0, N_EXPERTS] + t * tm, tm))
    n = n_pad_rows * ROW_PARTS
    pltpu.make_async_copy(xs_hbm.at[pl.ds(0, n)], xs_hbm.at[pl.ds(0, n)], zsem).wait()


def _dispatch_kernel(tm, n_pad_rows, pads_ref, dest_ref, u_ref, xs_hbm, sem, zero_ref, zsem):
    R = dest_ref.shape[-1]
    pl.when(pl.program_id(0) == 0)(
        functools.partial(_zero_padding_rows, tm, n_pad_rows, pads_ref, xs_hbm, zero_ref, zsem))

    def row(ref, i):
        return ref.at[pl.ds(pl.multiple_of(i * ROW_PARTS, ROW_PARTS), ROW_PARTS)]

    def issue(t, carry):
        for k in range(TOP_K):
            pltpu.make_async_copy(row(u_ref, t), row(xs_hbm, dest_ref[0, 0, t * TOP_K + k]), sem).start(priority=k % 2)
        return carry

    lax.fori_loop(0, R // TOP_K, issue, 0, unroll=2)
    n = (R // TOP_K) * ROW_PARTS
    for _ in range(TOP_K):
        pltpu.make_async_copy(u_ref, xs_hbm.at[pl.ds(0, n)], sem).wait()


def _dispatch(u, dest, pads, rows, tm):
    P = dest.shape[0]
    R = DISPATCH_ROWS
    return pl.pallas_call(
        functools.partial(_dispatch_kernel, tm, rows - P),
        grid=(P // R,),
        in_specs=[pl.BlockSpec(memory_space=pltpu.SMEM),
                  pl.BlockSpec((1, 1, R), lambda s: (s, 0, 0), memory_space=pltpu.SMEM),
                  pl.BlockSpec(((R // TOP_K) * ROW_PARTS, u.shape[1]), lambda s: (s, 0))],
        out_specs=pl.BlockSpec(memory_space=pl.ANY),
        out_shape=jax.ShapeDtypeStruct((rows * ROW_PARTS, u.shape[1]), u.dtype),
        scratch_shapes=[pltpu.SemaphoreType.DMA(()), pltpu.VMEM((tm * ROW_PARTS, u.shape[1]), u.dtype),
                        pltpu.SemaphoreType.DMA(())],
        compiler_params=_params("arbitrary"), name="moe_dispatch",
    )(pads, dest.reshape(P // R, 1, R), u)


def _moe_ffn(u, top_idx, layer, w_gu, b_gu, w_down, b_down):
    D = D_MODEL
    tm = MOE_ROW_TILE
    dest, tile_expert, tile_valid, tile_next, tile_slot, pads, n_tiles = _route_metadata(top_idx, tm)
    tile_expert = tile_expert + layer * N_EXPERTS
    tile_next = jnp.where(tile_next >= 0, tile_next + layer * N_EXPERTS, -1)
    sched = (tile_expert, tile_valid, tile_next, tile_slot)
    w_gu, w_down = (w.reshape((-1,) + w.shape[2:]) for w in (w_gu, w_down))
    b_gu, b_down = (b.reshape(-1, b.shape[-1]) for b in (b_gu, b_down))
    rows = n_tiles * tm
    x_sorted = _dispatch(u, dest, pads, rows, tm)
    ff2 = 2 * EXPERT_FF
    tiled = pl.BlockSpec((tm * ROW_PARTS, V7X_LANES), lambda t, *_: (t, 0))
    hidden = pl.BlockSpec((tm, EXPERT_FF), lambda t, *_: (t, 0))
    weights = pl.BlockSpec(memory_space=pl.ANY)

    def bias(n):
        return pl.BlockSpec((1, 1, n), lambda t, te, *_: (te[t], 0, 0))

    def weight_scratch(k, n):
        return [pltpu.VMEM((2, k, n), F32), pltpu.VMEM((k, n), BF16), pltpu.SemaphoreType.DMA((2,))]

    h = pl.pallas_call(
        _moe_gu_kernel,
        grid_spec=pltpu.PrefetchScalarGridSpec(
            num_scalar_prefetch=4, grid=(n_tiles,),
            in_specs=[tiled, weights, bias(ff2)], out_specs=hidden, scratch_shapes=weight_scratch(D, ff2)),
        out_shape=jax.ShapeDtypeStruct((rows, EXPERT_FF), BF16),
        compiler_params=_params("arbitrary"), name="moe_gate_up",
    )(*sched, x_sorted, w_gu, b_gu[:, None, :])
    y = pl.pallas_call(
        _moe_down_kernel,
        grid_spec=pltpu.PrefetchScalarGridSpec(
            num_scalar_prefetch=4, grid=(n_tiles,),
            in_specs=[hidden, weights, bias(D)], out_specs=tiled, scratch_shapes=weight_scratch(EXPERT_FF, D)),
        out_shape=jax.ShapeDtypeStruct((rows * ROW_PARTS, V7X_LANES), U32),
        compiler_params=_params("arbitrary"), name="moe_down",
    )(*sched, h, w_down, b_down[:, None, :])
    return y, dest


COMBINE_TILE = 256
COMBINE_SUB = 64


def _combine_ln_kernel(alpha, dest_ref, next_ref, y_hbm, x_ref, wgt_ref, gate_ref, g_ref, b_ref, sc_ref, sh_ref,
                       xo_ref, u_ref, buf0_ref, buf1_ref, sems):
    i = pl.program_id(0)
    bufs = (buf0_ref, buf1_ref)

    def row(ref, r):
        return ref.at[pl.ds(pl.multiple_of(r * ROW_PARTS, ROW_PARTS), ROW_PARTS)]

    def gather(idx_ref, slot):
        def issue(tok, carry):
            for k in range(TOP_K):
                pltpu.make_async_copy(row(y_hbm, idx_ref[0, 0, tok * TOP_K + k]),
                                      row(bufs[slot], k * COMBINE_TILE + tok), sems.at[slot]).start(priority=k % 2)
            return carry

        lax.fori_loop(0, COMBINE_TILE, issue, 0, unroll=2)

    def drain(slot):
        buf = bufs[slot]
        pltpu.make_async_copy(y_hbm.at[pl.ds(0, buf.shape[0])], buf, sems.at[slot]).wait()

    def finish(slot):
        buf = bufs[slot]
        drain(slot)
        for lo in range(0, COMBINE_TILE, COMBINE_SUB):
            for tok in range(lo, lo + COMBINE_SUB):
                for k in range(TOP_K):
                    pltpu.make_async_copy(row(y_hbm, next_ref[0, 0, tok * TOP_K + k]),
                                          row(bufs[1 - slot], k * COMBINE_TILE + tok),
                                          sems.at[1 - slot]).start(priority=k % 2)
            wgt = wgt_ref[pl.ds(lo, COMBINE_SUB), :]
            f = sum(_load_row_tiled(buf, k * COMBINE_TILE + lo, COMBINE_SUB) * wgt[:, k:k + 1] for k in range(TOP_K))
            xn = _deepnorm(alpha, x_ref[pl.ds(lo, COMBINE_SUB), :], f, gate_ref[0], g_ref[...], b_ref[...])
            xo_ref[pl.ds(lo, COMBINE_SUB), :] = xn
            u_ref[pl.ds(lo, COMBINE_SUB), :] = (xn * (1.0 + sc_ref[0]) + sh_ref[0]).astype(u_ref.dtype)

        pl.when(i + 1 == pl.num_programs(0))(functools.partial(drain, 1 - slot))

    pl.when(i == 0)(functools.partial(gather, dest_ref, 0))
    for slot in range(2):
        pl.when(i % 2 == slot)(functools.partial(finish, slot))


def _combine_ln(x, y, dest, top_w, gate, ln_g, ln_b, sc, sh, S, alpha):
    T, D = x.shape
    tr = COMBINE_TILE
    vec, row, par = _row_specs(D, S, tr)
    n_steps = T // tr
    dest = dest.reshape(n_steps, 1, tr * TOP_K)
    gathered = pltpu.VMEM((tr * TOP_K * ROW_PARTS, V7X_LANES), U32)
    return pl.pallas_call(
        functools.partial(_combine_ln_kernel, alpha), grid=(n_steps,),
        in_specs=[pl.BlockSpec((1, 1, tr * TOP_K), lambda i: (i, 0, 0), memory_space=pltpu.SMEM),
                  pl.BlockSpec((1, 1, tr * TOP_K), lambda i: (jnp.minimum(i + 1, n_steps - 1), 0, 0),
                               memory_space=pltpu.SMEM),
                  pl.BlockSpec(memory_space=pl.ANY), row,
                  pl.BlockSpec((tr, TOP_K), lambda i: (i, 0)), vec, par, par, vec, vec],
        out_specs=[row, row],
        out_shape=[jax.ShapeDtypeStruct((T, D), F32), jax.ShapeDtypeStruct((T, D), BF16)],
        scratch_shapes=[gathered, gathered, pltpu.SemaphoreType.DMA((2,))],
        compiler_params=_params("arbitrary"), name="moe_combine_ln",
    )(dest, dest, y, x, top_w, gate[:, None, :], ln_g[None, :], ln_b[None, :], sc[:, None, :], sh[:, None, :])


ATTN_TILE = 256


LOG2E = math.log2(math.e)


def _softmax_pv(s2, v):
    p = jnp.exp2(s2 - jnp.max(s2, -1, keepdims=True))
    return _dot(p.astype(BF16), v) / jnp.sum(p, -1, keepdims=True)


CAUSAL_VARIANTS = 4


def _causal_widths(n_tiles):
    step = max(n_tiles // CAUSAL_VARIANTS, 1)
    bounds = list(range(step, n_tiles, step)) + [n_tiles]
    return [(n, n) for n in bounds]


def _mla_attn_kernel(scale, qn_ref, qr_ref, cos_ref, sin_ref, kv_ref, kr_ref, o_ref):
    qi = pl.program_id(2)
    t = ATTN_TILE
    n_tiles = kv_ref.shape[1] // t
    x = qr_ref[0].astype(F32)
    half = MLA_ROPE // 2
    first_half = _iota(x.shape, 1) % MLA_ROPE < half
    swapped = jnp.where(first_half, pltpu.roll(x, x.shape[1] - half, 1), pltpu.roll(x, half, 1))
    qr_all = (x * cos_ref[...] + swapped * sin_ref[...]).astype(BF16)

    def attend(n_kv):
        w = n_kv * t
        visible = _iota((t, w), 1) <= _iota((t, w), 0) + qi * t
        outs = []
        for hh in range(2):
            qn = qn_ref[0, :, hh * MLA_NOPE:(hh + 1) * MLA_NOPE]
            qr = qr_all[:, hh * MLA_ROPE:(hh + 1) * MLA_ROPE]
            c0 = hh * (MLA_NOPE + MLA_V)
            s = (_dot_nt(qn, kv_ref[0, 0:w, c0:c0 + MLA_NOPE]) + _dot_nt(qr, kr_ref[0, 0:w, :])) * scale
            s = jnp.where(visible, s, NEG_INF)
            outs.append(_softmax_pv(s, kv_ref[0, 0:w, c0 + MLA_NOPE:c0 + MLA_NOPE + MLA_V]))
        o_ref[0] = jnp.concatenate(outs, -1).astype(o_ref.dtype)

    lo = 0
    for hi, n_kv in _causal_widths(n_tiles):
        pl.when(jnp.logical_and(qi >= lo, qi < hi))(functools.partial(attend, n_kv))
        lo = hi


def _rope(x, cos, sin):
    half = x.shape[-1] // 2
    x1, x2 = x[..., :half], x[..., half:]
    return jnp.concatenate([x1 * cos - x2 * sin, x2 * cos + x1 * sin], -1)


def _mla_mixer(u, B, S, w_in, q_norm, kv_norm, w_qb, w_kvb, w_o):
    H = MLA_HEADS
    n_lat = MLA_Q_LORA + MLA_KV_LORA
    lat = _mm(u, w_in, F32, n_cols=n_lat)
    k_rope = _mm(u, w_in, F32, col0=n_lat, n_cols=V7X_LANES)[:, :MLA_ROPE]
    wq = w_qb.reshape(MLA_Q_LORA, H, MLA_NOPE + MLA_ROPE)
    q_nope = _mm(lat, wq[:, :, :MLA_NOPE].reshape(MLA_Q_LORA, H * MLA_NOPE), BF16, rms_gain=q_norm, x_col=0)
    q_rope = _mm(lat, wq[:, :, MLA_NOPE:].reshape(MLA_Q_LORA, H * MLA_ROPE), F32, rms_gain=q_norm, x_col=0)
    kv = _mm(lat, w_kvb, BF16, rms_gain=kv_norm, x_col=1)
    inv_freq = ROPE_THETA ** (-jnp.arange(MLA_ROPE // 2, dtype=F32) / (MLA_ROPE // 2))
    ang = jnp.arange(S, dtype=F32)[:, None] * inv_freq[None, :]
    cos, sin = jnp.cos(ang), jnp.sin(ang)
    k_rope = _rope(k_rope.reshape(B, S, MLA_ROPE), cos, sin).astype(BF16)
    cos2 = jnp.tile(jnp.concatenate([cos, cos], -1), (1, 2))
    sin2 = jnp.tile(jnp.concatenate([-sin, sin], -1), (1, 2))
    t = ATTN_TILE
    o = pl.pallas_call(
        functools.partial(_mla_attn_kernel, (MLA_NOPE + MLA_ROPE) ** -0.5 * LOG2E),
        grid=(B, H // 2, S // t),
        in_specs=[pl.BlockSpec((1, t, 2 * MLA_NOPE), lambda b, h, i: (b, i, h)),
                  pl.BlockSpec((1, t, 2 * MLA_ROPE), lambda b, h, i: (b, i, h)),
                  pl.BlockSpec((t, 2 * MLA_ROPE), lambda b, h, i: (i, 0)),
                  pl.BlockSpec((t, 2 * MLA_ROPE), lambda b, h, i: (i, 0)),
                  pl.BlockSpec((1, S, 2 * (MLA_NOPE + MLA_V)), lambda b, h, i: (b, 0, h)),
                  pl.BlockSpec((1, S, MLA_ROPE), lambda b, h, i: (b, 0, 0))],
        out_specs=pl.BlockSpec((1, t, 2 * MLA_V), lambda b, h, i: (b, i, h)),
        out_shape=jax.ShapeDtypeStruct((B, S, H * MLA_V), BF16),
        compiler_params=_params("parallel", "parallel", "arbitrary"), name="mla_attention",
    )(q_nope.reshape(B, S, -1), q_rope.reshape(B, S, -1), cos2, sin2, kv.reshape(B, S, -1), k_rope)
    return _mm(o.reshape(B * S, H * MLA_V), w_o, F32)


def _t5_bucket(dist):
    n = jnp.maximum(dist, 0)
    max_exact = REL_BUCKETS // 2
    large = max_exact + (jnp.log(jnp.maximum(n, 1).astype(F32) / max_exact)
                         / math.log(REL_MAX_DIST / max_exact) * (REL_BUCKETS - max_exact)).astype(jnp.int32)
    large = jnp.minimum(large, REL_BUCKETS - 1)
    return jnp.where(n < max_exact, n, large)


MOBA_FAR = 2


def _moba_kernel(scale, n_sel, tab_ref, q_ref, k_ref, v_ref, bkt_ref, o_ref, kbf_ref, vbf_ref, kmean_ref, bias_ref):
    h, b, qi = pl.program_id(0), pl.program_id(1), pl.program_id(2)
    L = MOBA_BLOCK
    S = k_ref.shape[1]
    n_blk = S // L

    @pl.when(jnp.logical_and(b == 0, qi == 0))
    def _():
        for d in range(MOBA_FAR + 1):
            bucket = bkt_ref[d]
            tile = jnp.zeros((L, L), F32)
            for e in range(REL_BUCKETS):
                tile = jnp.where(bucket == e, tab_ref[e, h] * LOG2E, tile)
            bias_ref[d] = tile

    @pl.when(qi == 0)
    def _():
        k = k_ref[0].astype(F32)
        kbf_ref[...] = k.astype(BF16)
        vbf_ref[...] = v_ref[0].astype(BF16)
        kmean_ref[...] = jnp.zeros(kmean_ref.shape, F32)
        kmean_ref[0:n_blk, :] = jnp.mean(k.reshape(n_blk, L, k.shape[-1]), axis=1)

    q = q_ref[0].astype(F32)
    qb = q.astype(BF16)
    nb = kmean_ref.shape[0]
    blk = _iota((nb, L), 0)
    gate = jnp.where(blk < qi, _dot_nt(kmean_ref[...], q, precision=HIGHEST), NEG_INF)
    picked = jnp.zeros((nb, L), F32)
    for j in range(n_blk - 1):
        gj = gate[j:j + 1, :]
        beats = jnp.logical_or(gate > gj, jnp.logical_and(gate == gj, blk < j))
        in_topk = jnp.sum(jnp.where(beats, 1.0, 0.0), 0, keepdims=True) < n_sel
        picked = jnp.where(blk == j, jnp.where(jnp.logical_and(in_topk, j < qi), 1.0, 0.0), picked)
    picked = _dot_tn(picked, jnp.where(_iota((nb, nb), 0) == _iota((nb, nb), 1), 1.0, 0.0))
    causal_add = jnp.where(_iota((L, L), 1) <= _iota((L, L), 0), 0.0, NEG_INF)
    past_add = [jnp.where(picked[:, j:j + 1] > 0.5, 0.0, NEG_INF) for j in range(n_blk)]

    def attend(n_kv):
        w = n_kv * L
        s = _dot_nt(qb, kbf_ref[0:w, :]) * scale
        parts = []
        for j in range(n_kv):
            bias = bias_ref[jnp.clip(qi - j, 0, MOBA_FAR)]
            parts.append(s[:, j * L:(j + 1) * L] + bias + jnp.where(j == qi, causal_add, past_add[j]))
        o_ref[0] = _softmax_pv(jnp.concatenate(parts, -1), vbf_ref[0:w, :]).astype(o_ref.dtype)

    lo = 0
    for hi, n_kv in _causal_widths(n_blk):
        pl.when(jnp.logical_and(qi >= lo, qi < hi))(functools.partial(attend, n_kv))
        lo = hi


def _moba_mixer(u, B, S, w_in, w_o, rel_bias):
    H, Dh, L = MOBA_HEADS, u.shape[1] // MOBA_HEADS, MOBA_BLOCK
    n_blk = S // L
    assert S % L == 0 and Dh == V7X_LANES and (MOBA_FAR - 1) * L >= REL_MAX_DIST
    n_sel = max(min(MOBA_TOPK, n_blk - 1), 1)
    qkv = _mm(u, w_in, BF16).reshape(B, S, 3 * H * Dh)
    qk = jnp.arange(L)[:, None] - jnp.arange(L)[None, :]
    bucket = _t5_bucket(jnp.stack([qk + d * L for d in range(MOBA_FAR + 1)])).astype(jnp.int32)
    o = pl.pallas_call(
        functools.partial(_moba_kernel, Dh ** -0.5 * LOG2E, n_sel),
        grid=(H, B, n_blk),
        in_specs=[pl.BlockSpec(memory_space=pltpu.SMEM),
                  pl.BlockSpec((1, L, Dh), lambda h, b, i: (b, i, h)),
                  pl.BlockSpec((1, S, Dh), lambda h, b, i: (b, 0, H + h)),
                  pl.BlockSpec((1, S, Dh), lambda h, b, i: (b, 0, 2 * H + h)),
                  pl.BlockSpec((MOBA_FAR + 1, L, L), lambda h, b, i: (0, 0, 0))],
        out_specs=pl.BlockSpec((1, L, Dh), lambda h, b, i: (b, i, h)),
        out_shape=jax.ShapeDtypeStruct((B, S, H * Dh), BF16),
        scratch_shapes=[pltpu.VMEM((S, Dh), BF16), pltpu.VMEM((S, Dh), BF16),
                        pltpu.VMEM((-(-n_blk // 8) * 8, Dh), F32),
                        pltpu.VMEM((MOBA_FAR + 1, L, L), F32)],
        compiler_params=_params("parallel", "arbitrary", "arbitrary"), name="moba_attention",
    )(rel_bias, qkv, qkv, qkv, bucket)
    return _mm(o.reshape(B * S, H * Dh), w_o, F32)


GDN_HEAD_GROUP = 16


def _lane_sum(x):
    return _dot(x.astype(BF16), jnp.ones((x.shape[1], x.shape[1]), BF16))


def _l2norm(x):
    return x * lax.rsqrt(_lane_sum(x * x) + L2_EPS)


def _dotb(a, b):
    return _dot(a.astype(BF16), b.astype(BF16))


def _unit_lower_inverse(a_lows, block):
    C = a_lows[0].shape[0]
    r = _iota((C, C), 0)
    c = _iota((C, C), 1)
    eye = jnp.where(r == c, 1.0, 0.0)
    same = (r // block) == (c // block)
    a_d = [jnp.where(same, a, 0.0) for a in a_lows]
    a_off = [a - d for a, d in zip(a_lows, a_d)]
    inv_d = [eye - d for d in a_d]
    pw = a_d
    k = 2
    while k < block:
        pw = [_dotb(p, p) for p in pw]
        inv_d = [_dotb(i, eye + p) for i, p in zip(inv_d, pw)]
        k *= 2
    n = [_dotb(i, o) for i, o in zip(inv_d, a_off)]
    inv_n = [eye - x for x in n]
    pw = n
    k = 2
    while k < C // block:
        pw = [_dotb(p, p) for p in pw]
        inv_n = [_dotb(i, eye + p) for i, p in zip(inv_n, pw)]
        k *= 2
    return [_dotb(i, d) for i, d in zip(inv_n, inv_d)]


GDN_HIST = 16


def _causal_conv_silu(x_ref, w_ref, hist_ref):
    C = x_ref.shape[1]
    x = x_ref[0]
    hist_ref[GDN_HIST:GDN_HIST + C, :] = x
    taps = GDN_CONV - 1
    o = _iota((taps * C, GDN_HIST + C), 0)
    r = _iota((taps * C, GDN_HIST + C), 1)
    shift = jnp.where(r == o % C + o // C + (GDN_HIST - taps), 1.0, 0.0).astype(BF16)
    shifted = _dot(shift, hist_ref[...])
    y = x.astype(F32) * w_ref[taps:taps + 1, :]
    for i in range(taps):
        y = y + shifted[i * C:(i + 1) * C, :] * w_ref[i:i + 1, :]
    hist_ref[0:GDN_HIST, :] = hist_ref[C:C + GDN_HIST, :]
    return _silu(y)


def _gdn_kernel(q_ref, k_ref, v_ref, z_ref, cwq_ref, cwk_ref, cwv_ref, gc_ref, gct_ref, beta_ref, ng_ref,
                o_ref, state_ref, hq_ref, hk_ref, hv_ref):
    G, C, DK, DV = GDN_HEAD_GROUP, GDN_CHUNK, GDN_DK, GDN_DV
    rep = GDN_V_HEADS // GDN_K_HEADS
    heads = range(G)

    @pl.when(pl.program_id(2) == 0)
    def _():
        state_ref[...] = jnp.zeros(state_ref.shape, F32)
        for hist in (hq_ref, hk_ref, hv_ref):
            hist[0:GDN_HIST, :] = jnp.zeros((GDN_HIST, hist.shape[1]), hist.dtype)

    qc = _causal_conv_silu(q_ref, cwq_ref, hq_ref)
    kc = _causal_conv_silu(k_ref, cwk_ref, hk_ref)
    vc = _causal_conv_silu(v_ref, cwv_ref, hv_ref)
    r = _iota((C, C), 0)
    c = _iota((C, C), 1)
    tri = c <= r
    strict = c < r
    gc = gc_ref[0, 0]
    gct = gct_ref[0, 0, 0]
    beta = beta_ref[0, 0]
    q = [_l2norm(qc[:, i * DK:(i + 1) * DK]) * (DK ** -0.5) for i in range(G // rep)]
    k = [_l2norm(kc[:, i * DK:(i + 1) * DK]) for i in range(G // rep)]
    assert DK == V7X_LANES and DV == V7X_LANES
    kb = [x.astype(BF16) for x in k]
    qk = [_dot_nt(a.astype(BF16), b) for a, b in zip(q, kb)]
    kk = [_dot_nt(b, b) for b in kb]
    g_col = [gc[:, h:h + 1] for h in heads]
    b_col = [beta[:, h:h + 1] for h in heads]
    decay = [jnp.where(tri, jnp.exp(jnp.where(tri, g_col[h] - gct[h:h + 1, :], 0.0)), 0.0) for h in heads]
    t_inv = _unit_lower_inverse([jnp.where(strict, kk[h // rep] * b_col[h] * decay[h], 0.0) for h in heads], 16)
    e_g = [jnp.exp(g) for g in g_col]
    rhs = [jnp.concatenate([vc[:, h * DV:(h + 1) * DV] * b_col[h], k[h // rep] * (b_col[h] * e_g[h])], -1)
           for h in heads]
    sol = [_dot(t_inv[h].astype(BF16), rhs[h].astype(BF16)) for h in heads]
    state = [state_ref[h] for h in heads]
    state_b = [s.astype(BF16) for s in state]
    v_new = [sol[h][:, :DV] - _dot(sol[h][:, DV:].astype(BF16), state_b[h]) for h in heads]
    v_new_b = [x.astype(BF16) for x in v_new]
    attn = [jnp.where(tri, qk[h // rep] * decay[h], 0.0).astype(BF16) for h in heads]
    o = [_dot((q[h // rep] * e_g[h]).astype(BF16), state_b[h]) + _dot(attn[h], v_new_b[h]) for h in heads]
    g_last = [g[C - 1:C, :] for g in g_col]
    k_tail = [(k[h // rep] * jnp.exp(g_last[h] - g_col[h])).astype(BF16) for h in heads]
    new_state = [state[h] * jnp.exp(g_last[h]) + _dot_tn(k_tail[h], v_new_b[h]) for h in heads]
    o = [x * lax.rsqrt(_lane_sum(x * x) * (1.0 / DV) + RMS_EPS) * ng_ref[...] for x in o]
    o = [o[h] * _silu(z_ref[0, :, h * DV:(h + 1) * DV].astype(F32)) for h in heads]
    for h in heads:
        state_ref[h] = new_state[h]
    o_ref[0] = jnp.concatenate(o, -1).astype(o_ref.dtype)


def _gdn_mixer(u, B, S, w_in, conv_w, a_log, dt_bias, norm_g, w_o):
    HK, HV, DK, DV, C, G = GDN_K_HEADS, GDN_V_HEADS, GDN_DK, GDN_DV, GDN_CHUNK, GDN_HEAD_GROUP
    qk_dim, v_dim = HK * DK, HV * DV
    n_main = 2 * qk_dim + 2 * v_dim
    w_t = w_in.T
    proj = _mm(u, w_t, BF16, n_cols=n_main, w_transposed=True).reshape(B, S, n_main)
    ba = _mm(u, w_t, F32, col0=n_main, n_cols=V7X_LANES, w_transposed=True).reshape(B, S, -1)
    n_conv = 2 * qk_dim + v_dim
    beta = jax.nn.sigmoid(ba[:, :, :HV])
    g = -jnp.exp(a_log) * jax.nn.softplus(ba[:, :, HV:2 * HV] + dt_bias)
    N = S // C
    gc = jnp.cumsum(g.reshape(B, N, C, HV), axis=2)
    HG = HV // G
    gc_g = gc.reshape(B, N, C, HG, G).transpose(0, 3, 1, 2, 4).reshape(B, HG, S, G)
    gct_g = gc.reshape(B, N, C, HG, G).transpose(0, 3, 1, 4, 2)
    beta_g = beta.reshape(B, S, HG, G).transpose(0, 2, 1, 3)
    kw = (G // (HV // HK)) * DK
    vw = G * DV
    k_blk, v_blk, z_blk = qk_dim // kw, 2 * qk_dim // vw, n_conv // vw
    o = pl.pallas_call(
        _gdn_kernel,
        grid=(B, HG, N),
        in_specs=[pl.BlockSpec((1, C, kw), lambda b, h, n: (b, n, h)),
                  pl.BlockSpec((1, C, kw), lambda b, h, n: (b, n, k_blk + h)),
                  pl.BlockSpec((1, C, vw), lambda b, h, n: (b, n, v_blk + h)),
                  pl.BlockSpec((1, C, vw), lambda b, h, n: (b, n, z_blk + h)),
                  pl.BlockSpec((GDN_CONV, kw), lambda b, h, n: (0, h)),
                  pl.BlockSpec((GDN_CONV, kw), lambda b, h, n: (0, k_blk + h)),
                  pl.BlockSpec((GDN_CONV, vw), lambda b, h, n: (0, v_blk + h)),
                  pl.BlockSpec((1, 1, C, G), lambda b, h, n: (b, h, n, 0)),
                  pl.BlockSpec((1, 1, 1, G, C), lambda b, h, n: (b, h, n, 0, 0)),
                  pl.BlockSpec((1, 1, C, G), lambda b, h, n: (b, h, n, 0)),
                  pl.BlockSpec((1, DV), lambda b, h, n: (0, 0))],
        out_specs=pl.BlockSpec((1, C, vw), lambda b, h, n: (b, n, h)),
        out_shape=jax.ShapeDtypeStruct((B, S, v_dim), BF16),
        scratch_shapes=[pltpu.VMEM((G, DK, DV), F32), pltpu.VMEM((GDN_HIST + C, kw), BF16),
                        pltpu.VMEM((GDN_HIST + C, kw), BF16), pltpu.VMEM((GDN_HIST + C, vw), BF16)],
        compiler_params=_params("parallel", "parallel", "arbitrary"), name="gdn_chunked",
    )(proj, proj, proj, proj, conv_w, conv_w, conv_w, gc_g, gct_g, beta_g, norm_g[None, :])
    return _mm(o.reshape(B * S, v_dim), w_o, F32)


def _gla_kernel(scale, q_ref, k_ref, v_ref, og_ref, gk_ref, wgk_ref, bgk_ref, ng_ref, o_ref, state_ref):
    C = GLA_CHUNK
    H, dv, dk = state_ref.shape
    heads = range(H)

    @pl.when(pl.program_id(1) == 0)
    def _():
        state_ref[...] = jnp.zeros(state_ref.shape, F32)

    x = _dot(gk_ref[0].astype(BF16), wgk_ref[...].astype(BF16)) + bgk_ref[...]
    log_alpha = (jnp.minimum(x, 0.0) - jnp.log(1.0 + jnp.exp(-jnp.abs(x)))) / GLA_GATE_NORMALIZER
    r = _iota((C, C), 0)
    c = _iota((C, C), 1)
    causal = c <= r
    b = _dot(jnp.where(causal, 1.0, 0.0), log_alpha, precision=HIGHEST)
    b_last = b[C - 1:C, :]
    q = q_ref[0].astype(F32) * scale
    k = k_ref[0].astype(F32)
    q_dec = (q * jnp.exp(b)).astype(BF16)
    k_inv = (k * jnp.exp(-b)).astype(BF16)
    k_tail = (k * jnp.exp(b_last - b)).astype(BF16)
    decay = jnp.exp(b_last)
    ks = [slice(h * dk, (h + 1) * dk) for h in heads]
    v = [v_ref[0, :, h * dv:(h + 1) * dv] for h in heads]
    attn = [jnp.where(causal, _dot_nt(q_dec[:, ks[h]], k_inv[:, ks[h]]), 0.0).astype(BF16) for h in heads]
    state_t = [state_ref[h] for h in heads]
    o = [_dot(attn[h], v[h]) + _dot_nt(q_dec[:, ks[h]], state_t[h].astype(BF16)) for h in heads]
    new_state = [state_t[h] * decay[:, ks[h]] + _dot_tn(v[h], k_tail[:, ks[h]]) for h in heads]
    o = [x * lax.rsqrt(jnp.mean(x * x, -1, keepdims=True) + RMS_EPS) * ng_ref[...] for x in o]
    for h in heads:
        state_ref[h] = new_state[h]
    o_ref[0] = (jnp.concatenate(o, -1) * _silu(og_ref[0].astype(F32))).astype(o_ref.dtype)


def _gla_mixer(u, B, S, w_in, w_gk, b_gk, norm_g, w_o):
    D = u.shape[1]
    H, C = GLA_HEADS, GLA_CHUNK
    key_dim, val_dim = D // 2, D
    dk, dv = key_dim // H, val_dim // H
    n_main = 2 * key_dim + 2 * val_dim
    w_t = w_in.T
    proj = _mm(u, w_t, BF16, n_cols=n_main, w_transposed=True).reshape(B, S, n_main)
    gk = _mm(u, w_t, F32, col0=n_main, n_cols=V7X_LANES, w_transposed=True).reshape(B, S, -1)
    wgk = jnp.pad(w_gk, ((0, gk.shape[-1] - GLA_GATE_RANK), (0, 0)))
    o = pl.pallas_call(
        functools.partial(_gla_kernel, dk ** -0.5),
        grid=(B, S // C),
        in_specs=[pl.BlockSpec((1, C, key_dim), lambda b, n: (b, n, 0)),
                  pl.BlockSpec((1, C, key_dim), lambda b, n: (b, n, 1)),
                  pl.BlockSpec((1, C, val_dim), lambda b, n: (b, n, 2 * key_dim // val_dim)),
                  pl.BlockSpec((1, C, val_dim), lambda b, n: (b, n, 2 * key_dim // val_dim + 1)),
                  pl.BlockSpec((1, C, gk.shape[-1]), lambda b, n: (b, n, 0)),
                  pl.BlockSpec((gk.shape[-1], key_dim), lambda b, n: (0, 0)),
                  pl.BlockSpec((1, key_dim), lambda b, n: (0, 0)),
                  pl.BlockSpec((1, dv), lambda b, n: (0, 0))],
        out_specs=pl.BlockSpec((1, C, val_dim), lambda b, n: (b, n, 0)),
        out_shape=jax.ShapeDtypeStruct((B, S, val_dim), BF16),
        scratch_shapes=[pltpu.VMEM((H, dv, dk), F32)],
        compiler_params=_params("parallel", "arbitrary"), name="gla_chunked",
    )(proj, proj, proj, proj, gk, wgk, b_gk[None, :], norm_g[None, :])
    return _mm(o.reshape(B * S, val_dim), w_o, F32)


def kernel(x, c, rel_bias, mla_w_in, mla_q_norm, mla_kv_norm, mla_w_qb, mla_w_kvb, mla_w_o, gdn_w_in, gdn_conv_w, gdn_a_log, gdn_dt_bias, gdn_norm, gdn_w_o, gla_w_in, gla_w_gk, gla_b_gk, gla_norm, gla_w_o, moba_w_in, moba_w_o, ada_w, ada_b, ln_g, ln_b, router_w, router_b, moe_w_gu, moe_b_gu, moe_w_down, moe_b_down):
    B, S, D = x.shape
    assert D == D_MODEL
    depth = ada_w.shape[0]
    alpha = (2 * depth) ** 0.25
    mod = _ada_mod(c, ada_w, ada_b)
    sh_a, sc_a, g_a, sh_f, sc_f, g_f = (mod[:, :, k * D:(k + 1) * D] for k in range(6))
    xt = x.reshape(B * S, D)
    u = _modulate(xt, sc_a[0], sh_a[0], S)
    for i in range(depth):
        m, j = i % N_MIXERS, i // N_MIXERS
        if m == 0:
            h = _mla_mixer(u, B, S, mla_w_in[j], mla_q_norm[j], mla_kv_norm[j], mla_w_qb[j], mla_w_kvb[j], mla_w_o[j])
        elif m == 1:
            h = _gdn_mixer(u, B, S, gdn_w_in[j], gdn_conv_w[j], gdn_a_log[j], gdn_dt_bias[j], gdn_norm[j], gdn_w_o[j])
        elif m == 2:
            h = _gla_mixer(u, B, S, gla_w_in[j], gla_w_gk[j], gla_b_gk[j], gla_norm[j], gla_w_o[j])
        else:
            h = _moba_mixer(u, B, S, moba_w_in[j], moba_w_o[j], rel_bias)
        xt, u, top_idx, top_w = _ln_router(xt, h, g_a[i], ln_g[i, 0], ln_b[i, 0], sc_f[i], sh_f[i],
                                           router_w[i], router_b[i], S, alpha)
        y, dest = _moe_ffn(u, top_idx, i, moe_w_gu, moe_b_gu, moe_w_down, moe_b_down)
        nxt = (i + 1) % depth
        xt, u = _combine_ln(xt, y, dest, top_w, g_f[i], ln_g[i, 1], ln_b[i, 1], sc_a[nxt], sh_a[nxt], S, alpha)
    return xt.reshape(B, S, D)
```

```python
import functools
import math

import jax
import jax.numpy as jnp
from jax import lax
from jax.experimental import pallas as pl
from jax.experimental.pallas import tpu as pltpu

D_MODEL = 2048
N_MIXERS = 4
MLA_HEADS, MLA_Q_LORA, MLA_KV_LORA, MLA_NOPE, MLA_ROPE, MLA_V = 16, 512, 512, 128, 64, 128
ROPE_THETA = 10000.0
GDN_K_HEADS, GDN_V_HEADS, GDN_DK, GDN_DV, GDN_CONV, GDN_CHUNK = 16, 32, 128, 128, 4, 64
GLA_HEADS, GLA_GATE_RANK, GLA_GATE_NORMALIZER, GLA_CHUNK = 4, 16, 16.0, 64
MOBA_HEADS, MOBA_BLOCK, MOBA_TOPK = 16, 256, 3
REL_BUCKETS, REL_MAX_DIST = 32, 128
N_EXPERTS, TOP_K, EXPERT_FF = 32, 4, 768
SWIGLU_LIMIT, SWIGLU_ALPHA = 7.0, 1.702
LN_EPS, RMS_EPS, L2_EPS = 1e-5, 1e-6, 1e-6

V7X_LANES = 128
V7X_VMEM_LIMIT_BYTES = 56 * 1024 * 1024

F32 = jnp.float32
BF16 = jnp.bfloat16
HIGHEST = lax.Precision.HIGHEST
NEG_INF = float("-inf")


def _params(*sem):
    return pltpu.CompilerParams(dimension_semantics=sem, vmem_limit_bytes=V7X_VMEM_LIMIT_BYTES)


def _dot(a, b, dims=None, precision=None):
    if dims is None:
        dims = (((a.ndim - 1,), (0,)), ((), ()))
    return lax.dot_general(a, b, dims, precision=precision, preferred_element_type=F32)


def _dot_nt(a, b, precision=None):
    return _dot(a, b, (((1,), (1,)), ((), ())), precision)


def _dot_tn(a, b, precision=None):
    return _dot(a, b, (((0,), (0,)), ((), ())), precision)


def _sigmoid(x):
    return 1.0 / (1.0 + jnp.exp(-x))


def _silu(x):
    return x * _sigmoid(x)


def _iota(shape, dim):
    return lax.broadcasted_iota(jnp.int32, shape, dim)


def _mm_kernel(valid_cols, rms, w_transposed, x_ref, *refs):
    g_ref = refs[0] if rms else None
    w_ref, o_ref, wbf_ref = refs[-3:]

    @pl.when(pl.program_id(1) == 0)
    def _():
        w = w_ref[...]
        if valid_cols is not None:
            w = jnp.where(_iota(w.shape, 0 if w_transposed else 1) < valid_cols, w, 0.0)
        wbf_ref[...] = w.astype(BF16)

    x = x_ref[...]
    if rms:
        x = x.astype(F32)
        x = x * lax.rsqrt(jnp.mean(x * x, -1, keepdims=True) + RMS_EPS) * g_ref[...]
    dot = _dot_nt if w_transposed else _dot
    o_ref[...] = dot(x.astype(BF16), wbf_ref[...]).astype(o_ref.dtype)


def _mm_tiles(M, K, N):
    tm = 1024 if M % 1024 == 0 else M
    tn = N
    for cand in (1024, 512, 256, 128):
        if N % cand == 0 and K * cand * 4 <= 8 * 1024 * 1024:
            tn = cand
            break
    return tm, tn


def _mm(x, w, out_dtype, rms_gain=None, x_col=0, col0=0, n_cols=None, w_transposed=False):
    M = x.shape[0]
    K, n_w = (w.shape[1], w.shape[0]) if w_transposed else w.shape
    N = n_w if n_cols is None else n_cols
    tm, tn = _mm_tiles(M, K, N)
    assert col0 % tn == 0
    valid_cols = n_w - col0 if col0 + N > n_w else None
    assert valid_cols is None or N == tn
    c0 = col0 // tn
    x_spec = pl.BlockSpec((tm, K), lambda n, m: (m, x_col))
    if w_transposed:
        w_spec, w_tile = pl.BlockSpec((tn, K), lambda n, m: (c0 + n, 0)), (tn, K)
    else:
        w_spec, w_tile = pl.BlockSpec((K, tn), lambda n, m: (0, c0 + n)), (K, tn)
    if rms_gain is None:
        in_specs, args = [x_spec, w_spec], (x, w)
    else:
        in_specs, args = [x_spec, pl.BlockSpec((1, K), lambda n, m: (0, 0)), w_spec], (x, rms_gain[None, :], w)
    return pl.pallas_call(
        functools.partial(_mm_kernel, valid_cols, rms_gain is not None, w_transposed),
        grid=(N // tn, M // tm),
        in_specs=in_specs,
        out_specs=pl.BlockSpec((tm, tn), lambda n, m: (m, n)),
        out_shape=jax.ShapeDtypeStruct((M, N), out_dtype),
        scratch_shapes=[pltpu.VMEM(w_tile, BF16)],
        compiler_params=_params("parallel", "arbitrary"),
        name="proj_matmul",
    )(*args)


def _ada_kernel(c_ref, w_ref, b_ref, o_ref):
    c = _silu(c_ref[...]).astype(BF16)
    o_ref[0] = _dot(c, w_ref[0].astype(BF16)) + b_ref[0]


def _ada_mod(c, ada_w, ada_b):
    depth, D, N = ada_w.shape
    B = c.shape[0]
    rows = 8
    c_pad = jnp.pad(c, ((0, rows - B), (0, 0)))
    tn = 1024
    out = pl.pallas_call(
        _ada_kernel,
        grid=(depth, N // tn),
        in_specs=[pl.BlockSpec((rows, D), lambda i, n: (0, 0)),
                  pl.BlockSpec((1, D, tn), lambda i, n: (i, 0, n)),
                  pl.BlockSpec((1, 1, tn), lambda i, n: (i, 0, n))],
        out_specs=pl.BlockSpec((1, rows, tn), lambda i, n: (i, 0, n)),
        out_shape=jax.ShapeDtypeStruct((depth, rows, N), F32),
        compiler_params=_params("parallel", "parallel"),
        name="ada_mod",
    )(c_pad, ada_w, ada_b.reshape(depth, 1, N))
    return out[:, :B]


ROW_TILE = 256


def _modulate_kernel(x_ref, sc_ref, sh_ref, u_ref):
    u_ref[...] = (x_ref[...] * (1.0 + sc_ref[0]) + sh_ref[0]).astype(u_ref.dtype)


def _row_specs(D, S, tr):
    vec = pl.BlockSpec((1, 1, D), lambda i: ((i * tr) // S, 0, 0))
    row = pl.BlockSpec((tr, D), lambda i: (i, 0))
    par = pl.BlockSpec((1, D), lambda i: (0, 0))
    return vec, row, par


def _modulate(x, sc, sh, S):
    T, D = x.shape
    tr = ROW_TILE
    vec, row, _ = _row_specs(D, S, tr)
    return pl.pallas_call(
        _modulate_kernel, grid=(T // tr,), in_specs=[row, vec, vec], out_specs=row,
        out_shape=jax.ShapeDtypeStruct((T, D), BF16),
        compiler_params=_params("parallel"), name="modulate",
    )(x, sc[:, None, :], sh[:, None, :])


def _deepnorm(alpha, x, h, gate, g, b):
    y = alpha * x + (1.0 + gate) * h
    mu = jnp.mean(y, -1, keepdims=True)
    yc = y - mu
    var = jnp.mean(yc * yc, -1, keepdims=True)
    return yc * lax.rsqrt(var + LN_EPS) * g + b


U32 = jnp.uint32
ROW_PARTS = D_MODEL // (2 * V7X_LANES)
assert ROW_PARTS == 8


def _bits(x):
    return lax.bitcast_convert_type(x.astype(BF16).astype(F32), U32)


def _store_row_tiled(ref, row0, value):
    n = value.shape[0]
    for j in range(ROW_PARTS):
        lo = _bits(value[:, (2 * j) * V7X_LANES:(2 * j + 1) * V7X_LANES])
        hi = _bits(value[:, (2 * j + 1) * V7X_LANES:(2 * j + 2) * V7X_LANES])
        ref[pl.ds(row0 * ROW_PARTS + j, n, stride=ROW_PARTS), :] = lax.shift_right_logical(lo, U32(16)) | hi


def _load_row_tiled(ref, row0, n):
    parts = []
    for j in range(ROW_PARTS):
        w = ref[pl.ds(row0 * ROW_PARTS + j, n, stride=ROW_PARTS), :]
        parts.append(lax.bitcast_convert_type(lax.shift_left(w, U32(16)), F32))
        parts.append(lax.bitcast_convert_type(w & U32(0xFFFF0000), F32))
    return jnp.concatenate(parts, -1)


def _ln_router_kernel(alpha, x_ref, h_ref, gate_ref, g_ref, b_ref, sc_ref, sh_ref, rwh_ref, rwl_ref, rb_ref,
                      xo_ref, u_ref, idx_ref, wgt_ref):
    xn = _deepnorm(alpha, x_ref[...], h_ref[...].astype(F32), gate_ref[0], g_ref[...], b_ref[...])
    xo_ref[...] = xn
    u = xn * (1.0 + sc_ref[0]) + sh_ref[0]
    _store_row_tiled(u_ref, 0, u)
    u_hi = u.astype(BF16)
    u_lo = (u - u_hi.astype(F32)).astype(BF16)
    logits = (_dot_nt(rwh_ref[...], u_hi) + (_dot_nt(rwh_ref[...], u_lo) + _dot_nt(rwl_ref[...], u_hi))) + rb_ref[...]
    expert = _iota(logits.shape, 0)
    vals = logits
    top_v, top_i = [], []
    for _ in range(TOP_K):
        m = jnp.max(vals, 0, keepdims=True)
        i = jnp.min(jnp.where(vals == m, expert, N_EXPERTS), 0, keepdims=True)
        top_v.append(m)
        top_i.append(i)
        vals = jnp.where(expert == i, NEG_INF, vals)
    exps = [jnp.exp(v - top_v[0]) for v in top_v]
    denom = functools.reduce(lambda a, b: a + b, exps)
    slot = _iota(idx_ref.shape, 0)
    idx = jnp.zeros(idx_ref.shape, jnp.int32)
    wgt = jnp.zeros(wgt_ref.shape, F32)
    for k in range(TOP_K):
        idx = jnp.where(slot == k, top_i[k], idx)
        wgt = jnp.where(slot == k, exps[k] / denom, wgt)
    idx_ref[...] = idx
    wgt_ref[...] = wgt


def _ln_router(x, h, gate, ln_g, ln_b, sc, sh, router_w, router_b, S, alpha):
    T, D = x.shape
    tr = ROW_TILE
    vec, row, par = _row_specs(D, S, tr)
    slots = 8
    choice = pl.BlockSpec((slots, tr), lambda i: (0, i))
    tiled_row = pl.BlockSpec((tr * ROW_PARTS, V7X_LANES), lambda i: (i, 0))
    rwt = router_w.T
    rw_hi = rwt.astype(BF16)
    rw_lo = (rwt - rw_hi.astype(F32)).astype(BF16)
    rw_spec = pl.BlockSpec((N_EXPERTS, D), lambda i: (0, 0))
    xo, u, idx, wgt = pl.pallas_call(
        functools.partial(_ln_router_kernel, alpha), grid=(T // tr,),
        in_specs=[row, row, vec, par, par, vec, vec, rw_spec, rw_spec,
                  pl.BlockSpec((N_EXPERTS, 1), lambda i: (0, 0))],
        out_specs=[row, tiled_row, choice, choice],
        out_shape=[jax.ShapeDtypeStruct((T, D), F32), jax.ShapeDtypeStruct((T * ROW_PARTS, V7X_LANES), U32),
                   jax.ShapeDtypeStruct((slots, T), jnp.int32), jax.ShapeDtypeStruct((slots, T), F32)],
        compiler_params=_params("parallel"), name="deepnorm_ln_router",
    )(x, h, gate[:, None, :], ln_g[None, :], ln_b[None, :], sc[:, None, :], sh[:, None, :], rw_hi, rw_lo,
      router_b[:, None])
    return xo, u, idx[:TOP_K].T, wgt[:TOP_K].T


MOE_ROW_TILE = 512


def _expert_weights(te_ref, tn_ref, ts_ref, w_hbm, wbuf_ref, wbf_ref, sems):
    t = pl.program_id(0)
    slot = ts_ref[t]

    def fetch(expert, s):
        return pltpu.make_async_copy(w_hbm.at[expert], wbuf_ref.at[s], sems.at[s])

    @pl.when(t == 0)
    def _():
        fetch(te_ref[0], 0).start(priority=1)

    @pl.when(tn_ref[t] >= 0)
    def _():
        fetch(tn_ref[t], 1 - slot).start(priority=1)

    @pl.when(slot >= 0)
    def _():
        fetch(te_ref[t], slot).wait()
        wbf_ref[...] = wbuf_ref[slot].astype(BF16)


def _moe_gu_kernel(te_ref, tv_ref, tn_ref, ts_ref, x_ref, w_hbm, b_ref, h_ref, wbuf_ref, wbf_ref, sems):
    _expert_weights(te_ref, tn_ref, ts_ref, w_hbm, wbuf_ref, wbf_ref, sems)

    @pl.when(tv_ref[pl.program_id(0)] > 0)
    def _():
        x = _load_row_tiled(x_ref, 0, x_ref.shape[0] // ROW_PARTS).astype(BF16)
        gu = _dot(x, wbf_ref[...]) + b_ref[0]
        gl = jnp.minimum(gu[:, :EXPERT_FF], SWIGLU_LIMIT)
        up = jnp.clip(gu[:, EXPERT_FF:], -SWIGLU_LIMIT, SWIGLU_LIMIT)
        h_ref[...] = ((up + 1.0) * gl * _sigmoid(gl * SWIGLU_ALPHA)).astype(h_ref.dtype)

    @pl.when(tv_ref[pl.program_id(0)] == 0)
    def _():
        h_ref[...] = jnp.zeros(h_ref.shape, h_ref.dtype)


def _moe_down_kernel(te_ref, tv_ref, tn_ref, ts_ref, h_ref, w_hbm, b_ref, y_ref, wbuf_ref, wbf_ref, sems):
    _expert_weights(te_ref, tn_ref, ts_ref, w_hbm, wbuf_ref, wbf_ref, sems)

    @pl.when(tv_ref[pl.program_id(0)] > 0)
    def _():
        _store_row_tiled(y_ref, 0, _dot(h_ref[...], wbf_ref[...]) + b_ref[0])

    @pl.when(tv_ref[pl.program_id(0)] == 0)
    def _():
        y_ref[...] = jnp.zeros(y_ref.shape, y_ref.dtype)


def _route_metadata(top_idx, tm):
    T = top_idx.shape[0]
    P = T * TOP_K
    n_tiles = (P + N_EXPERTS * (tm - 1)) // tm
    e_flat = top_idx.reshape(P)
    onehot = (e_flat[:, None] == jnp.arange(N_EXPERTS)[None, :]).astype(jnp.int32)
    csum = jnp.cumsum(onehot, axis=0)
    counts = csum[-1]
    rank = jnp.sum(csum * onehot, axis=1) - 1
    padded = ((counts + tm - 1) // tm) * tm
    ends_p = jnp.cumsum(padded)
    starts_p = ends_p - padded
    dest = (starts_p[e_flat] + rank).astype(jnp.int32)
    tile_start = jnp.arange(n_tiles, dtype=jnp.int32) * tm
    tile_valid = (tile_start < ends_p[-1]).astype(jnp.int32)
    tile_expert = jnp.sum((tile_start[:, None] >= ends_p[None, :]).astype(jnp.int32), axis=1)
    last_expert = jnp.max(jnp.where(counts > 0, jnp.arange(N_EXPERTS), 0))
    tile_expert = jnp.where(tile_valid > 0, tile_expert, last_expert).astype(jnp.int32)
    first = jnp.concatenate([jnp.ones((1,), bool), tile_expert[1:] != tile_expert[:-1]])
    later = tile_expert[None, :] > tile_expert[:, None]
    nxt = jnp.min(jnp.where(later, tile_expert[None, :], N_EXPERTS), axis=1)
    tile_next = jnp.where(jnp.logical_and(first, nxt < N_EXPERTS), nxt, -1).astype(jnp.int32)
    tile_slot = jnp.where(first, (jnp.cumsum(first.astype(jnp.int32)) - 1) % 2, -1).astype(jnp.int32)
    pads = jnp.stack([jnp.concatenate([starts_p + counts, ends_p[-1:]]),
                      jnp.concatenate([padded - counts, n_tiles - ends_p[-1:] // tm])]).astype(jnp.int32)
    return dest, tile_expert, tile_valid, tile_next, tile_slot, pads, n_tiles


DISPATCH_ROWS = 2048


def _zero_padding_rows(tm, n_pad_rows, pads_ref, xs_hbm, zero_ref, zsem):
    zero_ref[...] = jnp.zeros(zero_ref.shape, zero_ref.dtype)

    def put(first_row, n_rows):
        dst = xs_hbm.at[pl.ds(pl.multiple_of(first_row * ROW_PARTS, ROW_PARTS), n_rows * ROW_PARTS)]
        pltpu.make_async_copy(zero_ref.at[pl.ds(0, n_rows * ROW_PARTS)], dst, zsem).start()

    for e in range(N_EXPERTS):
        start, length = pads_ref[0, e], pads_ref[1, e]
        size = tm // 2
        while size >= 1:
            pl.when((length & size) != 0)(functools.partial(put, start + (length & ~(2 * size - 1)), size))
            size //= 2
    max_tail = (n_pad_rows + tm - 1) // tm
    for t in range(max_tail):
        pl.when(t < pads_ref[1, N_EXPERTS])(functools.partial(put, pads_ref[0, N_EXPERTS] + t * tm, tm))
    n = n_pad_rows * ROW_PARTS
    pltpu.make_async_copy(xs_hbm.at[pl.ds(0, n)], xs_hbm.at[pl.ds(0, n)], zsem).wait()


def _dispatch_kernel(tm, n_pad_rows, pads_ref, dest_ref, u_ref, xs_hbm, sem, zero_ref, zsem):
    R = dest_ref.shape[-1]
    pl.when(pl.program_id(0) == 0)(
        functools.partial(_zero_padding_rows, tm, n_pad_rows, pads_ref, xs_hbm, zero_ref, zsem))

    def row(ref, i):
        return ref.at[pl.ds(pl.multiple_of(i * ROW_PARTS, ROW_PARTS), ROW_PARTS)]

    def issue(t, carry):
        for k in range(TOP_K):
            pltpu.make_async_copy(row(u_ref, t), row(xs_hbm, dest_ref[0, 0, t * TOP_K + k]), sem).start(priority=k % 2)
        return carry

    lax.fori_loop(0, R // TOP_K, issue, 0, unroll=2)
    n = (R // TOP_K) * ROW_PARTS
    for _ in range(TOP_K):
        pltpu.make_async_copy(u_ref, xs_hbm.at[pl.ds(0, n)], sem).wait()


def _dispatch(u, dest, pads, rows, tm):
    P = dest.shape[0]
    R = DISPATCH_ROWS
    return pl.pallas_call(
        functools.partial(_dispatch_kernel, tm, rows - P),
        grid=(P // R,),
        in_specs=[pl.BlockSpec(memory_space=pltpu.SMEM),
                  pl.BlockSpec((1, 1, R), lambda s: (s, 0, 0), memory_space=pltpu.SMEM),
                  pl.BlockSpec(((R // TOP_K) * ROW_PARTS, u.shape[1]), lambda s: (s, 0))],
        out_specs=pl.BlockSpec(memory_space=pl.ANY),
        out_shape=jax.ShapeDtypeStruct((rows * ROW_PARTS, u.shape[1]), u.dtype),
        scratch_shapes=[pltpu.SemaphoreType.DMA(()), pltpu.VMEM((tm * ROW_PARTS, u.shape[1]), u.dtype),
                        pltpu.SemaphoreType.DMA(())],
        compiler_params=_params("arbitrary"), name="moe_dispatch",
    )(pads, dest.reshape(P // R, 1, R), u)


def _moe_ffn(u, top_idx, layer, w_gu, b_gu, w_down, b_down):
    D = D_MODEL
    tm = MOE_ROW_TILE
    dest, tile_expert, tile_valid, tile_next, tile_slot, pads, n_tiles = _route_metadata(top_idx, tm)
    tile_expert = tile_expert + layer * N_EXPERTS
    tile_next = jnp.where(tile_next >= 0, tile_next + layer * N_EXPERTS, -1)
    sched = (tile_expert, tile_valid, tile_next, tile_slot)
    w_gu, w_down = (w.reshape((-1,) + w.shape[2:]) for w in (w_gu, w_down))
    b_gu, b_down = (b.reshape(-1, b.shape[-1]) for b in (b_gu, b_down))
    rows = n_tiles * tm
    x_sorted = _dispatch(u, dest, pads, rows, tm)
    ff2 = 2 * EXPERT_FF
    tiled = pl.BlockSpec((tm * ROW_PARTS, V7X_LANES), lambda t, *_: (t, 0))
    hidden = pl.BlockSpec((tm, EXPERT_FF), lambda t, *_: (t, 0))
    weights = pl.BlockSpec(memory_space=pl.ANY)

    def bias(n):
        return pl.BlockSpec((1, 1, n), lambda t, te, *_: (te[t], 0, 0))

    def weight_scratch(k, n):
        return [pltpu.VMEM((2, k, n), F32), pltpu.VMEM((k, n), BF16), pltpu.SemaphoreType.DMA((2,))]

    h = pl.pallas_call(
        _moe_gu_kernel,
        grid_spec=pltpu.PrefetchScalarGridSpec(
            num_scalar_prefetch=4, grid=(n_tiles,),
            in_specs=[tiled, weights, bias(ff2)], out_specs=hidden, scratch_shapes=weight_scratch(D, ff2)),
        out_shape=jax.ShapeDtypeStruct((rows, EXPERT_FF), BF16),
        compiler_params=_params("arbitrary"), name="moe_gate_up",
    )(*sched, x_sorted, w_gu, b_gu[:, None, :])
    y = pl.pallas_call(
        _moe_down_kernel,
        grid_spec=pltpu.PrefetchScalarGridSpec(
            num_scalar_prefetch=4, grid=(n_tiles,),
            in_specs=[hidden, weights, bias(D)], out_specs=tiled, scratch_shapes=weight_scratch(EXPERT_FF, D)),
        out_shape=jax.ShapeDtypeStruct((rows * ROW_PARTS, V7X_LANES), U32),
        compiler_params=_params("arbitrary"), name="moe_down",
    )(*sched, h, w_down, b_down[:, None, :])
    return y, dest


COMBINE_TILE = 256
COMBINE_SUB = 64


def _combine_ln_kernel(alpha, dest_ref, next_ref, y_hbm, x_ref, wgt_ref, gate_ref, g_ref, b_ref, sc_ref, sh_ref,
                       xo_ref, u_ref, buf0_ref, buf1_ref, sems):
    i = pl.program_id(0)
    bufs = (buf0_ref, buf1_ref)

    def row(ref, r):
        return ref.at[pl.ds(pl.multiple_of(r * ROW_PARTS, ROW_PARTS), ROW_PARTS)]

    def gather(idx_ref, slot):
        def issue(tok, carry):
            for k in range(TOP_K):
                pltpu.make_async_copy(row(y_hbm, idx_ref[0, 0, tok * TOP_K + k]),
                                      row(bufs[slot], k * COMBINE_TILE + tok), sems.at[slot]).start(priority=k % 2)
            return carry

        lax.fori_loop(0, COMBINE_TILE, issue, 0, unroll=2)

    def drain(slot):
        buf = bufs[slot]
        pltpu.make_async_copy(y_hbm.at[pl.ds(0, buf.shape[0])], buf, sems.at[slot]).wait()

    def finish(slot):
        buf = bufs[slot]
        drain(slot)
        for lo in range(0, COMBINE_TILE, COMBINE_SUB):
            for tok in range(lo, lo + COMBINE_SUB):
                for k in range(TOP_K):
                    pltpu.make_async_copy(row(y_hbm, next_ref[0, 0, tok * TOP_K + k]),
                                          row(bufs[1 - slot], k * COMBINE_TILE + tok),
                                          sems.at[1 - slot]).start(priority=k % 2)
            wgt = wgt_ref[pl.ds(lo, COMBINE_SUB), :]
            f = sum(_load_row_tiled(buf, k * COMBINE_TILE + lo, COMBINE_SUB) * wgt[:, k:k + 1] for k in range(TOP_K))
            xn = _deepnorm(alpha, x_ref[pl.ds(lo, COMBINE_SUB), :], f, gate_ref[0], g_ref[...], b_ref[...])
            xo_ref[pl.ds(lo, COMBINE_SUB), :] = xn
            u_ref[pl.ds(lo, COMBINE_SUB), :] = (xn * (1.0 + sc_ref[0]) + sh_ref[0]).astype(u_ref.dtype)

        pl.when(i + 1 == pl.num_programs(0))(functools.partial(drain, 1 - slot))

    pl.when(i == 0)(functools.partial(gather, dest_ref, 0))
    for slot in range(2):
        pl.when(i % 2 == slot)(functools.partial(finish, slot))


def _combine_ln(x, y, dest, top_w, gate, ln_g, ln_b, sc, sh, S, alpha):
    T, D = x.shape
    tr = COMBINE_TILE
    vec, row, par = _row_specs(D, S, tr)
    n_steps = T // tr
    dest = dest.reshape(n_steps, 1, tr * TOP_K)
    gathered = pltpu.VMEM((tr * TOP_K * ROW_PARTS, V7X_LANES), U32)
    return pl.pallas_call(
        functools.partial(_combine_ln_kernel, alpha), grid=(n_steps,),
        in_specs=[pl.BlockSpec((1, 1, tr * TOP_K), lambda i: (i, 0, 0), memory_space=pltpu.SMEM),
                  pl.BlockSpec((1, 1, tr * TOP_K), lambda i: (jnp.minimum(i + 1, n_steps - 1), 0, 0),
                               memory_space=pltpu.SMEM),
                  pl.BlockSpec(memory_space=pl.ANY), row,
                  pl.BlockSpec((tr, TOP_K), lambda i: (i, 0)), vec, par, par, vec, vec],
        out_specs=[row, row],
        out_shape=[jax.ShapeDtypeStruct((T, D), F32), jax.ShapeDtypeStruct((T, D), BF16)],
        scratch_shapes=[gathered, gathered, pltpu.SemaphoreType.DMA((2,))],
        compiler_params=_params("arbitrary"), name="moe_combine_ln",
    )(dest, dest, y, x, top_w, gate[:, None, :], ln_g[None, :], ln_b[None, :], sc[:, None, :], sh[:, None, :])


ATTN_TILE = 256


LOG2E = math.log2(math.e)


def _softmax_pv(s2, v):
    p = jnp.exp2(s2 - jnp.max(s2, -1, keepdims=True))
    return _dot(p.astype(BF16), v) / jnp.sum(p, -1, keepdims=True)


CAUSAL_VARIANTS = 4


def _causal_widths(n_tiles):
    step = max(n_tiles // CAUSAL_VARIANTS, 1)
    bounds = list(range(step, n_tiles, step)) + [n_tiles]
    return [(n, n) for n in bounds]


def _mla_attn_kernel(scale, qn_ref, qr_ref, cos_ref, sin_ref, kv_ref, kr_ref, o_ref):
    qi = pl.program_id(2)
    t = ATTN_TILE
    n_tiles = kv_ref.shape[1] // t
    x = qr_ref[0].astype(F32)
    half = MLA_ROPE // 2
    first_half = _iota(x.shape, 1) % MLA_ROPE < half
    swapped = jnp.where(first_half, pltpu.roll(x, x.shape[1] - half, 1), pltpu.roll(x, half, 1))
    qr_all = (x * cos_ref[...] + swapped * sin_ref[...]).astype(BF16)

    def attend(n_kv):
        w = n_kv * t
        visible = _iota((t, w), 1) <= _iota((t, w), 0) + qi * t
        outs = []
        for hh in range(2):
            qn = qn_ref[0, :, hh * MLA_NOPE:(hh + 1) * MLA_NOPE]
            qr = qr_all[:, hh * MLA_ROPE:(hh + 1) * MLA_ROPE]
            c0 = hh * (MLA_NOPE + MLA_V)
            s = (_dot_nt(qn, kv_ref[0, 0:w, c0:c0 + MLA_NOPE]) + _dot_nt(qr, kr_ref[0, 0:w, :])) * scale
            s = jnp.where(visible, s, NEG_INF)
            outs.append(_softmax_pv(s, kv_ref[0, 0:w, c0 + MLA_NOPE:c0 + MLA_NOPE + MLA_V]))
        o_ref[0] = jnp.concatenate(outs, -1).astype(o_ref.dtype)

    lo = 0
    for hi, n_kv in _causal_widths(n_tiles):
        pl.when(jnp.logical_and(qi >= lo, qi < hi))(functools.partial(attend, n_kv))
        lo = hi


def _rope(x, cos, sin):
    half = x.shape[-1] // 2
    x1, x2 = x[..., :half], x[..., half:]
    return jnp.concatenate([x1 * cos - x2 * sin, x2 * cos + x1 * sin], -1)


def _mla_mixer(u, B, S, w_in, q_norm, kv_norm, w_qb, w_kvb, w_o):
    H = MLA_HEADS
    n_lat = MLA_Q_LORA + MLA_KV_LORA
    lat = _mm(u, w_in, F32, n_cols=n_lat)
    k_rope = _mm(u, w_in, F32, col0=n_lat, n_cols=V7X_LANES)[:, :MLA_ROPE]
    wq = w_qb.reshape(MLA_Q_LORA, H, MLA_NOPE + MLA_ROPE)
    q_nope = _mm(lat, wq[:, :, :MLA_NOPE].reshape(MLA_Q_LORA, H * MLA_NOPE), BF16, rms_gain=q_norm, x_col=0)
    q_rope = _mm(lat, wq[:, :, MLA_NOPE:].reshape(MLA_Q_LORA, H * MLA_ROPE), F32, rms_gain=q_norm, x_col=0)
    kv = _mm(lat, w_kvb, BF16, rms_gain=kv_norm, x_col=1)
    inv_freq = ROPE_THETA ** (-jnp.arange(MLA_ROPE // 2, dtype=F32) / (MLA_ROPE // 2))
    ang = jnp.arange(S, dtype=F32)[:, None] * inv_freq[None, :]
    cos, sin = jnp.cos(ang), jnp.sin(ang)
    k_rope = _rope(k_rope.reshape(B, S, MLA_ROPE), cos, sin).astype(BF16)
    cos2 = jnp.tile(jnp.concatenate([cos, cos], -1), (1, 2))
    sin2 = jnp.tile(jnp.concatenate([-sin, sin], -1), (1, 2))
    t = ATTN_TILE
    o = pl.pallas_call(
        functools.partial(_mla_attn_kernel, (MLA_NOPE + MLA_ROPE) ** -0.5 * LOG2E),
        grid=(B, H // 2, S // t),
        in_specs=[pl.BlockSpec((1, t, 2 * MLA_NOPE), lambda b, h, i: (b, i, h)),
                  pl.BlockSpec((1, t, 2 * MLA_ROPE), lambda b, h, i: (b, i, h)),
                  pl.BlockSpec((t, 2 * MLA_ROPE), lambda b, h, i: (i, 0)),
                  pl.BlockSpec((t, 2 * MLA_ROPE), lambda b, h, i: (i, 0)),
                  pl.BlockSpec((1, S, 2 * (MLA_NOPE + MLA_V)), lambda b, h, i: (b, 0, h)),
                  pl.BlockSpec((1, S, MLA_ROPE), lambda b, h, i: (b, 0, 0))],
        out_specs=pl.BlockSpec((1, t, 2 * MLA_V), lambda b, h, i: (b, i, h)),
        out_shape=jax.ShapeDtypeStruct((B, S, H * MLA_V), BF16),
        compiler_params=_params("parallel", "parallel", "arbitrary"), name="mla_attention",
    )(q_nope.reshape(B, S, -1), q_rope.reshape(B, S, -1), cos2, sin2, kv.reshape(B, S, -1), k_rope)
    return _mm(o.reshape(B * S, H * MLA_V), w_o, F32)


def _t5_bucket(dist):
    n = jnp.maximum(dist, 0)
    max_exact = REL_BUCKETS // 2
    large = max_exact + (jnp.log(jnp.maximum(n, 1).astype(F32) / max_exact)
                         / math.log(REL_MAX_DIST / max_exact) * (REL_BUCKETS - max_exact)).astype(jnp.int32)
    large = jnp.minimum(large, REL_BUCKETS - 1)
    return jnp.where(n < max_exact, n, large)


MOBA_FAR = 2


MOBA_HEADS_PER_STEP = 2


def _moba_kernel(scale, n_sel, tab_ref, q_ref, k_ref, v_ref, bkt_ref, o_ref, kbf_ref, vbf_ref, kmean_ref, bias_ref):
    hp, b, qi = pl.program_id(0), pl.program_id(1), pl.program_id(2)
    L, Dh = MOBA_BLOCK, V7X_LANES
    S = k_ref.shape[1]
    n_blk = S // L
    heads = range(MOBA_HEADS_PER_STEP)
    lanes = [slice(hh * Dh, (hh + 1) * Dh) for hh in heads]

    @pl.when(jnp.logical_and(b == 0, qi == 0))
    def _():
        for hh in heads:
            for d in range(MOBA_FAR + 1):
                bucket = bkt_ref[d]
                tile = jnp.zeros((L, L), F32)
                for e in range(REL_BUCKETS):
                    tile = jnp.where(bucket == e, tab_ref[e, hp * MOBA_HEADS_PER_STEP + hh] * LOG2E, tile)
                bias_ref[hh, d] = tile

    @pl.when(qi == 0)
    def _():
        k = k_ref[0].astype(F32)
        kbf_ref[...] = k.astype(BF16)
        vbf_ref[...] = v_ref[0].astype(BF16)
        kmean_ref[...] = jnp.zeros(kmean_ref.shape, F32)
        for hh in heads:
            kmean_ref[hh, 0:n_blk, :] = jnp.mean(k[:, lanes[hh]].reshape(n_blk, L, Dh), axis=1)

    nb = kmean_ref.shape[1]
    blk = _iota((nb, L), 0)
    causal_add = jnp.where(_iota((L, L), 1) <= _iota((L, L), 0), 0.0, NEG_INF)
    qb, past_add = [], []
    for hh in heads:
        q = q_ref[0, :, lanes[hh]].astype(F32)
        qb.append(q.astype(BF16))
        gate = jnp.where(blk < qi, _dot_nt(kmean_ref[hh], q, precision=HIGHEST), NEG_INF)
        picked = jnp.zeros((nb, L), F32)
        for j in range(n_blk - 1):
            gj = gate[j:j + 1, :]
            beats = jnp.logical_or(gate > gj, jnp.logical_and(gate == gj, blk < j))
            in_topk = jnp.sum(jnp.where(beats, 1.0, 0.0), 0, keepdims=True) < n_sel
            picked = jnp.where(blk == j, jnp.where(jnp.logical_and(in_topk, j < qi), 1.0, 0.0), picked)
        picked = _dot_tn(picked, jnp.where(_iota((nb, nb), 0) == _iota((nb, nb), 1), 1.0, 0.0))
        past_add.append([jnp.where(picked[:, j:j + 1] > 0.5, 0.0, NEG_INF) for j in range(n_blk)])

    def attend(n_kv):
        w = n_kv * L
        outs = []
        for hh in heads:
            s = _dot_nt(qb[hh], kbf_ref[0:w, lanes[hh]]) * scale
            parts = []
            for j in range(n_kv):
                bias = bias_ref[hh, jnp.clip(qi - j, 0, MOBA_FAR)]
                parts.append(s[:, j * L:(j + 1) * L] + bias + jnp.where(j == qi, causal_add, past_add[hh][j]))
            outs.append(_softmax_pv(jnp.concatenate(parts, -1), vbf_ref[0:w, lanes[hh]]))
        o_ref[0] = jnp.concatenate(outs, -1).astype(o_ref.dtype)

    lo = 0
    for hi, n_kv in _causal_widths(n_blk):
        pl.when(jnp.logical_and(qi >= lo, qi < hi))(functools.partial(attend, n_kv))
        lo = hi


def _moba_mixer(u, B, S, w_in, w_o, rel_bias):
    H, Dh, L = MOBA_HEADS, u.shape[1] // MOBA_HEADS, MOBA_BLOCK
    n_blk = S // L
    assert S % L == 0 and Dh == V7X_LANES and (MOBA_FAR - 1) * L >= REL_MAX_DIST
    n_sel = max(min(MOBA_TOPK, n_blk - 1), 1)
    hps = MOBA_HEADS_PER_STEP
    HG, hw = H // hps, hps * Dh
    qkv = _mm(u, w_in, BF16).reshape(B, S, 3 * H * Dh)
    qk = jnp.arange(L)[:, None] - jnp.arange(L)[None, :]
    bucket = _t5_bucket(jnp.stack([qk + d * L for d in range(MOBA_FAR + 1)])).astype(jnp.int32)
    o = pl.pallas_call(
        functools.partial(_moba_kernel, Dh ** -0.5 * LOG2E, n_sel),
        grid=(HG, B, n_blk),
        in_specs=[pl.BlockSpec(memory_space=pltpu.SMEM),
                  pl.BlockSpec((1, L, hw), lambda h, b, i: (b, i, h)),
                  pl.BlockSpec((1, S, hw), lambda h, b, i: (b, 0, HG + h)),
                  pl.BlockSpec((1, S, hw), lambda h, b, i: (b, 0, 2 * HG + h)),
                  pl.BlockSpec((MOBA_FAR + 1, L, L), lambda h, b, i: (0, 0, 0))],
        out_specs=pl.BlockSpec((1, L, hw), lambda h, b, i: (b, i, h)),
        out_shape=jax.ShapeDtypeStruct((B, S, H * Dh), BF16),
        scratch_shapes=[pltpu.VMEM((S, hw), BF16), pltpu.VMEM((S, hw), BF16),
                        pltpu.VMEM((hps, -(-n_blk // 8) * 8, Dh), F32),
                        pltpu.VMEM((hps, MOBA_FAR + 1, L, L), F32)],
        compiler_params=_params("parallel", "arbitrary", "arbitrary"), name="moba_attention",
    )(rel_bias, qkv, qkv, qkv, bucket)
    return _mm(o.reshape(B * S, H * Dh), w_o, F32)


GDN_HEAD_GROUP = 32


def _lane_sum(x):
    return _dot(x.astype(BF16), jnp.ones((x.shape[1], x.shape[1]), BF16))


def _l2norm(x):
    return x * lax.rsqrt(_lane_sum(x * x) + L2_EPS)


def _dotb(a, b):
    return _dot(a.astype(BF16), b.astype(BF16))


def _unit_lower_inverse(a_lows, block):
    C = a_lows[0].shape[0]
    r = _iota((C, C), 0)
    c = _iota((C, C), 1)
    eye = jnp.where(r == c, 1.0, 0.0)
    same = (r // block) == (c // block)
    a_d = [jnp.where(same, a, 0.0) for a in a_lows]
    a_off = [a - d for a, d in zip(a_lows, a_d)]
    inv_d = [eye - d for d in a_d]
    pw = a_d
    k = 2
    while k < block:
        pw = [_dotb(p, p) for p in pw]
        inv_d = [_dotb(i, eye + p) for i, p in zip(inv_d, pw)]
        k *= 2
    n = [_dotb(i, o) for i, o in zip(inv_d, a_off)]
    inv_n = [eye - x for x in n]
    pw = n
    k = 2
    while k < C // block:
        pw = [_dotb(p, p) for p in pw]
        inv_n = [_dotb(i, eye + p) for i, p in zip(inv_n, pw)]
        k *= 2
    return [_dotb(i, d) for i, d in zip(inv_n, inv_d)]


GDN_HIST = 16


def _causal_conv_silu(x_ref, w_ref, hist_ref):
    C = x_ref.shape[1]
    x = x_ref[0]
    hist_ref[GDN_HIST:GDN_HIST + C, :] = x
    taps = GDN_CONV - 1
    o = _iota((taps * C, GDN_HIST + C), 0)
    r = _iota((taps * C, GDN_HIST + C), 1)
    shift = jnp.where(r == o % C + o // C + (GDN_HIST - taps), 1.0, 0.0).astype(BF16)
    shifted = _dot(shift, hist_ref[...])
    y = x.astype(F32) * w_ref[taps:taps + 1, :]
    for i in range(taps):
        y = y + shifted[i * C:(i + 1) * C, :] * w_ref[i:i + 1, :]
    hist_ref[0:GDN_HIST, :] = hist_ref[C:C + GDN_HIST, :]
    return _silu(y)


def _gdn_kernel(q_ref, k_ref, v_ref, z_ref, cwq_ref, cwk_ref, cwv_ref, gc_ref, gct_ref, beta_ref, ng_ref,
                o_ref, state_ref, hq_ref, hk_ref, hv_ref):
    G, C, DK, DV = GDN_HEAD_GROUP, GDN_CHUNK, GDN_DK, GDN_DV
    rep = GDN_V_HEADS // GDN_K_HEADS
    heads = range(G)

    @pl.when(pl.program_id(2) == 0)
    def _():
        state_ref[...] = jnp.zeros(state_ref.shape, F32)
        for hist in (hq_ref, hk_ref, hv_ref):
            hist[0:GDN_HIST, :] = jnp.zeros((GDN_HIST, hist.shape[1]), hist.dtype)

    qc = _causal_conv_silu(q_ref, cwq_ref, hq_ref)
    kc = _causal_conv_silu(k_ref, cwk_ref, hk_ref)
    vc = _causal_conv_silu(v_ref, cwv_ref, hv_ref)
    r = _iota((C, C), 0)
    c = _iota((C, C), 1)
    tri = c <= r
    strict = c < r
    gc = gc_ref[0, 0]
    gct = gct_ref[0, 0, 0]
    beta = beta_ref[0, 0]
    q = [_l2norm(qc[:, i * DK:(i + 1) * DK]) * (DK ** -0.5) for i in range(G // rep)]
    k = [_l2norm(kc[:, i * DK:(i + 1) * DK]) for i in range(G // rep)]
    assert DK == V7X_LANES and DV == V7X_LANES
    kb = [x.astype(BF16) for x in k]
    qk = [_dot_nt(a.astype(BF16), b) for a, b in zip(q, kb)]
    kk = [_dot_nt(b, b) for b in kb]
    g_col = [gc[:, h:h + 1] for h in heads]
    b_col = [beta[:, h:h + 1] for h in heads]
    decay = [jnp.where(tri, jnp.exp(jnp.where(tri, g_col[h] - gct[h:h + 1, :], 0.0)), 0.0) for h in heads]
    t_inv = _unit_lower_inverse([jnp.where(strict, kk[h // rep] * b_col[h] * decay[h], 0.0) for h in heads], 16)
    e_g = [jnp.exp(g) for g in g_col]
    rhs = [jnp.concatenate([vc[:, h * DV:(h + 1) * DV] * b_col[h], k[h // rep] * (b_col[h] * e_g[h])], -1)
           for h in heads]
    sol = [_dot(t_inv[h].astype(BF16), rhs[h].astype(BF16)) for h in heads]
    state = [state_ref[h] for h in heads]
    state_b = [s.astype(BF16) for s in state]
    v_new = [sol[h][:, :DV] - _dot(sol[h][:, DV:].astype(BF16), state_b[h]) for h in heads]
    v_new_b = [x.astype(BF16) for x in v_new]
    attn = [jnp.where(tri, qk[h // rep] * decay[h], 0.0).astype(BF16) for h in heads]
    o = [_dot((q[h // rep] * e_g[h]).astype(BF16), state_b[h]) + _dot(attn[h], v_new_b[h]) for h in heads]
    g_last = [g[C - 1:C, :] for g in g_col]
    k_tail = [(k[h // rep] * jnp.exp(g_last[h] - g_col[h])).astype(BF16) for h in heads]
    new_state = [state[h] * jnp.exp(g_last[h]) + _dot_tn(k_tail[h], v_new_b[h]) for h in heads]
    o = [x * lax.rsqrt(_lane_sum(x * x) * (1.0 / DV) + RMS_EPS) * ng_ref[...] for x in o]
    o = [o[h] * _silu(z_ref[0, :, h * DV:(h + 1) * DV].astype(F32)) for h in heads]
    for h in heads:
        state_ref[h] = new_state[h]
    o_ref[0] = jnp.concatenate(o, -1).astype(o_ref.dtype)


def _gdn_mixer(u, B, S, w_in, conv_w, a_log, dt_bias, norm_g, w_o):
    HK, HV, DK, DV, C, G = GDN_K_HEADS, GDN_V_HEADS, GDN_DK, GDN_DV, GDN_CHUNK, GDN_HEAD_GROUP
    qk_dim, v_dim = HK * DK, HV * DV
    n_main = 2 * qk_dim + 2 * v_dim
    w_t = w_in.T
    proj = _mm(u, w_t, BF16, n_cols=n_main, w_transposed=True).reshape(B, S, n_main)
    ba = _mm(u, w_t, F32, col0=n_main, n_cols=V7X_LANES, w_transposed=True).reshape(B, S, -1)
    n_conv = 2 * qk_dim + v_dim
    beta = jax.nn.sigmoid(ba[:, :, :HV])
    g = -jnp.exp(a_log) * jax.nn.softplus(ba[:, :, HV:2 * HV] + dt_bias)
    N = S // C
    gc = jnp.cumsum(g.reshape(B, N, C, HV), axis=2)
    HG = HV // G
    gc_g = gc.reshape(B, N, C, HG, G).transpose(0, 3, 1, 2, 4).reshape(B, HG, S, G)
    gct_g = gc.reshape(B, N, C, HG, G).transpose(0, 3, 1, 4, 2)
    beta_g = beta.reshape(B, S, HG, G).transpose(0, 2, 1, 3)
    kw = (G // (HV // HK)) * DK
    vw = G * DV
    k_blk, v_blk, z_blk = qk_dim // kw, 2 * qk_dim // vw, n_conv // vw
    o = pl.pallas_call(
        _gdn_kernel,
        grid=(B, HG, N),
        in_specs=[pl.BlockSpec((1, C, kw), lambda b, h, n: (b, n, h)),
                  pl.BlockSpec((1, C, kw), lambda b, h, n: (b, n, k_blk + h)),
                  pl.BlockSpec((1, C, vw), lambda b, h, n: (b, n, v_blk + h)),
                  pl.BlockSpec((1, C, vw), lambda b, h, n: (b, n, z_blk + h)),
                  pl.BlockSpec((GDN_CONV, kw), lambda b, h, n: (0, h)),
                  pl.BlockSpec((GDN_CONV, kw), lambda b, h, n: (0, k_blk + h)),
                  pl.BlockSpec((GDN_CONV, vw), lambda b, h, n: (0, v_blk + h)),
                  pl.BlockSpec((1, 1, C, G), lambda b, h, n: (b, h, n, 0)),
                  pl.BlockSpec((1, 1, 1, G, C), lambda b, h, n: (b, h, n, 0, 0)),
                  pl.BlockSpec((1, 1, C, G), lambda b, h, n: (b, h, n, 0)),
                  pl.BlockSpec((1, DV), lambda b, h, n: (0, 0))],
        out_specs=pl.BlockSpec((1, C, vw), lambda b, h, n: (b, n, h)),
        out_shape=jax.ShapeDtypeStruct((B, S, v_dim), BF16),
        scratch_shapes=[pltpu.VMEM((G, DK, DV), F32), pltpu.VMEM((GDN_HIST + C, kw), BF16),
                        pltpu.VMEM((GDN_HIST + C, kw), BF16), pltpu.VMEM((GDN_HIST + C, vw), BF16)],
        compiler_params=_params("parallel", "parallel", "arbitrary"), name="gdn_chunked",
    )(proj, proj, proj, proj, conv_w, conv_w, conv_w, gc_g, gct_g, beta_g, norm_g[None, :])
    return _mm(o.reshape(B * S, v_dim), w_o, F32)


def _gla_kernel(scale, q_ref, k_ref, v_ref, og_ref, gk_ref, wgk_ref, bgk_ref, ng_ref, o_ref, state_ref):
    C = GLA_CHUNK
    H, dv, dk = state_ref.shape
    heads = range(H)

    @pl.when(pl.program_id(1) == 0)
    def _():
        state_ref[...] = jnp.zeros(state_ref.shape, F32)

    x = _dot(gk_ref[0].astype(BF16), wgk_ref[...].astype(BF16)) + bgk_ref[...]
    log_alpha = (jnp.minimum(x, 0.0) - jnp.log(1.0 + jnp.exp(-jnp.abs(x)))) / GLA_GATE_NORMALIZER
    r = _iota((C, C), 0)
    c = _iota((C, C), 1)
    causal = c <= r
    b = _dot(jnp.where(causal, 1.0, 0.0), log_alpha, precision=HIGHEST)
    b_last = b[C - 1:C, :]
    q = q_ref[0].astype(F32) * scale
    k = k_ref[0].astype(F32)
    q_dec = (q * jnp.exp(b)).astype(BF16)
    k_inv = (k * jnp.exp(-b)).astype(BF16)
    k_tail = (k * jnp.exp(b_last - b)).astype(BF16)
    decay = jnp.exp(b_last)
    ks = [slice(h * dk, (h + 1) * dk) for h in heads]
    v = [v_ref[0, :, h * dv:(h + 1) * dv] for h in heads]
    attn = [jnp.where(causal, _dot_nt(q_dec[:, ks[h]], k_inv[:, ks[h]]), 0.0).astype(BF16) for h in heads]
    state_t = [state_ref[h] for h in heads]
    o = [_dot(attn[h], v[h]) + _dot_nt(q_dec[:, ks[h]], state_t[h].astype(BF16)) for h in heads]
    new_state = [state_t[h] * decay[:, ks[h]] + _dot_tn(v[h], k_tail[:, ks[h]]) for h in heads]
    o = [x * lax.rsqrt(jnp.mean(x * x, -1, keepdims=True) + RMS_EPS) * ng_ref[...] for x in o]
    for h in heads:
        state_ref[h] = new_state[h]
    o_ref[0] = (jnp.concatenate(o, -1) * _silu(og_ref[0].astype(F32))).astype(o_ref.dtype)


def _gla_mixer(u, B, S, w_in, w_gk, b_gk, norm_g, w_o):
    D = u.shape[1]
    H, C = GLA_HEADS, GLA_CHUNK
    key_dim, val_dim = D // 2, D
    dk, dv = key_dim // H, val_dim // H
    n_main = 2 * key_dim + 2 * val_dim
    w_t = w_in.T
    proj = _mm(u, w_t, BF16, n_cols=n_main, w_transposed=True).reshape(B, S, n_main)
    gk = _mm(u, w_t, F32, col0=n_main, n_cols=V7X_LANES, w_transposed=True).reshape(B, S, -1)
    wgk = jnp.pad(w_gk, ((0, gk.shape[-1] - GLA_GATE_RANK), (0, 0)))
    o = pl.pallas_call(
        functools.partial(_gla_kernel, dk ** -0.5),
        grid=(B, S // C),
        in_specs=[pl.BlockSpec((1, C, key_dim), lambda b, n: (b, n, 0)),
                  pl.BlockSpec((1, C, key_dim), lambda b, n: (b, n, 1)),
                  pl.BlockSpec((1, C, val_dim), lambda b, n: (b, n, 2 * key_dim // val_dim)),
                  pl.BlockSpec((1, C, val_dim), lambda b, n: (b, n, 2 * key_dim // val_dim + 1)),
                  pl.BlockSpec((1, C, gk.shape[-1]), lambda b, n: (b, n, 0)),
                  pl.BlockSpec((gk.shape[-1], key_dim), lambda b, n: (0, 0)),
                  pl.BlockSpec((1, key_dim), lambda b, n: (0, 0)),
                  pl.BlockSpec((1, dv), lambda b, n: (0, 0))],
        out_specs=pl.BlockSpec((1, C, val_dim), lambda b, n: (b, n, 0)),
        out_shape=jax.ShapeDtypeStruct((B, S, val_dim), BF16),
        scratch_shapes=[pltpu.VMEM((H, dv, dk), F32)],
        compiler_params=_params("parallel", "arbitrary"), name="gla_chunked",
    )(proj, proj, proj, proj, gk, wgk, b_gk[None, :], norm_g[None, :])
    return _mm(o.reshape(B * S, val_dim), w_o, F32)


def kernel(x, c, rel_bias, mla_w_in, mla_q_norm, mla_kv_norm, mla_w_qb, mla_w_kvb, mla_w_o, gdn_w_in, gdn_conv_w, gdn_a_log, gdn_dt_bias, gdn_norm, gdn_w_o, gla_w_in, gla_w_gk, gla_b_gk, gla_norm, gla_w_o, moba_w_in, moba_w_o, ada_w, ada_b, ln_g, ln_b, router_w, router_b, moe_w_gu, moe_b_gu, moe_w_down, moe_b_down):
    B, S, D = x.shape
    assert D == D_MODEL
    depth = ada_w.shape[0]
    alpha = (2 * depth) ** 0.25
    mod = _ada_mod(c, ada_w, ada_b)
    sh_a, sc_a, g_a, sh_f, sc_f, g_f = (mod[:, :, k * D:(k + 1) * D] for k in range(6))
    xt = x.reshape(B * S, D)
    u = _modulate(xt, sc_a[0], sh_a[0], S)
    for i in range(depth):
        m, j = i % N_MIXERS, i // N_MIXERS
        if m == 0:
            h = _mla_mixer(u, B, S, mla_w_in[j], mla_q_norm[j], mla_kv_norm[j], mla_w_qb[j], mla_w_kvb[j], mla_w_o[j])
        elif m == 1:
            h = _gdn_mixer(u, B, S, gdn_w_in[j], gdn_conv_w[j], gdn_a_log[j], gdn_dt_bias[j], gdn_norm[j], gdn_w_o[j])
        elif m == 2:
            h = _gla_mixer(u, B, S, gla_w_in[j], gla_w_gk[j], gla_b_gk[j], gla_norm[j], gla_w_o[j])
        else:
            h = _moba_mixer(u, B, S, moba_w_in[j], moba_w_o[j], rel_bias)
        xt, u, top_idx, top_w = _ln_router(xt, h, g_a[i], ln_g[i, 0], ln_b[i, 0], sc_f[i], sh_f[i],
                                           router_w[i], router_b[i], S, alpha)
        y, dest = _moe_ffn(u, top_idx, i, moe_w_gu, moe_b_gu, moe_w_down, moe_b_down)
        nxt = (i + 1) % depth
        xt, u = _combine_ln(xt, y, dest, top_w, g_f[i], ln_g[i, 1], ln_b[i, 1], sc_a[nxt], sh_a[nxt], S, alpha)
    return xt.reshape(B, S, D)
```

```python
import functools
import math

import jax
import jax.numpy as jnp
from jax import lax
from jax.experimental import pallas as pl
from jax.experimental.pallas import tpu as pltpu

D_MODEL = 2048
N_MIXERS = 4
MLA_HEADS, MLA_Q_LORA, MLA_KV_LORA, MLA_NOPE, MLA_ROPE, MLA_V = 16, 512, 512, 128, 64, 128
ROPE_THETA = 10000.0
GDN_K_HEADS, GDN_V_HEADS, GDN_DK, GDN_DV, GDN_CONV, GDN_CHUNK = 16, 32, 128, 128, 4, 64
GLA_HEADS, GLA_GATE_RANK, GLA_GATE_NORMALIZER, GLA_CHUNK = 4, 16, 16.0, 64
MOBA_HEADS, MOBA_BLOCK, MOBA_TOPK = 16, 256, 3
REL_BUCKETS, REL_MAX_DIST = 32, 128
N_EXPERTS, TOP_K, EXPERT_FF = 32, 4, 768
SWIGLU_LIMIT, SWIGLU_ALPHA = 7.0, 1.702
LN_EPS, RMS_EPS, L2_EPS = 1e-5, 1e-6, 1e-6

V7X_LANES = 128
V7X_VMEM_LIMIT_BYTES = 56 * 1024 * 1024

F32 = jnp.float32
BF16 = jnp.bfloat16
HIGHEST = lax.Precision.HIGHEST
NEG_INF = float("-inf")


def _params(*sem):
    return pltpu.CompilerParams(dimension_semantics=sem, vmem_limit_bytes=V7X_VMEM_LIMIT_BYTES)


def _dot(a, b, dims=None, precision=None):
    if dims is None:
        dims = (((a.ndim - 1,), (0,)), ((), ()))
    return lax.dot_general(a, b, dims, precision=precision, preferred_element_type=F32)


def _dot_nt(a, b, precision=None):
    return _dot(a, b, (((1,), (1,)), ((), ())), precision)


def _dot_tn(a, b, precision=None):
    return _dot(a, b, (((0,), (0,)), ((), ())), precision)


def _sigmoid(x):
    return 1.0 / (1.0 + jnp.exp(-x))


def _silu(x):
    return x * _sigmoid(x)


def _iota(shape, dim):
    return lax.broadcasted_iota(jnp.int32, shape, dim)


def _mm_kernel(valid_cols, rms, w_transposed, x_ref, *refs):
    g_ref = refs[0] if rms else None
    w_ref, o_ref, wbf_ref = refs[-3:]

    @pl.when(pl.program_id(1) == 0)
    def _():
        w = w_ref[...]
        if valid_cols is not None:
            w = jnp.where(_iota(w.shape, 0 if w_transposed else 1) < valid_cols, w, 0.0)
        wbf_ref[...] = w.astype(BF16)

    x = x_ref[...]
    if rms:
        x = x.astype(F32)
        x = x * lax.rsqrt(jnp.mean(x * x, -1, keepdims=True) + RMS_EPS) * g_ref[...]
    dot = _dot_nt if w_transposed else _dot
    o_ref[...] = dot(x.astype(BF16), wbf_ref[...]).astype(o_ref.dtype)


def _mm_tiles(M, K, N):
    tm = 1024 if M % 1024 == 0 else M
    tn = N
    for cand in (1024, 512, 256, 128):
        if N % cand == 0 and K * cand * 4 <= 8 * 1024 * 1024:
            tn = cand
            break
    return tm, tn


def _mm(x, w, out_dtype, rms_gain=None, x_col=0, col0=0, n_cols=None, w_transposed=False):
    M = x.shape[0]
    K, n_w = (w.shape[1], w.shape[0]) if w_transposed else w.shape
    N = n_w if n_cols is None else n_cols
    tm, tn = _mm_tiles(M, K, N)
    assert col0 % tn == 0
    valid_cols = n_w - col0 if col0 + N > n_w else None
    assert valid_cols is None or N == tn
    c0 = col0 // tn
    x_spec = pl.BlockSpec((tm, K), lambda n, m: (m, x_col))
    if w_transposed:
        w_spec, w_tile = pl.BlockSpec((tn, K), lambda n, m: (c0 + n, 0)), (tn, K)
    else:
        w_spec, w_tile = pl.BlockSpec((K, tn), lambda n, m: (0, c0 + n)), (K, tn)
    if rms_gain is None:
        in_specs, args = [x_spec, w_spec], (x, w)
    else:
        in_specs, args = [x_spec, pl.BlockSpec((1, K), lambda n, m: (0, 0)), w_spec], (x, rms_gain[None, :], w)
    return pl.pallas_call(
        functools.partial(_mm_kernel, valid_cols, rms_gain is not None, w_transposed),
        grid=(N // tn, M // tm),
        in_specs=in_specs,
        out_specs=pl.BlockSpec((tm, tn), lambda n, m: (m, n)),
        out_shape=jax.ShapeDtypeStruct((M, N), out_dtype),
        scratch_shapes=[pltpu.VMEM(w_tile, BF16)],
        compiler_params=_params("parallel", "arbitrary"),
        name="proj_matmul",
    )(*args)


def _ada_kernel(c_ref, w_ref, b_ref, o_ref):
    c = _silu(c_ref[...]).astype(BF16)
    o_ref[0] = _dot(c, w_ref[0].astype(BF16)) + b_ref[0]


def _ada_mod(c, ada_w, ada_b):
    depth, D, N = ada_w.shape
    B = c.shape[0]
    rows = 8
    c_pad = jnp.pad(c, ((0, rows - B), (0, 0)))
    tn = 1024
    out = pl.pallas_call(
        _ada_kernel,
        grid=(depth, N // tn),
        in_specs=[pl.BlockSpec((rows, D), lambda i, n: (0, 0)),
                  pl.BlockSpec((1, D, tn), lambda i, n: (i, 0, n)),
                  pl.BlockSpec((1, 1, tn), lambda i, n: (i, 0, n))],
        out_specs=pl.BlockSpec((1, rows, tn), lambda i, n: (i, 0, n)),
        out_shape=jax.ShapeDtypeStruct((depth, rows, N), F32),
        compiler_params=_params("parallel", "parallel"),
        name="ada_mod",
    )(c_pad, ada_w, ada_b.reshape(depth, 1, N))
    return out[:, :B]


ROW_TILE = 256


def _modulate_kernel(x_ref, sc_ref, sh_ref, u_ref):
    u_ref[...] = (x_ref[...] * (1.0 + sc_ref[0]) + sh_ref[0]).astype(u_ref.dtype)


def _row_specs(D, S, tr):
    vec = pl.BlockSpec((1, 1, D), lambda i: ((i * tr) // S, 0, 0))
    row = pl.BlockSpec((tr, D), lambda i: (i, 0))
    par = pl.BlockSpec((1, D), lambda i: (0, 0))
    return vec, row, par


def _modulate(x, sc, sh, S):
    T, D = x.shape
    tr = ROW_TILE
    vec, row, _ = _row_specs(D, S, tr)
    return pl.pallas_call(
        _modulate_kernel, grid=(T // tr,), in_specs=[row, vec, vec], out_specs=row,
        out_shape=jax.ShapeDtypeStruct((T, D), BF16),
        compiler_params=_params("parallel"), name="modulate",
    )(x, sc[:, None, :], sh[:, None, :])


def _deepnorm(alpha, x, h, gate, g, b):
    y = alpha * x + (1.0 + gate) * h
    mu = jnp.mean(y, -1, keepdims=True)
    yc = y - mu
    var = jnp.mean(yc * yc, -1, keepdims=True)
    return yc * lax.rsqrt(var + LN_EPS) * g + b


U32 = jnp.uint32
ROW_PARTS = D_MODEL // (2 * V7X_LANES)
assert ROW_PARTS == 8


def _bits(x):
    return lax.bitcast_convert_type(x.astype(BF16).astype(F32), U32)


def _store_row_tiled(ref, row0, value):
    n = value.shape[0]
    for j in range(ROW_PARTS):
        lo = _bits(value[:, (2 * j) * V7X_LANES:(2 * j + 1) * V7X_LANES])
        hi = _bits(value[:, (2 * j + 1) * V7X_LANES:(2 * j + 2) * V7X_LANES])
        ref[pl.ds(row0 * ROW_PARTS + j, n, stride=ROW_PARTS), :] = lax.shift_right_logical(lo, U32(16)) | hi


def _load_row_tiled(ref, row0, n):
    parts = []
    for j in range(ROW_PARTS):
        w = ref[pl.ds(row0 * ROW_PARTS + j, n, stride=ROW_PARTS), :]
        parts.append(lax.bitcast_convert_type(lax.shift_left(w, U32(16)), F32))
        parts.append(lax.bitcast_convert_type(w & U32(0xFFFF0000), F32))
    return jnp.concatenate(parts, -1)


def _ln_router_kernel(alpha, x_ref, h_ref, gate_ref, g_ref, b_ref, sc_ref, sh_ref, rwh_ref, rwl_ref, rb_ref,
                      xo_ref, u_ref, idx_ref, wgt_ref):
    xn = _deepnorm(alpha, x_ref[...], h_ref[...].astype(F32), gate_ref[0], g_ref[...], b_ref[...])
    xo_ref[...] = xn
    u = xn * (1.0 + sc_ref[0]) + sh_ref[0]
    _store_row_tiled(u_ref, 0, u)
    u_hi = u.astype(BF16)
    u_lo = (u - u_hi.astype(F32)).astype(BF16)
    logits = (_dot_nt(rwh_ref[...], u_hi) + (_dot_nt(rwh_ref[...], u_lo) + _dot_nt(rwl_ref[...], u_hi))) + rb_ref[...]
    expert = _iota(logits.shape, 0)
    vals = logits
    top_v, top_i = [], []
    for _ in range(TOP_K):
        m = jnp.max(vals, 0, keepdims=True)
        i = jnp.min(jnp.where(vals == m, expert, N_EXPERTS), 0, keepdims=True)
        top_v.append(m)
        top_i.append(i)
        vals = jnp.where(expert == i, NEG_INF, vals)
    exps = [jnp.exp(v - top_v[0]) for v in top_v]
    denom = functools.reduce(lambda a, b: a + b, exps)
    slot = _iota(idx_ref.shape, 0)
    idx = jnp.zeros(idx_ref.shape, jnp.int32)
    wgt = jnp.zeros(wgt_ref.shape, F32)
    for k in range(TOP_K):
        idx = jnp.where(slot == k, top_i[k], idx)
        wgt = jnp.where(slot == k, exps[k] / denom, wgt)
    idx_ref[...] = idx
    wgt_ref[...] = wgt


def _ln_router(x, h, gate, ln_g, ln_b, sc, sh, router_w, router_b, S, alpha):
    T, D = x.shape
    tr = ROW_TILE
    vec, row, par = _row_specs(D, S, tr)
    slots = 8
    choice = pl.BlockSpec((slots, tr), lambda i: (0, i))
    tiled_row = pl.BlockSpec((tr * ROW_PARTS, V7X_LANES), lambda i: (i, 0))
    rwt = router_w.T
    rw_hi = rwt.astype(BF16)
    rw_lo = (rwt - rw_hi.astype(F32)).astype(BF16)
    rw_spec = pl.BlockSpec((N_EXPERTS, D), lambda i: (0, 0))
    xo, u, idx, wgt = pl.pallas_call(
        functools.partial(_ln_router_kernel, alpha), grid=(T // tr,),
        in_specs=[row, row, vec, par, par, vec, vec, rw_spec, rw_spec,
                  pl.BlockSpec((N_EXPERTS, 1), lambda i: (0, 0))],
        out_specs=[row, tiled_row, choice, choice],
        out_shape=[jax.ShapeDtypeStruct((T, D), F32), jax.ShapeDtypeStruct((T * ROW_PARTS, V7X_LANES), U32),
                   jax.ShapeDtypeStruct((slots, T), jnp.int32), jax.ShapeDtypeStruct((slots, T), F32)],
        compiler_params=_params("parallel"), name="deepnorm_ln_router",
    )(x, h, gate[:, None, :], ln_g[None, :], ln_b[None, :], sc[:, None, :], sh[:, None, :], rw_hi, rw_lo,
      router_b[:, None])
    return xo, u, idx[:TOP_K].T, wgt[:TOP_K].T


MOE_ROW_TILE = 512


def _expert_weights(te_ref, tn_ref, ts_ref, w_hbm, wbuf_ref, wbf_ref, sems):
    t = pl.program_id(0)
    slot = ts_ref[t]

    def fetch(expert, s):
        return pltpu.make_async_copy(w_hbm.at[expert], wbuf_ref.at[s], sems.at[s])

    @pl.when(t == 0)
    def _():
        fetch(te_ref[0], 0).start(priority=1)

    @pl.when(tn_ref[t] >= 0)
    def _():
        fetch(tn_ref[t], 1 - slot).start(priority=1)

    @pl.when(slot >= 0)
    def _():
        fetch(te_ref[t], slot).wait()
        wbf_ref[...] = wbuf_ref[slot].astype(BF16)


def _moe_gu_kernel(te_ref, tv_ref, tn_ref, ts_ref, x_ref, w_hbm, b_ref, h_ref, wbuf_ref, wbf_ref, sems):
    _expert_weights(te_ref, tn_ref, ts_ref, w_hbm, wbuf_ref, wbf_ref, sems)

    @pl.when(tv_ref[pl.program_id(0)] > 0)
    def _():
        x = _load_row_tiled(x_ref, 0, x_ref.shape[0] // ROW_PARTS).astype(BF16)
        gu = _dot(x, wbf_ref[...]) + b_ref[0]
        gl = jnp.minimum(gu[:, :EXPERT_FF], SWIGLU_LIMIT)
        up = jnp.clip(gu[:, EXPERT_FF:], -SWIGLU_LIMIT, SWIGLU_LIMIT)
        h_ref[...] = ((up + 1.0) * gl * _sigmoid(gl * SWIGLU_ALPHA)).astype(h_ref.dtype)

    @pl.when(tv_ref[pl.program_id(0)] == 0)
    def _():
        h_ref[...] = jnp.zeros(h_ref.shape, h_ref.dtype)


def _moe_down_kernel(te_ref, tv_ref, tn_ref, ts_ref, h_ref, w_hbm, b_ref, y_ref, wbuf_ref, wbf_ref, sems):
    _expert_weights(te_ref, tn_ref, ts_ref, w_hbm, wbuf_ref, wbf_ref, sems)

    @pl.when(tv_ref[pl.program_id(0)] > 0)
    def _():
        _store_row_tiled(y_ref, 0, _dot(h_ref[...], wbf_ref[...]) + b_ref[0])

    @pl.when(tv_ref[pl.program_id(0)] == 0)
    def _():
        y_ref[...] = jnp.zeros(y_ref.shape, y_ref.dtype)


def _route_metadata(top_idx, tm):
    T = top_idx.shape[0]
    P = T * TOP_K
    n_tiles = (P + N_EXPERTS * (tm - 1)) // tm
    e_flat = top_idx.reshape(P)
    onehot = (e_flat[:, None] == jnp.arange(N_EXPERTS)[None, :]).astype(jnp.int32)
    csum = jnp.cumsum(onehot, axis=0)
    counts = csum[-1]
    rank = jnp.sum(csum * onehot, axis=1) - 1
    padded = ((counts + tm - 1) // tm) * tm
    ends_p = jnp.cumsum(padded)
    starts_p = ends_p - padded
    dest = (starts_p[e_flat] + rank).astype(jnp.int32)
    tile_start = jnp.arange(n_tiles, dtype=jnp.int32) * tm
    tile_valid = (tile_start < ends_p[-1]).astype(jnp.int32)
    tile_expert = jnp.sum((tile_start[:, None] >= ends_p[None, :]).astype(jnp.int32), axis=1)
    last_expert = jnp.max(jnp.where(counts > 0, jnp.arange(N_EXPERTS), 0))
    tile_expert = jnp.where(tile_valid > 0, tile_expert, last_expert).astype(jnp.int32)
    first = jnp.concatenate([jnp.ones((1,), bool), tile_expert[1:] != tile_expert[:-1]])
    later = tile_expert[None, :] > tile_expert[:, None]
    nxt = jnp.min(jnp.where(later, tile_expert[None, :], N_EXPERTS), axis=1)
    tile_next = jnp.where(jnp.logical_and(first, nxt < N_EXPERTS), nxt, -1).astype(jnp.int32)
    tile_slot = jnp.where(first, (jnp.cumsum(first.astype(jnp.int32)) - 1) % 2, -1).astype(jnp.int32)
    pads = jnp.stack([jnp.concatenate([starts_p + counts, ends_p[-1:]]),
                      jnp.concatenate([padded - counts, n_tiles - ends_p[-1:] // tm])]).astype(jnp.int32)
    return dest, tile_expert, tile_valid, tile_next, tile_slot, pads, n_tiles


DISPATCH_ROWS = 2048


def _zero_padding_rows(tm, n_pad_rows, pads_ref, xs_hbm, zero_ref, zsem):
    zero_ref[...] = jnp.zeros(zero_ref.shape, zero_ref.dtype)

    def put(first_row, n_rows):
        dst = xs_hbm.at[pl.ds(pl.multiple_of(first_row * ROW_PARTS, ROW_PARTS), n_rows * ROW_PARTS)]
        pltpu.make_async_copy(zero_ref.at[pl.ds(0, n_rows * ROW_PARTS)], dst, zsem).start()

    for e in range(N_EXPERTS):
        start, length = pads_ref[0, e], pads_ref[1, e]
        size = tm // 2
        while size >= 1:
            pl.when((length & size) != 0)(functools.partial(put, start + (length & ~(2 * size - 1)), size))
            size //= 2
    max_tail = (n_pad_rows + tm - 1) // tm
    for t in range(max_tail):
        pl.when(t < pads_ref[1, N_EXPERTS])(functools.partial(put, pads_ref[0, N_EXPERTS] + t * tm, tm))
    n = n_pad_rows * ROW_PARTS
    pltpu.make_async_copy(xs_hbm.at[pl.ds(0, n)], xs_hbm.at[pl.ds(0, n)], zsem).wait()


def _dispatch_kernel(tm, n_pad_rows, pads_ref, dest_ref, u_ref, xs_hbm, sem, zero_ref, zsem):
    R = dest_ref.shape[-1]
    pl.when(pl.program_id(0) == 0)(
        functools.partial(_zero_padding_rows, tm, n_pad_rows, pads_ref, xs_hbm, zero_ref, zsem))

    def row(ref, i):
        return ref.at[pl.ds(pl.multiple_of(i * ROW_PARTS, ROW_PARTS), ROW_PARTS)]

    def issue(t, carry):
        for k in range(TOP_K):
            pltpu.make_async_copy(row(u_ref, t), row(xs_hbm, dest_ref[0, 0, t * TOP_K + k]), sem).start(priority=k % 2)
        return carry

    lax.fori_loop(0, R // TOP_K, issue, 0, unroll=2)
    n = (R // TOP_K) * ROW_PARTS
    for _ in range(TOP_K):
        pltpu.make_async_copy(u_ref, xs_hbm.at[pl.ds(0, n)], sem).wait()


def _dispatch(u, dest, pads, rows, tm):
    P = dest.shape[0]
    R = DISPATCH_ROWS
    return pl.pallas_call(
        functools.partial(_dispatch_kernel, tm, rows - P),
        grid=(P // R,),
        in_specs=[pl.BlockSpec(memory_space=pltpu.SMEM),
                  pl.BlockSpec((1, 1, R), lambda s: (s, 0, 0), memory_space=pltpu.SMEM),
                  pl.BlockSpec(((R // TOP_K) * ROW_PARTS, u.shape[1]), lambda s: (s, 0))],
        out_specs=pl.BlockSpec(memory_space=pl.ANY),
        out_shape=jax.ShapeDtypeStruct((rows * ROW_PARTS, u.shape[1]), u.dtype),
        scratch_shapes=[pltpu.SemaphoreType.DMA(()), pltpu.VMEM((tm * ROW_PARTS, u.shape[1]), u.dtype),
                        pltpu.SemaphoreType.DMA(())],
        compiler_params=_params("arbitrary"), name="moe_dispatch",
    )(pads, dest.reshape(P // R, 1, R), u)


def _moe_ffn(u, top_idx, layer, w_gu, b_gu, w_down, b_down):
    D = D_MODEL
    tm = MOE_ROW_TILE
    dest, tile_expert, tile_valid, tile_next, tile_slot, pads, n_tiles = _route_metadata(top_idx, tm)
    tile_expert = tile_expert + layer * N_EXPERTS
    tile_next = jnp.where(tile_next >= 0, tile_next + layer * N_EXPERTS, -1)
    sched = (tile_expert, tile_valid, tile_next, tile_slot)
    w_gu, w_down = (w.reshape((-1,) + w.shape[2:]) for w in (w_gu, w_down))
    b_gu, b_down = (b.reshape(-1, b.shape[-1]) for b in (b_gu, b_down))
    rows = n_tiles * tm
    x_sorted = _dispatch(u, dest, pads, rows, tm)
    ff2 = 2 * EXPERT_FF
    tiled = pl.BlockSpec((tm * ROW_PARTS, V7X_LANES), lambda t, *_: (t, 0))
    hidden = pl.BlockSpec((tm, EXPERT_FF), lambda t, *_: (t, 0))
    weights = pl.BlockSpec(memory_space=pl.ANY)

    def bias(n):
        return pl.BlockSpec((1, 1, n), lambda t, te, *_: (te[t], 0, 0))

    def weight_scratch(k, n):
        return [pltpu.VMEM((2, k, n), F32), pltpu.VMEM((k, n), BF16), pltpu.SemaphoreType.DMA((2,))]

    h = pl.pallas_call(
        _moe_gu_kernel,
        grid_spec=pltpu.PrefetchScalarGridSpec(
            num_scalar_prefetch=4, grid=(n_tiles,),
            in_specs=[tiled, weights, bias(ff2)], out_specs=hidden, scratch_shapes=weight_scratch(D, ff2)),
        out_shape=jax.ShapeDtypeStruct((rows, EXPERT_FF), BF16),
        compiler_params=_params("arbitrary"), name="moe_gate_up",
    )(*sched, x_sorted, w_gu, b_gu[:, None, :])
    y = pl.pallas_call(
        _moe_down_kernel,
        grid_spec=pltpu.PrefetchScalarGridSpec(
            num_scalar_prefetch=4, grid=(n_tiles,),
            in_specs=[hidden, weights, bias(D)], out_specs=tiled, scratch_shapes=weight_scratch(EXPERT_FF, D)),
        out_shape=jax.ShapeDtypeStruct((rows * ROW_PARTS, V7X_LANES), U32),
        compiler_params=_params("arbitrary"), name="moe_down",
    )(*sched, h, w_down, b_down[:, None, :])
    return y, dest


COMBINE_TILE = 256
COMBINE_SUB = 64


def _combine_ln_kernel(alpha, dest_ref, next_ref, y_hbm, x_ref, wgt_ref, gate_ref, g_ref, b_ref, sc_ref, sh_ref,
                       xo_ref, u_ref, buf0_ref, buf1_ref, sems):
    i = pl.program_id(0)
    bufs = (buf0_ref, buf1_ref)

    def row(ref, r):
        return ref.at[pl.ds(pl.multiple_of(r * ROW_PARTS, ROW_PARTS), ROW_PARTS)]

    def gather(idx_ref, slot):
        def issue(tok, carry):
            for k in range(TOP_K):
                pltpu.make_async_copy(row(y_hbm, idx_ref[0, 0, tok * TOP_K + k]),
                                      row(bufs[slot], k * COMBINE_TILE + tok), sems.at[slot]).start(priority=k % 2)
            return carry

        lax.fori_loop(0, COMBINE_TILE, issue, 0, unroll=2)

    def drain(slot):
        buf = bufs[slot]
        pltpu.make_async_copy(y_hbm.at[pl.ds(0, buf.shape[0])], buf, sems.at[slot]).wait()

    def finish(slot):
        buf = bufs[slot]
        drain(slot)
        for lo in range(0, COMBINE_TILE, COMBINE_SUB):
            for tok in range(lo, lo + COMBINE_SUB):
                for k in range(TOP_K):
                    pltpu.make_async_copy(row(y_hbm, next_ref[0, 0, tok * TOP_K + k]),
                                          row(bufs[1 - slot], k * COMBINE_TILE + tok),
                                          sems.at[1 - slot]).start(priority=k % 2)
            wgt = wgt_ref[pl.ds(lo, COMBINE_SUB), :]
            f = sum(_load_row_tiled(buf, k * COMBINE_TILE + lo, COMBINE_SUB) * wgt[:, k:k + 1] for k in range(TOP_K))
            xn = _deepnorm(alpha, x_ref[pl.ds(lo, COMBINE_SUB), :], f, gate_ref[0], g_ref[...], b_ref[...])
            xo_ref[pl.ds(lo, COMBINE_SUB), :] = xn
            u_ref[pl.ds(lo, COMBINE_SUB), :] = (xn * (1.0 + sc_ref[0]) + sh_ref[0]).astype(u_ref.dtype)

        pl.when(i + 1 == pl.num_programs(0))(functools.partial(drain, 1 - slot))

    pl.when(i == 0)(functools.partial(gather, dest_ref, 0))
    for slot in range(2):
        pl.when(i % 2 == slot)(functools.partial(finish, slot))


def _combine_ln(x, y, dest, top_w, gate, ln_g, ln_b, sc, sh, S, alpha):
    T, D = x.shape
    tr = COMBINE_TILE
    vec, row, par = _row_specs(D, S, tr)
    n_steps = T // tr
    dest = dest.reshape(n_steps, 1, tr * TOP_K)
    gathered = pltpu.VMEM((tr * TOP_K * ROW_PARTS, V7X_LANES), U32)
    return pl.pallas_call(
        functools.partial(_combine_ln_kernel, alpha), grid=(n_steps,),
        in_specs=[pl.BlockSpec((1, 1, tr * TOP_K), lambda i: (i, 0, 0), memory_space=pltpu.SMEM),
                  pl.BlockSpec((1, 1, tr * TOP_K), lambda i: (jnp.minimum(i + 1, n_steps - 1), 0, 0),
                               memory_space=pltpu.SMEM),
                  pl.BlockSpec(memory_space=pl.ANY), row,
                  pl.BlockSpec((tr, TOP_K), lambda i: (i, 0)), vec, par, par, vec, vec],
        out_specs=[row, row],
        out_shape=[jax.ShapeDtypeStruct((T, D), F32), jax.ShapeDtypeStruct((T, D), BF16)],
        scratch_shapes=[gathered, gathered, pltpu.SemaphoreType.DMA((2,))],
        compiler_params=_params("arbitrary"), name="moe_combine_ln",
    )(dest, dest, y, x, top_w, gate[:, None, :], ln_g[None, :], ln_b[None, :], sc[:, None, :], sh[:, None, :])


ATTN_TILE = 256


LOG2E = math.log2(math.e)


def _softmax_pv(s2, v):
    p = jnp.exp2(s2 - jnp.max(s2, -1, keepdims=True))
    return _dot(p.astype(BF16), v) / jnp.sum(p, -1, keepdims=True)


CAUSAL_VARIANTS = 4


def _causal_widths(n_tiles):
    step = max(n_tiles // CAUSAL_VARIANTS, 1)
    bounds = list(range(step, n_tiles, step)) + [n_tiles]
    return [(n, n) for n in bounds]


MLA_HEADS_PER_STEP = 4


def _mla_attn_kernel(scale, qn_ref, qr_ref, cos_ref, sin_ref, kv_ref, kr_ref, o_ref):
    qi = pl.program_id(2)
    t = ATTN_TILE
    n_tiles = kv_ref.shape[1] // t
    x = qr_ref[0].astype(F32)
    half = MLA_ROPE // 2
    first_half = _iota(x.shape, 1) % MLA_ROPE < half
    swapped = jnp.where(first_half, pltpu.roll(x, x.shape[1] - half, 1), pltpu.roll(x, half, 1))
    qr_all = (x * cos_ref[...] + swapped * sin_ref[...]).astype(BF16)

    def attend(n_kv):
        w = n_kv * t
        visible = _iota((t, w), 1) <= _iota((t, w), 0) + qi * t
        outs = []
        for hh in range(MLA_HEADS_PER_STEP):
            qn = qn_ref[0, :, hh * MLA_NOPE:(hh + 1) * MLA_NOPE]
            qr = qr_all[:, hh * MLA_ROPE:(hh + 1) * MLA_ROPE]
            c0 = hh * (MLA_NOPE + MLA_V)
            s = (_dot_nt(qn, kv_ref[0, 0:w, c0:c0 + MLA_NOPE]) + _dot_nt(qr, kr_ref[0, 0:w, :])) * scale
            s = jnp.where(visible, s, NEG_INF)
            outs.append(_softmax_pv(s, kv_ref[0, 0:w, c0 + MLA_NOPE:c0 + MLA_NOPE + MLA_V]))
        o_ref[0] = jnp.concatenate(outs, -1).astype(o_ref.dtype)

    lo = 0
    for hi, n_kv in _causal_widths(n_tiles):
        pl.when(jnp.logical_and(qi >= lo, qi < hi))(functools.partial(attend, n_kv))
        lo = hi


def _rope(x, cos, sin):
    half = x.shape[-1] // 2
    x1, x2 = x[..., :half], x[..., half:]
    return jnp.concatenate([x1 * cos - x2 * sin, x2 * cos + x1 * sin], -1)


def _mla_mixer(u, B, S, w_in, q_norm, kv_norm, w_qb, w_kvb, w_o):
    H = MLA_HEADS
    n_lat = MLA_Q_LORA + MLA_KV_LORA
    lat = _mm(u, w_in, F32, n_cols=n_lat)
    k_rope = _mm(u, w_in, F32, col0=n_lat, n_cols=V7X_LANES)[:, :MLA_ROPE]
    wq = w_qb.reshape(MLA_Q_LORA, H, MLA_NOPE + MLA_ROPE)
    q_nope = _mm(lat, wq[:, :, :MLA_NOPE].reshape(MLA_Q_LORA, H * MLA_NOPE), BF16, rms_gain=q_norm, x_col=0)
    q_rope = _mm(lat, wq[:, :, MLA_NOPE:].reshape(MLA_Q_LORA, H * MLA_ROPE), F32, rms_gain=q_norm, x_col=0)
    kv = _mm(lat, w_kvb, BF16, rms_gain=kv_norm, x_col=1)
    inv_freq = ROPE_THETA ** (-jnp.arange(MLA_ROPE // 2, dtype=F32) / (MLA_ROPE // 2))
    ang = jnp.arange(S, dtype=F32)[:, None] * inv_freq[None, :]
    cos, sin = jnp.cos(ang), jnp.sin(ang)
    k_rope = _rope(k_rope.reshape(B, S, MLA_ROPE), cos, sin).astype(BF16)
    g = MLA_HEADS_PER_STEP
    cos2 = jnp.tile(jnp.concatenate([cos, cos], -1), (1, g))
    sin2 = jnp.tile(jnp.concatenate([-sin, sin], -1), (1, g))
    t = ATTN_TILE
    o = pl.pallas_call(
        functools.partial(_mla_attn_kernel, (MLA_NOPE + MLA_ROPE) ** -0.5 * LOG2E),
        grid=(B, H // g, S // t),
        in_specs=[pl.BlockSpec((1, t, g * MLA_NOPE), lambda b, h, i: (b, i, h)),
                  pl.BlockSpec((1, t, g * MLA_ROPE), lambda b, h, i: (b, i, h)),
                  pl.BlockSpec((t, g * MLA_ROPE), lambda b, h, i: (i, 0)),
                  pl.BlockSpec((t, g * MLA_ROPE), lambda b, h, i: (i, 0)),
                  pl.BlockSpec((1, S, g * (MLA_NOPE + MLA_V)), lambda b, h, i: (b, 0, h)),
                  pl.BlockSpec((1, S, MLA_ROPE), lambda b, h, i: (b, 0, 0))],
        out_specs=pl.BlockSpec((1, t, g * MLA_V), lambda b, h, i: (b, i, h)),
        out_shape=jax.ShapeDtypeStruct((B, S, H * MLA_V), BF16),
        compiler_params=_params("parallel", "parallel", "arbitrary"), name="mla_attention",
    )(q_nope.reshape(B, S, -1), q_rope.reshape(B, S, -1), cos2, sin2, kv.reshape(B, S, -1), k_rope)
    return _mm(o.reshape(B * S, H * MLA_V), w_o, F32)


def _t5_bucket(dist):
    n = jnp.maximum(dist, 0)
    max_exact = REL_BUCKETS // 2
    large = max_exact + (jnp.log(jnp.maximum(n, 1).astype(F32) / max_exact)
                         / math.log(REL_MAX_DIST / max_exact) * (REL_BUCKETS - max_exact)).astype(jnp.int32)
    large = jnp.minimum(large, REL_BUCKETS - 1)
    return jnp.where(n < max_exact, n, large)


MOBA_FAR = 2


MOBA_HEADS_PER_STEP = 4


def _moba_kernel(scale, n_sel, tab_ref, q_ref, k_ref, v_ref, bkt_ref, o_ref, kbf_ref, vbf_ref, kmean_ref, bias_ref):
    hp, b, qi = pl.program_id(0), pl.program_id(1), pl.program_id(2)
    L, Dh = MOBA_BLOCK, V7X_LANES
    S = k_ref.shape[1]
    n_blk = S // L
    heads = range(MOBA_HEADS_PER_STEP)
    lanes = [slice(hh * Dh, (hh + 1) * Dh) for hh in heads]

    @pl.when(jnp.logical_and(b == 0, qi == 0))
    def _():
        for hh in heads:
            for d in range(MOBA_FAR + 1):
                bucket = bkt_ref[d]
                tile = jnp.zeros((L, L), F32)
                for e in range(REL_BUCKETS):
                    tile = jnp.where(bucket == e, tab_ref[e, hp * MOBA_HEADS_PER_STEP + hh] * LOG2E, tile)
                bias_ref[hh, d] = tile

    @pl.when(qi == 0)
    def _():
        k = k_ref[0].astype(F32)
        kbf_ref[...] = k.astype(BF16)
        vbf_ref[...] = v_ref[0].astype(BF16)
        kmean_ref[...] = jnp.zeros(kmean_ref.shape, F32)
        for hh in heads:
            kmean_ref[hh, 0:n_blk, :] = jnp.mean(k[:, lanes[hh]].reshape(n_blk, L, Dh), axis=1)

    nb = kmean_ref.shape[1]
    blk = _iota((nb, L), 0)
    causal_add = jnp.where(_iota((L, L), 1) <= _iota((L, L), 0), 0.0, NEG_INF)
    qb, past_add = [], []
    for hh in heads:
        q = q_ref[0, :, lanes[hh]].astype(F32)
        qb.append(q.astype(BF16))
        gate = jnp.where(blk < qi, _dot_nt(kmean_ref[hh], q, precision=HIGHEST), NEG_INF)
        picked = jnp.zeros((nb, L), F32)
        for j in range(n_blk - 1):
            gj = gate[j:j + 1, :]
            beats = jnp.logical_or(gate > gj, jnp.logical_and(gate == gj, blk < j))
            in_topk = jnp.sum(jnp.where(beats, 1.0, 0.0), 0, keepdims=True) < n_sel
            picked = jnp.where(blk == j, jnp.where(jnp.logical_and(in_topk, j < qi), 1.0, 0.0), picked)
        picked = _dot_tn(picked, jnp.where(_iota((nb, nb), 0) == _iota((nb, nb), 1), 1.0, 0.0))
        past_add.append([jnp.where(picked[:, j:j + 1] > 0.5, 0.0, NEG_INF) for j in range(n_blk)])

    def attend(n_kv):
        w = n_kv * L
        outs = []
        for hh in heads:
            s = _dot_nt(qb[hh], kbf_ref[0:w, lanes[hh]]) * scale
            parts = []
            for j in range(n_kv):
                bias = bias_ref[hh, jnp.clip(qi - j, 0, MOBA_FAR)]
                parts.append(s[:, j * L:(j + 1) * L] + bias + jnp.where(j == qi, causal_add, past_add[hh][j]))
            outs.append(_softmax_pv(jnp.concatenate(parts, -1), vbf_ref[0:w, lanes[hh]]))
        o_ref[0] = jnp.concatenate(outs, -1).astype(o_ref.dtype)

    lo = 0
    for hi, n_kv in _causal_widths(n_blk):
        pl.when(jnp.logical_and(qi >= lo, qi < hi))(functools.partial(attend, n_kv))
        lo = hi


def _moba_mixer(u, B, S, w_in, w_o, rel_bias):
    H, Dh, L = MOBA_HEADS, u.shape[1] // MOBA_HEADS, MOBA_BLOCK
    n_blk = S // L
    assert S % L == 0 and Dh == V7X_LANES and (MOBA_FAR - 1) * L >= REL_MAX_DIST
    n_sel = max(min(MOBA_TOPK, n_blk - 1), 1)
    hps = MOBA_HEADS_PER_STEP
    HG, hw = H // hps, hps * Dh
    qkv = _mm(u, w_in, BF16).reshape(B, S, 3 * H * Dh)
    qk = jnp.arange(L)[:, None] - jnp.arange(L)[None, :]
    bucket = _t5_bucket(jnp.stack([qk + d * L for d in range(MOBA_FAR + 1)])).astype(jnp.int32)
    o = pl.pallas_call(
        functools.partial(_moba_kernel, Dh ** -0.5 * LOG2E, n_sel),
        grid=(HG, B, n_blk),
        in_specs=[pl.BlockSpec(memory_space=pltpu.SMEM),
                  pl.BlockSpec((1, L, hw), lambda h, b, i: (b, i, h)),
                  pl.BlockSpec((1, S, hw), lambda h, b, i: (b, 0, HG + h)),
                  pl.BlockSpec((1, S, hw), lambda h, b, i: (b, 0, 2 * HG + h)),
                  pl.BlockSpec((MOBA_FAR + 1, L, L), lambda h, b, i: (0, 0, 0))],
        out_specs=pl.BlockSpec((1, L, hw), lambda h, b, i: (b, i, h)),
        out_shape=jax.ShapeDtypeStruct((B, S, H * Dh), BF16),
        scratch_shapes=[pltpu.VMEM((S, hw), BF16), pltpu.VMEM((S, hw), BF16),
                        pltpu.VMEM((hps, -(-n_blk // 8) * 8, Dh), F32),
                        pltpu.VMEM((hps, MOBA_FAR + 1, L, L), F32)],
        compiler_params=_params("parallel", "arbitrary", "arbitrary"), name="moba_attention",
    )(rel_bias, qkv, qkv, qkv, bucket)
    return _mm(o.reshape(B * S, H * Dh), w_o, F32)


GDN_HEAD_GROUP = 32


def _lane_sum(x):
    return _dot(x.astype(BF16), jnp.ones((x.shape[1], x.shape[1]), BF16))


def _l2norm(x):
    return x * lax.rsqrt(_lane_sum(x * x) + L2_EPS)


def _dotb(a, b):
    return _dot(a.astype(BF16), b.astype(BF16))


def _unit_lower_inverse(a_lows, block):
    C = a_lows[0].shape[0]
    r = _iota((C, C), 0)
    c = _iota((C, C), 1)
    eye = jnp.where(r == c, 1.0, 0.0)
    same = (r // block) == (c // block)
    a_d = [jnp.where(same, a, 0.0) for a in a_lows]
    a_off = [a - d for a, d in zip(a_lows, a_d)]
    inv_d = [eye - d for d in a_d]
    pw = a_d
    k = 2
    while k < block:
        pw = [_dotb(p, p) for p in pw]
        inv_d = [_dotb(i, eye + p) for i, p in zip(inv_d, pw)]
        k *= 2
    n = [_dotb(i, o) for i, o in zip(inv_d, a_off)]
    inv_n = [eye - x for x in n]
    pw = n
    k = 2
    while k < C // block:
        pw = [_dotb(p, p) for p in pw]
        inv_n = [_dotb(i, eye + p) for i, p in zip(inv_n, pw)]
        k *= 2
    return [_dotb(i, d) for i, d in zip(inv_n, inv_d)]


GDN_HIST = 16


def _causal_conv_silu(x_ref, w_ref, hist_ref):
    C = x_ref.shape[1]
    x = x_ref[0]
    hist_ref[GDN_HIST:GDN_HIST + C, :] = x
    taps = GDN_CONV - 1
    o = _iota((taps * C, GDN_HIST + C), 0)
    r = _iota((taps * C, GDN_HIST + C), 1)
    shift = jnp.where(r == o % C + o // C + (GDN_HIST - taps), 1.0, 0.0).astype(BF16)
    shifted = _dot(shift, hist_ref[...])
    y = x.astype(F32) * w_ref[taps:taps + 1, :]
    for i in range(taps):
        y = y + shifted[i * C:(i + 1) * C, :] * w_ref[i:i + 1, :]
    hist_ref[0:GDN_HIST, :] = hist_ref[C:C + GDN_HIST, :]
    return _silu(y)


def _gdn_kernel(q_ref, k_ref, v_ref, z_ref, cwq_ref, cwk_ref, cwv_ref, gc_ref, gct_ref, beta_ref, ng_ref,
                o_ref, state_ref, hq_ref, hk_ref, hv_ref):
    G, C, DK, DV = GDN_HEAD_GROUP, GDN_CHUNK, GDN_DK, GDN_DV
    rep = GDN_V_HEADS // GDN_K_HEADS
    heads = range(G)

    @pl.when(pl.program_id(2) == 0)
    def _():
        state_ref[...] = jnp.zeros(state_ref.shape, F32)
        for hist in (hq_ref, hk_ref, hv_ref):
            hist[0:GDN_HIST, :] = jnp.zeros((GDN_HIST, hist.shape[1]), hist.dtype)

    qc = _causal_conv_silu(q_ref, cwq_ref, hq_ref)
    kc = _causal_conv_silu(k_ref, cwk_ref, hk_ref)
    vc = _causal_conv_silu(v_ref, cwv_ref, hv_ref)
    r = _iota((C, C), 0)
    c = _iota((C, C), 1)
    tri = c <= r
    strict = c < r
    gc = gc_ref[0, 0]
    gct = gct_ref[0, 0, 0]
    beta = beta_ref[0, 0]
    q = [_l2norm(qc[:, i * DK:(i + 1) * DK]) * (DK ** -0.5) for i in range(G // rep)]
    k = [_l2norm(kc[:, i * DK:(i + 1) * DK]) for i in range(G // rep)]
    assert DK == V7X_LANES and DV == V7X_LANES
    kb = [x.astype(BF16) for x in k]
    qk = [_dot_nt(a.astype(BF16), b) for a, b in zip(q, kb)]
    kk = [_dot_nt(b, b) for b in kb]
    g_col = [gc[:, h:h + 1] for h in heads]
    b_col = [beta[:, h:h + 1] for h in heads]
    decay = [jnp.where(tri, jnp.exp(jnp.where(tri, g_col[h] - gct[h:h + 1, :], 0.0)), 0.0) for h in heads]
    t_inv = _unit_lower_inverse([jnp.where(strict, kk[h // rep] * b_col[h] * decay[h], 0.0) for h in heads], 16)
    e_g = [jnp.exp(g) for g in g_col]
    rhs = [jnp.concatenate([vc[:, h * DV:(h + 1) * DV] * b_col[h], k[h // rep] * (b_col[h] * e_g[h])], -1)
           for h in heads]
    sol = [_dot(t_inv[h].astype(BF16), rhs[h].astype(BF16)) for h in heads]
    state = [state_ref[h] for h in heads]
    state_b = [s.astype(BF16) for s in state]
    v_new = [sol[h][:, :DV] - _dot(sol[h][:, DV:].astype(BF16), state_b[h]) for h in heads]
    v_new_b = [x.astype(BF16) for x in v_new]
    attn = [jnp.where(tri, qk[h // rep] * decay[h], 0.0).astype(BF16) for h in heads]
    o = [_dot((q[h // rep] * e_g[h]).astype(BF16), state_b[h]) + _dot(attn[h], v_new_b[h]) for h in heads]
    g_last = [g[C - 1:C, :] for g in g_col]
    k_tail = [(k[h // rep] * jnp.exp(g_last[h] - g_col[h])).astype(BF16) for h in heads]
    new_state = [state[h] * jnp.exp(g_last[h]) + _dot_tn(k_tail[h], v_new_b[h]) for h in heads]
    o = [x * lax.rsqrt(_lane_sum(x * x) * (1.0 / DV) + RMS_EPS) * ng_ref[...] for x in o]
    o = [o[h] * _silu(z_ref[0, :, h * DV:(h + 1) * DV].astype(F32)) for h in heads]
    for h in heads:
        state_ref[h] = new_state[h]
    o_ref[0] = jnp.concatenate(o, -1).astype(o_ref.dtype)


def _gdn_mixer(u, B, S, w_in, conv_w, a_log, dt_bias, norm_g, w_o):
    HK, HV, DK, DV, C, G = GDN_K_HEADS, GDN_V_HEADS, GDN_DK, GDN_DV, GDN_CHUNK, GDN_HEAD_GROUP
    qk_dim, v_dim = HK * DK, HV * DV
    n_main = 2 * qk_dim + 2 * v_dim
    w_t = w_in.T
    proj = _mm(u, w_t, BF16, n_cols=n_main, w_transposed=True).reshape(B, S, n_main)
    ba = _mm(u, w_t, F32, col0=n_main, n_cols=V7X_LANES, w_transposed=True).reshape(B, S, -1)
    n_conv = 2 * qk_dim + v_dim
    beta = jax.nn.sigmoid(ba[:, :, :HV])
    g = -jnp.exp(a_log) * jax.nn.softplus(ba[:, :, HV:2 * HV] + dt_bias)
    N = S // C
    gc = jnp.cumsum(g.reshape(B, N, C, HV), axis=2)
    HG = HV // G
    gc_g = gc.reshape(B, N, C, HG, G).transpose(0, 3, 1, 2, 4).reshape(B, HG, S, G)
    gct_g = gc.reshape(B, N, C, HG, G).transpose(0, 3, 1, 4, 2)
    beta_g = beta.reshape(B, S, HG, G).transpose(0, 2, 1, 3)
    kw = (G // (HV // HK)) * DK
    vw = G * DV
    k_blk, v_blk, z_blk = qk_dim // kw, 2 * qk_dim // vw, n_conv // vw
    o = pl.pallas_call(
        _gdn_kernel,
        grid=(B, HG, N),
        in_specs=[pl.BlockSpec((1, C, kw), lambda b, h, n: (b, n, h)),
                  pl.BlockSpec((1, C, kw), lambda b, h, n: (b, n, k_blk + h)),
                  pl.BlockSpec((1, C, vw), lambda b, h, n: (b, n, v_blk + h)),
                  pl.BlockSpec((1, C, vw), lambda b, h, n: (b, n, z_blk + h)),
                  pl.BlockSpec((GDN_CONV, kw), lambda b, h, n: (0, h)),
                  pl.BlockSpec((GDN_CONV, kw), lambda b, h, n: (0, k_blk + h)),
                  pl.BlockSpec((GDN_CONV, vw), lambda b, h, n: (0, v_blk + h)),
                  pl.BlockSpec((1, 1, C, G), lambda b, h, n: (b, h, n, 0)),
                  pl.BlockSpec((1, 1, 1, G, C), lambda b, h, n: (b, h, n, 0, 0)),
                  pl.BlockSpec((1, 1, C, G), lambda b, h, n: (b, h, n, 0)),
                  pl.BlockSpec((1, DV), lambda b, h, n: (0, 0))],
        out_specs=pl.BlockSpec((1, C, vw), lambda b, h, n: (b, n, h)),
        out_shape=jax.ShapeDtypeStruct((B, S, v_dim), BF16),
        scratch_shapes=[pltpu.VMEM((G, DK, DV), F32), pltpu.VMEM((GDN_HIST + C, kw), BF16),
                        pltpu.VMEM((GDN_HIST + C, kw), BF16), pltpu.VMEM((GDN_HIST + C, vw), BF16)],
        compiler_params=_params("parallel", "parallel", "arbitrary"), name="gdn_chunked",
    )(proj, proj, proj, proj, conv_w, conv_w, conv_w, gc_g, gct_g, beta_g, norm_g[None, :])
    return _mm(o.reshape(B * S, v_dim), w_o, F32)


def _gla_kernel(scale, q_ref, k_ref, v_ref, og_ref, gk_ref, wgk_ref, bgk_ref, ng_ref, o_ref, state_ref):
    C = GLA_CHUNK
    H, dv, dk = state_ref.shape
    heads = range(H)

    @pl.when(pl.program_id(1) == 0)
    def _():
        state_ref[...] = jnp.zeros(state_ref.shape, F32)

    x = _dot(gk_ref[0].astype(BF16), wgk_ref[...].astype(BF16)) + bgk_ref[...]
    log_alpha = (jnp.minimum(x, 0.0) - jnp.log(1.0 + jnp.exp(-jnp.abs(x)))) / GLA_GATE_NORMALIZER
    r = _iota((C, C), 0)
    c = _iota((C, C), 1)
    causal = c <= r
    b = _dot(jnp.where(causal, 1.0, 0.0), log_alpha, precision=HIGHEST)
    b_last = b[C - 1:C, :]
    q = q_ref[0].astype(F32) * scale
    k = k_ref[0].astype(F32)
    q_dec = (q * jnp.exp(b)).astype(BF16)
    k_inv = (k * jnp.exp(-b)).astype(BF16)
    k_tail = (k * jnp.exp(b_last - b)).astype(BF16)
    decay = jnp.exp(b_last)
    ks = [slice(h * dk, (h + 1) * dk) for h in heads]
    v = [v_ref[0, :, h * dv:(h + 1) * dv] for h in heads]
    attn = [jnp.where(causal, _dot_nt(q_dec[:, ks[h]], k_inv[:, ks[h]]), 0.0).astype(BF16) for h in heads]
    state_t = [state_ref[h] for h in heads]
    o = [_dot(attn[h], v[h]) + _dot_nt(q_dec[:, ks[h]], state_t[h].astype(BF16)) for h in heads]
    new_state = [state_t[h] * decay[:, ks[h]] + _dot_tn(v[h], k_tail[:, ks[h]]) for h in heads]
    o = [x * lax.rsqrt(jnp.mean(x * x, -1, keepdims=True) + RMS_EPS) * ng_ref[...] for x in o]
    for h in heads:
        state_ref[h] = new_state[h]
    o_ref[0] = (jnp.concatenate(o, -1) * _silu(og_ref[0].astype(F32))).astype(o_ref.dtype)


def _gla_mixer(u, B, S, w_in, w_gk, b_gk, norm_g, w_o):
    D = u.shape[1]
    H, C = GLA_HEADS, GLA_CHUNK
    key_dim, val_dim = D // 2, D
    dk, dv = key_dim // H, val_dim // H
    n_main = 2 * key_dim + 2 * val_dim
    w_t = w_in.T
    proj = _mm(u, w_t, BF16, n_cols=n_main, w_transposed=True).reshape(B, S, n_main)
    gk = _mm(u, w_t, F32, col0=n_main, n_cols=V7X_LANES, w_transposed=True).reshape(B, S, -1)
    wgk = jnp.pad(w_gk, ((0, gk.shape[-1] - GLA_GATE_RANK), (0, 0)))
    o = pl.pallas_call(
        functools.partial(_gla_kernel, dk ** -0.5),
        grid=(B, S // C),
        in_specs=[pl.BlockSpec((1, C, key_dim), lambda b, n: (b, n, 0)),
                  pl.BlockSpec((1, C, key_dim), lambda b, n: (b, n, 1)),
                  pl.BlockSpec((1, C, val_dim), lambda b, n: (b, n, 2 * key_dim // val_dim)),
                  pl.BlockSpec((1, C, val_dim), lambda b, n: (b, n, 2 * key_dim // val_dim + 1)),
                  pl.BlockSpec((1, C, gk.shape[-1]), lambda b, n: (b, n, 0)),
                  pl.BlockSpec((gk.shape[-1], key_dim), lambda b, n: (0, 0)),
                  pl.BlockSpec((1, key_dim), lambda b, n: (0, 0)),
                  pl.BlockSpec((1, dv), lambda b, n: (0, 0))],
        out_specs=pl.BlockSpec((1, C, val_dim), lambda b, n: (b, n, 0)),
        out_shape=jax.ShapeDtypeStruct((B, S, val_dim), BF16),
        scratch_shapes=[pltpu.VMEM((H, dv, dk), F32)],
        compiler_params=_params("parallel", "arbitrary"), name="gla_chunked",
    )(proj, proj, proj, proj, gk, wgk, b_gk[None, :], norm_g[None, :])
    return _mm(o.reshape(B * S, val_dim), w_o, F32)


def kernel(x, c, rel_bias, mla_w_in, mla_q_norm, mla_kv_norm, mla_w_qb, mla_w_kvb, mla_w_o, gdn_w_in, gdn_conv_w, gdn_a_log, gdn_dt_bias, gdn_norm, gdn_w_o, gla_w_in, gla_w_gk, gla_b_gk, gla_norm, gla_w_o, moba_w_in, moba_w_o, ada_w, ada_b, ln_g, ln_b, router_w, router_b, moe_w_gu, moe_b_gu, moe_w_down, moe_b_down):
    B, S, D = x.shape
    assert D == D_MODEL
    depth = ada_w.shape[0]
    alpha = (2 * depth) ** 0.25
    mod = _ada_mod(c, ada_w, ada_b)
    sh_a, sc_a, g_a, sh_f, sc_f, g_f = (mod[:, :, k * D:(k + 1) * D] for k in range(6))
    xt = x.reshape(B * S, D)
    u = _modulate(xt, sc_a[0], sh_a[0], S)
    for i in range(depth):
        m, j = i % N_MIXERS, i // N_MIXERS
        if m == 0:
            h = _mla_mixer(u, B, S, mla_w_in[j], mla_q_norm[j], mla_kv_norm[j], mla_w_qb[j], mla_w_kvb[j], mla_w_o[j])
        elif m == 1:
            h = _gdn_mixer(u, B, S, gdn_w_in[j], gdn_conv_w[j], gdn_a_log[j], gdn_dt_bias[j], gdn_norm[j], gdn_w_o[j])
        elif m == 2:
            h = _gla_mixer(u, B, S, gla_w_in[j], gla_w_gk[j], gla_b_gk[j], gla_norm[j], gla_w_o[j])
        else:
            h = _moba_mixer(u, B, S, moba_w_in[j], moba_w_o[j], rel_bias)
        xt, u, top_idx, top_w = _ln_router(xt, h, g_a[i], ln_g[i, 0], ln_b[i, 0], sc_f[i], sh_f[i],
                                           router_w[i], router_b[i], S, alpha)
        y, dest = _moe_ffn(u, top_idx, i, moe_w_gu, moe_b_gu, moe_w_down, moe_b_down)
        nxt = (i + 1) % depth
        xt, u = _combine_ln(xt, y, dest, top_w, g_f[i], ln_g[i, 1], ln_b[i, 1], sc_a[nxt], sh_a[nxt], S, alpha)
    return xt.reshape(B, S, D)
```

```python
import functools
import math

import jax
import jax.numpy as jnp
from jax import lax
from jax.experimental import pallas as pl
from jax.experimental.pallas import tpu as pltpu

D_MODEL = 2048
N_MIXERS = 4
MLA_HEADS, MLA_Q_LORA, MLA_KV_LORA, MLA_NOPE, MLA_ROPE, MLA_V = 16, 512, 512, 128, 64, 128
ROPE_THETA = 10000.0
GDN_K_HEADS, GDN_V_HEADS, GDN_DK, GDN_DV, GDN_CONV, GDN_CHUNK = 16, 32, 128, 128, 4, 64
GLA_HEADS, GLA_GATE_RANK, GLA_GATE_NORMALIZER, GLA_CHUNK = 4, 16, 16.0, 64
MOBA_HEADS, MOBA_BLOCK, MOBA_TOPK = 16, 256, 3
REL_BUCKETS, REL_MAX_DIST = 32, 128
N_EXPERTS, TOP_K, EXPERT_FF = 32, 4, 768
SWIGLU_LIMIT, SWIGLU_ALPHA = 7.0, 1.702
LN_EPS, RMS_EPS, L2_EPS = 1e-5, 1e-6, 1e-6

V7X_LANES = 128
V7X_VMEM_LIMIT_BYTES = 56 * 1024 * 1024

F32 = jnp.float32
BF16 = jnp.bfloat16
HIGHEST = lax.Precision.HIGHEST
NEG_INF = float("-inf")


def _params(*sem):
    return pltpu.CompilerParams(dimension_semantics=sem, vmem_limit_bytes=V7X_VMEM_LIMIT_BYTES)


def _dot(a, b, dims=None, precision=None):
    if dims is None:
        dims = (((a.ndim - 1,), (0,)), ((), ()))
    return lax.dot_general(a, b, dims, precision=precision, preferred_element_type=F32)


def _dot_nt(a, b, precision=None):
    return _dot(a, b, (((1,), (1,)), ((), ())), precision)


def _dot_tn(a, b, precision=None):
    return _dot(a, b, (((0,), (0,)), ((), ())), precision)


def _sigmoid(x):
    return 1.0 / (1.0 + jnp.exp(-x))


def _silu(x):
    return x * _sigmoid(x)


def _iota(shape, dim):
    return lax.broadcasted_iota(jnp.int32, shape, dim)


def _mm_kernel(valid_cols, rms, w_transposed, x_ref, *refs):
    g_ref = refs[0] if rms else None
    w_ref, o_ref, wbf_ref = refs[-3:]

    @pl.when(pl.program_id(1) == 0)
    def _():
        w = w_ref[...]
        if valid_cols is not None:
            w = jnp.where(_iota(w.shape, 0 if w_transposed else 1) < valid_cols, w, 0.0)
        wbf_ref[...] = w.astype(BF16)

    x = x_ref[...]
    if rms:
        x = x.astype(F32)
        x = x * lax.rsqrt(jnp.mean(x * x, -1, keepdims=True) + RMS_EPS) * g_ref[...]
    dot = _dot_nt if w_transposed else _dot
    o_ref[...] = dot(x.astype(BF16), wbf_ref[...]).astype(o_ref.dtype)


def _mm_tiles(M, K, N):
    tm = 1024 if M % 1024 == 0 else M
    tn = N
    for cand in (1024, 512, 256, 128):
        if N % cand == 0 and K * cand * 4 <= 8 * 1024 * 1024:
            tn = cand
            break
    return tm, tn


def _mm(x, w, out_dtype, rms_gain=None, x_col=0, col0=0, n_cols=None, w_transposed=False):
    M = x.shape[0]
    K, n_w = (w.shape[1], w.shape[0]) if w_transposed else w.shape
    N = n_w if n_cols is None else n_cols
    tm, tn = _mm_tiles(M, K, N)
    assert col0 % tn == 0
    valid_cols = n_w - col0 if col0 + N > n_w else None
    assert valid_cols is None or N == tn
    c0 = col0 // tn
    x_spec = pl.BlockSpec((tm, K), lambda n, m: (m, x_col))
    if w_transposed:
        w_spec, w_tile = pl.BlockSpec((tn, K), lambda n, m: (c0 + n, 0)), (tn, K)
    else:
        w_spec, w_tile = pl.BlockSpec((K, tn), lambda n, m: (0, c0 + n)), (K, tn)
    if rms_gain is None:
        in_specs, args = [x_spec, w_spec], (x, w)
    else:
        in_specs, args = [x_spec, pl.BlockSpec((1, K), lambda n, m: (0, 0)), w_spec], (x, rms_gain[None, :], w)
    return pl.pallas_call(
        functools.partial(_mm_kernel, valid_cols, rms_gain is not None, w_transposed),
        grid=(N // tn, M // tm),
        in_specs=in_specs,
        out_specs=pl.BlockSpec((tm, tn), lambda n, m: (m, n)),
        out_shape=jax.ShapeDtypeStruct((M, N), out_dtype),
        scratch_shapes=[pltpu.VMEM(w_tile, BF16)],
        compiler_params=_params("parallel", "arbitrary"),
        name="proj_matmul",
    )(*args)


def _ada_kernel(c_ref, w_ref, b_ref, o_ref):
    c = _silu(c_ref[...]).astype(BF16)
    o_ref[0] = _dot(c, w_ref[0].astype(BF16)) + b_ref[0]


def _ada_mod(c, ada_w, ada_b):
    depth, D, N = ada_w.shape
    B = c.shape[0]
    rows = 8
    c_pad = jnp.pad(c, ((0, rows - B), (0, 0)))
    tn = 1024
    out = pl.pallas_call(
        _ada_kernel,
        grid=(depth, N // tn),
        in_specs=[pl.BlockSpec((rows, D), lambda i, n: (0, 0)),
                  pl.BlockSpec((1, D, tn), lambda i, n: (i, 0, n)),
                  pl.BlockSpec((1, 1, tn), lambda i, n: (i, 0, n))],
        out_specs=pl.BlockSpec((1, rows, tn), lambda i, n: (i, 0, n)),
        out_shape=jax.ShapeDtypeStruct((depth, rows, N), F32),
        compiler_params=_params("parallel", "parallel"),
        name="ada_mod",
    )(c_pad, ada_w, ada_b.reshape(depth, 1, N))
    return out[:, :B]


ROW_TILE = 512


def _modulate_kernel(x_ref, sc_ref, sh_ref, u_ref):
    u_ref[...] = (x_ref[...] * (1.0 + sc_ref[0]) + sh_ref[0]).astype(u_ref.dtype)


def _row_specs(D, S, tr):
    vec = pl.BlockSpec((1, 1, D), lambda i: ((i * tr) // S, 0, 0))
    row = pl.BlockSpec((tr, D), lambda i: (i, 0))
    par = pl.BlockSpec((1, D), lambda i: (0, 0))
    return vec, row, par


def _modulate(x, sc, sh, S):
    T, D = x.shape
    tr = ROW_TILE
    vec, row, _ = _row_specs(D, S, tr)
    return pl.pallas_call(
        _modulate_kernel, grid=(T // tr,), in_specs=[row, vec, vec], out_specs=row,
        out_shape=jax.ShapeDtypeStruct((T, D), BF16),
        compiler_params=_params("parallel"), name="modulate",
    )(x, sc[:, None, :], sh[:, None, :])


def _deepnorm(alpha, x, h, gate, g, b):
    y = alpha * x + (1.0 + gate) * h
    mu = jnp.mean(y, -1, keepdims=True)
    yc = y - mu
    var = jnp.mean(yc * yc, -1, keepdims=True)
    return yc * lax.rsqrt(var + LN_EPS) * g + b


U32 = jnp.uint32
ROW_PARTS = D_MODEL // (2 * V7X_LANES)
assert ROW_PARTS == 8


def _bits(x):
    return lax.bitcast_convert_type(x.astype(BF16).astype(F32), U32)


def _store_row_tiled(ref, row0, value):
    n = value.shape[0]
    for j in range(ROW_PARTS):
        lo = _bits(value[:, (2 * j) * V7X_LANES:(2 * j + 1) * V7X_LANES])
        hi = _bits(value[:, (2 * j + 1) * V7X_LANES:(2 * j + 2) * V7X_LANES])
        ref[pl.ds(row0 * ROW_PARTS + j, n, stride=ROW_PARTS), :] = lax.shift_right_logical(lo, U32(16)) | hi


def _load_row_tiled(ref, row0, n):
    parts = []
    for j in range(ROW_PARTS):
        w = ref[pl.ds(row0 * ROW_PARTS + j, n, stride=ROW_PARTS), :]
        parts.append(lax.bitcast_convert_type(lax.shift_left(w, U32(16)), F32))
        parts.append(lax.bitcast_convert_type(w & U32(0xFFFF0000), F32))
    return jnp.concatenate(parts, -1)


def _ln_router_kernel(alpha, x_ref, h_ref, gate_ref, g_ref, b_ref, sc_ref, sh_ref, rwh_ref, rwl_ref, rb_ref,
                      xo_ref, u_ref, idx_ref, wgt_ref):
    xn = _deepnorm(alpha, x_ref[...], h_ref[...].astype(F32), gate_ref[0], g_ref[...], b_ref[...])
    xo_ref[...] = xn
    u = xn * (1.0 + sc_ref[0]) + sh_ref[0]
    _store_row_tiled(u_ref, 0, u)
    u_hi = u.astype(BF16)
    u_lo = (u - u_hi.astype(F32)).astype(BF16)
    logits = (_dot_nt(rwh_ref[...], u_hi) + (_dot_nt(rwh_ref[...], u_lo) + _dot_nt(rwl_ref[...], u_hi))) + rb_ref[...]
    expert = _iota(logits.shape, 0)
    vals = logits
    top_v, top_i = [], []
    for _ in range(TOP_K):
        m = jnp.max(vals, 0, keepdims=True)
        i = jnp.min(jnp.where(vals == m, expert, N_EXPERTS), 0, keepdims=True)
        top_v.append(m)
        top_i.append(i)
        vals = jnp.where(expert == i, NEG_INF, vals)
    exps = [jnp.exp(v - top_v[0]) for v in top_v]
    denom = functools.reduce(lambda a, b: a + b, exps)
    slot = _iota(idx_ref.shape, 0)
    idx = jnp.zeros(idx_ref.shape, jnp.int32)
    wgt = jnp.zeros(wgt_ref.shape, F32)
    for k in range(TOP_K):
        idx = jnp.where(slot == k, top_i[k], idx)
        wgt = jnp.where(slot == k, exps[k] / denom, wgt)
    idx_ref[...] = idx
    wgt_ref[...] = wgt


def _ln_router(x, h, gate, ln_g, ln_b, sc, sh, router_w, router_b, S, alpha):
    T, D = x.shape
    tr = ROW_TILE
    vec, row, par = _row_specs(D, S, tr)
    slots = 8
    choice = pl.BlockSpec((slots, tr), lambda i: (0, i))
    tiled_row = pl.BlockSpec((tr * ROW_PARTS, V7X_LANES), lambda i: (i, 0))
    rwt = router_w.T
    rw_hi = rwt.astype(BF16)
    rw_lo = (rwt - rw_hi.astype(F32)).astype(BF16)
    rw_spec = pl.BlockSpec((N_EXPERTS, D), lambda i: (0, 0))
    xo, u, idx, wgt = pl.pallas_call(
        functools.partial(_ln_router_kernel, alpha), grid=(T // tr,),
        in_specs=[row, row, vec, par, par, vec, vec, rw_spec, rw_spec,
                  pl.BlockSpec((N_EXPERTS, 1), lambda i: (0, 0))],
        out_specs=[row, tiled_row, choice, choice],
        out_shape=[jax.ShapeDtypeStruct((T, D), F32), jax.ShapeDtypeStruct((T * ROW_PARTS, V7X_LANES), U32),
                   jax.ShapeDtypeStruct((slots, T), jnp.int32), jax.ShapeDtypeStruct((slots, T), F32)],
        compiler_params=_params("parallel"), name="deepnorm_ln_router",
    )(x, h, gate[:, None, :], ln_g[None, :], ln_b[None, :], sc[:, None, :], sh[:, None, :], rw_hi, rw_lo,
      router_b[:, None])
    return xo, u, idx[:TOP_K].T, wgt[:TOP_K].T


MOE_ROW_TILE = 512


def _expert_weights(te_ref, tn_ref, ts_ref, w_hbm, wbuf_ref, wbf_ref, sems):
    t = pl.program_id(0)
    slot = ts_ref[t]

    def fetch(expert, s):
        return pltpu.make_async_copy(w_hbm.at[expert], wbuf_ref.at[s], sems.at[s])

    @pl.when(t == 0)
    def _():
        fetch(te_ref[0], 0).start(priority=1)

    @pl.when(tn_ref[t] >= 0)
    def _():
        fetch(tn_ref[t], 1 - slot).start(priority=1)

    @pl.when(slot >= 0)
    def _():
        fetch(te_ref[t], slot).wait()
        wbf_ref[...] = wbuf_ref[slot].astype(BF16)


def _moe_gu_kernel(te_ref, tv_ref, tn_ref, ts_ref, x_ref, w_hbm, b_ref, h_ref, wbuf_ref, wbf_ref, sems):
    _expert_weights(te_ref, tn_ref, ts_ref, w_hbm, wbuf_ref, wbf_ref, sems)

    @pl.when(tv_ref[pl.program_id(0)] > 0)
    def _():
        x = _load_row_tiled(x_ref, 0, x_ref.shape[0] // ROW_PARTS).astype(BF16)
        gu = _dot(x, wbf_ref[...]) + b_ref[0]
        gl = jnp.minimum(gu[:, :EXPERT_FF], SWIGLU_LIMIT)
        up = jnp.clip(gu[:, EXPERT_FF:], -SWIGLU_LIMIT, SWIGLU_LIMIT)
        h_ref[...] = ((up + 1.0) * gl * _sigmoid(gl * SWIGLU_ALPHA)).astype(h_ref.dtype)

    @pl.when(tv_ref[pl.program_id(0)] == 0)
    def _():
        h_ref[...] = jnp.zeros(h_ref.shape, h_ref.dtype)


def _moe_down_kernel(te_ref, tv_ref, tn_ref, ts_ref, h_ref, w_hbm, b_ref, y_ref, wbuf_ref, wbf_ref, sems):
    _expert_weights(te_ref, tn_ref, ts_ref, w_hbm, wbuf_ref, wbf_ref, sems)

    @pl.when(tv_ref[pl.program_id(0)] > 0)
    def _():
        _store_row_tiled(y_ref, 0, _dot(h_ref[...], wbf_ref[...]) + b_ref[0])

    @pl.when(tv_ref[pl.program_id(0)] == 0)
    def _():
        y_ref[...] = jnp.zeros(y_ref.shape, y_ref.dtype)


def _route_metadata(top_idx, tm):
    T = top_idx.shape[0]
    P = T * TOP_K
    n_tiles = (P + N_EXPERTS * (tm - 1)) // tm
    e_flat = top_idx.reshape(P)
    onehot = (e_flat[:, None] == jnp.arange(N_EXPERTS)[None, :]).astype(jnp.int32)
    csum = jnp.cumsum(onehot, axis=0)
    counts = csum[-1]
    rank = jnp.sum(csum * onehot, axis=1) - 1
    padded = ((counts + tm - 1) // tm) * tm
    ends_p = jnp.cumsum(padded)
    starts_p = ends_p - padded
    dest = (starts_p[e_flat] + rank).astype(jnp.int32)
    tile_start = jnp.arange(n_tiles, dtype=jnp.int32) * tm
    tile_valid = (tile_start < ends_p[-1]).astype(jnp.int32)
    tile_expert = jnp.sum((tile_start[:, None] >= ends_p[None, :]).astype(jnp.int32), axis=1)
    last_expert = jnp.max(jnp.where(counts > 0, jnp.arange(N_EXPERTS), 0))
    tile_expert = jnp.where(tile_valid > 0, tile_expert, last_expert).astype(jnp.int32)
    first = jnp.concatenate([jnp.ones((1,), bool), tile_expert[1:] != tile_expert[:-1]])
    later = tile_expert[None, :] > tile_expert[:, None]
    nxt = jnp.min(jnp.where(later, tile_expert[None, :], N_EXPERTS), axis=1)
    tile_next = jnp.where(jnp.logical_and(first, nxt < N_EXPERTS), nxt, -1).astype(jnp.int32)
    tile_slot = jnp.where(first, (jnp.cumsum(first.astype(jnp.int32)) - 1) % 2, -1).astype(jnp.int32)
    pads = jnp.stack([jnp.concatenate([starts_p + counts, ends_p[-1:]]),
                      jnp.concatenate([padded - counts, n_tiles - ends_p[-1:] // tm])]).astype(jnp.int32)
    return dest, tile_expert, tile_valid, tile_next, tile_slot, pads, n_tiles


DISPATCH_ROWS = 2048


def _zero_padding_rows(tm, n_pad_rows, pads_ref, xs_hbm, zero_ref, zsem):
    zero_ref[...] = jnp.zeros(zero_ref.shape, zero_ref.dtype)

    def put(first_row, n_rows):
        dst = xs_hbm.at[pl.ds(pl.multiple_of(first_row * ROW_PARTS, ROW_PARTS), n_rows * ROW_PARTS)]
        pltpu.make_async_copy(zero_ref.at[pl.ds(0, n_rows * ROW_PARTS)], dst, zsem).start()

    for e in range(N_EXPERTS):
        start, length = pads_ref[0, e], pads_ref[1, e]
        size = tm // 2
        while size >= 1:
            pl.when((length & size) != 0)(functools.partial(put, start + (length & ~(2 * size - 1)), size))
            size //= 2
    max_tail = (n_pad_rows + tm - 1) // tm
    for t in range(max_tail):
        pl.when(t < pads_ref[1, N_EXPERTS])(functools.partial(put, pads_ref[0, N_EXPERTS] + t * tm, tm))
    n = n_pad_rows * ROW_PARTS
    pltpu.make_async_copy(xs_hbm.at[pl.ds(0, n)], xs_hbm.at[pl.ds(0, n)], zsem).wait()


def _dispatch_kernel(tm, n_pad_rows, pads_ref, dest_ref, u_ref, xs_hbm, sem, zero_ref, zsem):
    R = dest_ref.shape[-1]
    pl.when(pl.program_id(0) == 0)(
        functools.partial(_zero_padding_rows, tm, n_pad_rows, pads_ref, xs_hbm, zero_ref, zsem))

    def row(ref, i):
        return ref.at[pl.ds(pl.multiple_of(i * ROW_PARTS, ROW_PARTS), ROW_PARTS)]

    def issue(t, carry):
        for k in range(TOP_K):
            pltpu.make_async_copy(row(u_ref, t), row(xs_hbm, dest_ref[0, 0, t * TOP_K + k]), sem).start(priority=k % 2)
        return carry

    lax.fori_loop(0, R // TOP_K, issue, 0, unroll=2)
    n = (R // TOP_K) * ROW_PARTS
    for _ in range(TOP_K):
        pltpu.make_async_copy(u_ref, xs_hbm.at[pl.ds(0, n)], sem).wait()


def _dispatch(u, dest, pads, rows, tm):
    P = dest.shape[0]
    R = DISPATCH_ROWS
    return pl.pallas_call(
        functools.partial(_dispatch_kernel, tm, rows - P),
        grid=(P // R,),
        in_specs=[pl.BlockSpec(memory_space=pltpu.SMEM),
                  pl.BlockSpec((1, 1, R), lambda s: (s, 0, 0), memory_space=pltpu.SMEM),
                  pl.BlockSpec(((R // TOP_K) * ROW_PARTS, u.shape[1]), lambda s: (s, 0))],
        out_specs=pl.BlockSpec(memory_space=pl.ANY),
        out_shape=jax.ShapeDtypeStruct((rows * ROW_PARTS, u.shape[1]), u.dtype),
        scratch_shapes=[pltpu.SemaphoreType.DMA(()), pltpu.VMEM((tm * ROW_PARTS, u.shape[1]), u.dtype),
                        pltpu.SemaphoreType.DMA(())],
        compiler_params=_params("arbitrary"), name="moe_dispatch",
    )(pads, dest.reshape(P // R, 1, R), u)


def _moe_ffn(u, top_idx, layer, w_gu, b_gu, w_down, b_down):
    D = D_MODEL
    tm = MOE_ROW_TILE
    dest, tile_expert, tile_valid, tile_next, tile_slot, pads, n_tiles = _route_metadata(top_idx, tm)
    tile_expert = tile_expert + layer * N_EXPERTS
    tile_next = jnp.where(tile_next >= 0, tile_next + layer * N_EXPERTS, -1)
    sched = (tile_expert, tile_valid, tile_next, tile_slot)
    w_gu, w_down = (w.reshape((-1,) + w.shape[2:]) for w in (w_gu, w_down))
    b_gu, b_down = (b.reshape(-1, b.shape[-1]) for b in (b_gu, b_down))
    rows = n_tiles * tm
    x_sorted = _dispatch(u, dest, pads, rows, tm)
    ff2 = 2 * EXPERT_FF
    tiled = pl.BlockSpec((tm * ROW_PARTS, V7X_LANES), lambda t, *_: (t, 0))
    hidden = pl.BlockSpec((tm, EXPERT_FF), lambda t, *_: (t, 0))
    weights = pl.BlockSpec(memory_space=pl.ANY)

    def bias(n):
        return pl.BlockSpec((1, 1, n), lambda t, te, *_: (te[t], 0, 0))

    def weight_scratch(k, n):
        return [pltpu.VMEM((2, k, n), F32), pltpu.VMEM((k, n), BF16), pltpu.SemaphoreType.DMA((2,))]

    h = pl.pallas_call(
        _moe_gu_kernel,
        grid_spec=pltpu.PrefetchScalarGridSpec(
            num_scalar_prefetch=4, grid=(n_tiles,),
            in_specs=[tiled, weights, bias(ff2)], out_specs=hidden, scratch_shapes=weight_scratch(D, ff2)),
        out_shape=jax.ShapeDtypeStruct((rows, EXPERT_FF), BF16),
        compiler_params=_params("arbitrary"), name="moe_gate_up",
    )(*sched, x_sorted, w_gu, b_gu[:, None, :])
    y = pl.pallas_call(
        _moe_down_kernel,
        grid_spec=pltpu.PrefetchScalarGridSpec(
            num_scalar_prefetch=4, grid=(n_tiles,),
            in_specs=[hidden, weights, bias(D)], out_specs=tiled, scratch_shapes=weight_scratch(EXPERT_FF, D)),
        out_shape=jax.ShapeDtypeStruct((rows * ROW_PARTS, V7X_LANES), U32),
        compiler_params=_params("arbitrary"), name="moe_down",
    )(*sched, h, w_down, b_down[:, None, :])
    return y, dest


COMBINE_TILE = 256
COMBINE_SUB = 64


def _combine_ln_kernel(alpha, dest_ref, next_ref, y_hbm, x_ref, wgt_ref, gate_ref, g_ref, b_ref, sc_ref, sh_ref,
                       xo_ref, u_ref, buf0_ref, buf1_ref, sems):
    i = pl.program_id(0)
    bufs = (buf0_ref, buf1_ref)

    def row(ref, r):
        return ref.at[pl.ds(pl.multiple_of(r * ROW_PARTS, ROW_PARTS), ROW_PARTS)]

    def gather(idx_ref, slot):
        def issue(tok, carry):
            for k in range(TOP_K):
                pltpu.make_async_copy(row(y_hbm, idx_ref[0, 0, tok * TOP_K + k]),
                                      row(bufs[slot], k * COMBINE_TILE + tok), sems.at[slot]).start(priority=k % 2)
            return carry

        lax.fori_loop(0, COMBINE_TILE, issue, 0, unroll=2)

    def drain(slot):
        buf = bufs[slot]
        pltpu.make_async_copy(y_hbm.at[pl.ds(0, buf.shape[0])], buf, sems.at[slot]).wait()

    def finish(slot):
        buf = bufs[slot]
        drain(slot)
        for lo in range(0, COMBINE_TILE, COMBINE_SUB):
            for tok in range(lo, lo + COMBINE_SUB):
                for k in range(TOP_K):
                    pltpu.make_async_copy(row(y_hbm, next_ref[0, 0, tok * TOP_K + k]),
                                          row(bufs[1 - slot], k * COMBINE_TILE + tok),
                                          sems.at[1 - slot]).start(priority=k % 2)
            wgt = wgt_ref[pl.ds(lo, COMBINE_SUB), :]
            f = sum(_load_row_tiled(buf, k * COMBINE_TILE + lo, COMBINE_SUB) * wgt[:, k:k + 1] for k in range(TOP_K))
            xn = _deepnorm(alpha, x_ref[pl.ds(lo, COMBINE_SUB), :], f, gate_ref[0], g_ref[...], b_ref[...])
            xo_ref[pl.ds(lo, COMBINE_SUB), :] = xn
            u_ref[pl.ds(lo, COMBINE_SUB), :] = (xn * (1.0 + sc_ref[0]) + sh_ref[0]).astype(u_ref.dtype)

        pl.when(i + 1 == pl.num_programs(0))(functools.partial(drain, 1 - slot))

    pl.when(i == 0)(functools.partial(gather, dest_ref, 0))
    for slot in range(2):
        pl.when(i % 2 == slot)(functools.partial(finish, slot))


def _combine_ln(x, y, dest, top_w, gate, ln_g, ln_b, sc, sh, S, alpha):
    T, D = x.shape
    tr = COMBINE_TILE
    vec, row, par = _row_specs(D, S, tr)
    n_steps = T // tr
    dest = dest.reshape(n_steps, 1, tr * TOP_K)
    gathered = pltpu.VMEM((tr * TOP_K * ROW_PARTS, V7X_LANES), U32)
    return pl.pallas_call(
        functools.partial(_combine_ln_kernel, alpha), grid=(n_steps,),
        in_specs=[pl.BlockSpec((1, 1, tr * TOP_K), lambda i: (i, 0, 0), memory_space=pltpu.SMEM),
                  pl.BlockSpec((1, 1, tr * TOP_K), lambda i: (jnp.minimum(i + 1, n_steps - 1), 0, 0),
                               memory_space=pltpu.SMEM),
                  pl.BlockSpec(memory_space=pl.ANY), row,
                  pl.BlockSpec((tr, TOP_K), lambda i: (i, 0)), vec, par, par, vec, vec],
        out_specs=[row, row],
        out_shape=[jax.ShapeDtypeStruct((T, D), F32), jax.ShapeDtypeStruct((T, D), BF16)],
        scratch_shapes=[gathered, gathered, pltpu.SemaphoreType.DMA((2,))],
        compiler_params=_params("arbitrary"), name="moe_combine_ln",
    )(dest, dest, y, x, top_w, gate[:, None, :], ln_g[None, :], ln_b[None, :], sc[:, None, :], sh[:, None, :])


ATTN_TILE = 256


LOG2E = math.log2(math.e)


def _softmax_pv(s2, v):
    p = jnp.exp2(s2 - jnp.max(s2, -1, keepdims=True))
    return _dot(p.astype(BF16), v) / jnp.sum(p, -1, keepdims=True)


CAUSAL_VARIANTS = 4


def _causal_widths(n_tiles):
    step = max(n_tiles // CAUSAL_VARIANTS, 1)
    bounds = list(range(step, n_tiles, step)) + [n_tiles]
    return [(n, n) for n in bounds]


MLA_HEADS_PER_STEP = 4


def _mla_attn_kernel(scale, qn_ref, qr_ref, cos_ref, sin_ref, kv_ref, kr_ref, o_ref):
    qi = pl.program_id(2)
    t = ATTN_TILE
    n_tiles = kv_ref.shape[1] // t
    x = qr_ref[0].astype(F32)
    half = MLA_ROPE // 2
    first_half = _iota(x.shape, 1) % MLA_ROPE < half
    swapped = jnp.where(first_half, pltpu.roll(x, x.shape[1] - half, 1), pltpu.roll(x, half, 1))
    qr_all = (x * cos_ref[...] + swapped * sin_ref[...]).astype(BF16)

    def attend(n_kv):
        w = n_kv * t
        visible = _iota((t, w), 1) <= _iota((t, w), 0) + qi * t
        outs = []
        for hh in range(MLA_HEADS_PER_STEP):
            qn = qn_ref[0, :, hh * MLA_NOPE:(hh + 1) * MLA_NOPE]
            qr = qr_all[:, hh * MLA_ROPE:(hh + 1) * MLA_ROPE]
            c0 = hh * (MLA_NOPE + MLA_V)
            s = (_dot_nt(qn, kv_ref[0, 0:w, c0:c0 + MLA_NOPE]) + _dot_nt(qr, kr_ref[0, 0:w, :])) * scale
            s = jnp.where(visible, s, NEG_INF)
            outs.append(_softmax_pv(s, kv_ref[0, 0:w, c0 + MLA_NOPE:c0 + MLA_NOPE + MLA_V]))
        o_ref[0] = jnp.concatenate(outs, -1).astype(o_ref.dtype)

    lo = 0
    for hi, n_kv in _causal_widths(n_tiles):
        pl.when(jnp.logical_and(qi >= lo, qi < hi))(functools.partial(attend, n_kv))
        lo = hi


def _rope(x, cos, sin):
    half = x.shape[-1] // 2
    x1, x2 = x[..., :half], x[..., half:]
    return jnp.concatenate([x1 * cos - x2 * sin, x2 * cos + x1 * sin], -1)


def _mla_mixer(u, B, S, w_in, q_norm, kv_norm, w_qb, w_kvb, w_o):
    H = MLA_HEADS
    n_lat = MLA_Q_LORA + MLA_KV_LORA
    lat = _mm(u, w_in, F32, n_cols=n_lat)
    k_rope = _mm(u, w_in, F32, col0=n_lat, n_cols=V7X_LANES)[:, :MLA_ROPE]
    wq = w_qb.reshape(MLA_Q_LORA, H, MLA_NOPE + MLA_ROPE)
    q_nope = _mm(lat, wq[:, :, :MLA_NOPE].reshape(MLA_Q_LORA, H * MLA_NOPE), BF16, rms_gain=q_norm, x_col=0)
    q_rope = _mm(lat, wq[:, :, MLA_NOPE:].reshape(MLA_Q_LORA, H * MLA_ROPE), F32, rms_gain=q_norm, x_col=0)
    kv = _mm(lat, w_kvb, BF16, rms_gain=kv_norm, x_col=1)
    inv_freq = ROPE_THETA ** (-jnp.arange(MLA_ROPE // 2, dtype=F32) / (MLA_ROPE // 2))
    ang = jnp.arange(S, dtype=F32)[:, None] * inv_freq[None, :]
    cos, sin = jnp.cos(ang), jnp.sin(ang)
    k_rope = _rope(k_rope.reshape(B, S, MLA_ROPE), cos, sin).astype(BF16)
    g = MLA_HEADS_PER_STEP
    cos2 = jnp.tile(jnp.concatenate([cos, cos], -1), (1, g))
    sin2 = jnp.tile(jnp.concatenate([-sin, sin], -1), (1, g))
    t = ATTN_TILE
    o = pl.pallas_call(
        functools.partial(_mla_attn_kernel, (MLA_NOPE + MLA_ROPE) ** -0.5 * LOG2E),
        grid=(B, H // g, S // t),
        in_specs=[pl.BlockSpec((1, t, g * MLA_NOPE), lambda b, h, i: (b, i, h)),
                  pl.BlockSpec((1, t, g * MLA_ROPE), lambda b, h, i: (b, i, h)),
                  pl.BlockSpec((t, g * MLA_ROPE), lambda b, h, i: (i, 0)),
                  pl.BlockSpec((t, g * MLA_ROPE), lambda b, h, i: (i, 0)),
                  pl.BlockSpec((1, S, g * (MLA_NOPE + MLA_V)), lambda b, h, i: (b, 0, h)),
                  pl.BlockSpec((1, S, MLA_ROPE), lambda b, h, i: (b, 0, 0))],
        out_specs=pl.BlockSpec((1, t, g * MLA_V), lambda b, h, i: (b, i, h)),
        out_shape=jax.ShapeDtypeStruct((B, S, H * MLA_V), BF16),
        compiler_params=_params("parallel", "parallel", "arbitrary"), name="mla_attention",
    )(q_nope.reshape(B, S, -1), q_rope.reshape(B, S, -1), cos2, sin2, kv.reshape(B, S, -1), k_rope)
    return _mm(o.reshape(B * S, H * MLA_V), w_o, F32)


def _t5_bucket(dist):
    n = jnp.maximum(dist, 0)
    max_exact = REL_BUCKETS // 2
    large = max_exact + (jnp.log(jnp.maximum(n, 1).astype(F32) / max_exact)
                         / math.log(REL_MAX_DIST / max_exact) * (REL_BUCKETS - max_exact)).astype(jnp.int32)
    large = jnp.minimum(large, REL_BUCKETS - 1)
    return jnp.where(n < max_exact, n, large)


MOBA_FAR = 2


MOBA_HEADS_PER_STEP = 8


def _moba_kernel(scale, n_sel, tab_ref, q_ref, k_ref, v_ref, bkt_ref, o_ref, kbf_ref, vbf_ref, kmean_ref, bias_ref):
    hp, b, qi = pl.program_id(0), pl.program_id(1), pl.program_id(2)
    L, Dh = MOBA_BLOCK, V7X_LANES
    S = k_ref.shape[1]
    n_blk = S // L
    heads = range(MOBA_HEADS_PER_STEP)
    lanes = [slice(hh * Dh, (hh + 1) * Dh) for hh in heads]

    @pl.when(jnp.logical_and(b == 0, qi == 0))
    def _():
        for hh in heads:
            for d in range(MOBA_FAR + 1):
                bucket = bkt_ref[d]
                tile = jnp.zeros((L, L), F32)
                for e in range(REL_BUCKETS):
                    tile = jnp.where(bucket == e, tab_ref[e, hp * MOBA_HEADS_PER_STEP + hh] * LOG2E, tile)
                bias_ref[hh, d] = tile

    @pl.when(qi == 0)
    def _():
        k = k_ref[0].astype(F32)
        kbf_ref[...] = k.astype(BF16)
        vbf_ref[...] = v_ref[0].astype(BF16)
        kmean_ref[...] = jnp.zeros(kmean_ref.shape, F32)
        for hh in heads:
            kmean_ref[hh, 0:n_blk, :] = jnp.mean(k[:, lanes[hh]].reshape(n_blk, L, Dh), axis=1)

    nb = kmean_ref.shape[1]
    blk = _iota((nb, L), 0)
    causal_add = jnp.where(_iota((L, L), 1) <= _iota((L, L), 0), 0.0, NEG_INF)
    qb, past_add = [], []
    for hh in heads:
        q = q_ref[0, :, lanes[hh]].astype(F32)
        qb.append(q.astype(BF16))
        gate = jnp.where(blk < qi, _dot_nt(kmean_ref[hh], q, precision=HIGHEST), NEG_INF)
        picked = jnp.zeros((nb, L), F32)
        for j in range(n_blk - 1):
            gj = gate[j:j + 1, :]
            beats = jnp.logical_or(gate > gj, jnp.logical_and(gate == gj, blk < j))
            in_topk = jnp.sum(jnp.where(beats, 1.0, 0.0), 0, keepdims=True) < n_sel
            picked = jnp.where(blk == j, jnp.where(jnp.logical_and(in_topk, j < qi), 1.0, 0.0), picked)
        picked = _dot_tn(picked, jnp.where(_iota((nb, nb), 0) == _iota((nb, nb), 1), 1.0, 0.0))
        past_add.append([jnp.where(picked[:, j:j + 1] > 0.5, 0.0, NEG_INF) for j in range(n_blk)])

    def attend(n_kv):
        w = n_kv * L
        outs = []
        for hh in heads:
            s = _dot_nt(qb[hh], kbf_ref[0:w, lanes[hh]]) * scale
            parts = []
            for j in range(n_kv):
                bias = bias_ref[hh, jnp.clip(qi - j, 0, MOBA_FAR)]
                parts.append(s[:, j * L:(j + 1) * L] + bias + jnp.where(j == qi, causal_add, past_add[hh][j]))
            outs.append(_softmax_pv(jnp.concatenate(parts, -1), vbf_ref[0:w, lanes[hh]]))
        o_ref[0] = jnp.concatenate(outs, -1).astype(o_ref.dtype)

    lo = 0
    for hi, n_kv in _causal_widths(n_blk):
        pl.when(jnp.logical_and(qi >= lo, qi < hi))(functools.partial(attend, n_kv))
        lo = hi


def _moba_mixer(u, B, S, w_in, w_o, rel_bias):
    H, Dh, L = MOBA_HEADS, u.shape[1] // MOBA_HEADS, MOBA_BLOCK
    n_blk = S // L
    assert S % L == 0 and Dh == V7X_LANES and (MOBA_FAR - 1) * L >= REL_MAX_DIST
    n_sel = max(min(MOBA_TOPK, n_blk - 1), 1)
    hps = MOBA_HEADS_PER_STEP
    HG, hw = H // hps, hps * Dh
    qkv = _mm(u, w_in, BF16).reshape(B, S, 3 * H * Dh)
    qk = jnp.arange(L)[:, None] - jnp.arange(L)[None, :]
    bucket = _t5_bucket(jnp.stack([qk + d * L for d in range(MOBA_FAR + 1)])).astype(jnp.int32)
    o = pl.pallas_call(
        functools.partial(_moba_kernel, Dh ** -0.5 * LOG2E, n_sel),
        grid=(HG, B, n_blk),
        in_specs=[pl.BlockSpec(memory_space=pltpu.SMEM),
                  pl.BlockSpec((1, L, hw), lambda h, b, i: (b, i, h)),
                  pl.BlockSpec((1, S, hw), lambda h, b, i: (b, 0, HG + h)),
                  pl.BlockSpec((1, S, hw), lambda h, b, i: (b, 0, 2 * HG + h)),
                  pl.BlockSpec((MOBA_FAR + 1, L, L), lambda h, b, i: (0, 0, 0))],
        out_specs=pl.BlockSpec((1, L, hw), lambda h, b, i: (b, i, h)),
        out_shape=jax.ShapeDtypeStruct((B, S, H * Dh), BF16),
        scratch_shapes=[pltpu.VMEM((S, hw), BF16), pltpu.VMEM((S, hw), BF16),
                        pltpu.VMEM((hps, -(-n_blk // 8) * 8, Dh), F32),
                        pltpu.VMEM((hps, MOBA_FAR + 1, L, L), F32)],
        compiler_params=_params("parallel", "arbitrary", "arbitrary"), name="moba_attention",
    )(rel_bias, qkv, qkv, qkv, bucket)
    return _mm(o.reshape(B * S, H * Dh), w_o, F32)


GDN_HEAD_GROUP = 32


def _lane_sum(x):
    return _dot(x.astype(BF16), jnp.ones((x.shape[1], x.shape[1]), BF16))


def _l2norm(x):
    return x * lax.rsqrt(_lane_sum(x * x) + L2_EPS)


def _dotb(a, b):
    return _dot(a.astype(BF16), b.astype(BF16))


def _unit_lower_inverse(a_lows, block):
    C = a_lows[0].shape[0]
    r = _iota((C, C), 0)
    c = _iota((C, C), 1)
    eye = jnp.where(r == c, 1.0, 0.0)
    same = (r // block) == (c // block)
    a_d = [jnp.where(same, a, 0.0) for a in a_lows]
    a_off = [a - d for a, d in zip(a_lows, a_d)]
    inv_d = [eye - d for d in a_d]
    pw = a_d
    k = 2
    while k < block:
        pw = [_dotb(p, p) for p in pw]
        inv_d = [_dotb(i, eye + p) for i, p in zip(inv_d, pw)]
        k *= 2
    n = [_dotb(i, o) for i, o in zip(inv_d, a_off)]
    inv_n = [eye - x for x in n]
    pw = n
    k = 2
    while k < C // block:
        pw = [_dotb(p, p) for p in pw]
        inv_n = [_dotb(i, eye + p) for i, p in zip(inv_n, pw)]
        k *= 2
    return [_dotb(i, d) for i, d in zip(inv_n, inv_d)]


GDN_HIST = 16


def _causal_conv_silu(x_ref, w_ref, hist_ref):
    C = x_ref.shape[1]
    x = x_ref[0]
    hist_ref[GDN_HIST:GDN_HIST + C, :] = x
    taps = GDN_CONV - 1
    o = _iota((taps * C, GDN_HIST + C), 0)
    r = _iota((taps * C, GDN_HIST + C), 1)
    shift = jnp.where(r == o % C + o // C + (GDN_HIST - taps), 1.0, 0.0).astype(BF16)
    shifted = _dot(shift, hist_ref[...])
    y = x.astype(F32) * w_ref[taps:taps + 1, :]
    for i in range(taps):
        y = y + shifted[i * C:(i + 1) * C, :] * w_ref[i:i + 1, :]
    hist_ref[0:GDN_HIST, :] = hist_ref[C:C + GDN_HIST, :]
    return _silu(y)


def _gdn_kernel(q_ref, k_ref, v_ref, z_ref, cwq_ref, cwk_ref, cwv_ref, gc_ref, gct_ref, beta_ref, ng_ref,
                o_ref, state_ref, hq_ref, hk_ref, hv_ref):
    G, C, DK, DV = GDN_HEAD_GROUP, GDN_CHUNK, GDN_DK, GDN_DV
    rep = GDN_V_HEADS // GDN_K_HEADS
    heads = range(G)

    @pl.when(pl.program_id(2) == 0)
    def _():
        state_ref[...] = jnp.zeros(state_ref.shape, F32)
        for hist in (hq_ref, hk_ref, hv_ref):
            hist[0:GDN_HIST, :] = jnp.zeros((GDN_HIST, hist.shape[1]), hist.dtype)

    qc = _causal_conv_silu(q_ref, cwq_ref, hq_ref)
    kc = _causal_conv_silu(k_ref, cwk_ref, hk_ref)
    vc = _causal_conv_silu(v_ref, cwv_ref, hv_ref)
    r = _iota((C, C), 0)
    c = _iota((C, C), 1)
    tri = c <= r
    strict = c < r
    gc = gc_ref[0, 0]
    gct = gct_ref[0, 0, 0]
    beta = beta_ref[0, 0]
    q = [_l2norm(qc[:, i * DK:(i + 1) * DK]) * (DK ** -0.5) for i in range(G // rep)]
    k = [_l2norm(kc[:, i * DK:(i + 1) * DK]) for i in range(G // rep)]
    assert DK == V7X_LANES and DV == V7X_LANES
    kb = [x.astype(BF16) for x in k]
    qk = [_dot_nt(a.astype(BF16), b) for a, b in zip(q, kb)]
    kk = [_dot_nt(b, b) for b in kb]
    g_col = [gc[:, h:h + 1] for h in heads]
    b_col = [beta[:, h:h + 1] for h in heads]
    decay = [jnp.where(tri, jnp.exp(jnp.where(tri, g_col[h] - gct[h:h + 1, :], 0.0)), 0.0) for h in heads]
    t_inv = _unit_lower_inverse([jnp.where(strict, kk[h // rep] * b_col[h] * decay[h], 0.0) for h in heads], 16)
    e_g = [jnp.exp(g) for g in g_col]
    rhs = [jnp.concatenate([vc[:, h * DV:(h + 1) * DV] * b_col[h], k[h // rep] * (b_col[h] * e_g[h])], -1)
           for h in heads]
    sol = [_dot(t_inv[h].astype(BF16), rhs[h].astype(BF16)) for h in heads]
    state = [state_ref[h] for h in heads]
    state_b = [s.astype(BF16) for s in state]
    v_new = [sol[h][:, :DV] - _dot(sol[h][:, DV:].astype(BF16), state_b[h]) for h in heads]
    v_new_b = [x.astype(BF16) for x in v_new]
    attn = [jnp.where(tri, qk[h // rep] * decay[h], 0.0).astype(BF16) for h in heads]
    o = [_dot((q[h // rep] * e_g[h]).astype(BF16), state_b[h]) + _dot(attn[h], v_new_b[h]) for h in heads]
    g_last = [g[C - 1:C, :] for g in g_col]
    k_tail = [(k[h // rep] * jnp.exp(g_last[h] - g_col[h])).astype(BF16) for h in heads]
    new_state = [state[h] * jnp.exp(g_last[h]) + _dot_tn(k_tail[h], v_new_b[h]) for h in heads]
    o = [x * lax.rsqrt(_lane_sum(x * x) * (1.0 / DV) + RMS_EPS) * ng_ref[...] for x in o]
    o = [o[h] * _silu(z_ref[0, :, h * DV:(h + 1) * DV].astype(F32)) for h in heads]
    for h in heads:
        state_ref[h] = new_state[h]
    o_ref[0] = jnp.concatenate(o, -1).astype(o_ref.dtype)


def _gdn_mixer(u, B, S, w_in, conv_w, a_log, dt_bias, norm_g, w_o):
    HK, HV, DK, DV, C, G = GDN_K_HEADS, GDN_V_HEADS, GDN_DK, GDN_DV, GDN_CHUNK, GDN_HEAD_GROUP
    qk_dim, v_dim = HK * DK, HV * DV
    n_main = 2 * qk_dim + 2 * v_dim
    w_t = w_in.T
    proj = _mm(u, w_t, BF16, n_cols=n_main, w_transposed=True).reshape(B, S, n_main)
    ba = _mm(u, w_t, F32, col0=n_main, n_cols=V7X_LANES, w_transposed=True).reshape(B, S, -1)
    n_conv = 2 * qk_dim + v_dim
    beta = jax.nn.sigmoid(ba[:, :, :HV])
    g = -jnp.exp(a_log) * jax.nn.softplus(ba[:, :, HV:2 * HV] + dt_bias)
    N = S // C
    gc = jnp.cumsum(g.reshape(B, N, C, HV), axis=2)
    HG = HV // G
    gc_g = gc.reshape(B, N, C, HG, G).transpose(0, 3, 1, 2, 4).reshape(B, HG, S, G)
    gct_g = gc.reshape(B, N, C, HG, G).transpose(0, 3, 1, 4, 2)
    beta_g = beta.reshape(B, S, HG, G).transpose(0, 2, 1, 3)
    kw = (G // (HV // HK)) * DK
    vw = G * DV
    k_blk, v_blk, z_blk = qk_dim // kw, 2 * qk_dim // vw, n_conv // vw
    o = pl.pallas_call(
        _gdn_kernel,
        grid=(B, HG, N),
        in_specs=[pl.BlockSpec((1, C, kw), lambda b, h, n: (b, n, h)),
                  pl.BlockSpec((1, C, kw), lambda b, h, n: (b, n, k_blk + h)),
                  pl.BlockSpec((1, C, vw), lambda b, h, n: (b, n, v_blk + h)),
                  pl.BlockSpec((1, C, vw), lambda b, h, n: (b, n, z_blk + h)),
                  pl.BlockSpec((GDN_CONV, kw), lambda b, h, n: (0, h)),
                  pl.BlockSpec((GDN_CONV, kw), lambda b, h, n: (0, k_blk + h)),
                  pl.BlockSpec((GDN_CONV, vw), lambda b, h, n: (0, v_blk + h)),
                  pl.BlockSpec((1, 1, C, G), lambda b, h, n: (b, h, n, 0)),
                  pl.BlockSpec((1, 1, 1, G, C), lambda b, h, n: (b, h, n, 0, 0)),
                  pl.BlockSpec((1, 1, C, G), lambda b, h, n: (b, h, n, 0)),
                  pl.BlockSpec((1, DV), lambda b, h, n: (0, 0))],
        out_specs=pl.BlockSpec((1, C, vw), lambda b, h, n: (b, n, h)),
        out_shape=jax.ShapeDtypeStruct((B, S, v_dim), BF16),
        scratch_shapes=[pltpu.VMEM((G, DK, DV), F32), pltpu.VMEM((GDN_HIST + C, kw), BF16),
                        pltpu.VMEM((GDN_HIST + C, kw), BF16), pltpu.VMEM((GDN_HIST + C, vw), BF16)],
        compiler_params=_params("parallel", "parallel", "arbitrary"), name="gdn_chunked",
    )(proj, proj, proj, proj, conv_w, conv_w, conv_w, gc_g, gct_g, beta_g, norm_g[None, :])
    return _mm(o.reshape(B * S, v_dim), w_o, F32)


def _gla_kernel(scale, q_ref, k_ref, v_ref, og_ref, gk_ref, wgk_ref, bgk_ref, ng_ref, o_ref, state_ref):
    C = GLA_CHUNK
    H, dv, dk = state_ref.shape
    heads = range(H)

    @pl.when(pl.program_id(1) == 0)
    def _():
        state_ref[...] = jnp.zeros(state_ref.shape, F32)

    x = _dot(gk_ref[0].astype(BF16), wgk_ref[...].astype(BF16)) + bgk_ref[...]
    log_alpha = (jnp.minimum(x, 0.0) - jnp.log(1.0 + jnp.exp(-jnp.abs(x)))) / GLA_GATE_NORMALIZER
    r = _iota((C, C), 0)
    c = _iota((C, C), 1)
    causal = c <= r
    b = _dot(jnp.where(causal, 1.0, 0.0), log_alpha, precision=HIGHEST)
    b_last = b[C - 1:C, :]
    q = q_ref[0].astype(F32) * scale
    k = k_ref[0].astype(F32)
    q_dec = (q * jnp.exp(b)).astype(BF16)
    k_inv = (k * jnp.exp(-b)).astype(BF16)
    k_tail = (k * jnp.exp(b_last - b)).astype(BF16)
    decay = jnp.exp(b_last)
    ks = [slice(h * dk, (h + 1) * dk) for h in heads]
    v = [v_ref[0, :, h * dv:(h + 1) * dv] for h in heads]
    attn = [jnp.where(causal, _dot_nt(q_dec[:, ks[h]], k_inv[:, ks[h]]), 0.0).astype(BF16) for h in heads]
    state_t = [state_ref[h] for h in heads]
    o = [_dot(attn[h], v[h]) + _dot_nt(q_dec[:, ks[h]], state_t[h].astype(BF16)) for h in heads]
    new_state = [state_t[h] * decay[:, ks[h]] + _dot_tn(v[h], k_tail[:, ks[h]]) for h in heads]
    o = [x * lax.rsqrt(jnp.mean(x * x, -1, keepdims=True) + RMS_EPS) * ng_ref[...] for x in o]
    for h in heads:
        state_ref[h] = new_state[h]
    o_ref[0] = (jnp.concatenate(o, -1) * _silu(og_ref[0].astype(F32))).astype(o_ref.dtype)


def _gla_mixer(u, B, S, w_in, w_gk, b_gk, norm_g, w_o):
    D = u.shape[1]
    H, C = GLA_HEADS, GLA_CHUNK
    key_dim, val_dim = D // 2, D
    dk, dv = key_dim // H, val_dim // H
    n_main = 2 * key_dim + 2 * val_dim
    w_t = w_in.T
    proj = _mm(u, w_t, BF16, n_cols=n_main, w_transposed=True).reshape(B, S, n_main)
    gk = _mm(u, w_t, F32, col0=n_main, n_cols=V7X_LANES, w_transposed=True).reshape(B, S, -1)
    wgk = jnp.pad(w_gk, ((0, gk.shape[-1] - GLA_GATE_RANK), (0, 0)))
    o = pl.pallas_call(
        functools.partial(_gla_kernel, dk ** -0.5),
        grid=(B, S // C),
        in_specs=[pl.BlockSpec((1, C, key_dim), lambda b, n: (b, n, 0)),
                  pl.BlockSpec((1, C, key_dim), lambda b, n: (b, n, 1)),
                  pl.BlockSpec((1, C, val_dim), lambda b, n: (b, n, 2 * key_dim // val_dim)),
                  pl.BlockSpec((1, C, val_dim), lambda b, n: (b, n, 2 * key_dim // val_dim + 1)),
                  pl.BlockSpec((1, C, gk.shape[-1]), lambda b, n: (b, n, 0)),
                  pl.BlockSpec((gk.shape[-1], key_dim), lambda b, n: (0, 0)),
                  pl.BlockSpec((1, key_dim), lambda b, n: (0, 0)),
                  pl.BlockSpec((1, dv), lambda b, n: (0, 0))],
        out_specs=pl.BlockSpec((1, C, val_dim), lambda b, n: (b, n, 0)),
        out_shape=jax.ShapeDtypeStruct((B, S, val_dim), BF16),
        scratch_shapes=[pltpu.VMEM((H, dv, dk), F32)],
        compiler_params=_params("parallel", "arbitrary"), name="gla_chunked",
    )(proj, proj, proj, proj, gk, wgk, b_gk[None, :], norm_g[None, :])
    return _mm(o.reshape(B * S, val_dim), w_o, F32)


def kernel(x, c, rel_bias, mla_w_in, mla_q_norm, mla_kv_norm, mla_w_qb, mla_w_kvb, mla_w_o, gdn_w_in, gdn_conv_w, gdn_a_log, gdn_dt_bias, gdn_norm, gdn_w_o, gla_w_in, gla_w_gk, gla_b_gk, gla_norm, gla_w_o, moba_w_in, moba_w_o, ada_w, ada_b, ln_g, ln_b, router_w, router_b, moe_w_gu, moe_b_gu, moe_w_down, moe_b_down):
    B, S, D = x.shape
    assert D == D_MODEL
    depth = ada_w.shape[0]
    alpha = (2 * depth) ** 0.25
    mod = _ada_mod(c, ada_w, ada_b)
    sh_a, sc_a, g_a, sh_f, sc_f, g_f = (mod[:, :, k * D:(k + 1) * D] for k in range(6))
    xt = x.reshape(B * S, D)
    u = _modulate(xt, sc_a[0], sh_a[0], S)
    for i in range(depth):
        m, j = i % N_MIXERS, i // N_MIXERS
        if m == 0:
            h = _mla_mixer(u, B, S, mla_w_in[j], mla_q_norm[j], mla_kv_norm[j], mla_w_qb[j], mla_w_kvb[j], mla_w_o[j])
        elif m == 1:
            h = _gdn_mixer(u, B, S, gdn_w_in[j], gdn_conv_w[j], gdn_a_log[j], gdn_dt_bias[j], gdn_norm[j], gdn_w_o[j])
        elif m == 2:
            h = _gla_mixer(u, B, S, gla_w_in[j], gla_w_gk[j], gla_b_gk[j], gla_norm[j], gla_w_o[j])
        else:
            h = _moba_mixer(u, B, S, moba_w_in[j], moba_w_o[j], rel_bias)
        xt, u, top_idx, top_w = _ln_router(xt, h, g_a[i], ln_g[i, 0], ln_b[i, 0], sc_f[i], sh_f[i],
                                           router_w[i], router_b[i], S, alpha)
        y, dest = _moe_ffn(u, top_idx, i, moe_w_gu, moe_b_gu, moe_w_down, moe_b_down)
        nxt = (i + 1) % depth
        xt, u = _combine_ln(xt, y, dest, top_w, g_f[i], ln_g[i, 1], ln_b[i, 1], sc_a[nxt], sh_a[nxt], S, alpha)
    return xt.reshape(B, S, D)
```
